```python
import jax, jax.numpy as jnp
from jax import lax
import numpy as np

D_MODEL = 1024
BATCH = 2
SEQ = 8192
DEPTH = 1

N_META = 16
CHUNK = 128
META_PAD = CHUNK - N_META
RET_HEADS = 4
RET_DK = 128
RET_DV = 256
ATT_HEADS = 8
ATT_GROUPS = 2
ATT_HD = 128
WINDOW = 128
ROPE_THETA = 10000.0
EPS = 1e-6
NEG_INF = -1e30
RET_QK = RET_HEADS * RET_DK
RET_V = RET_HEADS * RET_DV
ATT_Q = ATT_HEADS * ATT_HD
ATT_KV = ATT_GROUPS * ATT_HD
IN_SPLIT = (RET_QK, RET_QK, RET_V, RET_V, ATT_Q, ATT_KV, ATT_KV, ATT_Q, D_MODEL, D_MODEL)
D_IN = 2 * RET_QK + 2 * RET_V + 2 * ATT_Q + 2 * ATT_KV + 2 * D_MODEL

kernel_name = "hybrid_retention_swa_encoder_block"


def _rms_norm(x, w):
    xf = x.astype(jnp.float32)
    y = xf * lax.rsqrt(jnp.mean(xf * xf, axis=-1, keepdims=True) + EPS)
    return (y * w.astype(jnp.float32)).astype(x.dtype)


def _split_in(z):
    parts = []
    start = 0
    for size in IN_SPLIT:
        parts.append(z[..., start:start + size])
        start += size
    return parts


def _rotary(t, pos):
    d = t.shape[-1]
    half = d // 2
    inv = ROPE_THETA ** (-jnp.arange(half, dtype=jnp.float32) * 2.0 / d)
    ang = pos[:, None] * inv[None, :]
    cos = jnp.cos(ang)[None, :, None, :]
    sin = jnp.sin(ang)[None, :, None, :]
    tf = t.astype(jnp.float32)
    t1, t2 = tf[..., :half], tf[..., half:]
    return jnp.concatenate([t1 * cos - t2 * sin, t2 * cos + t1 * sin], axis=-1)


def _pad_front(t):
    return jnp.pad(t, ((0, 0), (META_PAD, 0)) + ((0, 0),) * (t.ndim - 2))


def _retention_direction(q, k, v, log_gamma, strict):
    B, L, H, DK = q.shape
    DV = v.shape[-1]
    nc = L // CHUNK
    q = q.reshape(B, nc, CHUNK, H, DK)
    k = k.reshape(B, nc, CHUNK, H, DK)
    v = v.reshape(B, nc, CHUNK, H, DV)
    idx = jnp.arange(CHUNK, dtype=jnp.float32)
    rel = idx[:, None] - idx[None, :]
    vis = (rel > 0) if strict else (rel >= 0)
    decay = jnp.where(vis[None], jnp.exp(jnp.where(vis, rel, 0.0)[None] * log_gamma[:, None, None]), 0.0)
    s = jnp.einsum("bnihd,bnjhd->bnhij", q, k) * decay[None, None]
    inner = jnp.einsum("bnhij,bnjhv->bnihv", s, v)
    k_decay = jnp.exp((CHUNK - 1 - idx)[:, None] * log_gamma[None, :])
    kv = jnp.einsum("bnjhd,bnjhv->bnhdv", k * k_decay[None, None, :, :, None], v)
    chunk_decay = jnp.exp(CHUNK * log_gamma)[None, :, None, None]

    def step(state, kv_c):
        return chunk_decay * state + kv_c, state

    _, s_prev = lax.scan(step, jnp.zeros((B, H, DK, DV), jnp.float32), jnp.moveaxis(kv, 1, 0))
    s_prev = jnp.moveaxis(s_prev, 0, 1)
    q_decay = jnp.exp((idx + 1.0)[:, None] * log_gamma[None, :])
    cross = jnp.einsum("bnihd,bnhdv->bnihv", q * q_decay[None, None, :, :, None], s_prev)
    return (inner + cross).reshape(B, L, H, DV)


def _bidirectional_retention(q, k, v, log_gamma_f, log_gamma_b):
    fwd = _retention_direction(q, k, v, log_gamma_f, False)
    flip = lambda t: jnp.flip(t, axis=1)
    bwd = flip(_retention_direction(flip(q), flip(k), flip(v), log_gamma_b, True))
    return fwd + bwd


def _windowed_sink_attention(q, k, v, sink):
    B, Lp, H, hd = q.shape
    G = k.shape[2]
    R = H // G
    nb = Lp // CHUNK
    qb = q.reshape(B, nb, CHUNK, G, R, hd)

    def band(t):
        tp = jnp.pad(t, ((0, 0), (CHUNK, CHUNK), (0, 0), (0, 0))).reshape(B, nb + 2, CHUNK, G, hd)
        return jnp.concatenate([tp[:, :-2], tp[:, 1:-1], tp[:, 2:]], axis=2)

    kb, vb = band(k), band(v)
    km, vm = k[:, META_PAD:CHUNK], v[:, META_PAD:CHUNK]
    qpos = jnp.arange(Lp).reshape(nb, CHUNK)
    kpos = (jnp.arange(nb)[:, None] - 1) * CHUNK + jnp.arange(3 * CHUNK)[None, :]
    kp = kpos[:, None, :]
    ok = (jnp.abs(qpos[:, :, None] - kp) <= WINDOW) & (kp >= CHUNK) & (kp < Lp)
    scale = ATT_HD ** -0.5
    s_band = jnp.einsum("bnqgrd,bnkgd->bngrqk", qb, kb) * scale
    s_band = jnp.where(ok[None, :, None, None], s_band, NEG_INF)
    s_meta = jnp.einsum("bnqgrd,bmgd->bngrqm", qb, km) * scale
    s = jnp.concatenate([s_band, s_meta], axis=-1)
    sink_l = sink.astype(jnp.float32).reshape(G, R)[None, None, :, :, None, None]
    m = jnp.maximum(jnp.max(s, axis=-1, keepdims=True), sink_l)
    p = jnp.exp(s - m)
    p = p / (jnp.sum(p, axis=-1, keepdims=True) + jnp.exp(sink_l - m))
    o = (jnp.einsum("bngrqk,bnkgd->bnqgrd", p[..., :3 * CHUNK], vb)
         + jnp.einsum("bngrqm,bmgd->bnqgrd", p[..., 3 * CHUNK:], vm))
    return o.reshape(B, Lp, H, hd)


def _hybrid_layer(h, pre_w, w_in, dec_f, dec_b, ret_nw, w_rb, sink, w_ab, w_o, post_w):
    B, T, _ = h.shape
    u = _rms_norm(h, pre_w)
    z = u @ w_in
    rq, rk, rv, rg, aq, ak, av, ag, gr, ga = _split_in(z)
    pos = jnp.arange(T, dtype=jnp.float32)

    rq = _pad_front(_rotary(rq.reshape(B, T, RET_HEADS, RET_DK), pos))
    rk = _pad_front(_rotary(rk.reshape(B, T, RET_HEADS, RET_DK), pos) * (RET_DK ** -0.5))
    rv = _pad_front(rv.reshape(B, T, RET_HEADS, RET_DV).astype(jnp.float32))
    lg_f = jax.nn.log_sigmoid(dec_f.astype(jnp.float32))
    lg_b = jax.nn.log_sigmoid(dec_b.astype(jnp.float32))
    o_r = _bidirectional_retention(rq, rk, rv, lg_f, lg_b)[:, META_PAD:]
    mu = jnp.mean(o_r, axis=-1, keepdims=True)
    var = jnp.mean(jnp.square(o_r - mu), axis=-1, keepdims=True)
    o_r = ((o_r - mu) * lax.rsqrt(var + EPS)).reshape(B, T, RET_V)
    o_r = o_r * ret_nw.astype(jnp.float32) * jax.nn.silu(rg.astype(jnp.float32))
    y_r = o_r.astype(h.dtype) @ w_rb

    aq = _pad_front(_rotary(aq.reshape(B, T, ATT_HEADS, ATT_HD), pos))
    ak = _pad_front(_rotary(ak.reshape(B, T, ATT_GROUPS, ATT_HD), pos))
    av = _pad_front(av.reshape(B, T, ATT_GROUPS, ATT_HD).astype(jnp.float32))
    o_a = _windowed_sink_attention(aq, ak, av, sink)[:, META_PAD:].reshape(B, T, ATT_Q)
    o_a = o_a * jax.nn.silu(ag.astype(jnp.float32))
    y_a = o_a.astype(h.dtype) @ w_ab

    mix = jax.nn.sigmoid(gr) * y_r + jax.nn.sigmoid(ga) * y_a
    out = mix @ w_o
    return h + _rms_norm(out, post_w)


def setup_inputs(seed: int = 0) -> dict:
    key = jax.random.key(seed)
    ks = jax.random.split(key, 12)
    f32 = jnp.float32
    nrm = jax.random.normal
    base_decay = jnp.log(2.0 ** (5.0 + jnp.arange(RET_HEADS, dtype=f32)) - 1.0)
    return {
        "x": nrm(ks[0], (BATCH, SEQ, D_MODEL), f32),
        "meta_tokens": nrm(ks[1], (N_META, D_MODEL), f32),
        "pre_norm_w": 1.0 + 0.02 * nrm(ks[2], (DEPTH, D_MODEL), f32),
        "w_in": nrm(ks[3], (DEPTH, D_MODEL, D_IN), f32) * D_MODEL ** -0.5,
        "ret_decay_fwd": base_decay[None] + 0.1 * nrm(ks[4], (DEPTH, RET_HEADS), f32),
        "ret_decay_bwd": base_decay[None] + 0.1 * nrm(ks[5], (DEPTH, RET_HEADS), f32),
        "ret_norm_w": 1.0 + 0.02 * nrm(ks[6], (DEPTH, RET_V), f32),
        "w_ret_branch": nrm(ks[7], (DEPTH, RET_V, D_MODEL), f32) * RET_V ** -0.5,
        "attn_sink": 0.5 * nrm(ks[8], (DEPTH, ATT_HEADS), f32),
        "w_attn_branch": nrm(ks[9], (DEPTH, ATT_Q, D_MODEL), f32) * ATT_Q ** -0.5,
        "w_out": nrm(ks[10], (DEPTH, D_MODEL, D_MODEL), f32) * D_MODEL ** -0.5,
        "post_norm_w": 1.0 + 0.02 * nrm(ks[11], (DEPTH, D_MODEL), f32),
    }


def reference(x, meta_tokens, pre_norm_w, w_in, ret_decay_fwd, ret_decay_bwd, ret_norm_w,
              w_ret_branch, attn_sink, w_attn_branch, w_out, post_norm_w):
    B = x.shape[0]
    meta = jnp.broadcast_to(meta_tokens.astype(x.dtype)[None], (B, N_META, x.shape[-1]))
    h = jnp.concatenate([meta, x], axis=1)
    for layer in range(DEPTH):
        h = _hybrid_layer(h, pre_norm_w[layer], w_in[layer], ret_decay_fwd[layer], ret_decay_bwd[layer],
                          ret_norm_w[layer], w_ret_branch[layer], attn_sink[layer], w_attn_branch[layer],
                          w_out[layer], post_norm_w[layer])
    return h[:, N_META:]
```

```python
import functools

import jax
import jax.numpy as jnp
from jax import lax
from jax.experimental import pallas as pl
from jax.experimental.pallas import tpu as pltpu

D_MODEL = 1024
SEQ = 8192
N_META = 16
CHUNK = 128
RET_HEADS = 4
RET_DK = 128
RET_DV = 256
ATT_HEADS = 8
ATT_GROUPS = 2
ATT_REP = ATT_HEADS // ATT_GROUPS
ATT_HD = 128
ROPE_THETA = 10000.0
EPS = 1e-6
NEG_INF = -1e30
RET_QK = RET_HEADS * RET_DK
RET_V = RET_HEADS * RET_DV
ATT_Q = ATT_HEADS * ATT_HD
ATT_KV = ATT_GROUPS * ATT_HD
N_CHUNKS = SEQ // CHUNK

_OFF = {}
_o = 0
for _name, _size in (("rq", RET_QK), ("rk", RET_QK), ("rv", RET_V), ("rg", RET_V), ("aq", ATT_Q),
                     ("ak", ATT_KV), ("av", ATT_KV), ("ag", ATT_Q), ("gr", D_MODEL), ("ga", D_MODEL)):
    _OFF[_name] = (_o, _o + _size)
    _o += _size

KV_COLS = RET_QK + RET_V + 2 * ATT_KV
KV_RK, KV_RV, KV_AK, KV_AV = 0, RET_QK, RET_QK + RET_V, RET_QK + RET_V + ATT_KV
RKV_COLS = RET_QK + RET_V
AKV_COLS = 2 * ATT_KV
MAIN_COLS = RET_QK + ATT_Q + RET_V + ATT_Q + 2 * D_MODEL
M_RQ, M_AQ, M_RG, M_AG, M_GR, M_GA = (0, RET_QK, RET_QK + ATT_Q, RET_QK + ATT_Q + RET_V,
                                       RET_QK + 2 * ATT_Q + RET_V, RET_QK + 2 * ATT_Q + RET_V + D_MODEL)

CHUNKS_PER_TILE = 2
V7X_VMEM_LIMIT_BYTES = 56 * 1024 * 1024

F32 = jnp.float32
BF16 = jnp.bfloat16


def _rms_norm(x, w):
    return x * lax.rsqrt(jnp.mean(x * x, axis=-1, keepdims=True) + EPS) * w


def _log_sigmoid(x):
    return jnp.minimum(x, 0.0) - jnp.log(1.0 + jnp.exp(-jnp.abs(x)))


def _rope_tables(pos0, rows, inv):
    pos = (pos0 + lax.broadcasted_iota(jnp.int32, (rows, ATT_HD), 0)).astype(F32)
    ang = pos * inv
    lane = lax.broadcasted_iota(jnp.int32, (rows, ATT_HD), 1)
    sin = jnp.sin(ang)
    return jnp.cos(ang), jnp.where(lane < ATT_HD // 2, -sin, sin)


def _rope(t, cos, sin_signed):
    return t * cos + pltpu.roll(t, ATT_HD // 2, axis=1) * sin_signed


def _dot(a, b):
    return jnp.dot(a, b, preferred_element_type=F32)


def _dot_nt(a, b):
    return lax.dot_general(a, b, (((1,), (1,)), ((), ())), preferred_element_type=F32)


def _dot_tn(a, b):
    return lax.dot_general(a, b, (((0,), (0,)), ((), ())), preferred_element_type=F32)


def _row_index(shape):
    return lax.broadcasted_iota(jnp.int32, shape, 0).astype(F32)


def _meta_kernel(meta_ref, prew_ref, wkv_ref, inv_ref, decf_ref, kvf0_ref, km_ref, vm_ref):
    u = _rms_norm(meta_ref[...], prew_ref[...]).astype(BF16)
    z = _dot(u, wkv_ref[...])
    cos, sin = _rope_tables(0, N_META, inv_ref[...])
    kdec_rows = (N_META - 1) - _row_index((N_META, RET_DK))
    for h in range(RET_HEADS):
        lg = _log_sigmoid(decf_ref[h:h + 1, :RET_DK])
        k = _rope(z[:, KV_RK + h * RET_DK:KV_RK + (h + 1) * RET_DK], cos, sin) * (RET_DK ** -0.5)
        k = (k * jnp.exp(kdec_rows * lg)).astype(BF16)
        v = z[:, KV_RV + h * RET_DV:KV_RV + (h + 1) * RET_DV].astype(BF16)
        kvf0_ref[h] = _dot_tn(k, v)
    for g in range(ATT_GROUPS):
        km_ref[:, g * ATT_HD:(g + 1) * ATT_HD] = _rope(
            z[:, KV_AK + g * ATT_HD:KV_AK + (g + 1) * ATT_HD], cos, sin).astype(BF16)
    vm_ref[...] = z[:, KV_AV:KV_AV + ATT_KV].astype(BF16)


def _kv_kernel(x_ref, prew_ref, wkv_ref, inv_ref, decb_ref, rkv_ref, akv_ref, sb_ref, state_ref, *, cpt):
    t = pl.program_id(1)
    tile = pl.num_programs(1) - 1 - t
    tm = cpt * CHUNK

    @pl.when(t == 0)
    def _():
        state_ref[...] = jnp.zeros_like(state_ref)

    u = _rms_norm(x_ref[...], prew_ref[...]).astype(BF16)
    z = _dot(u, wkv_ref[...])
    cos, sin = _rope_tables(N_META + tile * tm, tm, inv_ref[...])
    rk = [_rope(z[:, KV_RK + h * RET_DK:KV_RK + (h + 1) * RET_DK], cos, sin) * (RET_DK ** -0.5)
          for h in range(RET_HEADS)]
    for h in range(RET_HEADS):
        rkv_ref[:, h * RET_DK:(h + 1) * RET_DK] = rk[h].astype(BF16)
    rkv_ref[:, RET_QK:] = z[:, KV_RV:KV_RV + RET_V].astype(BF16)
    for g in range(ATT_GROUPS):
        akv_ref[:, g * ATT_HD:(g + 1) * ATT_HD] = _rope(
            z[:, KV_AK + g * ATT_HD:KV_AK + (g + 1) * ATT_HD], cos, sin).astype(BF16)
    akv_ref[:, ATT_KV:] = z[:, KV_AV:KV_AV + ATT_KV].astype(BF16)

    rows_k = _row_index((CHUNK, RET_DK))
    for h in range(RET_HEADS):
        lg = _log_sigmoid(decb_ref[h:h + 1, :])
        kdec = jnp.exp(rows_k * lg[:, :RET_DK])
        cdec = jnp.exp(CHUNK * lg)
        for lc in reversed(range(cpt)):
            r0 = lc * CHUNK
            state = state_ref[h]
            sb_ref[lc, h] = state.astype(BF16)
            k = (rk[h][r0:r0 + CHUNK] * kdec).astype(BF16)
            v = z[r0:r0 + CHUNK, KV_RV + h * RET_DV:KV_RV + (h + 1) * RET_DV].astype(BF16)
            state_ref[h] = cdec * state + _dot_tn(k, v)


def _main_kernel(x_ref, prew_ref, postw_ref, retnw_ref, wmain_ref, wrb_ref, wab_ref, wo_ref, inv_ref,
                 decf_ref, decb_ref, sink_ref, rkv_ref, akp_ref, akc_ref, akn_ref, sb_ref, kvf0_ref,
                 km_ref, vm_ref, out_ref, sf_ref, *, cpt):
    t = pl.program_id(1)
    tm = cpt * CHUNK

    @pl.when(t == 0)
    def _():
        sf_ref[...] = kvf0_ref[...]

    x = x_ref[...]
    u = _rms_norm(x, prew_ref[...]).astype(BF16)
    cos, sin = _rope_tables(N_META + t * tm, tm, inv_ref[...])

    def proj(c0, width):
        return _dot(u, wmain_ref[:, c0:c0 + width])

    zq = proj(M_RQ, RET_QK)
    rq = [_rope(zq[:, h * RET_DK:(h + 1) * RET_DK], cos, sin) for h in range(RET_HEADS)]
    za = proj(M_AQ, ATT_Q)
    aq = [(_rope(za[:, h * ATT_HD:(h + 1) * ATT_HD], cos, sin) * (ATT_HD ** -0.5)).astype(BF16)
          for h in range(ATT_HEADS)]

    rows = _row_index((CHUNK, CHUNK))
    cols = lax.broadcasted_iota(jnp.int32, (CHUNK, CHUNK), 1).astype(F32)
    rel = rows - cols
    dmat, qdf, qdb, kdf, cdf = [], [], [], [], []
    for h in range(RET_HEADS):
        lgf = _log_sigmoid(decf_ref[h:h + 1, :])
        lgb = _log_sigmoid(decb_ref[h:h + 1, :])
        lgf_k, lgb_k = lgf[:, :RET_DK], lgb[:, :RET_DK]
        dmat.append(jnp.where(rel >= 0, jnp.exp(jnp.maximum(rel, 0.0) * lgf_k),
                              jnp.exp(jnp.maximum(-rel, 0.0) * lgb_k)))
        qdf.append(jnp.exp((rows + 1.0) * lgf_k))
        qdb.append(jnp.exp((CHUNK - rows) * lgb_k))
        kdf.append(jnp.exp((CHUNK - 1.0 - rows) * lgf_k))
        cdf.append(jnp.exp(CHUNK * lgf))

    n_band = 3 * CHUNK
    srow = ATT_REP * CHUNK
    qi = lax.broadcasted_iota(jnp.int32, (srow, n_band), 0) & (CHUNK - 1)
    kk = lax.broadcasted_iota(jnp.int32, (srow, n_band), 1)
    in_window = (kk >= qi) & (kk <= qi + 2 * CHUNK)

    ak_all = jnp.concatenate([akp_ref[...], akc_ref[...], akn_ref[...]], axis=0)

    o_r_parts, o_a_parts = [], []
    for lc in range(cpt):
        r0 = lc * CHUNK
        chunk = t * cpt + lc
        heads = []
        for h in range(RET_HEADS):
            q = rq[h][r0:r0 + CHUNK]
            k = rkv_ref[r0:r0 + CHUNK, h * RET_DK:(h + 1) * RET_DK]
            v = rkv_ref[r0:r0 + CHUNK, RET_QK + h * RET_DV:RET_QK + (h + 1) * RET_DV]
            s = _dot_nt(q.astype(BF16), k) * dmat[h]
            sf = sf_ref[h]
            o = (_dot(s.astype(BF16), v)
                 + _dot((q * qdf[h]).astype(BF16), sf.astype(BF16))
                 + _dot((q * qdb[h]).astype(BF16), sb_ref[lc, h]))
            sf_ref[h] = cdf[h] * sf + _dot_tn((k.astype(F32) * kdf[h]).astype(BF16), v)
            mu = jnp.mean(o, axis=-1, keepdims=True)
            d = o - mu
            var = jnp.mean(d * d, axis=-1, keepdims=True)
            heads.append(d * lax.rsqrt(var + EPS))
        o_r_parts.append(jnp.concatenate(heads, axis=1))
        lo = jnp.where(chunk > 0, 0, CHUNK)
        hi = jnp.where(chunk < N_CHUNKS - 1, n_band, 2 * CHUNK)
        ok = in_window & (kk >= lo) & (kk < hi)
        band = ak_all[r0:r0 + n_band]
        outs = [None] * ATT_HEADS
        for g in range(ATT_GROUPS):
            qs = jnp.concatenate([aq[g * ATT_REP + r][r0:r0 + CHUNK] for r in range(ATT_REP)], axis=0)
            kb = band[:, g * ATT_HD:(g + 1) * ATT_HD]
            vb = band[:, ATT_KV + g * ATT_HD:ATT_KV + (g + 1) * ATT_HD]
            kmg = km_ref[:, g * ATT_HD:(g + 1) * ATT_HD]
            vmg = vm_ref[:, g * ATT_HD:(g + 1) * ATT_HD]
            s = jnp.where(ok, _dot_nt(qs, kb), NEG_INF)
            sm = _dot_nt(qs, kmg)
            sk = jnp.concatenate(
                [jnp.broadcast_to(sink_ref[g * ATT_REP + r:g * ATT_REP + r + 1, :], (CHUNK, ATT_HD))
                 for r in range(ATT_REP)], axis=0)[:, :1]
            m = jnp.maximum(jnp.maximum(jnp.max(s, axis=-1, keepdims=True),
                                        jnp.max(sm, axis=-1, keepdims=True)), sk)
            p = jnp.exp(s - m)
            pm = jnp.exp(sm - m)
            denom = (jnp.sum(p, axis=-1, keepdims=True) + jnp.sum(pm, axis=-1, keepdims=True)
                     + jnp.exp(sk - m))
            o = (_dot(p.astype(BF16), vb) + _dot(pm.astype(BF16), vmg)) / denom
            for r in range(ATT_REP):
                outs[g * ATT_REP + r] = o[r * CHUNK:(r + 1) * CHUNK]
        o_a_parts.append(jnp.concatenate(outs, axis=1))

    o_r = jnp.concatenate(o_r_parts, axis=0) if cpt > 1 else o_r_parts[0]
    o_a = jnp.concatenate(o_a_parts, axis=0) if cpt > 1 else o_a_parts[0]
    o_r = o_r * retnw_ref[...] * jax.nn.silu(proj(M_RG, RET_V))
    o_a = o_a * jax.nn.silu(proj(M_AG, ATT_Q))
    y_r = _dot(o_r.astype(BF16), wrb_ref[...])
    y_a = _dot(o_a.astype(BF16), wab_ref[...])
    mix = jax.nn.sigmoid(proj(M_GR, D_MODEL)) * y_r + jax.nn.sigmoid(proj(M_GA, D_MODEL)) * y_a
    out = _dot(mix.astype(BF16), wo_ref[...])
    out_ref[...] = x + _rms_norm(out, postw_ref[...])


def _resident(shape):
    nd = len(shape)
    return pl.BlockSpec(shape, lambda *_: (0,) * nd, pipeline_mode=pl.Buffered(1))


def _cols(w, names):
    return jnp.concatenate([w[:, _OFF[n][0]:_OFF[n][1]] for n in names], axis=1).astype(BF16)


def kernel(x, meta_tokens, pre_norm_w, w_in, ret_decay_fwd, ret_decay_bwd, ret_norm_w, w_ret_branch, attn_sink,
           w_attn_branch, w_out, post_norm_w):
    B = x.shape[0]
    assert x.shape == (B, SEQ, D_MODEL) and pre_norm_w.shape[0] == 1
    cpt = CHUNKS_PER_TILE
    tm = cpt * CHUNK
    nt = N_CHUNKS // cpt

    w_kv = _cols(w_in[0], ("rk", "rv", "ak", "av"))
    w_main = _cols(w_in[0], ("rq", "aq", "rg", "ag", "gr", "ga"))
    w_rb = w_ret_branch[0].astype(BF16)
    w_ab = w_attn_branch[0].astype(BF16)
    w_o = w_out[0].astype(BF16)
    pre_w = pre_norm_w.astype(F32)
    post_w = post_norm_w.astype(F32)
    ret_nw = ret_norm_w.astype(F32)
    half = ATT_HD // 2
    inv = ROPE_THETA ** (-jnp.arange(half, dtype=F32) * 2.0 / ATT_HD)
    inv = jnp.concatenate([inv, inv])[None, :]
    dec_f = jnp.broadcast_to(ret_decay_fwd[0].astype(F32)[:, None], (RET_HEADS, RET_DV))
    dec_b = jnp.broadcast_to(ret_decay_bwd[0].astype(F32)[:, None], (RET_HEADS, RET_DV))
    sink = jnp.broadcast_to(attn_sink[0].astype(F32)[:, None], (ATT_HEADS, ATT_HD))

    params = pltpu.CompilerParams(dimension_semantics=("arbitrary", "arbitrary"),
                                  vmem_limit_bytes=V7X_VMEM_LIMIT_BYTES)

    kvf0, km, vm = pl.pallas_call(
        _meta_kernel,
        out_shape=(jax.ShapeDtypeStruct((RET_HEADS, RET_DK, RET_DV), F32),
                   jax.ShapeDtypeStruct((N_META, ATT_KV), BF16),
                   jax.ShapeDtypeStruct((N_META, ATT_KV), BF16)),
        compiler_params=pltpu.CompilerParams(vmem_limit_bytes=V7X_VMEM_LIMIT_BYTES),
        name="meta",
    )(meta_tokens.astype(F32), pre_w, w_kv, inv, dec_f)

    rkv, akv, sb = pl.pallas_call(
        functools.partial(_kv_kernel, cpt=cpt),
        grid=(B, nt),
        in_specs=[
            pl.BlockSpec((None, tm, D_MODEL), lambda b, t: (b, nt - 1 - t, 0)),
            _resident((1, D_MODEL)),
            _resident((D_MODEL, KV_COLS)),
            _resident((1, ATT_HD)),
            _resident((RET_HEADS, RET_DV)),
        ],
        out_specs=(
            pl.BlockSpec((None, tm, RKV_COLS), lambda b, t: (b, nt - 1 - t, 0)),
            pl.BlockSpec((None, tm, AKV_COLS), lambda b, t: (b, nt - 1 - t, 0)),
            pl.BlockSpec((None, cpt, RET_HEADS, RET_DK, RET_DV), lambda b, t: (b, nt - 1 - t, 0, 0, 0)),
        ),
        out_shape=(jax.ShapeDtypeStruct((B, SEQ, RKV_COLS), BF16),
                   jax.ShapeDtypeStruct((B, SEQ, AKV_COLS), BF16),
                   jax.ShapeDtypeStruct((B, N_CHUNKS, RET_HEADS, RET_DK, RET_DV), BF16)),
        scratch_shapes=[pltpu.VMEM((RET_HEADS, RET_DK, RET_DV), F32)],
        compiler_params=params,
        name="kv",
    )(x, pre_w, w_kv, inv, dec_b)

    out = pl.pallas_call(
        functools.partial(_main_kernel, cpt=cpt),
        grid=(B, nt),
        in_specs=[
            pl.BlockSpec((None, tm, D_MODEL), lambda b, t: (b, t, 0)),
            _resident((1, D_MODEL)),
            _resident((1, D_MODEL)),
            _resident((1, RET_V)),
            _resident((D_MODEL, MAIN_COLS)),
            _resident((RET_V, D_MODEL)),
            _resident((ATT_Q, D_MODEL)),
            _resident((D_MODEL, D_MODEL)),
            _resident((1, ATT_HD)),
            _resident((RET_HEADS, RET_DV)),
            _resident((RET_HEADS, RET_DV)),
            _resident((ATT_HEADS, ATT_HD)),
            pl.BlockSpec((None, tm, RKV_COLS), lambda b, t: (b, t, 0)),
            pl.BlockSpec((None, CHUNK, AKV_COLS), lambda b, t: (b, jnp.maximum(t * cpt - 1, 0), 0)),
            pl.BlockSpec((None, tm, AKV_COLS), lambda b, t: (b, t, 0)),
            pl.BlockSpec((None, CHUNK, AKV_COLS), lambda b, t: (b, jnp.minimum((t + 1) * cpt, N_CHUNKS - 1), 0)),
            pl.BlockSpec((None, cpt, RET_HEADS, RET_DK, RET_DV), lambda b, t: (b, t, 0, 0, 0)),
            _resident((RET_HEADS, RET_DK, RET_DV)),
            _resident((N_META, ATT_KV)),
            _resident((N_META, ATT_KV)),
        ],
        out_specs=pl.BlockSpec((None, tm, D_MODEL), lambda b, t: (b, t, 0)),
        out_shape=jax.ShapeDtypeStruct((B, SEQ, D_MODEL), x.dtype),
        scratch_shapes=[pltpu.VMEM((RET_HEADS, RET_DK, RET_DV), F32)],
        compiler_params=params,
        name="main",
    )(x, pre_w, post_w, ret_nw, w_main, w_rb, w_ab, w_o, inv, dec_f, dec_b, sink, rkv, akv, akv, akv, sb,
      kvf0, km, vm)
    return out
```

```python
import functools

import jax
import jax.numpy as jnp
from jax import lax
from jax.experimental import pallas as pl
from jax.experimental.pallas import tpu as pltpu

D_MODEL = 1024
SEQ = 8192
N_META = 16
CHUNK = 128
RET_HEADS = 4
RET_DK = 128
RET_DV = 256
ATT_HEADS = 8
ATT_GROUPS = 2
ATT_REP = ATT_HEADS // ATT_GROUPS
ATT_HD = 128
ROPE_THETA = 10000.0
EPS = 1e-6
NEG_INF = -1e30
RET_QK = RET_HEADS * RET_DK
RET_V = RET_HEADS * RET_DV
ATT_Q = ATT_HEADS * ATT_HD
ATT_KV = ATT_GROUPS * ATT_HD
N_CHUNKS = SEQ // CHUNK
N_BAND = 3 * CHUNK
N_KEYS = N_BAND + N_META
Q_ROWS = ATT_REP * CHUNK

_OFF = {}
_o = 0
for _name, _size in (("rq", RET_QK), ("rk", RET_QK), ("rv", RET_V), ("rg", RET_V), ("aq", ATT_Q),
                     ("ak", ATT_KV), ("av", ATT_KV), ("ag", ATT_Q), ("gr", D_MODEL), ("ga", D_MODEL)):
    _OFF[_name] = (_o, _o + _size)
    _o += _size

KV_COLS = RET_QK + RET_V + 2 * ATT_KV
KV_RK, KV_RV, KV_AK, KV_AV = 0, RET_QK, RET_QK + RET_V, RET_QK + RET_V + ATT_KV
RKV_COLS = RET_QK + RET_V
MAIN_COLS = RET_QK + ATT_Q + RET_V + ATT_Q + 2 * D_MODEL
M_RQ, M_AQ, M_RG, M_AG, M_GR, M_GA = (0, RET_QK, RET_QK + ATT_Q, RET_QK + ATT_Q + RET_V,
                                       RET_QK + 2 * ATT_Q + RET_V, RET_QK + 2 * ATT_Q + RET_V + D_MODEL)

CHUNKS_PER_TILE = 2
V7X_VMEM_LIMIT_BYTES = 56 * 1024 * 1024

F32 = jnp.float32
BF16 = jnp.bfloat16


def _rms_norm(x, w):
    return x * lax.rsqrt(jnp.mean(x * x, axis=-1, keepdims=True) + EPS) * w


def _log_sigmoid(x):
    return jnp.minimum(x, 0.0) - jnp.log(1.0 + jnp.exp(-jnp.abs(x)))


def _sign_fold(sin):
    lane = lax.broadcasted_iota(jnp.int32, sin.shape, 1)
    return jnp.where(lane < ATT_HD // 2, -sin, sin)


def _rope_tables(rows, inv):
    ang = lax.broadcasted_iota(jnp.int32, (rows, ATT_HD), 0).astype(F32) * inv
    return jnp.cos(ang), _sign_fold(jnp.sin(ang))


def _fill_rope_base(inv, cosb_ref, sinb_ref):
    ang = lax.broadcasted_iota(jnp.int32, cosb_ref.shape, 0).astype(F32) * inv
    cosb_ref[...] = jnp.cos(ang)
    sinb_ref[...] = jnp.sin(ang)


def _rope_tables_from_base(pos0, inv, cosb_ref, sinb_ref):
    base = pos0.astype(F32) * inv
    ca, sa = jnp.cos(base), jnp.sin(base)
    cb, sb = cosb_ref[...], sinb_ref[...]
    return ca * cb - sa * sb, _sign_fold(sa * cb + ca * sb)


def _rope(t, cos, sin_signed):
    return t * cos + pltpu.roll(t, ATT_HD // 2, axis=1) * sin_signed


def _dot(a, b):
    return jnp.dot(a, b, preferred_element_type=F32)


def _dot_nt(a, b):
    return lax.dot_general(a, b, (((1,), (1,)), ((), ())), preferred_element_type=F32)


def _dot_tn(a, b):
    return lax.dot_general(a, b, (((0,), (0,)), ((), ())), preferred_element_type=F32)


def _row_index(shape):
    return lax.broadcasted_iota(jnp.int32, shape, 0).astype(F32)


def _meta_kernel(meta_ref, prew_ref, wkv_ref, inv_ref, decf_ref, kvf0_ref, km_ref, vmt_ref):
    u = _rms_norm(meta_ref[...], prew_ref[...]).astype(BF16)
    z = _dot(u, wkv_ref[...])
    cos, sin = _rope_tables(N_META, inv_ref[...])
    kdec_rows = (N_META - 1) - _row_index((N_META, RET_DK))
    for h in range(RET_HEADS):
        lg = _log_sigmoid(decf_ref[h:h + 1, :RET_DK])
        k = _rope(z[:, KV_RK + h * RET_DK:KV_RK + (h + 1) * RET_DK], cos, sin) * (RET_DK ** -0.5)
        k = (k * jnp.exp(kdec_rows * lg)).astype(BF16)
        v = z[:, KV_RV + h * RET_DV:KV_RV + (h + 1) * RET_DV].astype(BF16)
        kvf0_ref[h] = _dot_tn(k, v)
    for g in range(ATT_GROUPS):
        km_ref[:, g * ATT_HD:(g + 1) * ATT_HD] = _rope(
            z[:, KV_AK + g * ATT_HD:KV_AK + (g + 1) * ATT_HD], cos, sin).astype(BF16)
    vmt_ref[...] = z[:, KV_AV:KV_AV + ATT_KV].T.astype(BF16)


def _kv_kernel(x_ref, prew_ref, wkv_ref, inv_ref, decb_ref, rkv_ref, ak_ref, avt_ref, sb_ref, state_ref,
               cosb_ref, sinb_ref, *, cpt):
    t = pl.program_id(1)
    tile = pl.num_programs(1) - 1 - t
    tm = cpt * CHUNK

    @pl.when((pl.program_id(0) == 0) & (t == 0))
    def _():
        _fill_rope_base(inv_ref[...], cosb_ref, sinb_ref)

    @pl.when(t == 0)
    def _():
        state_ref[...] = jnp.zeros_like(state_ref)

    u = _rms_norm(x_ref[...], prew_ref[...]).astype(BF16)
    z = _dot(u, wkv_ref[...])
    cos, sin = _rope_tables_from_base(N_META + tile * tm, inv_ref[...], cosb_ref, sinb_ref)
    rk = [_rope(z[:, KV_RK + h * RET_DK:KV_RK + (h + 1) * RET_DK], cos, sin) * (RET_DK ** -0.5)
          for h in range(RET_HEADS)]
    for h in range(RET_HEADS):
        rkv_ref[:, h * RET_DK:(h + 1) * RET_DK] = rk[h].astype(BF16)
    rkv_ref[:, RET_QK:] = z[:, KV_RV:KV_RV + RET_V].astype(BF16)
    for g in range(ATT_GROUPS):
        ak_ref[:, g * ATT_HD:(g + 1) * ATT_HD] = _rope(
            z[:, KV_AK + g * ATT_HD:KV_AK + (g + 1) * ATT_HD], cos, sin).astype(BF16)
    avt_ref[...] = z[:, KV_AV:KV_AV + ATT_KV].T.astype(BF16)

    rows_k = _row_index((CHUNK, RET_DK))
    for h in range(RET_HEADS):
        lg = _log_sigmoid(decb_ref[h:h + 1, :])
        kdec = jnp.exp(rows_k * lg[:, :RET_DK])
        cdec = jnp.exp(CHUNK * lg)
        for lc in reversed(range(cpt)):
            r0 = lc * CHUNK
            state = state_ref[h]
            sb_ref[lc, h] = state.astype(BF16)
            k = (rk[h][r0:r0 + CHUNK] * kdec).astype(BF16)
            v = z[r0:r0 + CHUNK, KV_RV + h * RET_DV:KV_RV + (h + 1) * RET_DV].astype(BF16)
            state_ref[h] = cdec * state + _dot_tn(k, v)


def _fill_band_bias(bias_ref):
    kk = lax.broadcasted_iota(jnp.int32, bias_ref.shape, 0)
    qi = lax.broadcasted_iota(jnp.int32, bias_ref.shape, 1) & (CHUNK - 1)
    visible = (kk >= N_BAND) | ((kk >= qi) & (kk <= qi + 2 * CHUNK))
    bias_ref[...] = jnp.where(visible, 0.0, NEG_INF)


def _main_kernel(x_ref, prew_ref, postw_ref, retnw_ref, wmain_ref, wrb_ref, wab_ref, wo_ref, inv_ref,
                 decf_ref, decb_ref, sink_ref, rkv_ref, akp_ref, akc_ref, akn_ref, avp_ref, avc_ref, avn_ref,
                 sb_ref, kvf0_ref, km_ref, vmt_ref, out_ref, sf_ref, cosb_ref, sinb_ref, bias_ref, grg_ref, gag_ref,
                 ggr_ref, gga_ref, *, cpt):
    t = pl.program_id(1)
    tm = cpt * CHUNK

    @pl.when((pl.program_id(0) == 0) & (t == 0))
    def _():
        _fill_rope_base(inv_ref[...], cosb_ref, sinb_ref)
        _fill_band_bias(bias_ref)

    @pl.when(t == 0)
    def _():
        sf_ref[...] = kvf0_ref[...]

    x = x_ref[...]
    u = _rms_norm(x, prew_ref[...]).astype(BF16)
    cos, sin = _rope_tables_from_base(N_META + t * tm, inv_ref[...], cosb_ref, sinb_ref)

    def proj(c0, width):
        return _dot(u, wmain_ref[:, c0:c0 + width])

    gate_jobs = ((M_RG, jax.nn.silu, grg_ref), (M_AG, jax.nn.silu, gag_ref),
                 (M_GR, jax.nn.sigmoid, ggr_ref), (M_GA, jax.nn.sigmoid, gga_ref))
    gate_block = 256
    blocks_per_gate = D_MODEL // gate_block
    n_slots = cpt * ATT_GROUPS

    def emit_gates(slot):
        blocks = [(j, c) for j in range(len(gate_jobs)) for c in range(blocks_per_gate)]
        per_slot = -(-len(blocks) // n_slots)
        for j, c in blocks[slot * per_slot:(slot + 1) * per_slot]:
            c0, act, ref = gate_jobs[j]
            ref[:, c * gate_block:(c + 1) * gate_block] = act(proj(c0 + c * gate_block, gate_block))

    zq = proj(M_RQ, RET_QK)
    rq = [_rope(zq[:, h * RET_DK:(h + 1) * RET_DK], cos, sin) for h in range(RET_HEADS)]
    za = proj(M_AQ, ATT_Q)
    aq = [(_rope(za[:, h * ATT_HD:(h + 1) * ATT_HD], cos, sin) * (ATT_HD ** -0.5)).astype(BF16)
          for h in range(ATT_HEADS)]

    rows = _row_index((CHUNK, CHUNK))
    cols = lax.broadcasted_iota(jnp.int32, (CHUNK, CHUNK), 1).astype(F32)
    rel = rows - cols
    dmat, qdf, qdb, kdf, cdf = [], [], [], [], []
    for h in range(RET_HEADS):
        lgf = _log_sigmoid(decf_ref[h:h + 1, :])
        lgb = _log_sigmoid(decb_ref[h:h + 1, :])
        lgf_k, lgb_k = lgf[:, :RET_DK], lgb[:, :RET_DK]
        dmat.append(jnp.where(rel >= 0, jnp.exp(jnp.maximum(rel, 0.0) * lgf_k),
                              jnp.exp(jnp.maximum(-rel, 0.0) * lgb_k)))
        qdf.append(jnp.exp((rows + 1.0) * lgf_k))
        qdb.append(jnp.exp((CHUNK - rows) * lgb_k))
        kdf.append(jnp.exp((CHUNK - 1.0 - rows) * lgf_k))
        cdf.append(jnp.exp(CHUNK * lgf))

    k_cat = jnp.concatenate([akp_ref[...], akc_ref[...], akn_ref[...]], axis=0)
    vt_cat = jnp.concatenate([avp_ref[...], avc_ref[...], avn_ref[...]], axis=1)
    sink_rows = [jnp.concatenate([sink_ref[g * ATT_REP + r:g * ATT_REP + r + 1, :] for r in range(ATT_REP)],
                                 axis=1) for g in range(ATT_GROUPS)]

    o_r_parts, o_a_parts = [], []
    for lc in range(cpt):
        r0 = lc * CHUNK
        chunk = t * cpt + lc
        heads = []
        for h in range(RET_HEADS):
            q = rq[h][r0:r0 + CHUNK]
            k = rkv_ref[r0:r0 + CHUNK, h * RET_DK:(h + 1) * RET_DK]
            v = rkv_ref[r0:r0 + CHUNK, RET_QK + h * RET_DV:RET_QK + (h + 1) * RET_DV]
            s = _dot_nt(q.astype(BF16), k) * dmat[h]
            sf = sf_ref[h]
            q_cross = jnp.concatenate([q * qdf[h], q * qdb[h]], axis=1).astype(BF16)
            s_cross = jnp.concatenate([sf.astype(BF16), sb_ref[lc, h]], axis=0)
            o = _dot(s.astype(BF16), v) + _dot(q_cross, s_cross)
            sf_ref[h] = cdf[h] * sf + _dot_tn((k.astype(F32) * kdf[h]).astype(BF16), v)
            mu = jnp.mean(o, axis=-1, keepdims=True)
            d = o - mu
            var = jnp.mean(d * d, axis=-1, keepdims=True)
            heads.append(d * lax.rsqrt(var + EPS))
        o_r_parts.append(jnp.concatenate(heads, axis=1))
        outs = [None] * ATT_HEADS
        for g in range(ATT_GROUPS):
            gs = slice(g * ATT_HD, (g + 1) * ATT_HD)
            qs = jnp.concatenate([aq[g * ATT_REP + r][r0:r0 + CHUNK] for r in range(ATT_REP)], axis=0)
            k_all = jnp.concatenate([k_cat[r0:r0 + N_BAND, gs], km_ref[:, gs]], axis=0)
            vt_all = jnp.concatenate([vt_cat[gs, r0:r0 + N_BAND], vmt_ref[gs, :]], axis=1)
            s = _dot_nt(k_all, qs) + bias_ref[...]
            emit_gates(lc * ATT_GROUPS + g)
            parts = [s[0:CHUNK], s[CHUNK:2 * CHUNK], s[2 * CHUNK:N_BAND], s[N_BAND:]]
            if lc == 0:
                parts[0] = jnp.where(chunk > 0, parts[0], NEG_INF)
            if lc == cpt - 1:
                parts[2] = jnp.where(chunk < N_CHUNKS - 1, parts[2], NEG_INF)
            s = jnp.concatenate(parts, axis=0)
            sk = sink_rows[g]
            m = jnp.maximum(jnp.max(s, axis=0, keepdims=True), sk)
            p = jnp.exp(s - m)
            denom = jnp.sum(p, axis=0, keepdims=True) + jnp.exp(sk - m)
            ot = _dot(vt_all, p.astype(BF16)) / denom
            for r in range(ATT_REP):
                outs[g * ATT_REP + r] = ot[:, r * CHUNK:(r + 1) * CHUNK].T
        o_a_parts.append(jnp.concatenate(outs, axis=1))

    o_r = jnp.concatenate(o_r_parts, axis=0) if cpt > 1 else o_r_parts[0]
    o_a = jnp.concatenate(o_a_parts, axis=0) if cpt > 1 else o_a_parts[0]
    o_r = o_r * retnw_ref[...] * grg_ref[...]
    o_a = o_a * gag_ref[...]
    y_r = _dot(o_r.astype(BF16), wrb_ref[...])
    y_a = _dot(o_a.astype(BF16), wab_ref[...])
    mix = ggr_ref[...] * y_r + gga_ref[...] * y_a
    out = _dot(mix.astype(BF16), wo_ref[...])
    out_ref[...] = x + _rms_norm(out, postw_ref[...])


def _resident(shape):
    nd = len(shape)
    return pl.BlockSpec(shape, lambda *_: (0,) * nd, pipeline_mode=pl.Buffered(1))


def _cols(w, names):
    return jnp.concatenate([w[:, _OFF[n][0]:_OFF[n][1]] for n in names], axis=1).astype(BF16)


def kernel(x, meta_tokens, pre_norm_w, w_in, ret_decay_fwd, ret_decay_bwd, ret_norm_w, w_ret_branch, attn_sink,
           w_attn_branch, w_out, post_norm_w):
    B = x.shape[0]
    assert x.shape == (B, SEQ, D_MODEL) and pre_norm_w.shape[0] == 1
    cpt = CHUNKS_PER_TILE
    tm = cpt * CHUNK
    nt = N_CHUNKS // cpt

    w_kv = _cols(w_in[0], ("rk", "rv", "ak", "av"))
    w_main = _cols(w_in[0], ("rq", "aq", "rg", "ag", "gr", "ga"))
    w_rb = w_ret_branch[0].astype(BF16)
    w_ab = w_attn_branch[0].astype(BF16)
    w_o = w_out[0].astype(BF16)
    pre_w = pre_norm_w.astype(F32)
    post_w = post_norm_w.astype(F32)
    ret_nw = ret_norm_w.astype(F32)
    half = ATT_HD // 2
    inv = ROPE_THETA ** (-jnp.arange(half, dtype=F32) * 2.0 / ATT_HD)
    inv = jnp.concatenate([inv, inv])[None, :]
    dec_f = jnp.broadcast_to(ret_decay_fwd[0].astype(F32)[:, None], (RET_HEADS, RET_DV))
    dec_b = jnp.broadcast_to(ret_decay_bwd[0].astype(F32)[:, None], (RET_HEADS, RET_DV))
    sink = jnp.broadcast_to(attn_sink[0].astype(F32)[:, None], (ATT_HEADS, ATT_HD))

    params = pltpu.CompilerParams(dimension_semantics=("arbitrary", "arbitrary"),
                                  vmem_limit_bytes=V7X_VMEM_LIMIT_BYTES)
    rope_scratch = [pltpu.VMEM((tm, ATT_HD), F32), pltpu.VMEM((tm, ATT_HD), F32)]
    state_scratch = pltpu.VMEM((RET_HEADS, RET_DK, RET_DV), F32)

    kvf0, km, vmt = pl.pallas_call(
        _meta_kernel,
        out_shape=(jax.ShapeDtypeStruct((RET_HEADS, RET_DK, RET_DV), F32),
                   jax.ShapeDtypeStruct((N_META, ATT_KV), BF16),
                   jax.ShapeDtypeStruct((ATT_KV, N_META), BF16)),
        compiler_params=pltpu.CompilerParams(vmem_limit_bytes=V7X_VMEM_LIMIT_BYTES),
        name="meta",
    )(meta_tokens.astype(F32), pre_w, w_kv, inv, dec_f)

    rkv, ak, avt, sb = pl.pallas_call(
        functools.partial(_kv_kernel, cpt=cpt),
        grid=(B, nt),
        in_specs=[
            pl.BlockSpec((None, tm, D_MODEL), lambda b, t: (b, nt - 1 - t, 0)),
            _resident((1, D_MODEL)),
            _resident((D_MODEL, KV_COLS)),
            _resident((1, ATT_HD)),
            _resident((RET_HEADS, RET_DV)),
        ],
        out_specs=(
            pl.BlockSpec((None, tm, RKV_COLS), lambda b, t: (b, nt - 1 - t, 0)),
            pl.BlockSpec((None, tm, ATT_KV), lambda b, t: (b, nt - 1 - t, 0)),
            pl.BlockSpec((None, ATT_KV, tm), lambda b, t: (b, 0, nt - 1 - t)),
            pl.BlockSpec((None, cpt, RET_HEADS, RET_DK, RET_DV), lambda b, t: (b, nt - 1 - t, 0, 0, 0)),
        ),
        out_shape=(jax.ShapeDtypeStruct((B, SEQ, RKV_COLS), BF16),
                   jax.ShapeDtypeStruct((B, SEQ, ATT_KV), BF16),
                   jax.ShapeDtypeStruct((B, ATT_KV, SEQ), BF16),
                   jax.ShapeDtypeStruct((B, N_CHUNKS, RET_HEADS, RET_DK, RET_DV), BF16)),
        scratch_shapes=[state_scratch] + rope_scratch,
        compiler_params=params,
        name="kv",
    )(x, pre_w, w_kv, inv, dec_b)

    prev_chunk = lambda t: jnp.maximum(t * cpt - 1, 0)
    next_chunk = lambda t: jnp.minimum((t + 1) * cpt, N_CHUNKS - 1)
    out = pl.pallas_call(
        functools.partial(_main_kernel, cpt=cpt),
        grid=(B, nt),
        in_specs=[
            pl.BlockSpec((None, tm, D_MODEL), lambda b, t: (b, t, 0)),
            _resident((1, D_MODEL)),
            _resident((1, D_MODEL)),
            _resident((1, RET_V)),
            _resident((D_MODEL, MAIN_COLS)),
            _resident((RET_V, D_MODEL)),
            _resident((ATT_Q, D_MODEL)),
            _resident((D_MODEL, D_MODEL)),
            _resident((1, ATT_HD)),
            _resident((RET_HEADS, RET_DV)),
            _resident((RET_HEADS, RET_DV)),
            _resident((ATT_HEADS, ATT_HD)),
            pl.BlockSpec((None, tm, RKV_COLS), lambda b, t: (b, t, 0)),
            pl.BlockSpec((None, CHUNK, ATT_KV), lambda b, t: (b, prev_chunk(t), 0)),
            pl.BlockSpec((None, tm, ATT_KV), lambda b, t: (b, t, 0)),
            pl.BlockSpec((None, CHUNK, ATT_KV), lambda b, t: (b, next_chunk(t), 0)),
            pl.BlockSpec((None, ATT_KV, CHUNK), lambda b, t: (b, 0, prev_chunk(t))),
            pl.BlockSpec((None, ATT_KV, tm), lambda b, t: (b, 0, t)),
            pl.BlockSpec((None, ATT_KV, CHUNK), lambda b, t: (b, 0, next_chunk(t))),
            pl.BlockSpec((None, cpt, RET_HEADS, RET_DK, RET_DV), lambda b, t: (b, t, 0, 0, 0)),
            _resident((RET_HEADS, RET_DK, RET_DV)),
            _resident((N_META, ATT_KV)),
            _resident((ATT_KV, N_META)),
        ],
        out_specs=pl.BlockSpec((None, tm, D_MODEL), lambda b, t: (b, t, 0)),
        out_shape=jax.ShapeDtypeStruct((B, SEQ, D_MODEL), x.dtype),
        scratch_shapes=([state_scratch] + rope_scratch + [pltpu.VMEM((N_KEYS, Q_ROWS), F32)]
                        + [pltpu.VMEM((tm, D_MODEL), F32)] * 4),
        compiler_params=params,
        name="main",
    )(x, pre_w, post_w, ret_nw, w_main, w_rb, w_ab, w_o, inv, dec_f, dec_b, sink, rkv, ak, ak, ak, avt, avt, avt,
      sb, kvf0, km, vmt)
    return out
```

```python
import functools

import jax
import jax.numpy as jnp
from jax import lax
from jax.experimental import pallas as pl
from jax.experimental.pallas import tpu as pltpu

D_MODEL = 1024
SEQ = 8192
N_META = 16
CHUNK = 128
RET_HEADS = 4
RET_DK = 128
RET_DV = 256
ATT_HEADS = 8
ATT_GROUPS = 2
ATT_REP = ATT_HEADS // ATT_GROUPS
ATT_HD = 128
ROPE_THETA = 10000.0
EPS = 1e-6
NEG_INF = -1e30
RET_QK = RET_HEADS * RET_DK
RET_V = RET_HEADS * RET_DV
ATT_Q = ATT_HEADS * ATT_HD
ATT_KV = ATT_GROUPS * ATT_HD
N_CHUNKS = SEQ // CHUNK
N_BAND = 3 * CHUNK
N_KEYS = N_BAND + N_META
Q_ROWS = ATT_REP * CHUNK

_OFF = {}
_o = 0
for _name, _size in (("rq", RET_QK), ("rk", RET_QK), ("rv", RET_V), ("rg", RET_V), ("aq", ATT_Q),
                     ("ak", ATT_KV), ("av", ATT_KV), ("ag", ATT_Q), ("gr", D_MODEL), ("ga", D_MODEL)):
    _OFF[_name] = (_o, _o + _size)
    _o += _size

KV_COLS = RET_QK + RET_V + 2 * ATT_KV
KV_RK, KV_RV, KV_AK, KV_AV = 0, RET_QK, RET_QK + RET_V, RET_QK + RET_V + ATT_KV
RKV_COLS = RET_QK + RET_V
MAIN_COLS = RET_QK + ATT_Q + RET_V + ATT_Q + 2 * D_MODEL
M_RQ, M_AQ, M_RG, M_AG, M_GR, M_GA = (0, RET_QK, RET_QK + ATT_Q, RET_QK + ATT_Q + RET_V,
                                       RET_QK + 2 * ATT_Q + RET_V, RET_QK + 2 * ATT_Q + RET_V + D_MODEL)

CHUNKS_PER_TILE = 4
V7X_VMEM_LIMIT_BYTES = 56 * 1024 * 1024

F32 = jnp.float32
BF16 = jnp.bfloat16


def _rms_norm(x, w):
    return x * lax.rsqrt(jnp.mean(x * x, axis=-1, keepdims=True) + EPS) * w


def _log_sigmoid(x):
    return jnp.minimum(x, 0.0) - jnp.log(1.0 + jnp.exp(-jnp.abs(x)))


def _sign_fold(sin):
    lane = lax.broadcasted_iota(jnp.int32, sin.shape, 1)
    return jnp.where(lane < ATT_HD // 2, -sin, sin)


def _rope_tables(rows, inv):
    ang = lax.broadcasted_iota(jnp.int32, (rows, ATT_HD), 0).astype(F32) * inv
    return jnp.cos(ang), _sign_fold(jnp.sin(ang))


def _fill_rope_base(inv, cosb_ref, sinb_ref):
    ang = lax.broadcasted_iota(jnp.int32, cosb_ref.shape, 0).astype(F32) * inv
    cosb_ref[...] = jnp.cos(ang)
    sinb_ref[...] = jnp.sin(ang)


def _rope_tables_from_base(pos0, inv, cosb_ref, sinb_ref):
    base = pos0.astype(F32) * inv
    ca, sa = jnp.cos(base), jnp.sin(base)
    cb, sb = cosb_ref[...], sinb_ref[...]
    return ca * cb - sa * sb, _sign_fold(sa * cb + ca * sb)


def _rope(t, cos, sin_signed):
    return t * cos + pltpu.roll(t, ATT_HD // 2, axis=1) * sin_signed


def _dot(a, b):
    return jnp.dot(a, b, preferred_element_type=F32)


def _dot_nt(a, b):
    return lax.dot_general(a, b, (((1,), (1,)), ((), ())), preferred_element_type=F32)


def _dot_tn(a, b):
    return lax.dot_general(a, b, (((0,), (0,)), ((), ())), preferred_element_type=F32)


def _row_index(shape):
    return lax.broadcasted_iota(jnp.int32, shape, 0).astype(F32)


def _meta_kernel(meta_ref, prew_ref, wkv_ref, inv_ref, decf_ref, kvf0_ref, km_ref, vmt_ref):
    u = _rms_norm(meta_ref[...], prew_ref[...]).astype(BF16)
    z = _dot(u, wkv_ref[...])
    cos, sin = _rope_tables(N_META, inv_ref[...])
    kdec_rows = (N_META - 1) - _row_index((N_META, RET_DK))
    for h in range(RET_HEADS):
        lg = _log_sigmoid(decf_ref[h:h + 1, :RET_DK])
        k = _rope(z[:, KV_RK + h * RET_DK:KV_RK + (h + 1) * RET_DK], cos, sin) * (RET_DK ** -0.5)
        k = (k * jnp.exp(kdec_rows * lg)).astype(BF16)
        v = z[:, KV_RV + h * RET_DV:KV_RV + (h + 1) * RET_DV].astype(BF16)
        kvf0_ref[h] = _dot_tn(k, v)
    for g in range(ATT_GROUPS):
        km_ref[:, g * ATT_HD:(g + 1) * ATT_HD] = _rope(
            z[:, KV_AK + g * ATT_HD:KV_AK + (g + 1) * ATT_HD], cos, sin).astype(BF16)
    vmt_ref[...] = z[:, KV_AV:KV_AV + ATT_KV].T.astype(BF16)


def _kv_kernel(x_ref, prew_ref, wkv_ref, inv_ref, decb_ref, rkv_ref, ak_ref, avt_ref, sb_ref, state_ref,
               cosb_ref, sinb_ref, *, cpt):
    t = pl.program_id(1)
    tile = pl.num_programs(1) - 1 - t
    tm = cpt * CHUNK

    @pl.when((pl.program_id(0) == 0) & (t == 0))
    def _():
        _fill_rope_base(inv_ref[...], cosb_ref, sinb_ref)

    @pl.when(t == 0)
    def _():
        state_ref[...] = jnp.zeros_like(state_ref)

    u = _rms_norm(x_ref[...], prew_ref[...]).astype(BF16)
    z = _dot(u, wkv_ref[...])
    cos, sin = _rope_tables_from_base(N_META + tile * tm, inv_ref[...], cosb_ref, sinb_ref)
    rk = [_rope(z[:, KV_RK + h * RET_DK:KV_RK + (h + 1) * RET_DK], cos, sin) * (RET_DK ** -0.5)
          for h in range(RET_HEADS)]
    for h in range(RET_HEADS):
        rkv_ref[:, h * RET_DK:(h + 1) * RET_DK] = rk[h].astype(BF16)
    rkv_ref[:, RET_QK:] = z[:, KV_RV:KV_RV + RET_V].astype(BF16)
    for g in range(ATT_GROUPS):
        ak_ref[:, g * ATT_HD:(g + 1) * ATT_HD] = _rope(
            z[:, KV_AK + g * ATT_HD:KV_AK + (g + 1) * ATT_HD], cos, sin).astype(BF16)
    avt_ref[...] = z[:, KV_AV:KV_AV + ATT_KV].T.astype(BF16)

    rows_k = _row_index((CHUNK, RET_DK))
    for h in range(RET_HEADS):
        lg = _log_sigmoid(decb_ref[h:h + 1, :])
        kdec = jnp.exp(rows_k * lg[:, :RET_DK])
        cdec = jnp.exp(CHUNK * lg)
        for lc in reversed(range(cpt)):
            r0 = lc * CHUNK
            state = state_ref[h]
            sb_ref[lc, h] = state.astype(BF16)
            k = (rk[h][r0:r0 + CHUNK] * kdec).astype(BF16)
            v = z[r0:r0 + CHUNK, KV_RV + h * RET_DV:KV_RV + (h + 1) * RET_DV].astype(BF16)
            state_ref[h] = cdec * state + _dot_tn(k, v)


def _fill_band_bias(bias_ref):
    kk = lax.broadcasted_iota(jnp.int32, bias_ref.shape, 0)
    qi = lax.broadcasted_iota(jnp.int32, bias_ref.shape, 1) & (CHUNK - 1)
    visible = (kk >= N_BAND) | ((kk >= qi) & (kk <= qi + 2 * CHUNK))
    bias_ref[...] = jnp.where(visible, 0.0, NEG_INF)


def _main_kernel(x_ref, prew_ref, postw_ref, retnw_ref, wmain_ref, wrb_ref, wab_ref, wo_ref, inv_ref,
                 decf_ref, decb_ref, sink_ref, rkv_ref, akp_ref, akc_ref, akn_ref, avp_ref, avc_ref, avn_ref,
                 sb_ref, kvf0_ref, km_ref, vmt_ref, out_ref, sf_ref, cosb_ref, sinb_ref, bias_ref, grg_ref, gag_ref,
                 ggr_ref, gga_ref, *, cpt):
    t = pl.program_id(1)
    tm = cpt * CHUNK

    @pl.when((pl.program_id(0) == 0) & (t == 0))
    def _():
        _fill_rope_base(inv_ref[...], cosb_ref, sinb_ref)
        _fill_band_bias(bias_ref)

    @pl.when(t == 0)
    def _():
        sf_ref[...] = kvf0_ref[...]

    x = x_ref[...]
    u = _rms_norm(x, prew_ref[...]).astype(BF16)
    cos, sin = _rope_tables_from_base(N_META + t * tm, inv_ref[...], cosb_ref, sinb_ref)

    def proj(c0, width):
        return _dot(u, wmain_ref[:, c0:c0 + width])

    gate_jobs = ((M_RG, jax.nn.silu, grg_ref), (M_AG, jax.nn.silu, gag_ref),
                 (M_GR, jax.nn.sigmoid, ggr_ref), (M_GA, jax.nn.sigmoid, gga_ref))
    gate_block = 256
    blocks_per_gate = D_MODEL // gate_block
    n_slots = cpt * ATT_GROUPS

    def emit_gates(slot):
        blocks = [(j, c) for j in range(len(gate_jobs)) for c in range(blocks_per_gate)]
        per_slot = -(-len(blocks) // n_slots)
        for j, c in blocks[slot * per_slot:(slot + 1) * per_slot]:
            c0, act, ref = gate_jobs[j]
            ref[:, c * gate_block:(c + 1) * gate_block] = act(proj(c0 + c * gate_block, gate_block))

    zq = proj(M_RQ, RET_QK)
    rq = [_rope(zq[:, h * RET_DK:(h + 1) * RET_DK], cos, sin) for h in range(RET_HEADS)]
    za = proj(M_AQ, ATT_Q)
    aq = [(_rope(za[:, h * ATT_HD:(h + 1) * ATT_HD], cos, sin) * (ATT_HD ** -0.5)).astype(BF16)
          for h in range(ATT_HEADS)]

    rows = _row_index((CHUNK, CHUNK))
    cols = lax.broadcasted_iota(jnp.int32, (CHUNK, CHUNK), 1).astype(F32)
    rel = rows - cols
    dmat, qdf, qdb, kdf, cdf = [], [], [], [], []
    for h in range(RET_HEADS):
        lgf = _log_sigmoid(decf_ref[h:h + 1, :])
        lgb = _log_sigmoid(decb_ref[h:h + 1, :])
        lgf_k, lgb_k = lgf[:, :RET_DK], lgb[:, :RET_DK]
        dmat.append(jnp.where(rel >= 0, jnp.exp(jnp.maximum(rel, 0.0) * lgf_k),
                              jnp.exp(jnp.maximum(-rel, 0.0) * lgb_k)))
        qdf.append(jnp.exp((rows + 1.0) * lgf_k))
        qdb.append(jnp.exp((CHUNK - rows) * lgb_k))
        kdf.append(jnp.exp((CHUNK - 1.0 - rows) * lgf_k))
        cdf.append(jnp.exp(CHUNK * lgf))

    k_cat = jnp.concatenate([akp_ref[...], akc_ref[...], akn_ref[...]], axis=0)
    vt_cat = jnp.concatenate([avp_ref[...], avc_ref[...], avn_ref[...]], axis=1)
    sink_rows = [jnp.concatenate([sink_ref[g * ATT_REP + r:g * ATT_REP + r + 1, :] for r in range(ATT_REP)],
                                 axis=1) for g in range(ATT_GROUPS)]

    o_r_parts, o_a_parts = [], []
    for lc in range(cpt):
        r0 = lc * CHUNK
        chunk = t * cpt + lc
        heads = []
        for h in range(RET_HEADS):
            q = rq[h][r0:r0 + CHUNK]
            k = rkv_ref[r0:r0 + CHUNK, h * RET_DK:(h + 1) * RET_DK]
            v = rkv_ref[r0:r0 + CHUNK, RET_QK + h * RET_DV:RET_QK + (h + 1) * RET_DV]
            s = _dot_nt(q.astype(BF16), k) * dmat[h]
            sf = sf_ref[h]
            q_cross = jnp.concatenate([q * qdf[h], q * qdb[h]], axis=1).astype(BF16)
            s_cross = jnp.concatenate([sf.astype(BF16), sb_ref[lc, h]], axis=0)
            o = _dot(s.astype(BF16), v) + _dot(q_cross, s_cross)
            sf_ref[h] = cdf[h] * sf + _dot_tn((k.astype(F32) * kdf[h]).astype(BF16), v)
            mu = jnp.mean(o, axis=-1, keepdims=True)
            d = o - mu
            var = jnp.mean(d * d, axis=-1, keepdims=True)
            heads.append(d * lax.rsqrt(var + EPS))
        o_r_parts.append(jnp.concatenate(heads, axis=1))
        outs = [None] * ATT_HEADS
        for g in range(ATT_GROUPS):
            gs = slice(g * ATT_HD, (g + 1) * ATT_HD)
            qs = jnp.concatenate([aq[g * ATT_REP + r][r0:r0 + CHUNK] for r in range(ATT_REP)], axis=0)
            k_all = jnp.concatenate([k_cat[r0:r0 + N_BAND, gs], km_ref[:, gs]], axis=0)
            vt_all = jnp.concatenate([vt_cat[gs, r0:r0 + N_BAND], vmt_ref[gs, :]], axis=1)
            s = _dot_nt(k_all, qs) + bias_ref[...]
            emit_gates(lc * ATT_GROUPS + g)
            parts = [s[0:CHUNK], s[CHUNK:2 * CHUNK], s[2 * CHUNK:N_BAND], s[N_BAND:]]
            if lc == 0:
                parts[0] = jnp.where(chunk > 0, parts[0], NEG_INF)
            if lc == cpt - 1:
                parts[2] = jnp.where(chunk < N_CHUNKS - 1, parts[2], NEG_INF)
            s = jnp.concatenate(parts, axis=0)
            sk = sink_rows[g]
            m = jnp.maximum(jnp.max(s, axis=0, keepdims=True), sk)
            p = jnp.exp(s - m)
            denom = jnp.sum(p, axis=0, keepdims=True) + jnp.exp(sk - m)
            ot = _dot(vt_all, p.astype(BF16)) / denom
            for r in range(ATT_REP):
                outs[g * ATT_REP + r] = ot[:, r * CHUNK:(r + 1) * CHUNK].T
        o_a_parts.append(jnp.concatenate(outs, axis=1))

    o_r = jnp.concatenate(o_r_parts, axis=0) if cpt > 1 else o_r_parts[0]
    o_a = jnp.concatenate(o_a_parts, axis=0) if cpt > 1 else o_a_parts[0]
    o_r = o_r * retnw_ref[...] * grg_ref[...]
    o_a = o_a * gag_ref[...]
    y_r = _dot(o_r.astype(BF16), wrb_ref[...])
    y_a = _dot(o_a.astype(BF16), wab_ref[...])
    mix = ggr_ref[...] * y_r + gga_ref[...] * y_a
    out = _dot(mix.astype(BF16), wo_ref[...])
    out_ref[...] = x + _rms_norm(out, postw_ref[...])


def _resident(shape):
    nd = len(shape)
    return pl.BlockSpec(shape, lambda *_: (0,) * nd, pipeline_mode=pl.Buffered(1))


def _cols(w, names):
    return jnp.concatenate([w[:, _OFF[n][0]:_OFF[n][1]] for n in names], axis=1).astype(BF16)


def kernel(x, meta_tokens, pre_norm_w, w_in, ret_decay_fwd, ret_decay_bwd, ret_norm_w, w_ret_branch, attn_sink,
           w_attn_branch, w_out, post_norm_w):
    B = x.shape[0]
    assert x.shape == (B, SEQ, D_MODEL) and pre_norm_w.shape[0] == 1
    cpt = CHUNKS_PER_TILE
    tm = cpt * CHUNK
    nt = N_CHUNKS // cpt

    w_kv = _cols(w_in[0], ("rk", "rv", "ak", "av"))
    w_main = _cols(w_in[0], ("rq", "aq", "rg", "ag", "gr", "ga"))
    w_rb = w_ret_branch[0].astype(BF16)
    w_ab = w_attn_branch[0].astype(BF16)
    w_o = w_out[0].astype(BF16)
    pre_w = pre_norm_w.astype(F32)
    post_w = post_norm_w.astype(F32)
    ret_nw = ret_norm_w.astype(F32)
    half = ATT_HD // 2
    inv = ROPE_THETA ** (-jnp.arange(half, dtype=F32) * 2.0 / ATT_HD)
    inv = jnp.concatenate([inv, inv])[None, :]
    dec_f = jnp.broadcast_to(ret_decay_fwd[0].astype(F32)[:, None], (RET_HEADS, RET_DV))
    dec_b = jnp.broadcast_to(ret_decay_bwd[0].astype(F32)[:, None], (RET_HEADS, RET_DV))
    sink = jnp.broadcast_to(attn_sink[0].astype(F32)[:, None], (ATT_HEADS, ATT_HD))

    params = pltpu.CompilerParams(dimension_semantics=("arbitrary", "arbitrary"),
                                  vmem_limit_bytes=V7X_VMEM_LIMIT_BYTES)
    rope_scratch = [pltpu.VMEM((tm, ATT_HD), F32), pltpu.VMEM((tm, ATT_HD), F32)]
    state_scratch = pltpu.VMEM((RET_HEADS, RET_DK, RET_DV), F32)

    kvf0, km, vmt = pl.pallas_call(
        _meta_kernel,
        out_shape=(jax.ShapeDtypeStruct((RET_HEADS, RET_DK, RET_DV), F32),
                   jax.ShapeDtypeStruct((N_META, ATT_KV), BF16),
                   jax.ShapeDtypeStruct((ATT_KV, N_META), BF16)),
        compiler_params=pltpu.CompilerParams(vmem_limit_bytes=V7X_VMEM_LIMIT_BYTES),
        name="meta",
    )(meta_tokens.astype(F32), pre_w, w_kv, inv, dec_f)

    rkv, ak, avt, sb = pl.pallas_call(
        functools.partial(_kv_kernel, cpt=cpt),
        grid=(B, nt),
        in_specs=[
            pl.BlockSpec((None, tm, D_MODEL), lambda b, t: (b, nt - 1 - t, 0)),
            _resident((1, D_MODEL)),
            _resident((D_MODEL, KV_COLS)),
            _resident((1, ATT_HD)),
            _resident((RET_HEADS, RET_DV)),
        ],
        out_specs=(
            pl.BlockSpec((None, tm, RKV_COLS), lambda b, t: (b, nt - 1 - t, 0)),
            pl.BlockSpec((None, tm, ATT_KV), lambda b, t: (b, nt - 1 - t, 0)),
            pl.BlockSpec((None, ATT_KV, tm), lambda b, t: (b, 0, nt - 1 - t)),
            pl.BlockSpec((None, cpt, RET_HEADS, RET_DK, RET_DV), lambda b, t: (b, nt - 1 - t, 0, 0, 0)),
        ),
        out_shape=(jax.ShapeDtypeStruct((B, SEQ, RKV_COLS), BF16),
                   jax.ShapeDtypeStruct((B, SEQ, ATT_KV), BF16),
                   jax.ShapeDtypeStruct((B, ATT_KV, SEQ), BF16),
                   jax.ShapeDtypeStruct((B, N_CHUNKS, RET_HEADS, RET_DK, RET_DV), BF16)),
        scratch_shapes=[state_scratch] + rope_scratch,
        compiler_params=params,
        name="kv",
    )(x, pre_w, w_kv, inv, dec_b)

    prev_chunk = lambda t: jnp.maximum(t * cpt - 1, 0)
    next_chunk = lambda t: jnp.minimum((t + 1) * cpt, N_CHUNKS - 1)
    out = pl.pallas_call(
        functools.partial(_main_kernel, cpt=cpt),
        grid=(B, nt),
        in_specs=[
            pl.BlockSpec((None, tm, D_MODEL), lambda b, t: (b, t, 0)),
            _resident((1, D_MODEL)),
            _resident((1, D_MODEL)),
            _resident((1, RET_V)),
            _resident((D_MODEL, MAIN_COLS)),
            _resident((RET_V, D_MODEL)),
            _resident((ATT_Q, D_MODEL)),
            _resident((D_MODEL, D_MODEL)),
            _resident((1, ATT_HD)),
            _resident((RET_HEADS, RET_DV)),
            _resident((RET_HEADS, RET_DV)),
            _resident((ATT_HEADS, ATT_HD)),
            pl.BlockSpec((None, tm, RKV_COLS), lambda b, t: (b, t, 0)),
            pl.BlockSpec((None, CHUNK, ATT_KV), lambda b, t: (b, prev_chunk(t), 0)),
            pl.BlockSpec((None, tm, ATT_KV), lambda b, t: (b, t, 0)),
            pl.BlockSpec((None, CHUNK, ATT_KV), lambda b, t: (b, next_chunk(t), 0)),
            pl.BlockSpec((None, ATT_KV, CHUNK), lambda b, t: (b, 0, prev_chunk(t))),
            pl.BlockSpec((None, ATT_KV, tm), lambda b, t: (b, 0, t)),
            pl.BlockSpec((None, ATT_KV, CHUNK), lambda b, t: (b, 0, next_chunk(t))),
            pl.BlockSpec((None, cpt, RET_HEADS, RET_DK, RET_DV), lambda b, t: (b, t, 0, 0, 0)),
            _resident((RET_HEADS, RET_DK, RET_DV)),
            _resident((N_META, ATT_KV)),
            _resident((ATT_KV, N_META)),
        ],
        out_specs=pl.BlockSpec((None, tm, D_MODEL), lambda b, t: (b, t, 0)),
        out_shape=jax.ShapeDtypeStruct((B, SEQ, D_MODEL), x.dtype),
        scratch_shapes=([state_scratch] + rope_scratch + [pltpu.VMEM((N_KEYS, Q_ROWS), F32)]
                        + [pltpu.VMEM((tm, D_MODEL), F32)] * 4),
        compiler_params=params,
        name="main",
    )(x, pre_w, post_w, ret_nw, w_main, w_rb, w_ab, w_o, inv, dec_f, dec_b, sink, rkv, ak, ak, ak, avt, avt, avt,
      sb, kvf0, km, vmt)
    return out
```

```python
import functools

import jax
import jax.numpy as jnp
from jax import lax
from jax.experimental import pallas as pl
from jax.experimental.pallas import tpu as pltpu

D_MODEL = 1024
SEQ = 8192
N_META = 16
CHUNK = 128
RET_HEADS = 4
RET_DK = 128
RET_DV = 256
ATT_HEADS = 8
ATT_GROUPS = 2
ATT_REP = ATT_HEADS // ATT_GROUPS
ATT_HD = 128
ROPE_THETA = 10000.0
EPS = 1e-6
NEG_INF = -1e30
RET_QK = RET_HEADS * RET_DK
RET_V = RET_HEADS * RET_DV
ATT_Q = ATT_HEADS * ATT_HD
ATT_KV = ATT_GROUPS * ATT_HD
D_IN = 2 * RET_QK + 2 * RET_V + 2 * ATT_Q + 2 * ATT_KV + 2 * D_MODEL
N_CHUNKS = SEQ // CHUNK
N_BAND = 3 * CHUNK
N_KEYS = N_BAND + N_META
Q_ROWS = ATT_REP * CHUNK

OFF_RQ = 0
OFF_RK = OFF_RQ + RET_QK
OFF_RV = OFF_RK + RET_QK
OFF_RG = OFF_RV + RET_V
OFF_AQ = OFF_RG + RET_V
OFF_AK = OFF_AQ + ATT_Q
OFF_AV = OFF_AK + ATT_KV
OFF_AG = OFF_AV + ATT_KV
OFF_GR = OFF_AG + ATT_Q
OFF_GA = OFF_GR + D_MODEL
RKV_COLS = RET_QK + RET_V

KV_WINDOWS = ((OFF_RK, RET_QK), (OFF_RV, RET_V), (OFF_AK, 2 * ATT_KV))
TAIL_W = (D_IN - OFF_AG) // 2
MAIN_WINDOWS = ((OFF_RQ, RET_QK), (OFF_RG, RET_V), (OFF_AQ, ATT_Q), (OFF_AG, TAIL_W), (OFF_AG + TAIL_W, TAIL_W))

CHUNKS_PER_TILE = 4
COL_BLOCK = 256
V7X_VMEM_LIMIT_BYTES = 56 * 1024 * 1024

F32 = jnp.float32
BF16 = jnp.bfloat16


def _rms_norm(x, w):
    return x * lax.rsqrt(jnp.mean(x * x, axis=-1, keepdims=True) + EPS) * w


def _log_sigmoid(x):
    return jnp.minimum(x, 0.0) - jnp.log(1.0 + jnp.exp(-jnp.abs(x)))


def _sign_fold(sin):
    lane = lax.broadcasted_iota(jnp.int32, sin.shape, 1)
    return jnp.where(lane < ATT_HD // 2, -sin, sin)


def _rope_tables(rows, inv):
    ang = lax.broadcasted_iota(jnp.int32, (rows, ATT_HD), 0).astype(F32) * inv
    return jnp.cos(ang), _sign_fold(jnp.sin(ang))


def _fill_rope_base(inv, cosb_ref, sinb_ref):
    ang = lax.broadcasted_iota(jnp.int32, cosb_ref.shape, 0).astype(F32) * inv
    cosb_ref[...] = jnp.cos(ang)
    sinb_ref[...] = jnp.sin(ang)


def _rope_tables_from_base(pos0, inv, cosb_ref, sinb_ref):
    base = pos0.astype(F32) * inv
    ca, sa = jnp.cos(base), jnp.sin(base)
    cb, sb = cosb_ref[...], sinb_ref[...]
    return ca * cb - sa * sb, _sign_fold(sa * cb + ca * sb)


def _rope(t, cos, sin_signed):
    return t * cos + pltpu.roll(t, ATT_HD // 2, axis=1) * sin_signed


def _dot(a, b):
    return jnp.dot(a, b, preferred_element_type=F32)


def _dot_nt(a, b):
    return lax.dot_general(a, b, (((1,), (1,)), ((), ())), preferred_element_type=F32)


def _dot_tn(a, b):
    return lax.dot_general(a, b, (((0,), (0,)), ((), ())), preferred_element_type=F32)


def _row_index(shape):
    return lax.broadcasted_iota(jnp.int32, shape, 0).astype(F32)


def _window_cols(windows, refs, col, width):
    for (off, w), ref in zip(windows, refs):
        if off <= col and col + width <= off + w:
            return ref[:, col - off:col - off + width]
    raise ValueError("columns not resident")


def _meta_kernel(meta_ref, prew_ref, wrk_ref, wrv_ref, wakv_ref, inv_ref, decf_ref, kvf0_ref, km_ref, vmt_ref):
    u = _rms_norm(meta_ref[...], prew_ref[...]).astype(BF16)
    zk, zv, za = _dot(u, wrk_ref[...]), _dot(u, wrv_ref[...]), _dot(u, wakv_ref[...])
    cos, sin = _rope_tables(N_META, inv_ref[...])
    kdec_rows = (N_META - 1) - _row_index((N_META, RET_DK))
    for h in range(RET_HEADS):
        lg = _log_sigmoid(decf_ref[h:h + 1, :RET_DK])
        k = _rope(zk[:, h * RET_DK:(h + 1) * RET_DK], cos, sin) * (RET_DK ** -0.5)
        k = (k * jnp.exp(kdec_rows * lg)).astype(BF16)
        v = zv[:, h * RET_DV:(h + 1) * RET_DV].astype(BF16)
        kvf0_ref[h] = _dot_tn(k, v)
    for g in range(ATT_GROUPS):
        km_ref[:, g * ATT_HD:(g + 1) * ATT_HD] = _rope(za[:, g * ATT_HD:(g + 1) * ATT_HD], cos, sin).astype(BF16)
    vmt_ref[...] = za[:, ATT_KV:].T.astype(BF16)


def _kv_kernel(x_ref, prew_ref, wrk_ref, wrv_ref, wakv_ref, inv_ref, decb_ref, rkv_ref, ak_ref, avt_ref, sb_ref,
               state_ref, cosb_ref, sinb_ref, u_ref, kdec_ref, *, cpt):
    t = pl.program_id(1)
    tile = pl.num_programs(1) - 1 - t
    tm = cpt * CHUNK

    @pl.when((pl.program_id(0) == 0) & (t == 0))
    def _():
        _fill_rope_base(inv_ref[...], cosb_ref, sinb_ref)

    @pl.when(t == 0)
    def _():
        state_ref[...] = jnp.zeros_like(state_ref)

    for r0 in range(0, tm, CHUNK):
        u_ref[r0:r0 + CHUNK] = _rms_norm(x_ref[r0:r0 + CHUNK], prew_ref[...]).astype(BF16)
    cos, sin = _rope_tables_from_base(N_META + tile * tm, inv_ref[...], cosb_ref, sinb_ref)
    in_chunk = (lax.broadcasted_iota(jnp.int32, (tm, RET_DK), 0) & (CHUNK - 1)).astype(F32)

    heads_per_block = COL_BLOCK // RET_DK
    for c0 in range(0, RET_QK, COL_BLOCK):
        z = _dot(u_ref[...], wrk_ref[:, c0:c0 + COL_BLOCK])
        for i in range(heads_per_block):
            h = c0 // RET_DK + i
            k = _rope(z[:, i * RET_DK:(i + 1) * RET_DK], cos, sin) * (RET_DK ** -0.5)
            rkv_ref[:, h * RET_DK:(h + 1) * RET_DK] = k.astype(BF16)
            lg = _log_sigmoid(decb_ref[h:h + 1, :RET_DK])
            kdec_ref[:, h * RET_DK:(h + 1) * RET_DK] = (k * jnp.exp(in_chunk * lg)).astype(BF16)
    for c0 in range(0, RET_V, COL_BLOCK):
        rkv_ref[:, RET_QK + c0:RET_QK + c0 + COL_BLOCK] = _dot(u_ref[...], wrv_ref[:, c0:c0 + COL_BLOCK]).astype(BF16)
    za = _dot(u_ref[...], wakv_ref[:, :ATT_KV])
    for g in range(ATT_GROUPS):
        ak_ref[:, g * ATT_HD:(g + 1) * ATT_HD] = _rope(za[:, g * ATT_HD:(g + 1) * ATT_HD], cos, sin).astype(BF16)
    avt_ref[...] = _dot(u_ref[...], wakv_ref[:, ATT_KV:]).T.astype(BF16)

    for h in range(RET_HEADS):
        cdec = jnp.exp(CHUNK * _log_sigmoid(decb_ref[h:h + 1, :]))
        for lc in reversed(range(cpt)):
            r0 = lc * CHUNK
            state = state_ref[h]
            sb_ref[lc, h] = state.astype(BF16)
            k = kdec_ref[r0:r0 + CHUNK, h * RET_DK:(h + 1) * RET_DK]
            v = rkv_ref[r0:r0 + CHUNK, RET_QK + h * RET_DV:RET_QK + (h + 1) * RET_DV]
            state_ref[h] = cdec * state + _dot_tn(k, v)


def _fill_band_bias(bias_ref):
    kk = lax.broadcasted_iota(jnp.int32, bias_ref.shape, 0)
    qi = lax.broadcasted_iota(jnp.int32, bias_ref.shape, 1) & (CHUNK - 1)
    visible = (kk >= N_BAND) | ((kk >= qi) & (kk <= qi + 2 * CHUNK))
    bias_ref[...] = jnp.where(visible, 0.0, NEG_INF)


def _main_kernel(x_ref, prew_ref, postw_ref, retnw_ref, w0_ref, w1_ref, w2_ref, w3_ref, w4_ref, wrb_ref, wab_ref,
                 wo_ref, inv_ref, decf_ref, decb_ref, sink_ref, rkv_ref, akp_ref, akc_ref, akn_ref, avp_ref,
                 avc_ref, avn_ref, sb_ref, kvf0_ref, km_ref, vmt_ref, out_ref,
                 sf_ref, cosb_ref, sinb_ref, bias_ref, u_ref, rq_ref, aq_ref, or_ref, oa_ref, grg_ref, gag_ref,
                 ggr_ref, gga_ref, lhs_r_ref, lhs_a_ref, mix_ref, *, cpt):
    t = pl.program_id(1)
    tm = cpt * CHUNK
    wcols = functools.partial(_window_cols, MAIN_WINDOWS, (w0_ref, w1_ref, w2_ref, w3_ref, w4_ref))

    @pl.when((pl.program_id(0) == 0) & (t == 0))
    def _():
        _fill_rope_base(inv_ref[...], cosb_ref, sinb_ref)
        _fill_band_bias(bias_ref)

    @pl.when(t == 0)
    def _():
        sf_ref[...] = kvf0_ref[...]

    for r0 in range(0, tm, CHUNK):
        u_ref[r0:r0 + CHUNK] = _rms_norm(x_ref[r0:r0 + CHUNK], prew_ref[...]).astype(BF16)
    cos, sin = _rope_tables_from_base(N_META + t * tm, inv_ref[...], cosb_ref, sinb_ref)

    def proj(col, width=COL_BLOCK):
        return _dot(u_ref[...], wcols(col, width))

    heads_per_block = COL_BLOCK // ATT_HD
    for c0 in range(0, RET_QK, COL_BLOCK):
        z = proj(OFF_RQ + c0)
        for i in range(heads_per_block):
            c = c0 + i * RET_DK
            rq_ref[:, c:c + RET_DK] = _rope(z[:, i * RET_DK:(i + 1) * RET_DK], cos, sin)
    for c0 in range(0, ATT_Q, COL_BLOCK):
        z = proj(OFF_AQ + c0)
        for i in range(heads_per_block):
            c = c0 + i * ATT_HD
            aq_ref[:, c:c + ATT_HD] = (_rope(z[:, i * ATT_HD:(i + 1) * ATT_HD], cos, sin) * (ATT_HD ** -0.5)
                                       ).astype(BF16)

    gate_jobs = ((OFF_RG, jax.nn.silu, grg_ref), (OFF_AG, jax.nn.silu, gag_ref),
                 (OFF_GR, jax.nn.sigmoid, ggr_ref), (OFF_GA, jax.nn.sigmoid, gga_ref))
    gate_blocks = [(j, c) for j in range(len(gate_jobs)) for c in range(0, D_MODEL, COL_BLOCK)]
    assert len(gate_blocks) % cpt == 0

    def emit_gate_blocks(n):
        for _ in range(n):
            j, c = gate_blocks.pop(0)
            off, act, ref = gate_jobs[j]
            ref[:, c:c + COL_BLOCK] = act(proj(off + c))

    rows = _row_index((CHUNK, CHUNK))
    cols = lax.broadcasted_iota(jnp.int32, (CHUNK, CHUNK), 1).astype(F32)
    rel = rows - cols
    dmat, qdf, qdb, kdf, cdf = [], [], [], [], []
    for h in range(RET_HEADS):
        lgf = _log_sigmoid(decf_ref[h:h + 1, :])
        lgb = _log_sigmoid(decb_ref[h:h + 1, :])
        lgf_k, lgb_k = lgf[:, :RET_DK], lgb[:, :RET_DK]
        dmat.append(jnp.where(rel >= 0, jnp.exp(jnp.maximum(rel, 0.0) * lgf_k),
                              jnp.exp(jnp.maximum(-rel, 0.0) * lgb_k)))
        qdf.append(jnp.exp((rows + 1.0) * lgf_k))
        qdb.append(jnp.exp((CHUNK - rows) * lgb_k))
        kdf.append(jnp.exp((CHUNK - 1.0 - rows) * lgf_k))
        cdf.append(jnp.exp(CHUNK * lgf))

    k_cat = jnp.concatenate([akp_ref[...], akc_ref[...], akn_ref[...]], axis=0)
    vt_cat = jnp.concatenate([avp_ref[...], avc_ref[...], avn_ref[...]], axis=1)
    sink_rows = [jnp.concatenate([sink_ref[g * ATT_REP + r:g * ATT_REP + r + 1, :] for r in range(ATT_REP)],
                                 axis=1) for g in range(ATT_GROUPS)]

    for lc in range(cpt):
        r0 = lc * CHUNK
        chunk = t * cpt + lc
        for h in range(RET_HEADS):
            q = rq_ref[r0:r0 + CHUNK, h * RET_DK:(h + 1) * RET_DK]
            k = rkv_ref[r0:r0 + CHUNK, h * RET_DK:(h + 1) * RET_DK]
            v = rkv_ref[r0:r0 + CHUNK, RET_QK + h * RET_DV:RET_QK + (h + 1) * RET_DV]
            s = _dot_nt(q.astype(BF16), k) * dmat[h]
            sf = sf_ref[h]
            q_cross = jnp.concatenate([q * qdf[h], q * qdb[h]], axis=1).astype(BF16)
            s_cross = jnp.concatenate([sf.astype(BF16), sb_ref[lc, h]], axis=0)
            o = _dot(s.astype(BF16), v) + _dot(q_cross, s_cross)
            sf_ref[h] = cdf[h] * sf + _dot_tn((k.astype(F32) * kdf[h]).astype(BF16), v)
            mu = jnp.mean(o, axis=-1, keepdims=True)
            d = o - mu
            var = jnp.mean(d * d, axis=-1, keepdims=True)
            or_ref[r0:r0 + CHUNK, h * RET_DV:(h + 1) * RET_DV] = d * lax.rsqrt(var + EPS)
        for g in range(ATT_GROUPS):
            gs = slice(g * ATT_HD, (g + 1) * ATT_HD)
            qs = jnp.concatenate([aq_ref[r0:r0 + CHUNK, (g * ATT_REP + r) * ATT_HD:(g * ATT_REP + r + 1) * ATT_HD]
                                  for r in range(ATT_REP)], axis=0)
            k_all = jnp.concatenate([k_cat[r0:r0 + N_BAND, gs], km_ref[:, gs]], axis=0)
            vt_all = jnp.concatenate([vt_cat[gs, r0:r0 + N_BAND], vmt_ref[gs, :]], axis=1)
            s = _dot_nt(k_all, qs) + bias_ref[...]
            emit_gate_blocks(len(gate_jobs) * (D_MODEL // COL_BLOCK) // (cpt * ATT_GROUPS))
            parts = [s[0:CHUNK], s[CHUNK:2 * CHUNK], s[2 * CHUNK:N_BAND], s[N_BAND:]]
            if lc == 0:
                parts[0] = jnp.where(chunk > 0, parts[0], NEG_INF)
            if lc == cpt - 1:
                parts[2] = jnp.where(chunk < N_CHUNKS - 1, parts[2], NEG_INF)
            s = jnp.concatenate(parts, axis=0)
            sk = sink_rows[g]
            m = jnp.maximum(jnp.max(s, axis=0, keepdims=True), sk)
            p = jnp.exp(s - m)
            denom = jnp.sum(p, axis=0, keepdims=True) + jnp.exp(sk - m)
            ot = _dot(vt_all, p.astype(BF16)) / denom
            for r in range(ATT_REP):
                c = (g * ATT_REP + r) * ATT_HD
                oa_ref[r0:r0 + CHUNK, c:c + ATT_HD] = ot[:, r * CHUNK:(r + 1) * CHUNK].T

    for r0 in range(0, tm, CHUNK):
        rs = slice(r0, r0 + CHUNK)
        lhs_r_ref[rs] = (or_ref[rs] * retnw_ref[...] * grg_ref[rs]).astype(BF16)
        lhs_a_ref[rs] = (oa_ref[rs] * gag_ref[rs]).astype(BF16)
    for c0 in range(0, D_MODEL, COL_BLOCK):
        cs = slice(c0, c0 + COL_BLOCK)
        y_r = _dot(lhs_r_ref[...], wrb_ref[:, cs])
        y_a = _dot(lhs_a_ref[...], wab_ref[:, cs])
        mix_ref[:, cs] = (ggr_ref[:, cs] * y_r + gga_ref[:, cs] * y_a).astype(BF16)
    for c0 in range(0, D_MODEL, COL_BLOCK):
        cs = slice(c0, c0 + COL_BLOCK)
        out_ref[:, cs] = _dot(mix_ref[...], wo_ref[:, cs])
    for r0 in range(0, tm, CHUNK):
        rs = slice(r0, r0 + CHUNK)
        out_ref[rs] = x_ref[rs] + _rms_norm(out_ref[rs], postw_ref[...])


def _resident(shape, index=None):
    nd = len(shape)
    index = (0,) * nd if index is None else index
    return pl.BlockSpec(shape, lambda *_: index, pipeline_mode=pl.Buffered(1))


def _weight_windows(windows):
    specs = []
    for off, width in windows:
        assert off % width == 0
        specs.append(_resident((D_MODEL, width), (0, off // width)))
    return specs


def kernel(x, meta_tokens, pre_norm_w, w_in, ret_decay_fwd, ret_decay_bwd, ret_norm_w, w_ret_branch, attn_sink,
           w_attn_branch, w_out, post_norm_w):
    B = x.shape[0]
    assert x.shape == (B, SEQ, D_MODEL) and pre_norm_w.shape[0] == 1 and w_in.shape == (1, D_MODEL, D_IN)
    cpt = CHUNKS_PER_TILE
    tm = cpt * CHUNK
    nt = N_CHUNKS // cpt

    w_in_b = w_in[0].astype(BF16)
    w_rb = w_ret_branch[0].astype(BF16)
    w_ab = w_attn_branch[0].astype(BF16)
    w_o = w_out[0].astype(BF16)
    pre_w = pre_norm_w.astype(F32)
    post_w = post_norm_w.astype(F32)
    ret_nw = ret_norm_w.astype(F32)
    half = ATT_HD // 2
    inv = ROPE_THETA ** (-jnp.arange(half, dtype=F32) * 2.0 / ATT_HD)
    inv = jnp.concatenate([inv, inv])[None, :]
    dec_f = jnp.broadcast_to(ret_decay_fwd[0].astype(F32)[:, None], (RET_HEADS, RET_DV))
    dec_b = jnp.broadcast_to(ret_decay_bwd[0].astype(F32)[:, None], (RET_HEADS, RET_DV))
    sink = jnp.broadcast_to(attn_sink[0].astype(F32)[:, None], (ATT_HEADS, ATT_HD))

    params = pltpu.CompilerParams(dimension_semantics=("arbitrary", "arbitrary"),
                                  vmem_limit_bytes=V7X_VMEM_LIMIT_BYTES)
    rope_scratch = [pltpu.VMEM((tm, ATT_HD), F32), pltpu.VMEM((tm, ATT_HD), F32)]
    state_scratch = pltpu.VMEM((RET_HEADS, RET_DK, RET_DV), F32)
    kv_weights = [w_in_b] * len(KV_WINDOWS)

    kvf0, km, vmt = pl.pallas_call(
        _meta_kernel,
        grid=(1,),
        in_specs=[_resident((N_META, D_MODEL)), _resident((1, D_MODEL))] + _weight_windows(KV_WINDOWS)
        + [_resident((1, ATT_HD)), _resident((RET_HEADS, RET_DV))],
        out_specs=(pl.BlockSpec((RET_HEADS, RET_DK, RET_DV), lambda i: (0, 0, 0)),
                   pl.BlockSpec((N_META, ATT_KV), lambda i: (0, 0)),
                   pl.BlockSpec((ATT_KV, N_META), lambda i: (0, 0))),
        out_shape=(jax.ShapeDtypeStruct((RET_HEADS, RET_DK, RET_DV), F32),
                   jax.ShapeDtypeStruct((N_META, ATT_KV), BF16),
                   jax.ShapeDtypeStruct((ATT_KV, N_META), BF16)),
        compiler_params=pltpu.CompilerParams(dimension_semantics=("arbitrary",),
                                             vmem_limit_bytes=V7X_VMEM_LIMIT_BYTES),
        name="meta",
    )(meta_tokens.astype(F32), pre_w, *kv_weights, inv, dec_f)

    rkv, ak, avt, sb = pl.pallas_call(
        functools.partial(_kv_kernel, cpt=cpt),
        grid=(B, nt),
        in_specs=[pl.BlockSpec((None, tm, D_MODEL), lambda b, t: (b, nt - 1 - t, 0)), _resident((1, D_MODEL))]
        + _weight_windows(KV_WINDOWS) + [_resident((1, ATT_HD)), _resident((RET_HEADS, RET_DV))],
        out_specs=(
            pl.BlockSpec((None, tm, RKV_COLS), lambda b, t: (b, nt - 1 - t, 0)),
            pl.BlockSpec((None, tm, ATT_KV), lambda b, t: (b, nt - 1 - t, 0)),
            pl.BlockSpec((None, ATT_KV, tm), lambda b, t: (b, 0, nt - 1 - t)),
            pl.BlockSpec((None, cpt, RET_HEADS, RET_DK, RET_DV), lambda b, t: (b, nt - 1 - t, 0, 0, 0)),
        ),
        out_shape=(jax.ShapeDtypeStruct((B, SEQ, RKV_COLS), BF16),
                   jax.ShapeDtypeStruct((B, SEQ, ATT_KV), BF16),
                   jax.ShapeDtypeStruct((B, ATT_KV, SEQ), BF16),
                   jax.ShapeDtypeStruct((B, N_CHUNKS, RET_HEADS, RET_DK, RET_DV), BF16)),
        scratch_shapes=[state_scratch] + rope_scratch + [pltpu.VMEM((tm, D_MODEL), BF16),
                                                          pltpu.VMEM((tm, RET_QK), BF16)],
        compiler_params=params,
        name="kv",
    )(x, pre_w, *kv_weights, inv, dec_b)

    prev_chunk = lambda t: jnp.maximum(t * cpt - 1, 0)
    next_chunk = lambda t: jnp.minimum((t + 1) * cpt, N_CHUNKS - 1)
    out = pl.pallas_call(
        functools.partial(_main_kernel, cpt=cpt),
        grid=(B, nt),
        in_specs=[
            pl.BlockSpec((None, tm, D_MODEL), lambda b, t: (b, t, 0)),
            _resident((1, D_MODEL)),
            _resident((1, D_MODEL)),
            _resident((1, RET_V)),
        ] + _weight_windows(MAIN_WINDOWS) + [
            _resident((RET_V, D_MODEL)),
            _resident((ATT_Q, D_MODEL)),
            _resident((D_MODEL, D_MODEL)),
            _resident((1, ATT_HD)),
            _resident((RET_HEADS, RET_DV)),
            _resident((RET_HEADS, RET_DV)),
            _resident((ATT_HEADS, ATT_HD)),
            pl.BlockSpec((None, tm, RKV_COLS), lambda b, t: (b, t, 0)),
            pl.BlockSpec((None, CHUNK, ATT_KV), lambda b, t: (b, prev_chunk(t), 0)),
            pl.BlockSpec((None, tm, ATT_KV), lambda b, t: (b, t, 0)),
            pl.BlockSpec((None, CHUNK, ATT_KV), lambda b, t: (b, next_chunk(t), 0)),
            pl.BlockSpec((None, ATT_KV, CHUNK), lambda b, t: (b, 0, prev_chunk(t))),
            pl.BlockSpec((None, ATT_KV, tm), lambda b, t: (b, 0, t)),
            pl.BlockSpec((None, ATT_KV, CHUNK), lambda b, t: (b, 0, next_chunk(t))),
            pl.BlockSpec((None, cpt, RET_HEADS, RET_DK, RET_DV), lambda b, t: (b, t, 0, 0, 0)),
            _resident((RET_HEADS, RET_DK, RET_DV)),
            _resident((N_META, ATT_KV)),
            _resident((ATT_KV, N_META)),
        ],
        out_specs=pl.BlockSpec((None, tm, D_MODEL), lambda b, t: (b, t, 0)),
        out_shape=jax.ShapeDtypeStruct((B, SEQ, D_MODEL), x.dtype),
        scratch_shapes=(
            [state_scratch] + rope_scratch
            + [pltpu.VMEM((N_KEYS, Q_ROWS), F32),
               pltpu.VMEM((tm, D_MODEL), BF16),
               pltpu.VMEM((tm, RET_QK), F32),
               pltpu.VMEM((tm, ATT_Q), BF16),
               pltpu.VMEM((tm, RET_V), F32),
               pltpu.VMEM((tm, ATT_Q), F32)]
            + [pltpu.VMEM((tm, D_MODEL), F32)] * 4
            + [pltpu.VMEM((tm, D_MODEL), BF16)] * 3
        ),
        compiler_params=params,
        name="main",
    )(x, pre_w, post_w, ret_nw, *([w_in_b] * len(MAIN_WINDOWS)), w_rb, w_ab, w_o, inv, dec_f, dec_b, sink,
      rkv, ak, ak, ak, avt, avt, avt, sb, kvf0, km, vmt)
    return out
```

```python
import functools

import jax
import jax.numpy as jnp
from jax import lax
from jax.experimental import pallas as pl
from jax.experimental.pallas import tpu as pltpu

D_MODEL = 1024
SEQ = 8192
N_META = 16
CHUNK = 128
RET_HEADS = 4
RET_DK = 128
RET_DV = 256
ATT_HEADS = 8
ATT_GROUPS = 2
ATT_REP = ATT_HEADS // ATT_GROUPS
ATT_HD = 128
ROPE_THETA = 10000.0
EPS = 1e-6
NEG_INF = -1e30
RET_QK = RET_HEADS * RET_DK
RET_V = RET_HEADS * RET_DV
ATT_Q = ATT_HEADS * ATT_HD
ATT_KV = ATT_GROUPS * ATT_HD
D_IN = 2 * RET_QK + 2 * RET_V + 2 * ATT_Q + 2 * ATT_KV + 2 * D_MODEL
N_CHUNKS = SEQ // CHUNK
N_BAND = 3 * CHUNK
N_KEYS = N_BAND + N_META
Q_ROWS = ATT_REP * CHUNK

OFF_RQ = 0
OFF_RK = OFF_RQ + RET_QK
OFF_RV = OFF_RK + RET_QK
OFF_RG = OFF_RV + RET_V
OFF_AQ = OFF_RG + RET_V
OFF_AK = OFF_AQ + ATT_Q
OFF_AV = OFF_AK + ATT_KV
OFF_AG = OFF_AV + ATT_KV
OFF_GR = OFF_AG + ATT_Q
OFF_GA = OFF_GR + D_MODEL
RKV_COLS = RET_QK + RET_V

W_CHUNK = 512
KV_SRC = tuple(range(OFF_RK, OFF_RG, W_CHUNK)) + tuple(range(OFF_AK, OFF_AG, W_CHUNK))
KV_RK, KV_RV, KV_AKV = 0, RET_QK, RET_QK + RET_V
KV_COLS = len(KV_SRC) * W_CHUNK
MAIN_SEGMENTS = (("rq", OFF_RQ, RET_QK), ("aq", OFF_AQ, ATT_Q), ("rg", OFF_RG, RET_V), ("ag", OFF_AG, ATT_Q),
                 ("gr", OFF_GR, D_MODEL), ("ga", OFF_GA, D_MODEL))
MAIN_SRC = tuple(c for _, off, width in MAIN_SEGMENTS for c in range(off, off + width, W_CHUNK))
MAIN_OFF = {}
_o = 0
for _name, _, _width in MAIN_SEGMENTS:
    MAIN_OFF[_name] = _o
    _o += _width
MAIN_COLS = _o

CHUNKS_PER_TILE = 4
COL_BLOCK = 256
V7X_VMEM_LIMIT_BYTES = 56 * 1024 * 1024

F32 = jnp.float32
BF16 = jnp.bfloat16


def _rms_norm(x, w):
    return x * lax.rsqrt(jnp.mean(x * x, axis=-1, keepdims=True) + EPS) * w


def _log_sigmoid(x):
    return jnp.minimum(x, 0.0) - jnp.log(1.0 + jnp.exp(-jnp.abs(x)))


def _sign_fold(sin):
    lane = lax.broadcasted_iota(jnp.int32, sin.shape, 1)
    return jnp.where(lane < ATT_HD // 2, -sin, sin)


def _rope_tables(rows, inv):
    ang = lax.broadcasted_iota(jnp.int32, (rows, ATT_HD), 0).astype(F32) * inv
    return jnp.cos(ang), _sign_fold(jnp.sin(ang))


def _fill_rope_base(inv, cosb_ref, sinb_ref):
    ang = lax.broadcasted_iota(jnp.int32, cosb_ref.shape, 0).astype(F32) * inv
    cosb_ref[...] = jnp.cos(ang)
    sinb_ref[...] = jnp.sin(ang)


def _rope_tables_from_base(pos0, inv, cosb_ref, sinb_ref):
    base = pos0.astype(F32) * inv
    ca, sa = jnp.cos(base), jnp.sin(base)
    cb, sb = cosb_ref[...], sinb_ref[...]
    return ca * cb - sa * sb, _sign_fold(sa * cb + ca * sb)


def _rope(t, cos, sin_signed):
    return t * cos + pltpu.roll(t, ATT_HD // 2, axis=1) * sin_signed


def _dot(a, b):
    return jnp.dot(a, b, preferred_element_type=F32)


def _dot_nt(a, b):
    return lax.dot_general(a, b, (((1,), (1,)), ((), ())), preferred_element_type=F32)


def _dot_tn(a, b):
    return lax.dot_general(a, b, (((0,), (0,)), ((), ())), preferred_element_type=F32)


def _row_index(shape):
    return lax.broadcasted_iota(jnp.int32, shape, 0).astype(F32)


def _weight_chunk_copy(src_ref, col, stage_ref, sem_ref, slot):
    return pltpu.make_async_copy(src_ref.at[0, :, pl.ds(col, W_CHUNK)], stage_ref.at[slot], sem_ref.at[slot])


def _load_weights_bf16(jobs, stage_ref, sem_ref):
    copies = [_weight_chunk_copy(src, col, stage_ref, sem_ref, i % 2) for i, (src, col, _, _) in enumerate(jobs)]
    copies[0].start()
    for i, (_, _, dst, dcol) in enumerate(jobs):
        if i + 1 < len(jobs):
            copies[i + 1].start()
        copies[i].wait()
        dst[:, dcol:dcol + W_CHUNK] = stage_ref[i % 2].astype(BF16)


def _meta_kernel(meta_ref, prew_ref, wrk_ref, wrv_ref, wakv_ref, inv_ref, decf_ref, kvf0_ref, km_ref, vmt_ref):
    u = _rms_norm(meta_ref[...], prew_ref[...]).astype(BF16)
    zk, zv, za = (_dot(u, w[...].astype(BF16)) for w in (wrk_ref, wrv_ref, wakv_ref))
    cos, sin = _rope_tables(N_META, inv_ref[...])
    kdec_rows = (N_META - 1) - _row_index((N_META, RET_DK))
    for h in range(RET_HEADS):
        lg = _log_sigmoid(decf_ref[h:h + 1, :RET_DK])
        k = _rope(zk[:, h * RET_DK:(h + 1) * RET_DK], cos, sin) * (RET_DK ** -0.5)
        k = (k * jnp.exp(kdec_rows * lg)).astype(BF16)
        v = zv[:, h * RET_DV:(h + 1) * RET_DV].astype(BF16)
        kvf0_ref[h] = _dot_tn(k, v)
    for g in range(ATT_GROUPS):
        km_ref[:, g * ATT_HD:(g + 1) * ATT_HD] = _rope(za[:, g * ATT_HD:(g + 1) * ATT_HD], cos, sin).astype(BF16)
    vmt_ref[...] = za[:, ATT_KV:].T.astype(BF16)


def _kv_kernel(x_ref, prew_ref, win_ref, inv_ref, decb_ref, rkv_ref, ak_ref, avt_ref, sb_ref,
               state_ref, cosb_ref, sinb_ref, wkv_ref, stage_ref, sem_ref, *, cpt):
    t = pl.program_id(1)
    tile = pl.num_programs(1) - 1 - t
    tm = cpt * CHUNK

    @pl.when((pl.program_id(0) == 0) & (t == 0))
    def _():
        _load_weights_bf16([(win_ref, col, wkv_ref, i * W_CHUNK) for i, col in enumerate(KV_SRC)],
                           stage_ref, sem_ref)
        _fill_rope_base(inv_ref[...], cosb_ref, sinb_ref)

    @pl.when(t == 0)
    def _():
        state_ref[...] = jnp.zeros_like(state_ref)

    u = _rms_norm(x_ref[...], prew_ref[...]).astype(BF16)
    zk = _dot(u, wkv_ref[:, KV_RK:KV_RV])
    zv = _dot(u, wkv_ref[:, KV_RV:KV_AKV])
    za = _dot(u, wkv_ref[:, KV_AKV:])
    cos, sin = _rope_tables_from_base(N_META + tile * tm, inv_ref[...], cosb_ref, sinb_ref)
    rk = [_rope(zk[:, h * RET_DK:(h + 1) * RET_DK], cos, sin) * (RET_DK ** -0.5) for h in range(RET_HEADS)]
    for h in range(RET_HEADS):
        rkv_ref[:, h * RET_DK:(h + 1) * RET_DK] = rk[h].astype(BF16)
    rkv_ref[:, RET_QK:] = zv.astype(BF16)
    for g in range(ATT_GROUPS):
        ak_ref[:, g * ATT_HD:(g + 1) * ATT_HD] = _rope(za[:, g * ATT_HD:(g + 1) * ATT_HD], cos, sin).astype(BF16)
    avt_ref[...] = za[:, ATT_KV:].T.astype(BF16)

    rows_k = _row_index((CHUNK, RET_DK))
    for h in range(RET_HEADS):
        lg = _log_sigmoid(decb_ref[h:h + 1, :])
        kdec = jnp.exp(rows_k * lg[:, :RET_DK])
        cdec = jnp.exp(CHUNK * lg)
        for lc in reversed(range(cpt)):
            r0 = lc * CHUNK
            state = state_ref[h]
            sb_ref[lc, h] = state.astype(BF16)
            k = (rk[h][r0:r0 + CHUNK] * kdec).astype(BF16)
            v = zv[r0:r0 + CHUNK, h * RET_DV:(h + 1) * RET_DV].astype(BF16)
            state_ref[h] = cdec * state + _dot_tn(k, v)


def _fill_band_bias(bias_ref):
    kk = lax.broadcasted_iota(jnp.int32, bias_ref.shape, 0)
    qi = lax.broadcasted_iota(jnp.int32, bias_ref.shape, 1) & (CHUNK - 1)
    visible = (kk >= N_BAND) | ((kk >= qi) & (kk <= qi + 2 * CHUNK))
    bias_ref[...] = jnp.where(visible, 0.0, NEG_INF)


def _main_kernel(x_ref, prew_ref, postw_ref, retnw_ref, win_ref, wrb_hbm_ref, wab_hbm_ref, wo_hbm_ref, inv_ref,
                 decf_ref, decb_ref, sink_ref, rkv_ref, akp_ref, akc_ref, akn_ref, avp_ref,
                 avc_ref, avn_ref, sb_ref, kvf0_ref, km_ref, vmt_ref, out_ref,
                 sf_ref, cosb_ref, sinb_ref, bias_ref, u_ref, rq_ref, aq_ref, or_ref, oa_ref, grg_ref, gag_ref,
                 ggr_ref, gga_ref, lhs_r_ref, lhs_a_ref, mix_ref, wmain_ref, wrb_ref, wab_ref, wo_ref, stage_ref,
                 sem_ref, *, cpt):
    t = pl.program_id(1)
    tm = cpt * CHUNK

    @pl.when((pl.program_id(0) == 0) & (t == 0))
    def _():
        jobs = [(win_ref, col, wmain_ref, i * W_CHUNK) for i, col in enumerate(MAIN_SRC)]
        for src, dst in ((wrb_hbm_ref, wrb_ref), (wab_hbm_ref, wab_ref), (wo_hbm_ref, wo_ref)):
            jobs += [(src, c, dst, c) for c in range(0, D_MODEL, W_CHUNK)]
        _load_weights_bf16(jobs, stage_ref, sem_ref)
        _fill_rope_base(inv_ref[...], cosb_ref, sinb_ref)
        _fill_band_bias(bias_ref)

    @pl.when(t == 0)
    def _():
        sf_ref[...] = kvf0_ref[...]

    for r0 in range(0, tm, CHUNK):
        u_ref[r0:r0 + CHUNK] = _rms_norm(x_ref[r0:r0 + CHUNK], prew_ref[...]).astype(BF16)
    cos, sin = _rope_tables_from_base(N_META + t * tm, inv_ref[...], cosb_ref, sinb_ref)

    def proj(name, c0):
        col = MAIN_OFF[name] + c0
        return _dot(u_ref[...], wmain_ref[:, col:col + COL_BLOCK])

    heads_per_block = COL_BLOCK // ATT_HD
    for c0 in range(0, RET_QK, COL_BLOCK):
        z = proj("rq", c0)
        for i in range(heads_per_block):
            c = c0 + i * RET_DK
            rq_ref[:, c:c + RET_DK] = _rope(z[:, i * RET_DK:(i + 1) * RET_DK], cos, sin)
    for c0 in range(0, ATT_Q, COL_BLOCK):
        z = proj("aq", c0)
        for i in range(heads_per_block):
            c = c0 + i * ATT_HD
            aq_ref[:, c:c + ATT_HD] = (_rope(z[:, i * ATT_HD:(i + 1) * ATT_HD], cos, sin) * (ATT_HD ** -0.5)
                                       ).astype(BF16)

    gate_jobs = (("rg", jax.nn.silu, grg_ref), ("ag", jax.nn.silu, gag_ref),
                 ("gr", jax.nn.sigmoid, ggr_ref), ("ga", jax.nn.sigmoid, gga_ref))
    gate_blocks = [(j, c) for j in range(len(gate_jobs)) for c in range(0, D_MODEL, COL_BLOCK)]
    assert len(gate_blocks) % cpt == 0

    def emit_gate_blocks(n):
        for _ in range(n):
            j, c = gate_blocks.pop(0)
            name, act, ref = gate_jobs[j]
            ref[:, c:c + COL_BLOCK] = act(proj(name, c))

    rows = _row_index((CHUNK, CHUNK))
    cols = lax.broadcasted_iota(jnp.int32, (CHUNK, CHUNK), 1).astype(F32)
    rel = rows - cols
    dmat, qdf, qdb, kdf, cdf = [], [], [], [], []
    for h in range(RET_HEADS):
        lgf = _log_sigmoid(decf_ref[h:h + 1, :])
        lgb = _log_sigmoid(decb_ref[h:h + 1, :])
        lgf_k, lgb_k = lgf[:, :RET_DK], lgb[:, :RET_DK]
        dmat.append(jnp.where(rel >= 0, jnp.exp(jnp.maximum(rel, 0.0) * lgf_k),
                              jnp.exp(jnp.maximum(-rel, 0.0) * lgb_k)))
        qdf.append(jnp.exp((rows + 1.0) * lgf_k))
        qdb.append(jnp.exp((CHUNK - rows) * lgb_k))
        kdf.append(jnp.exp((CHUNK - 1.0 - rows) * lgf_k))
        cdf.append(jnp.exp(CHUNK * lgf))

    k_cat = jnp.concatenate([akp_ref[...], akc_ref[...], akn_ref[...]], axis=0)
    vt_cat = jnp.concatenate([avp_ref[...], avc_ref[...], avn_ref[...]], axis=1)
    sink_rows = [jnp.concatenate([sink_ref[g * ATT_REP + r:g * ATT_REP + r + 1, :] for r in range(ATT_REP)],
                                 axis=1) for g in range(ATT_GROUPS)]

    for lc in range(cpt):
        r0 = lc * CHUNK
        chunk = t * cpt + lc
        for h in range(RET_HEADS):
            q = rq_ref[r0:r0 + CHUNK, h * RET_DK:(h + 1) * RET_DK]
            k = rkv_ref[r0:r0 + CHUNK, h * RET_DK:(h + 1) * RET_DK]
            v = rkv_ref[r0:r0 + CHUNK, RET_QK + h * RET_DV:RET_QK + (h + 1) * RET_DV]
            s = _dot_nt(q.astype(BF16), k) * dmat[h]
            sf = sf_ref[h]
            q_cross = jnp.concatenate([q * qdf[h], q * qdb[h]], axis=1).astype(BF16)
            s_cross = jnp.concatenate([sf.astype(BF16), sb_ref[lc, h]], axis=0)
            o = _dot(s.astype(BF16), v) + _dot(q_cross, s_cross)
            sf_ref[h] = cdf[h] * sf + _dot_tn((k.astype(F32) * kdf[h]).astype(BF16), v)
            mu = jnp.mean(o, axis=-1, keepdims=True)
            d = o - mu
            var = jnp.mean(d * d, axis=-1, keepdims=True)
            or_ref[r0:r0 + CHUNK, h * RET_DV:(h + 1) * RET_DV] = d * lax.rsqrt(var + EPS)
        for g in range(ATT_GROUPS):
            gs = slice(g * ATT_HD, (g + 1) * ATT_HD)
            qs = jnp.concatenate([aq_ref[r0:r0 + CHUNK, (g * ATT_REP + r) * ATT_HD:(g * ATT_REP + r + 1) * ATT_HD]
                                  for r in range(ATT_REP)], axis=0)
            k_all = jnp.concatenate([k_cat[r0:r0 + N_BAND, gs], km_ref[:, gs]], axis=0)
            vt_all = jnp.concatenate([vt_cat[gs, r0:r0 + N_BAND], vmt_ref[gs, :]], axis=1)
            s = _dot_nt(k_all, qs) + bias_ref[...]
            emit_gate_blocks(len(gate_jobs) * (D_MODEL // COL_BLOCK) // (cpt * ATT_GROUPS))
            parts = [s[0:CHUNK], s[CHUNK:2 * CHUNK], s[2 * CHUNK:N_BAND], s[N_BAND:]]
            if lc == 0:
                parts[0] = jnp.where(chunk > 0, parts[0], NEG_INF)
            if lc == cpt - 1:
                parts[2] = jnp.where(chunk < N_CHUNKS - 1, parts[2], NEG_INF)
            s = jnp.concatenate(parts, axis=0)
            sk = sink_rows[g]
            m = jnp.maximum(jnp.max(s, axis=0, keepdims=True), sk)
            p = jnp.exp(s - m)
            denom = jnp.sum(p, axis=0, keepdims=True) + jnp.exp(sk - m)
            ot = _dot(vt_all, p.astype(BF16)) / denom
            for r in range(ATT_REP):
                c = (g * ATT_REP + r) * ATT_HD
                oa_ref[r0:r0 + CHUNK, c:c + ATT_HD] = ot[:, r * CHUNK:(r + 1) * CHUNK].T

    for r0 in range(0, tm, CHUNK):
        rs = slice(r0, r0 + CHUNK)
        lhs_r_ref[rs] = (or_ref[rs] * retnw_ref[...] * grg_ref[rs]).astype(BF16)
        lhs_a_ref[rs] = (oa_ref[rs] * gag_ref[rs]).astype(BF16)
    for c0 in range(0, D_MODEL, COL_BLOCK):
        cs = slice(c0, c0 + COL_BLOCK)
        y_r = _dot(lhs_r_ref[...], wrb_ref[:, cs])
        y_a = _dot(lhs_a_ref[...], wab_ref[:, cs])
        mix_ref[:, cs] = (ggr_ref[:, cs] * y_r + gga_ref[:, cs] * y_a).astype(BF16)
    for c0 in range(0, D_MODEL, COL_BLOCK):
        cs = slice(c0, c0 + COL_BLOCK)
        out_ref[:, cs] = _dot(mix_ref[...], wo_ref[:, cs])
    for r0 in range(0, tm, CHUNK):
        rs = slice(r0, r0 + CHUNK)
        out_ref[rs] = x_ref[rs] + _rms_norm(out_ref[rs], postw_ref[...])


def _resident(shape, index=None):
    nd = len(shape)
    index = (0,) * nd if index is None else index
    return pl.BlockSpec(shape, lambda *_: index, pipeline_mode=pl.Buffered(1))


def kernel(x, meta_tokens, pre_norm_w, w_in, ret_decay_fwd, ret_decay_bwd, ret_norm_w, w_ret_branch, attn_sink,
           w_attn_branch, w_out, post_norm_w):
    B = x.shape[0]
    assert x.shape == (B, SEQ, D_MODEL) and pre_norm_w.shape[0] == 1 and w_in.shape == (1, D_MODEL, D_IN)
    cpt = CHUNKS_PER_TILE
    tm = cpt * CHUNK
    nt = N_CHUNKS // cpt

    w_in = w_in.astype(F32)
    w_rb, w_ab, w_o = (w.astype(F32) for w in (w_ret_branch, w_attn_branch, w_out))
    pre_w = pre_norm_w.astype(F32)
    post_w = post_norm_w.astype(F32)
    ret_nw = ret_norm_w.astype(F32)
    half = ATT_HD // 2
    inv = ROPE_THETA ** (-jnp.arange(half, dtype=F32) * 2.0 / ATT_HD)
    inv = jnp.concatenate([inv, inv])[None, :]
    dec_f = jnp.broadcast_to(ret_decay_fwd[0].astype(F32)[:, None], (RET_HEADS, RET_DV))
    dec_b = jnp.broadcast_to(ret_decay_bwd[0].astype(F32)[:, None], (RET_HEADS, RET_DV))
    sink = jnp.broadcast_to(attn_sink[0].astype(F32)[:, None], (ATT_HEADS, ATT_HD))

    params = pltpu.CompilerParams(dimension_semantics=("arbitrary", "arbitrary"),
                                  vmem_limit_bytes=V7X_VMEM_LIMIT_BYTES)
    rope_scratch = [pltpu.VMEM((tm, ATT_HD), F32), pltpu.VMEM((tm, ATT_HD), F32)]
    state_scratch = pltpu.VMEM((RET_HEADS, RET_DK, RET_DV), F32)
    weight_stage = [pltpu.VMEM((2, D_MODEL, W_CHUNK), F32), pltpu.SemaphoreType.DMA((2,))]
    hbm = pl.BlockSpec(memory_space=pl.ANY)

    def win_window(off, width):
        assert off % width == 0
        return _resident((None, D_MODEL, width), (0, 0, off // width))

    kvf0, km, vmt = pl.pallas_call(
        _meta_kernel,
        grid=(1,),
        in_specs=[_resident((N_META, D_MODEL)), _resident((1, D_MODEL)), win_window(OFF_RK, RET_QK),
                  win_window(OFF_RV, RET_V), win_window(OFF_AK, 2 * ATT_KV), _resident((1, ATT_HD)),
                  _resident((RET_HEADS, RET_DV))],
        out_specs=(pl.BlockSpec((RET_HEADS, RET_DK, RET_DV), lambda i: (0, 0, 0)),
                   pl.BlockSpec((N_META, ATT_KV), lambda i: (0, 0)),
                   pl.BlockSpec((ATT_KV, N_META), lambda i: (0, 0))),
        out_shape=(jax.ShapeDtypeStruct((RET_HEADS, RET_DK, RET_DV), F32),
                   jax.ShapeDtypeStruct((N_META, ATT_KV), BF16),
                   jax.ShapeDtypeStruct((ATT_KV, N_META), BF16)),
        compiler_params=pltpu.CompilerParams(dimension_semantics=("arbitrary",),
                                             vmem_limit_bytes=V7X_VMEM_LIMIT_BYTES),
        name="meta",
    )(meta_tokens.astype(F32), pre_w, w_in, w_in, w_in, inv, dec_f)

    rkv, ak, avt, sb = pl.pallas_call(
        functools.partial(_kv_kernel, cpt=cpt),
        grid=(B, nt),
        in_specs=[pl.BlockSpec((None, tm, D_MODEL), lambda b, t: (b, nt - 1 - t, 0)), _resident((1, D_MODEL)), hbm,
                  _resident((1, ATT_HD)), _resident((RET_HEADS, RET_DV))],
        out_specs=(
            pl.BlockSpec((None, tm, RKV_COLS), lambda b, t: (b, nt - 1 - t, 0)),
            pl.BlockSpec((None, tm, ATT_KV), lambda b, t: (b, nt - 1 - t, 0)),
            pl.BlockSpec((None, ATT_KV, tm), lambda b, t: (b, 0, nt - 1 - t)),
            pl.BlockSpec((None, cpt, RET_HEADS, RET_DK, RET_DV), lambda b, t: (b, nt - 1 - t, 0, 0, 0)),
        ),
        out_shape=(jax.ShapeDtypeStruct((B, SEQ, RKV_COLS), BF16),
                   jax.ShapeDtypeStruct((B, SEQ, ATT_KV), BF16),
                   jax.ShapeDtypeStruct((B, ATT_KV, SEQ), BF16),
                   jax.ShapeDtypeStruct((B, N_CHUNKS, RET_HEADS, RET_DK, RET_DV), BF16)),
        scratch_shapes=[state_scratch] + rope_scratch + [pltpu.VMEM((D_MODEL, KV_COLS), BF16)] + weight_stage,
        compiler_params=params,
        name="kv",
    )(x, pre_w, w_in, inv, dec_b)

    prev_chunk = lambda t: jnp.maximum(t * cpt - 1, 0)
    next_chunk = lambda t: jnp.minimum((t + 1) * cpt, N_CHUNKS - 1)
    out = pl.pallas_call(
        functools.partial(_main_kernel, cpt=cpt),
        grid=(B, nt),
        in_specs=[
            pl.BlockSpec((None, tm, D_MODEL), lambda b, t: (b, t, 0)),
            _resident((1, D_MODEL)),
            _resident((1, D_MODEL)),
            _resident((1, RET_V)),
            hbm, hbm, hbm, hbm,
            _resident((1, ATT_HD)),
            _resident((RET_HEADS, RET_DV)),
            _resident((RET_HEADS, RET_DV)),
            _resident((ATT_HEADS, ATT_HD)),
            pl.BlockSpec((None, tm, RKV_COLS), lambda b, t: (b, t, 0)),
            pl.BlockSpec((None, CHUNK, ATT_KV), lambda b, t: (b, prev_chunk(t), 0)),
            pl.BlockSpec((None, tm, ATT_KV), lambda b, t: (b, t, 0)),
            pl.BlockSpec((None, CHUNK, ATT_KV), lambda b, t: (b, next_chunk(t), 0)),
            pl.BlockSpec((None, ATT_KV, CHUNK), lambda b, t: (b, 0, prev_chunk(t))),
            pl.BlockSpec((None, ATT_KV, tm), lambda b, t: (b, 0, t)),
            pl.BlockSpec((None, ATT_KV, CHUNK), lambda b, t: (b, 0, next_chunk(t))),
            pl.BlockSpec((None, cpt, RET_HEADS, RET_DK, RET_DV), lambda b, t: (b, t, 0, 0, 0)),
            _resident((RET_HEADS, RET_DK, RET_DV)),
            _resident((N_META, ATT_KV)),
            _resident((ATT_KV, N_META)),
        ],
        out_specs=pl.BlockSpec((None, tm, D_MODEL), lambda b, t: (b, t, 0)),
        out_shape=jax.ShapeDtypeStruct((B, SEQ, D_MODEL), x.dtype),
        scratch_shapes=(
            [state_scratch] + rope_scratch
            + [pltpu.VMEM((N_KEYS, Q_ROWS), F32),
               pltpu.VMEM((tm, D_MODEL), BF16),
               pltpu.VMEM((tm, RET_QK), F32),
               pltpu.VMEM((tm, ATT_Q), BF16),
               pltpu.VMEM((tm, RET_V), F32),
               pltpu.VMEM((tm, ATT_Q), F32)]
            + [pltpu.VMEM((tm, D_MODEL), F32)] * 4
            + [pltpu.VMEM((tm, D_MODEL), BF16)] * 3
            + [pltpu.VMEM((D_MODEL, MAIN_COLS), BF16)]
            + [pltpu.VMEM((D_MODEL, D_MODEL), BF16)] * 3
            + weight_stage
        ),
        compiler_params=params,
        name="main",
    )(x, pre_w, post_w, ret_nw, w_in, w_rb, w_ab, w_o, inv, dec_f, dec_b, sink,
      rkv, ak, ak, ak, avt, avt, avt, sb, kvf0, km, vmt)
    return out
```

```python
import functools

import jax
import jax.numpy as jnp
from jax import lax
from jax.experimental import pallas as pl
from jax.experimental.pallas import tpu as pltpu

D_MODEL = 1024
SEQ = 8192
N_META = 16
CHUNK = 128
RET_HEADS = 4
RET_DK = 128
RET_DV = 256
ATT_HEADS = 8
ATT_GROUPS = 2
ATT_REP = ATT_HEADS // ATT_GROUPS
ATT_HD = 128
ROPE_THETA = 10000.0
EPS = 1e-6
NEG_INF = -1e30
RET_QK = RET_HEADS * RET_DK
RET_V = RET_HEADS * RET_DV
ATT_Q = ATT_HEADS * ATT_HD
ATT_KV = ATT_GROUPS * ATT_HD
D_IN = 2 * RET_QK + 2 * RET_V + 2 * ATT_Q + 2 * ATT_KV + 2 * D_MODEL
N_CHUNKS = SEQ // CHUNK
N_BAND = 3 * CHUNK
N_KEYS = N_BAND + N_META
Q_ROWS = ATT_REP * CHUNK

OFF_RQ = 0
OFF_RK = OFF_RQ + RET_QK
OFF_RV = OFF_RK + RET_QK
OFF_RG = OFF_RV + RET_V
OFF_AQ = OFF_RG + RET_V
OFF_AK = OFF_AQ + ATT_Q
OFF_AV = OFF_AK + ATT_KV
OFF_AG = OFF_AV + ATT_KV
OFF_GR = OFF_AG + ATT_Q
OFF_GA = OFF_GR + D_MODEL
RKV_COLS = RET_QK + RET_V

W_CHUNK = 512
KV_SRC = tuple(range(OFF_RK, OFF_RG, W_CHUNK)) + tuple(range(OFF_AK, OFF_AG, W_CHUNK))
KV_RK, KV_RV, KV_AKV = 0, RET_QK, RET_QK + RET_V
KV_COLS = len(KV_SRC) * W_CHUNK
MAIN_SEGMENTS = (("rq", OFF_RQ, RET_QK), ("aq", OFF_AQ, ATT_Q), ("rg", OFF_RG, RET_V), ("ag", OFF_AG, ATT_Q),
                 ("gr", OFF_GR, D_MODEL), ("ga", OFF_GA, D_MODEL))
MAIN_SRC = tuple(c for _, off, width in MAIN_SEGMENTS for c in range(off, off + width, W_CHUNK))
MAIN_OFF = {}
_o = 0
for _name, _, _width in MAIN_SEGMENTS:
    MAIN_OFF[_name] = _o
    _o += _width
MAIN_COLS = _o

CHUNKS_PER_TILE = 4
KV_CHUNKS_PER_TILE = 8
RET_CHUNK = 2 * CHUNK
N_RET_CHUNKS = SEQ // RET_CHUNK
COL_BLOCK = 256
V7X_VMEM_LIMIT_BYTES = 56 * 1024 * 1024

F32 = jnp.float32
BF16 = jnp.bfloat16


def _rms_norm(x, w):
    return x * lax.rsqrt(jnp.mean(x * x, axis=-1, keepdims=True) + EPS) * w


def _log_sigmoid(x):
    return jnp.minimum(x, 0.0) - jnp.log(1.0 + jnp.exp(-jnp.abs(x)))


def _sign_fold(sin):
    lane = lax.broadcasted_iota(jnp.int32, sin.shape, 1)
    return jnp.where(lane < ATT_HD // 2, -sin, sin)


def _rope_tables(rows, inv):
    ang = lax.broadcasted_iota(jnp.int32, (rows, ATT_HD), 0).astype(F32) * inv
    return jnp.cos(ang), _sign_fold(jnp.sin(ang))


def _fill_rope_base(inv, cosb_ref, sinb_ref):
    ang = lax.broadcasted_iota(jnp.int32, cosb_ref.shape, 0).astype(F32) * inv
    cosb_ref[...] = jnp.cos(ang)
    sinb_ref[...] = jnp.sin(ang)


def _rope_tables_from_base(pos0, inv, cosb_ref, sinb_ref):
    base = pos0.astype(F32) * inv
    ca, sa = jnp.cos(base), jnp.sin(base)
    cb, sb = cosb_ref[...], sinb_ref[...]
    return ca * cb - sa * sb, _sign_fold(sa * cb + ca * sb)


def _rope(t, cos, sin_signed):
    return t * cos + pltpu.roll(t, ATT_HD // 2, axis=1) * sin_signed


def _dot(a, b):
    return jnp.dot(a, b, preferred_element_type=F32)


def _dot_nt(a, b):
    return lax.dot_general(a, b, (((1,), (1,)), ((), ())), preferred_element_type=F32)


def _dot_tn(a, b):
    return lax.dot_general(a, b, (((0,), (0,)), ((), ())), preferred_element_type=F32)


def _row_index(shape):
    return lax.broadcasted_iota(jnp.int32, shape, 0).astype(F32)


def _weight_chunk_copy(src_ref, col, stage_ref, sem_ref, slot):
    return pltpu.make_async_copy(src_ref.at[0, :, pl.ds(col, W_CHUNK)], stage_ref.at[slot], sem_ref.at[slot])


def _load_weights_bf16(jobs, stage_ref, sem_ref):
    copies = [_weight_chunk_copy(src, col, stage_ref, sem_ref, i % 2) for i, (src, col, _, _) in enumerate(jobs)]
    copies[0].start()
    for i, (_, _, dst, dcol) in enumerate(jobs):
        if i + 1 < len(jobs):
            copies[i + 1].start()
        copies[i].wait()
        dst[:, dcol:dcol + W_CHUNK] = stage_ref[i % 2].astype(BF16)


def _meta_kernel(meta_ref, prew_ref, wrk_ref, wrv_ref, wakv_ref, inv_ref, decf_ref, kvf0_ref, km_ref, vmt_ref):
    u = _rms_norm(meta_ref[...], prew_ref[...]).astype(BF16)
    zk, zv, za = (_dot(u, w[...].astype(BF16)) for w in (wrk_ref, wrv_ref, wakv_ref))
    cos, sin = _rope_tables(N_META, inv_ref[...])
    kdec_rows = (N_META - 1) - _row_index((N_META, RET_DK))
    for h in range(RET_HEADS):
        lg = _log_sigmoid(decf_ref[h:h + 1, :RET_DK])
        k = _rope(zk[:, h * RET_DK:(h + 1) * RET_DK], cos, sin) * (RET_DK ** -0.5)
        k = (k * jnp.exp(kdec_rows * lg)).astype(BF16)
        v = zv[:, h * RET_DV:(h + 1) * RET_DV].astype(BF16)
        kvf0_ref[h] = _dot_tn(k, v)
    for g in range(ATT_GROUPS):
        km_ref[:, g * ATT_HD:(g + 1) * ATT_HD] = _rope(za[:, g * ATT_HD:(g + 1) * ATT_HD], cos, sin).astype(BF16)
    vmt_ref[...] = za[:, ATT_KV:].T.astype(BF16)


def _kv_kernel(x_ref, prew_ref, win_ref, inv_ref, decb_ref, rkv_ref, ak_ref, avt_ref, sb_ref,
               state_ref, cosb_ref, sinb_ref, wkv_ref, stage_ref, sem_ref, *, cpt):
    t = pl.program_id(1)
    tile = pl.num_programs(1) - 1 - t
    tm = cpt * CHUNK

    @pl.when((pl.program_id(0) == 0) & (t == 0))
    def _():
        _load_weights_bf16([(win_ref, col, wkv_ref, i * W_CHUNK) for i, col in enumerate(KV_SRC)],
                           stage_ref, sem_ref)
        _fill_rope_base(inv_ref[...], cosb_ref, sinb_ref)

    @pl.when(t == 0)
    def _():
        state_ref[...] = jnp.zeros_like(state_ref)

    u = _rms_norm(x_ref[...], prew_ref[...]).astype(BF16)
    zk = _dot(u, wkv_ref[:, KV_RK:KV_RV])
    zv = _dot(u, wkv_ref[:, KV_RV:KV_AKV])
    za = _dot(u, wkv_ref[:, KV_AKV:])
    cos, sin = _rope_tables_from_base(N_META + tile * tm, inv_ref[...], cosb_ref, sinb_ref)
    rk = [_rope(zk[:, h * RET_DK:(h + 1) * RET_DK], cos, sin) * (RET_DK ** -0.5) for h in range(RET_HEADS)]
    for h in range(RET_HEADS):
        rkv_ref[:, h * RET_DK:(h + 1) * RET_DK] = rk[h].astype(BF16)
    rkv_ref[:, RET_QK:] = zv.astype(BF16)
    for g in range(ATT_GROUPS):
        ak_ref[:, g * ATT_HD:(g + 1) * ATT_HD] = _rope(za[:, g * ATT_HD:(g + 1) * ATT_HD], cos, sin).astype(BF16)
    avt_ref[...] = za[:, ATT_KV:].T.astype(BF16)

    rows_k = _row_index((RET_CHUNK, RET_DK))
    for h in range(RET_HEADS):
        lg = _log_sigmoid(decb_ref[h:h + 1, :])
        kdec = jnp.exp(rows_k * lg[:, :RET_DK])
        cdec = jnp.exp(RET_CHUNK * lg)
        for rc in reversed(range(tm // RET_CHUNK)):
            r0 = rc * RET_CHUNK
            state = state_ref[h]
            sb_ref[rc, h] = state.astype(BF16)
            k = (rk[h][r0:r0 + RET_CHUNK] * kdec).astype(BF16)
            v = zv[r0:r0 + RET_CHUNK, h * RET_DV:(h + 1) * RET_DV].astype(BF16)
            state_ref[h] = cdec * state + _dot_tn(k, v)


def _fill_band_bias(bias_ref):
    kk = lax.broadcasted_iota(jnp.int32, bias_ref.shape, 0)
    qi = lax.broadcasted_iota(jnp.int32, bias_ref.shape, 1) & (CHUNK - 1)
    visible = (kk >= N_BAND) | ((kk >= qi) & (kk <= qi + 2 * CHUNK))
    bias_ref[...] = jnp.where(visible, 0.0, NEG_INF)


def _fill_decay_tables(decf_ref, decb_ref, dmat_ref, qdec_ref, kdf_ref):
    rows = _row_index((RET_CHUNK, RET_CHUNK))
    rel = rows - lax.broadcasted_iota(jnp.int32, (RET_CHUNK, RET_CHUNK), 1).astype(F32)
    rows_k = _row_index((RET_CHUNK, RET_DK))
    for h in range(RET_HEADS):
        lgf = _log_sigmoid(decf_ref[h:h + 1, :])
        lgb = _log_sigmoid(decb_ref[h:h + 1, :])
        dmat_ref[h] = jnp.where(rel >= 0, jnp.exp(jnp.maximum(rel, 0.0) * lgf),
                                jnp.exp(jnp.maximum(-rel, 0.0) * lgb))
        lgf_k, lgb_k = lgf[:, :RET_DK], lgb[:, :RET_DK]
        qdec_ref[h, :, :RET_DK] = jnp.exp((rows_k + 1.0) * lgf_k)
        qdec_ref[h, :, RET_DK:] = jnp.exp((RET_CHUNK - rows_k) * lgb_k)
        kdf_ref[h] = jnp.exp((RET_CHUNK - 1.0 - rows_k) * lgf_k)


def _main_kernel(x_ref, prew_ref, postw_ref, retnw_ref, win_ref, wrb_hbm_ref, wab_hbm_ref, wo_hbm_ref, inv_ref,
                 decf_ref, decb_ref, sink_ref, rkv_ref, akp_ref, akc_ref, akn_ref, avp_ref,
                 avc_ref, avn_ref, sb_ref, kvf0_ref, km_ref, vmt_ref, out_ref,
                 sf_ref, cosb_ref, sinb_ref, bias_ref, u_ref, rq_ref, aq_ref, or_ref, oa_ref, grg_ref, gag_ref,
                 ggr_ref, gga_ref, lhs_r_ref, lhs_a_ref, mix_ref, dmat_ref, qdec_ref, kdf_ref, wmain_ref, wrb_ref,
                 wab_ref, wo_ref, stage_ref, sem_ref, *, cpt):
    t = pl.program_id(1)
    tm = cpt * CHUNK

    @pl.when((pl.program_id(0) == 0) & (t == 0))
    def _():
        jobs = [(win_ref, col, wmain_ref, i * W_CHUNK) for i, col in enumerate(MAIN_SRC)]
        for src, dst in ((wrb_hbm_ref, wrb_ref), (wab_hbm_ref, wab_ref), (wo_hbm_ref, wo_ref)):
            jobs += [(src, c, dst, c) for c in range(0, D_MODEL, W_CHUNK)]
        _load_weights_bf16(jobs, stage_ref, sem_ref)
        _fill_rope_base(inv_ref[...], cosb_ref, sinb_ref)
        _fill_band_bias(bias_ref)
        _fill_decay_tables(decf_ref, decb_ref, dmat_ref, qdec_ref, kdf_ref)

    @pl.when(t == 0)
    def _():
        sf_ref[...] = kvf0_ref[...]

    for r0 in range(0, tm, CHUNK):
        u_ref[r0:r0 + CHUNK] = _rms_norm(x_ref[r0:r0 + CHUNK], prew_ref[...]).astype(BF16)
    cos, sin = _rope_tables_from_base(N_META + t * tm, inv_ref[...], cosb_ref, sinb_ref)

    def proj(name, c0):
        col = MAIN_OFF[name] + c0
        return _dot(u_ref[...], wmain_ref[:, col:col + COL_BLOCK])

    heads_per_block = COL_BLOCK // ATT_HD
    for c0 in range(0, RET_QK, COL_BLOCK):
        z = proj("rq", c0)
        for i in range(heads_per_block):
            c = c0 + i * RET_DK
            rq_ref[:, c:c + RET_DK] = _rope(z[:, i * RET_DK:(i + 1) * RET_DK], cos, sin)
    for c0 in range(0, ATT_Q, COL_BLOCK):
        z = proj("aq", c0)
        for i in range(heads_per_block):
            c = c0 + i * ATT_HD
            aq_ref[:, c:c + ATT_HD] = (_rope(z[:, i * ATT_HD:(i + 1) * ATT_HD], cos, sin) * (ATT_HD ** -0.5)
                                       ).astype(BF16)

    gate_jobs = (("rg", jax.nn.silu, grg_ref), ("ag", jax.nn.silu, gag_ref),
                 ("gr", jax.nn.sigmoid, ggr_ref), ("ga", jax.nn.sigmoid, gga_ref))
    gate_blocks = [(j, c) for j in range(len(gate_jobs)) for c in range(0, D_MODEL, COL_BLOCK)]
    assert len(gate_blocks) % cpt == 0

    def emit_gate_blocks(n):
        for _ in range(n):
            j, c = gate_blocks.pop(0)
            name, act, ref = gate_jobs[j]
            ref[:, c:c + COL_BLOCK] = act(proj(name, c))

    cdf = [jnp.exp(RET_CHUNK * _log_sigmoid(decf_ref[h:h + 1, :])) for h in range(RET_HEADS)]

    k_cat = jnp.concatenate([akp_ref[...], akc_ref[...], akn_ref[...]], axis=0)
    vt_cat = jnp.concatenate([avp_ref[...], avc_ref[...], avn_ref[...]], axis=1)
    sink_rows = [jnp.concatenate([sink_ref[g * ATT_REP + r:g * ATT_REP + r + 1, :] for r in range(ATT_REP)],
                                 axis=1) for g in range(ATT_GROUPS)]

    def retention(rc):
        r0 = rc * RET_CHUNK
        rs = slice(r0, r0 + RET_CHUNK)
        for h in range(RET_HEADS):
            q = rq_ref[rs, h * RET_DK:(h + 1) * RET_DK]
            k = rkv_ref[rs, h * RET_DK:(h + 1) * RET_DK]
            v = rkv_ref[rs, RET_QK + h * RET_DV:RET_QK + (h + 1) * RET_DV]
            s = _dot_nt(q.astype(BF16), k) * dmat_ref[h]
            sf = sf_ref[h]
            q_cross = (jnp.concatenate([q, q], axis=1) * qdec_ref[h]).astype(BF16)
            s_cross = jnp.concatenate([sf.astype(BF16), sb_ref[rc, h]], axis=0)
            o = _dot(s.astype(BF16), v) + _dot(q_cross, s_cross)
            sf_ref[h] = cdf[h] * sf + _dot_tn((k.astype(F32) * kdf_ref[h]).astype(BF16), v)
            mu = jnp.mean(o, axis=-1, keepdims=True)
            d = o - mu
            var = jnp.mean(d * d, axis=-1, keepdims=True)
            or_ref[rs, h * RET_DV:(h + 1) * RET_DV] = d * lax.rsqrt(var + EPS)

    for lc in range(cpt):
        r0 = lc * CHUNK
        chunk = t * cpt + lc
        if r0 % RET_CHUNK == 0:
            retention(r0 // RET_CHUNK)
        for g in range(ATT_GROUPS):
            gs = slice(g * ATT_HD, (g + 1) * ATT_HD)
            qs = jnp.concatenate([aq_ref[r0:r0 + CHUNK, (g * ATT_REP + r) * ATT_HD:(g * ATT_REP + r + 1) * ATT_HD]
                                  for r in range(ATT_REP)], axis=0)
            k_all = jnp.concatenate([k_cat[r0:r0 + N_BAND, gs], km_ref[:, gs]], axis=0)
            vt_all = jnp.concatenate([vt_cat[gs, r0:r0 + N_BAND], vmt_ref[gs, :]], axis=1)
            s = _dot_nt(k_all, qs) + bias_ref[...]
            emit_gate_blocks(len(gate_jobs) * (D_MODEL // COL_BLOCK) // (cpt * ATT_GROUPS))
            parts = [s[0:CHUNK], s[CHUNK:2 * CHUNK], s[2 * CHUNK:N_BAND], s[N_BAND:]]
            if lc == 0:
                parts[0] = jnp.where(chunk > 0, parts[0], NEG_INF)
            if lc == cpt - 1:
                parts[2] = jnp.where(chunk < N_CHUNKS - 1, parts[2], NEG_INF)
            s = jnp.concatenate(parts, axis=0)
            sk = sink_rows[g]
            m = jnp.maximum(jnp.max(s, axis=0, keepdims=True), sk)
            p = jnp.exp(s - m)
            denom = jnp.sum(p, axis=0, keepdims=True) + jnp.exp(sk - m)
            ot = _dot(vt_all, p.astype(BF16)) / denom
            for r in range(ATT_REP):
                c = (g * ATT_REP + r) * ATT_HD
                oa_ref[r0:r0 + CHUNK, c:c + ATT_HD] = ot[:, r * CHUNK:(r + 1) * CHUNK].T

    for r0 in range(0, tm, CHUNK):
        rs = slice(r0, r0 + CHUNK)
        lhs_r_ref[rs] = (or_ref[rs] * retnw_ref[...] * grg_ref[rs]).astype(BF16)
        lhs_a_ref[rs] = (oa_ref[rs] * gag_ref[rs]).astype(BF16)
    for c0 in range(0, D_MODEL, COL_BLOCK):
        cs = slice(c0, c0 + COL_BLOCK)
        y_r = _dot(lhs_r_ref[...], wrb_ref[:, cs])
        y_a = _dot(lhs_a_ref[...], wab_ref[:, cs])
        mix_ref[:, cs] = (ggr_ref[:, cs] * y_r + gga_ref[:, cs] * y_a).astype(BF16)
    for c0 in range(0, D_MODEL, COL_BLOCK):
        cs = slice(c0, c0 + COL_BLOCK)
        out_ref[:, cs] = _dot(mix_ref[...], wo_ref[:, cs])
    for r0 in range(0, tm, CHUNK):
        rs = slice(r0, r0 + CHUNK)
        out_ref[rs] = x_ref[rs] + _rms_norm(out_ref[rs], postw_ref[...])


def _resident(shape, index=None):
    nd = len(shape)
    index = (0,) * nd if index is None else index
    return pl.BlockSpec(shape, lambda *_: index, pipeline_mode=pl.Buffered(1))


def kernel(x, meta_tokens, pre_norm_w, w_in, ret_decay_fwd, ret_decay_bwd, ret_norm_w, w_ret_branch, attn_sink,
           w_attn_branch, w_out, post_norm_w):
    B = x.shape[0]
    assert x.shape == (B, SEQ, D_MODEL) and pre_norm_w.shape[0] == 1 and w_in.shape == (1, D_MODEL, D_IN)
    cpt = CHUNKS_PER_TILE
    tm = cpt * CHUNK
    nt = N_CHUNKS // cpt

    w_in = w_in.astype(F32)
    w_rb, w_ab, w_o = (w.astype(F32) for w in (w_ret_branch, w_attn_branch, w_out))
    pre_w = pre_norm_w.astype(F32)
    post_w = post_norm_w.astype(F32)
    ret_nw = ret_norm_w.astype(F32)
    half = ATT_HD // 2
    inv = ROPE_THETA ** (-jnp.arange(half, dtype=F32) * 2.0 / ATT_HD)
    inv = jnp.concatenate([inv, inv])[None, :]
    dec_f = jnp.broadcast_to(ret_decay_fwd[0].astype(F32)[:, None], (RET_HEADS, RET_DV))
    dec_b = jnp.broadcast_to(ret_decay_bwd[0].astype(F32)[:, None], (RET_HEADS, RET_DV))
    sink = jnp.broadcast_to(attn_sink[0].astype(F32)[:, None], (ATT_HEADS, ATT_HD))

    params = pltpu.CompilerParams(dimension_semantics=("arbitrary", "arbitrary"),
                                  vmem_limit_bytes=V7X_VMEM_LIMIT_BYTES)
    rope_scratch = [pltpu.VMEM((tm, ATT_HD), F32), pltpu.VMEM((tm, ATT_HD), F32)]
    state_scratch = pltpu.VMEM((RET_HEADS, RET_DK, RET_DV), F32)
    weight_stage = [pltpu.VMEM((2, D_MODEL, W_CHUNK), F32), pltpu.SemaphoreType.DMA((2,))]
    hbm = pl.BlockSpec(memory_space=pl.ANY)

    def win_window(off, width):
        assert off % width == 0
        return _resident((None, D_MODEL, width), (0, 0, off // width))

    kvf0, km, vmt = pl.pallas_call(
        _meta_kernel,
        grid=(1,),
        in_specs=[_resident((N_META, D_MODEL)), _resident((1, D_MODEL)), win_window(OFF_RK, RET_QK),
                  win_window(OFF_RV, RET_V), win_window(OFF_AK, 2 * ATT_KV), _resident((1, ATT_HD)),
                  _resident((RET_HEADS, RET_DV))],
        out_specs=(pl.BlockSpec((RET_HEADS, RET_DK, RET_DV), lambda i: (0, 0, 0)),
                   pl.BlockSpec((N_META, ATT_KV), lambda i: (0, 0)),
                   pl.BlockSpec((ATT_KV, N_META), lambda i: (0, 0))),
        out_shape=(jax.ShapeDtypeStruct((RET_HEADS, RET_DK, RET_DV), F32),
                   jax.ShapeDtypeStruct((N_META, ATT_KV), BF16),
                   jax.ShapeDtypeStruct((ATT_KV, N_META), BF16)),
        compiler_params=pltpu.CompilerParams(dimension_semantics=("arbitrary",),
                                             vmem_limit_bytes=V7X_VMEM_LIMIT_BYTES),
        name="meta",
    )(meta_tokens.astype(F32), pre_w, w_in, w_in, w_in, inv, dec_f)

    kcpt = KV_CHUNKS_PER_TILE
    ktm, knt = kcpt * CHUNK, N_CHUNKS // kcpt
    rkv, ak, avt, sb = pl.pallas_call(
        functools.partial(_kv_kernel, cpt=kcpt),
        grid=(B, knt),
        in_specs=[pl.BlockSpec((None, ktm, D_MODEL), lambda b, t: (b, knt - 1 - t, 0)), _resident((1, D_MODEL)), hbm,
                  _resident((1, ATT_HD)), _resident((RET_HEADS, RET_DV))],
        out_specs=(
            pl.BlockSpec((None, ktm, RKV_COLS), lambda b, t: (b, knt - 1 - t, 0)),
            pl.BlockSpec((None, ktm, ATT_KV), lambda b, t: (b, knt - 1 - t, 0)),
            pl.BlockSpec((None, ATT_KV, ktm), lambda b, t: (b, 0, knt - 1 - t)),
            pl.BlockSpec((None, ktm // RET_CHUNK, RET_HEADS, RET_DK, RET_DV), lambda b, t: (b, knt - 1 - t, 0, 0, 0)),
        ),
        out_shape=(jax.ShapeDtypeStruct((B, SEQ, RKV_COLS), BF16),
                   jax.ShapeDtypeStruct((B, SEQ, ATT_KV), BF16),
                   jax.ShapeDtypeStruct((B, ATT_KV, SEQ), BF16),
                   jax.ShapeDtypeStruct((B, N_RET_CHUNKS, RET_HEADS, RET_DK, RET_DV), BF16)),
        scratch_shapes=[state_scratch, pltpu.VMEM((ktm, ATT_HD), F32), pltpu.VMEM((ktm, ATT_HD), F32)] + [pltpu.VMEM((D_MODEL, KV_COLS), BF16)] + weight_stage,
        compiler_params=params,
        name="kv",
    )(x, pre_w, w_in, inv, dec_b)

    prev_chunk = lambda t: jnp.maximum(t * cpt - 1, 0)
    next_chunk = lambda t: jnp.minimum((t + 1) * cpt, N_CHUNKS - 1)
    out = pl.pallas_call(
        functools.partial(_main_kernel, cpt=cpt),
        grid=(B, nt),
        in_specs=[
            pl.BlockSpec((None, tm, D_MODEL), lambda b, t: (b, t, 0)),
            _resident((1, D_MODEL)),
            _resident((1, D_MODEL)),
            _resident((1, RET_V)),
            hbm, hbm, hbm, hbm,
            _resident((1, ATT_HD)),
            _resident((RET_HEADS, RET_DV)),
            _resident((RET_HEADS, RET_DV)),
            _resident((ATT_HEADS, ATT_HD)),
            pl.BlockSpec((None, tm, RKV_COLS), lambda b, t: (b, t, 0)),
            pl.BlockSpec((None, CHUNK, ATT_KV), lambda b, t: (b, prev_chunk(t), 0)),
            pl.BlockSpec((None, tm, ATT_KV), lambda b, t: (b, t, 0)),
            pl.BlockSpec((None, CHUNK, ATT_KV), lambda b, t: (b, next_chunk(t), 0)),
            pl.BlockSpec((None, ATT_KV, CHUNK), lambda b, t: (b, 0, prev_chunk(t))),
            pl.BlockSpec((None, ATT_KV, tm), lambda b, t: (b, 0, t)),
            pl.BlockSpec((None, ATT_KV, CHUNK), lambda b, t: (b, 0, next_chunk(t))),
            pl.BlockSpec((None, tm // RET_CHUNK, RET_HEADS, RET_DK, RET_DV), lambda b, t: (b, t, 0, 0, 0)),
            _resident((RET_HEADS, RET_DK, RET_DV)),
            _resident((N_META, ATT_KV)),
            _resident((ATT_KV, N_META)),
        ],
        out_specs=pl.BlockSpec((None, tm, D_MODEL), lambda b, t: (b, t, 0)),
        out_shape=jax.ShapeDtypeStruct((B, SEQ, D_MODEL), x.dtype),
        scratch_shapes=(
            [state_scratch] + rope_scratch
            + [pltpu.VMEM((N_KEYS, Q_ROWS), F32),
               pltpu.VMEM((tm, D_MODEL), BF16),
               pltpu.VMEM((tm, RET_QK), F32),
               pltpu.VMEM((tm, ATT_Q), BF16),
               pltpu.VMEM((tm, RET_V), F32),
               pltpu.VMEM((tm, ATT_Q), F32)]
            + [pltpu.VMEM((tm, D_MODEL), F32)] * 4
            + [pltpu.VMEM((tm, D_MODEL), BF16)] * 3
            + [pltpu.VMEM((RET_HEADS, RET_CHUNK, RET_CHUNK), F32),
               pltpu.VMEM((RET_HEADS, RET_CHUNK, 2 * RET_DK), F32),
               pltpu.VMEM((RET_HEADS, RET_CHUNK, RET_DK), F32)]
            + [pltpu.VMEM((D_MODEL, MAIN_COLS), BF16)]
            + [pltpu.VMEM((D_MODEL, D_MODEL), BF16)] * 3
            + weight_stage
        ),
        compiler_params=params,
        name="main",
    )(x, pre_w, post_w, ret_nw, w_in, w_rb, w_ab, w_o, inv, dec_f, dec_b, sink,
      rkv, ak, ak, ak, avt, avt, avt, sb, kvf0, km, vmt)
    return out
```

```python
import functools

import jax
import jax.numpy as jnp
from jax import lax
from jax.experimental import pallas as pl
from jax.experimental.pallas import tpu as pltpu

D_MODEL = 1024
SEQ = 8192
N_META = 16
CHUNK = 128
RET_HEADS = 4
RET_DK = 128
RET_DV = 256
ATT_HEADS = 8
ATT_GROUPS = 2
ATT_REP = ATT_HEADS // ATT_GROUPS
ATT_HD = 128
ROPE_THETA = 10000.0
EPS = 1e-6
NEG_INF = -1e30
RET_QK = RET_HEADS * RET_DK
RET_V = RET_HEADS * RET_DV
ATT_Q = ATT_HEADS * ATT_HD
ATT_KV = ATT_GROUPS * ATT_HD
D_IN = 2 * RET_QK + 2 * RET_V + 2 * ATT_Q + 2 * ATT_KV + 2 * D_MODEL
N_CHUNKS = SEQ // CHUNK
N_BAND = 3 * CHUNK
N_KEYS = N_BAND + N_META
Q_ROWS = ATT_REP * CHUNK

OFF_RQ = 0
OFF_RK = OFF_RQ + RET_QK
OFF_RV = OFF_RK + RET_QK
OFF_RG = OFF_RV + RET_V
OFF_AQ = OFF_RG + RET_V
OFF_AK = OFF_AQ + ATT_Q
OFF_AV = OFF_AK + ATT_KV
OFF_AG = OFF_AV + ATT_KV
OFF_GR = OFF_AG + ATT_Q
OFF_GA = OFF_GR + D_MODEL
RKV_COLS = RET_QK + RET_V

W_CHUNK = 512
KV_SRC = tuple(range(OFF_RK, OFF_RG, W_CHUNK)) + tuple(range(OFF_AK, OFF_AG, W_CHUNK))
KV_RK, KV_RV, KV_AKV = 0, RET_QK, RET_QK + RET_V
KV_COLS = len(KV_SRC) * W_CHUNK
MAIN_SEGMENTS = (("rq", OFF_RQ, RET_QK), ("aq", OFF_AQ, ATT_Q), ("rg", OFF_RG, RET_V), ("ag", OFF_AG, ATT_Q),
                 ("gr", OFF_GR, D_MODEL), ("ga", OFF_GA, D_MODEL))
MAIN_SRC = tuple(c for _, off, width in MAIN_SEGMENTS for c in range(off, off + width, W_CHUNK))
MAIN_OFF = {}
_o = 0
for _name, _, _width in MAIN_SEGMENTS:
    MAIN_OFF[_name] = _o
    _o += _width
MAIN_COLS = _o

CHUNKS_PER_TILE = 4
KV_CHUNKS_PER_TILE = 8
RET_CHUNK = 2 * CHUNK
N_RET_CHUNKS = SEQ // RET_CHUNK
COL_BLOCK = 256
V7X_VMEM_LIMIT_BYTES = 56 * 1024 * 1024

F32 = jnp.float32
BF16 = jnp.bfloat16


def _rms_norm(x, w):
    return x * lax.rsqrt(jnp.mean(x * x, axis=-1, keepdims=True) + EPS) * w


def _log_sigmoid(x):
    return jnp.minimum(x, 0.0) - jnp.log(1.0 + jnp.exp(-jnp.abs(x)))


def _sign_fold(sin):
    lane = lax.broadcasted_iota(jnp.int32, sin.shape, 1)
    return jnp.where(lane < ATT_HD // 2, -sin, sin)


def _rope_tables(rows, inv):
    ang = lax.broadcasted_iota(jnp.int32, (rows, ATT_HD), 0).astype(F32) * inv
    return jnp.cos(ang), _sign_fold(jnp.sin(ang))


def _fill_rope_base(inv, cosb_ref, sinb_ref):
    ang = lax.broadcasted_iota(jnp.int32, cosb_ref.shape, 0).astype(F32) * inv
    cosb_ref[...] = jnp.cos(ang)
    sinb_ref[...] = jnp.sin(ang)


def _rope_tables_from_base(pos0, inv, cosb_ref, sinb_ref):
    base = pos0.astype(F32) * inv
    ca, sa = jnp.cos(base), jnp.sin(base)
    cb, sb = cosb_ref[...], sinb_ref[...]
    return ca * cb - sa * sb, _sign_fold(sa * cb + ca * sb)


def _rope(t, cos, sin_signed):
    return t * cos + pltpu.roll(t, ATT_HD // 2, axis=1) * sin_signed


def _dot(a, b):
    return jnp.dot(a, b, preferred_element_type=F32)


def _dot_nt(a, b):
    return lax.dot_general(a, b, (((1,), (1,)), ((), ())), preferred_element_type=F32)


def _dot_tn(a, b):
    return lax.dot_general(a, b, (((0,), (0,)), ((), ())), preferred_element_type=F32)


def _row_index(shape):
    return lax.broadcasted_iota(jnp.int32, shape, 0).astype(F32)


def _weight_chunk_copy(src_ref, col, stage_ref, sem_ref, slot):
    return pltpu.make_async_copy(src_ref.at[0, :, pl.ds(col, W_CHUNK)], stage_ref.at[slot], sem_ref.at[slot])


def _load_weights_bf16(jobs, stage_ref, sem_ref):
    copies = [_weight_chunk_copy(src, col, stage_ref, sem_ref, i % 2) for i, (src, col, _, _) in enumerate(jobs)]
    copies[0].start()
    for i, (_, _, dst, dcol) in enumerate(jobs):
        if i + 1 < len(jobs):
            copies[i + 1].start()
        copies[i].wait()
        dst[:, dcol:dcol + W_CHUNK] = stage_ref[i % 2].astype(BF16)


def _meta_kernel(meta_ref, prew_ref, wrk_ref, wrv_ref, wakv_ref, inv_ref, decf_ref, kvf0_ref, km_ref, vmt_ref):
    u = _rms_norm(meta_ref[...], prew_ref[...]).astype(BF16)
    zk, zv, za = (_dot(u, w[...].astype(BF16)) for w in (wrk_ref, wrv_ref, wakv_ref))
    cos, sin = _rope_tables(N_META, inv_ref[...])
    kdec_rows = (N_META - 1) - _row_index((N_META, RET_DK))
    for h in range(RET_HEADS):
        lg = _log_sigmoid(decf_ref[h:h + 1, :RET_DK])
        k = _rope(zk[:, h * RET_DK:(h + 1) * RET_DK], cos, sin) * (RET_DK ** -0.5)
        k = (k * jnp.exp(kdec_rows * lg)).astype(BF16)
        v = zv[:, h * RET_DV:(h + 1) * RET_DV].astype(BF16)
        kvf0_ref[h] = _dot_tn(k, v)
    for g in range(ATT_GROUPS):
        km_ref[:, g * ATT_HD:(g + 1) * ATT_HD] = _rope(za[:, g * ATT_HD:(g + 1) * ATT_HD], cos, sin).astype(BF16)
    vmt_ref[...] = za[:, ATT_KV:].T.astype(BF16)


def _kv_kernel(x_ref, prew_ref, win_ref, inv_ref, decb_ref, rkv_ref, ak_ref, avt_ref, sb_ref, u_out_ref,
               state_ref, cosb_ref, sinb_ref, wkv_ref, stage_ref, sem_ref, *, cpt):
    t = pl.program_id(1)
    tile = pl.num_programs(1) - 1 - t
    tm = cpt * CHUNK

    @pl.when((pl.program_id(0) == 0) & (t == 0))
    def _():
        _load_weights_bf16([(win_ref, col, wkv_ref, i * W_CHUNK) for i, col in enumerate(KV_SRC)],
                           stage_ref, sem_ref)
        _fill_rope_base(inv_ref[...], cosb_ref, sinb_ref)

    @pl.when(t == 0)
    def _():
        state_ref[...] = jnp.zeros_like(state_ref)

    u = _rms_norm(x_ref[...], prew_ref[...]).astype(BF16)
    u_out_ref[...] = u
    zk = _dot(u, wkv_ref[:, KV_RK:KV_RV])
    zv = _dot(u, wkv_ref[:, KV_RV:KV_AKV])
    za = _dot(u, wkv_ref[:, KV_AKV:])
    cos, sin = _rope_tables_from_base(N_META + tile * tm, inv_ref[...], cosb_ref, sinb_ref)
    rk = [_rope(zk[:, h * RET_DK:(h + 1) * RET_DK], cos, sin) * (RET_DK ** -0.5) for h in range(RET_HEADS)]
    for h in range(RET_HEADS):
        rkv_ref[:, h * RET_DK:(h + 1) * RET_DK] = rk[h].astype(BF16)
    rkv_ref[:, RET_QK:] = zv.astype(BF16)
    for g in range(ATT_GROUPS):
        ak_ref[:, g * ATT_HD:(g + 1) * ATT_HD] = _rope(za[:, g * ATT_HD:(g + 1) * ATT_HD], cos, sin).astype(BF16)
    avt_ref[...] = za[:, ATT_KV:].T.astype(BF16)

    rows_k = _row_index((RET_CHUNK, RET_DK))
    for h in range(RET_HEADS):
        lg = _log_sigmoid(decb_ref[h:h + 1, :])
        kdec = jnp.exp(rows_k * lg[:, :RET_DK])
        cdec = jnp.exp(RET_CHUNK * lg)
        for rc in reversed(range(tm // RET_CHUNK)):
            r0 = rc * RET_CHUNK
            state = state_ref[h]
            sb_ref[rc, h] = state.astype(BF16)
            k = (rk[h][r0:r0 + RET_CHUNK] * kdec).astype(BF16)
            v = zv[r0:r0 + RET_CHUNK, h * RET_DV:(h + 1) * RET_DV].astype(BF16)
            state_ref[h] = cdec * state + _dot_tn(k, v)


def _fill_band_bias(bias_ref):
    kk = lax.broadcasted_iota(jnp.int32, bias_ref.shape, 0)
    qi = lax.broadcasted_iota(jnp.int32, bias_ref.shape, 1) & (CHUNK - 1)
    visible = (kk >= N_BAND) | ((kk >= qi) & (kk <= qi + 2 * CHUNK))
    bias_ref[...] = jnp.where(visible, 0.0, NEG_INF)


def _fill_decay_tables(decf_ref, decb_ref, dmat_ref, qdec_ref, kdf_ref):
    rows = _row_index((RET_CHUNK, RET_CHUNK))
    rel = rows - lax.broadcasted_iota(jnp.int32, (RET_CHUNK, RET_CHUNK), 1).astype(F32)
    rows_k = _row_index((RET_CHUNK, RET_DK))
    for h in range(RET_HEADS):
        lgf = _log_sigmoid(decf_ref[h:h + 1, :])
        lgb = _log_sigmoid(decb_ref[h:h + 1, :])
        dmat_ref[h] = jnp.where(rel >= 0, jnp.exp(jnp.maximum(rel, 0.0) * lgf),
                                jnp.exp(jnp.maximum(-rel, 0.0) * lgb))
        lgf_k, lgb_k = lgf[:, :RET_DK], lgb[:, :RET_DK]
        qdec_ref[h, :, :RET_DK] = jnp.exp((rows_k + 1.0) * lgf_k)
        qdec_ref[h, :, RET_DK:] = jnp.exp((RET_CHUNK - rows_k) * lgb_k)
        kdf_ref[h] = jnp.exp((RET_CHUNK - 1.0 - rows_k) * lgf_k)


def _main_kernel(u_ref, xp_ref, postw_ref, retnw_ref, win_ref, wrb_hbm_ref, wab_hbm_ref, wo_hbm_ref, inv_ref,
                 decf_ref, decb_ref, sink_ref, rkv_ref, akp_ref, akc_ref, akn_ref, avp_ref,
                 avc_ref, avn_ref, sb_ref, kvf0_ref, km_ref, vmt_ref, out_ref,
                 sf_ref, cosb_ref, sinb_ref, bias_ref, aq_ref, or_ref, oa_ref, grg_ref, gag_ref,
                 ggr_ref, gga_ref, lhs_r_ref, lhs_a_ref, mix_ref, dmat_ref, qdec_ref, kdf_ref, wmain_ref, wrb_ref,
                 wab_ref, wo_ref, stage_ref, sem_ref, outp_ref, *, cpt, n_tiles, tiles_per_batch):
    step = pl.program_id(0)
    t = step % tiles_per_batch
    tm = cpt * CHUNK

    def finish_previous_tile():
        for r0 in range(0, tm, CHUNK):
            rs = slice(r0, r0 + CHUNK)
            out_ref[rs] = xp_ref[rs] + _rms_norm(outp_ref[rs], postw_ref[...])

    @pl.when(step == 0)
    def _():
        outp_ref[...] = jnp.zeros_like(outp_ref)
        jobs = [(win_ref, col, wmain_ref, i * W_CHUNK) for i, col in enumerate(MAIN_SRC)]
        for src, dst in ((wrb_hbm_ref, wrb_ref), (wab_hbm_ref, wab_ref), (wo_hbm_ref, wo_ref)):
            jobs += [(src, c, dst, c) for c in range(0, D_MODEL, W_CHUNK)]
        _load_weights_bf16(jobs, stage_ref, sem_ref)
        _fill_rope_base(inv_ref[...], cosb_ref, sinb_ref)
        _fill_band_bias(bias_ref)
        _fill_decay_tables(decf_ref, decb_ref, dmat_ref, qdec_ref, kdf_ref)

    @pl.when((t == 0) & (step < n_tiles))
    def _():
        sf_ref[...] = kvf0_ref[...]

    @pl.when(step == n_tiles)
    def _():
        finish_previous_tile()

    @pl.when(step < n_tiles)
    def _():
        finish_previous_tile()
        _main_tile(t, cpt, u_ref, retnw_ref, inv_ref, decf_ref, sink_ref, rkv_ref, akp_ref, akc_ref, akn_ref, avp_ref,
                   avc_ref, avn_ref, sb_ref, km_ref, vmt_ref, sf_ref, cosb_ref, sinb_ref, bias_ref, aq_ref,
                   or_ref, oa_ref, grg_ref, gag_ref, ggr_ref, gga_ref, lhs_r_ref, lhs_a_ref, mix_ref, dmat_ref,
                   qdec_ref, kdf_ref, wmain_ref, wrb_ref, wab_ref, wo_ref, outp_ref)


def _main_tile(t, cpt, u_ref, retnw_ref, inv_ref, decf_ref, sink_ref, rkv_ref, akp_ref, akc_ref, akn_ref, avp_ref,
               avc_ref, avn_ref, sb_ref, km_ref, vmt_ref, sf_ref, cosb_ref, sinb_ref, bias_ref, aq_ref,
               or_ref, oa_ref, grg_ref, gag_ref, ggr_ref, gga_ref, lhs_r_ref, lhs_a_ref, mix_ref, dmat_ref,
               qdec_ref, kdf_ref, wmain_ref, wrb_ref, wab_ref, wo_ref, outp_ref):
    tm = cpt * CHUNK
    rq_ref = outp_ref
    cos, sin = _rope_tables_from_base(N_META + t * tm, inv_ref[...], cosb_ref, sinb_ref)

    def proj(name, c0):
        col = MAIN_OFF[name] + c0
        return _dot(u_ref[...], wmain_ref[:, col:col + COL_BLOCK])

    heads_per_block = COL_BLOCK // ATT_HD
    for c0 in range(0, RET_QK, COL_BLOCK):
        z = proj("rq", c0)
        for i in range(heads_per_block):
            c = c0 + i * RET_DK
            rq_ref[:, c:c + RET_DK] = _rope(z[:, i * RET_DK:(i + 1) * RET_DK], cos, sin)
    for c0 in range(0, ATT_Q, COL_BLOCK):
        z = proj("aq", c0)
        for i in range(heads_per_block):
            c = c0 + i * ATT_HD
            aq_ref[:, c:c + ATT_HD] = (_rope(z[:, i * ATT_HD:(i + 1) * ATT_HD], cos, sin) * (ATT_HD ** -0.5)
                                       ).astype(BF16)

    gate_jobs = (("rg", jax.nn.silu, grg_ref), ("ag", jax.nn.silu, gag_ref),
                 ("gr", jax.nn.sigmoid, ggr_ref), ("ga", jax.nn.sigmoid, gga_ref))
    gate_blocks = [(j, c) for j in range(len(gate_jobs)) for c in range(0, D_MODEL, COL_BLOCK)]
    assert len(gate_blocks) % cpt == 0

    def emit_gate_blocks(n):
        for _ in range(n):
            j, c = gate_blocks.pop(0)
            name, act, ref = gate_jobs[j]
            ref[:, c:c + COL_BLOCK] = act(proj(name, c))

    cdf = [jnp.exp(RET_CHUNK * _log_sigmoid(decf_ref[h:h + 1, :])) for h in range(RET_HEADS)]

    k_cat = jnp.concatenate([akp_ref[...], akc_ref[...], akn_ref[...]], axis=0)
    vt_cat = jnp.concatenate([avp_ref[...], avc_ref[...], avn_ref[...]], axis=1)
    sink_rows = [jnp.concatenate([sink_ref[g * ATT_REP + r:g * ATT_REP + r + 1, :] for r in range(ATT_REP)],
                                 axis=1) for g in range(ATT_GROUPS)]

    def retention(rc):
        r0 = rc * RET_CHUNK
        rs = slice(r0, r0 + RET_CHUNK)
        for h in range(RET_HEADS):
            q = rq_ref[rs, h * RET_DK:(h + 1) * RET_DK]
            k = rkv_ref[rs, h * RET_DK:(h + 1) * RET_DK]
            v = rkv_ref[rs, RET_QK + h * RET_DV:RET_QK + (h + 1) * RET_DV]
            s = _dot_nt(q.astype(BF16), k) * dmat_ref[h]
            sf = sf_ref[h]
            q_cross = (jnp.concatenate([q, q], axis=1) * qdec_ref[h]).astype(BF16)
            s_cross = jnp.concatenate([sf.astype(BF16), sb_ref[rc, h]], axis=0)
            o = _dot(s.astype(BF16), v) + _dot(q_cross, s_cross)
            sf_ref[h] = cdf[h] * sf + _dot_tn((k.astype(F32) * kdf_ref[h]).astype(BF16), v)
            mu = jnp.mean(o, axis=-1, keepdims=True)
            d = o - mu
            var = jnp.mean(d * d, axis=-1, keepdims=True)
            or_ref[rs, h * RET_DV:(h + 1) * RET_DV] = d * lax.rsqrt(var + EPS)

    for lc in range(cpt):
        r0 = lc * CHUNK
        chunk = t * cpt + lc
        if r0 % RET_CHUNK == 0:
            retention(r0 // RET_CHUNK)
        for g in range(ATT_GROUPS):
            gs = slice(g * ATT_HD, (g + 1) * ATT_HD)
            qs = jnp.concatenate([aq_ref[r0:r0 + CHUNK, (g * ATT_REP + r) * ATT_HD:(g * ATT_REP + r + 1) * ATT_HD]
                                  for r in range(ATT_REP)], axis=0)
            k_all = jnp.concatenate([k_cat[r0:r0 + N_BAND, gs], km_ref[:, gs]], axis=0)
            vt_all = jnp.concatenate([vt_cat[gs, r0:r0 + N_BAND], vmt_ref[gs, :]], axis=1)
            s = _dot_nt(k_all, qs) + bias_ref[...]
            emit_gate_blocks(len(gate_jobs) * (D_MODEL // COL_BLOCK) // (cpt * ATT_GROUPS))
            parts = [s[0:CHUNK], s[CHUNK:2 * CHUNK], s[2 * CHUNK:N_BAND], s[N_BAND:]]
            if lc == 0:
                parts[0] = jnp.where(chunk > 0, parts[0], NEG_INF)
            if lc == cpt - 1:
                parts[2] = jnp.where(chunk < N_CHUNKS - 1, parts[2], NEG_INF)
            s = jnp.concatenate(parts, axis=0)
            sk = sink_rows[g]
            m = jnp.maximum(jnp.max(s, axis=0, keepdims=True), sk)
            p = jnp.exp(s - m)
            denom = jnp.sum(p, axis=0, keepdims=True) + jnp.exp(sk - m)
            ot = _dot(vt_all, p.astype(BF16)) / denom
            for r in range(ATT_REP):
                c = (g * ATT_REP + r) * ATT_HD
                oa_ref[r0:r0 + CHUNK, c:c + ATT_HD] = ot[:, r * CHUNK:(r + 1) * CHUNK].T

    for r0 in range(0, tm, CHUNK):
        rs = slice(r0, r0 + CHUNK)
        lhs_r_ref[rs] = (or_ref[rs] * retnw_ref[...] * grg_ref[rs]).astype(BF16)
        lhs_a_ref[rs] = (oa_ref[rs] * gag_ref[rs]).astype(BF16)
    for c0 in range(0, D_MODEL, COL_BLOCK):
        cs = slice(c0, c0 + COL_BLOCK)
        y_r = _dot(lhs_r_ref[...], wrb_ref[:, cs])
        y_a = _dot(lhs_a_ref[...], wab_ref[:, cs])
        mix_ref[:, cs] = (ggr_ref[:, cs] * y_r + gga_ref[:, cs] * y_a).astype(BF16)
    for c0 in range(0, D_MODEL, COL_BLOCK):
        cs = slice(c0, c0 + COL_BLOCK)
        outp_ref[:, cs] = _dot(mix_ref[...], wo_ref[:, cs])


def _resident(shape, index=None):
    nd = len(shape)
    index = (0,) * nd if index is None else index
    return pl.BlockSpec(shape, lambda *_: index, pipeline_mode=pl.Buffered(1))


def kernel(x, meta_tokens, pre_norm_w, w_in, ret_decay_fwd, ret_decay_bwd, ret_norm_w, w_ret_branch, attn_sink,
           w_attn_branch, w_out, post_norm_w):
    B = x.shape[0]
    assert x.shape == (B, SEQ, D_MODEL) and pre_norm_w.shape[0] == 1 and w_in.shape == (1, D_MODEL, D_IN)
    cpt = CHUNKS_PER_TILE
    tm = cpt * CHUNK
    nt = N_CHUNKS // cpt

    w_in = w_in.astype(F32)
    w_rb, w_ab, w_o = (w.astype(F32) for w in (w_ret_branch, w_attn_branch, w_out))
    pre_w = pre_norm_w.astype(F32)
    post_w = post_norm_w.astype(F32)
    ret_nw = ret_norm_w.astype(F32)
    half = ATT_HD // 2
    inv = ROPE_THETA ** (-jnp.arange(half, dtype=F32) * 2.0 / ATT_HD)
    inv = jnp.concatenate([inv, inv])[None, :]
    dec_f = jnp.broadcast_to(ret_decay_fwd[0].astype(F32)[:, None], (RET_HEADS, RET_DV))
    dec_b = jnp.broadcast_to(ret_decay_bwd[0].astype(F32)[:, None], (RET_HEADS, RET_DV))
    sink = jnp.broadcast_to(attn_sink[0].astype(F32)[:, None], (ATT_HEADS, ATT_HD))

    params = pltpu.CompilerParams(dimension_semantics=("arbitrary", "arbitrary"),
                                  vmem_limit_bytes=V7X_VMEM_LIMIT_BYTES)
    rope_scratch = [pltpu.VMEM((tm, ATT_HD), F32), pltpu.VMEM((tm, ATT_HD), F32)]
    state_scratch = pltpu.VMEM((RET_HEADS, RET_DK, RET_DV), F32)
    weight_stage = [pltpu.VMEM((2, D_MODEL, W_CHUNK), F32), pltpu.SemaphoreType.DMA((2,))]
    hbm = pl.BlockSpec(memory_space=pl.ANY)

    def win_window(off, width):
        assert off % width == 0
        return _resident((None, D_MODEL, width), (0, 0, off // width))

    kvf0, km, vmt = pl.pallas_call(
        _meta_kernel,
        grid=(1,),
        in_specs=[_resident((N_META, D_MODEL)), _resident((1, D_MODEL)), win_window(OFF_RK, RET_QK),
                  win_window(OFF_RV, RET_V), win_window(OFF_AK, 2 * ATT_KV), _resident((1, ATT_HD)),
                  _resident((RET_HEADS, RET_DV))],
        out_specs=(pl.BlockSpec((RET_HEADS, RET_DK, RET_DV), lambda i: (0, 0, 0)),
                   pl.BlockSpec((N_META, ATT_KV), lambda i: (0, 0)),
                   pl.BlockSpec((ATT_KV, N_META), lambda i: (0, 0))),
        out_shape=(jax.ShapeDtypeStruct((RET_HEADS, RET_DK, RET_DV), F32),
                   jax.ShapeDtypeStruct((N_META, ATT_KV), BF16),
                   jax.ShapeDtypeStruct((ATT_KV, N_META), BF16)),
        compiler_params=pltpu.CompilerParams(dimension_semantics=("arbitrary",),
                                             vmem_limit_bytes=V7X_VMEM_LIMIT_BYTES),
        name="meta",
    )(meta_tokens.astype(F32), pre_w, w_in, w_in, w_in, inv, dec_f)

    kcpt = KV_CHUNKS_PER_TILE
    ktm, knt = kcpt * CHUNK, N_CHUNKS // kcpt
    rkv, ak, avt, sb, u_norm = pl.pallas_call(
        functools.partial(_kv_kernel, cpt=kcpt),
        grid=(B, knt),
        in_specs=[pl.BlockSpec((None, ktm, D_MODEL), lambda b, t: (b, knt - 1 - t, 0)), _resident((1, D_MODEL)), hbm,
                  _resident((1, ATT_HD)), _resident((RET_HEADS, RET_DV))],
        out_specs=(
            pl.BlockSpec((None, ktm, RKV_COLS), lambda b, t: (b, knt - 1 - t, 0)),
            pl.BlockSpec((None, ktm, ATT_KV), lambda b, t: (b, knt - 1 - t, 0)),
            pl.BlockSpec((None, ATT_KV, ktm), lambda b, t: (b, 0, knt - 1 - t)),
            pl.BlockSpec((None, ktm // RET_CHUNK, RET_HEADS, RET_DK, RET_DV), lambda b, t: (b, knt - 1 - t, 0, 0, 0)),
            pl.BlockSpec((None, ktm, D_MODEL), lambda b, t: (b, knt - 1 - t, 0)),
        ),
        out_shape=(jax.ShapeDtypeStruct((B, SEQ, RKV_COLS), BF16),
                   jax.ShapeDtypeStruct((B, SEQ, ATT_KV), BF16),
                   jax.ShapeDtypeStruct((B, ATT_KV, SEQ), BF16),
                   jax.ShapeDtypeStruct((B, N_RET_CHUNKS, RET_HEADS, RET_DK, RET_DV), BF16),
                   jax.ShapeDtypeStruct((B, SEQ, D_MODEL), BF16)),
        scratch_shapes=[state_scratch, pltpu.VMEM((ktm, ATT_HD), F32), pltpu.VMEM((ktm, ATT_HD), F32)] + [pltpu.VMEM((D_MODEL, KV_COLS), BF16)] + weight_stage,
        compiler_params=params,
        name="kv",
    )(x, pre_w, w_in, inv, dec_b)

    n_tiles = B * nt

    def tile(s):
        s = jnp.clip(s, 0, n_tiles - 1)
        return s // nt, s % nt

    def rows(width, shift=0):
        return pl.BlockSpec((None, tm, width), lambda s: (*tile(s + shift), 0))

    def chunk_rows(width, pick):
        def index(s):
            b, t = tile(s)
            return b, pick(t), 0
        return pl.BlockSpec((None, CHUNK, width), index)

    def chunk_cols(pick):
        def index(s):
            b, t = tile(s)
            return b, 0, pick(t)
        return pl.BlockSpec((None, ATT_KV, CHUNK), index)

    prev_chunk = lambda t: jnp.maximum(t * cpt - 1, 0)
    next_chunk = lambda t: jnp.minimum((t + 1) * cpt, N_CHUNKS - 1)
    out = pl.pallas_call(
        functools.partial(_main_kernel, cpt=cpt, n_tiles=n_tiles, tiles_per_batch=nt),
        grid=(n_tiles + 1,),
        in_specs=[
            rows(D_MODEL),
            rows(D_MODEL, -1),
            _resident((1, D_MODEL)),
            _resident((1, RET_V)),
            hbm, hbm, hbm, hbm,
            _resident((1, ATT_HD)),
            _resident((RET_HEADS, RET_DV)),
            _resident((RET_HEADS, RET_DV)),
            _resident((ATT_HEADS, ATT_HD)),
            rows(RKV_COLS),
            chunk_rows(ATT_KV, prev_chunk),
            rows(ATT_KV),
            chunk_rows(ATT_KV, next_chunk),
            chunk_cols(prev_chunk),
            pl.BlockSpec((None, ATT_KV, tm), lambda s: (tile(s)[0], 0, tile(s)[1])),
            chunk_cols(next_chunk),
            pl.BlockSpec((None, tm // RET_CHUNK, RET_HEADS, RET_DK, RET_DV), lambda s: (*tile(s), 0, 0, 0)),
            _resident((RET_HEADS, RET_DK, RET_DV)),
            _resident((N_META, ATT_KV)),
            _resident((ATT_KV, N_META)),
        ],
        out_specs=rows(D_MODEL, -1),
        out_shape=jax.ShapeDtypeStruct((B, SEQ, D_MODEL), x.dtype),
        scratch_shapes=(
            [state_scratch] + rope_scratch
            + [pltpu.VMEM((N_KEYS, Q_ROWS), F32),
               pltpu.VMEM((tm, ATT_Q), BF16),
               pltpu.VMEM((tm, RET_V), F32),
               pltpu.VMEM((tm, ATT_Q), F32)]
            + [pltpu.VMEM((tm, D_MODEL), F32)] * 4
            + [pltpu.VMEM((tm, D_MODEL), BF16)] * 3
            + [pltpu.VMEM((RET_HEADS, RET_CHUNK, RET_CHUNK), F32),
               pltpu.VMEM((RET_HEADS, RET_CHUNK, 2 * RET_DK), F32),
               pltpu.VMEM((RET_HEADS, RET_CHUNK, RET_DK), F32)]
            + [pltpu.VMEM((D_MODEL, MAIN_COLS), BF16)]
            + [pltpu.VMEM((D_MODEL, D_MODEL), BF16)] * 3
            + weight_stage
            + [pltpu.VMEM((tm, D_MODEL), F32)]
        ),
        compiler_params=pltpu.CompilerParams(dimension_semantics=("arbitrary",),
                                             vmem_limit_bytes=V7X_VMEM_LIMIT_BYTES),
        name="main",
    )(u_norm, x, post_w, ret_nw, w_in, w_rb, w_ab, w_o, inv, dec_f, dec_b, sink,
      rkv, ak, ak, ak, avt, avt, avt, sb, kvf0, km, vmt)
    return out
```

```python
import functools

import jax
import jax.numpy as jnp
from jax import lax
from jax.experimental import pallas as pl
from jax.experimental.pallas import tpu as pltpu

D_MODEL = 1024
SEQ = 8192
N_META = 16
CHUNK = 128
RET_HEADS = 4
RET_DK = 128
RET_DV = 256
ATT_HEADS = 8
ATT_GROUPS = 2
ATT_REP = ATT_HEADS // ATT_GROUPS
ATT_HD = 128
ROPE_THETA = 10000.0
EPS = 1e-6
NEG_INF = -1e30
RET_QK = RET_HEADS * RET_DK
RET_V = RET_HEADS * RET_DV
ATT_Q = ATT_HEADS * ATT_HD
ATT_KV = ATT_GROUPS * ATT_HD
D_IN = 2 * RET_QK + 2 * RET_V + 2 * ATT_Q + 2 * ATT_KV + 2 * D_MODEL
N_CHUNKS = SEQ // CHUNK
N_BAND = 3 * CHUNK
N_KEYS = N_BAND + N_META
Q_ROWS = ATT_REP * CHUNK

OFF_RQ = 0
OFF_RK = OFF_RQ + RET_QK
OFF_RV = OFF_RK + RET_QK
OFF_RG = OFF_RV + RET_V
OFF_AQ = OFF_RG + RET_V
OFF_AK = OFF_AQ + ATT_Q
OFF_AV = OFF_AK + ATT_KV
OFF_AG = OFF_AV + ATT_KV
OFF_GR = OFF_AG + ATT_Q
OFF_GA = OFF_GR + D_MODEL
RKV_COLS = RET_QK + RET_V

W_CHUNK = 512
KV_SRC = tuple(range(OFF_RK, OFF_RG, W_CHUNK)) + tuple(range(OFF_AK, OFF_AG, W_CHUNK))
KV_RK, KV_RV, KV_AKV = 0, RET_QK, RET_QK + RET_V
KV_COLS = len(KV_SRC) * W_CHUNK
MAIN_SEGMENTS = (("rq", OFF_RQ, RET_QK), ("aq", OFF_AQ, ATT_Q), ("rg", OFF_RG, RET_V), ("ag", OFF_AG, ATT_Q),
                 ("gr", OFF_GR, D_MODEL), ("ga", OFF_GA, D_MODEL))
MAIN_SRC = tuple(c for _, off, width in MAIN_SEGMENTS for c in range(off, off + width, W_CHUNK))
MAIN_OFF = {}
_o = 0
for _name, _, _width in MAIN_SEGMENTS:
    MAIN_OFF[_name] = _o
    _o += _width
MAIN_COLS = _o

CHUNKS_PER_TILE = 4
KV_CHUNKS_PER_TILE = 8
RET_CHUNK = 2 * CHUNK
N_RET_CHUNKS = SEQ // RET_CHUNK
COL_BLOCK = 256
V7X_VMEM_LIMIT_BYTES = 56 * 1024 * 1024

F32 = jnp.float32
BF16 = jnp.bfloat16


def _rms_norm(x, w):
    return x * lax.rsqrt(jnp.mean(x * x, axis=-1, keepdims=True) + EPS) * w


def _log_sigmoid(x):
    return jnp.minimum(x, 0.0) - jnp.log(1.0 + jnp.exp(-jnp.abs(x)))


def _sign_fold(sin):
    lane = lax.broadcasted_iota(jnp.int32, sin.shape, 1)
    return jnp.where(lane < ATT_HD // 2, -sin, sin)


def _rope_tables(rows, inv):
    ang = lax.broadcasted_iota(jnp.int32, (rows, ATT_HD), 0).astype(F32) * inv
    return jnp.cos(ang), _sign_fold(jnp.sin(ang))


def _fill_rope_base(inv, cosb_ref, sinb_ref):
    ang = lax.broadcasted_iota(jnp.int32, cosb_ref.shape, 0).astype(F32) * inv
    cosb_ref[...] = jnp.cos(ang)
    sinb_ref[...] = jnp.sin(ang)


def _rope_tables_from_base(pos0, inv, cosb_ref, sinb_ref):
    base = pos0.astype(F32) * inv
    ca, sa = jnp.cos(base), jnp.sin(base)
    cb, sb = cosb_ref[...], sinb_ref[...]
    return ca * cb - sa * sb, _sign_fold(sa * cb + ca * sb)


def _rope(t, cos, sin_signed):
    return t * cos + pltpu.roll(t, ATT_HD // 2, axis=1) * sin_signed


def _dot(a, b):
    return jnp.dot(a, b, preferred_element_type=F32)


def _dot_nt(a, b):
    return lax.dot_general(a, b, (((1,), (1,)), ((), ())), preferred_element_type=F32)


def _dot_tn(a, b):
    return lax.dot_general(a, b, (((0,), (0,)), ((), ())), preferred_element_type=F32)


def _row_index(shape):
    return lax.broadcasted_iota(jnp.int32, shape, 0).astype(F32)


def _weight_chunk_copy(src_ref, col, stage_ref, sem_ref, slot):
    return pltpu.make_async_copy(src_ref.at[0, :, pl.ds(col, W_CHUNK)], stage_ref.at[slot], sem_ref.at[slot])


def _load_weights_bf16(jobs, stage_ref, sem_ref):
    copies = [_weight_chunk_copy(src, col, stage_ref, sem_ref, i % 2) for i, (src, col, _, _) in enumerate(jobs)]
    copies[0].start()
    for i, (_, _, dst, dcol) in enumerate(jobs):
        if i + 1 < len(jobs):
            copies[i + 1].start()
        copies[i].wait()
        dst[:, dcol:dcol + W_CHUNK] = stage_ref[i % 2].astype(BF16)


def _meta_kernel(meta_ref, prew_ref, wrk_ref, wrv_ref, wakv_ref, inv_ref, decf_ref, kvf0_ref, km_ref, vmt_ref):
    u = _rms_norm(meta_ref[...], prew_ref[...]).astype(BF16)
    zk, zv, za = (_dot(u, w[...].astype(BF16)) for w in (wrk_ref, wrv_ref, wakv_ref))
    cos, sin = _rope_tables(N_META, inv_ref[...])
    kdec_rows = (N_META - 1) - _row_index((N_META, RET_DK))
    for h in range(RET_HEADS):
        lg = _log_sigmoid(decf_ref[h:h + 1, :RET_DK])
        k = _rope(zk[:, h * RET_DK:(h + 1) * RET_DK], cos, sin) * (RET_DK ** -0.5)
        k = (k * jnp.exp(kdec_rows * lg)).astype(BF16)
        v = zv[:, h * RET_DV:(h + 1) * RET_DV].astype(BF16)
        kvf0_ref[h] = _dot_tn(k, v)
    for g in range(ATT_GROUPS):
        km_ref[:, g * ATT_HD:(g + 1) * ATT_HD] = _rope(za[:, g * ATT_HD:(g + 1) * ATT_HD], cos, sin).astype(BF16)
    vmt_ref[...] = za[:, ATT_KV:].T.astype(BF16)


def _kv_kernel(x_ref, prew_ref, win_ref, inv_ref, decb_ref, rkv_ref, ak_ref, avt_ref, sb_ref,
               state_ref, cosb_ref, sinb_ref, wkv_ref, stage_ref, sem_ref, *, cpt):
    t = pl.program_id(1)
    tile = pl.num_programs(1) - 1 - t
    tm = cpt * CHUNK

    @pl.when((pl.program_id(0) == 0) & (t == 0))
    def _():
        _load_weights_bf16([(win_ref, col, wkv_ref, i * W_CHUNK) for i, col in enumerate(KV_SRC)],
                           stage_ref, sem_ref)
        _fill_rope_base(inv_ref[...], cosb_ref, sinb_ref)

    @pl.when(t == 0)
    def _():
        state_ref[...] = jnp.zeros_like(state_ref)

    u = _rms_norm(x_ref[...], prew_ref[...]).astype(BF16)
    zk = _dot(u, wkv_ref[:, KV_RK:KV_RV])
    zv = _dot(u, wkv_ref[:, KV_RV:KV_AKV])
    za = _dot(u, wkv_ref[:, KV_AKV:])
    cos, sin = _rope_tables_from_base(N_META + tile * tm, inv_ref[...], cosb_ref, sinb_ref)
    rk = [_rope(zk[:, h * RET_DK:(h + 1) * RET_DK], cos, sin) * (RET_DK ** -0.5) for h in range(RET_HEADS)]
    for h in range(RET_HEADS):
        rkv_ref[:, h * RET_DK:(h + 1) * RET_DK] = rk[h].astype(BF16)
    rkv_ref[:, RET_QK:] = zv.astype(BF16)
    for g in range(ATT_GROUPS):
        ak_ref[:, g * ATT_HD:(g + 1) * ATT_HD] = _rope(za[:, g * ATT_HD:(g + 1) * ATT_HD], cos, sin).astype(BF16)
    avt_ref[...] = za[:, ATT_KV:].T.astype(BF16)

    rows_k = _row_index((RET_CHUNK, RET_DK))
    for h in range(RET_HEADS):
        lg = _log_sigmoid(decb_ref[h:h + 1, :])
        kdec = jnp.exp(rows_k * lg[:, :RET_DK])
        cdec = jnp.exp(RET_CHUNK * lg)
        for rc in reversed(range(tm // RET_CHUNK)):
            r0 = rc * RET_CHUNK
            state = state_ref[h]
            sb_ref[rc, h] = state.astype(BF16)
            k = (rk[h][r0:r0 + RET_CHUNK] * kdec).astype(BF16)
            v = zv[r0:r0 + RET_CHUNK, h * RET_DV:(h + 1) * RET_DV].astype(BF16)
            state_ref[h] = cdec * state + _dot_tn(k, v)


def _fill_band_bias(bias_ref):
    kk = lax.broadcasted_iota(jnp.int32, bias_ref.shape, 0)
    qi = lax.broadcasted_iota(jnp.int32, bias_ref.shape, 1) & (CHUNK - 1)
    visible = (kk >= N_BAND) | ((kk >= qi) & (kk <= qi + 2 * CHUNK))
    bias_ref[...] = jnp.where(visible, 0.0, NEG_INF)


def _fill_decay_tables(decf_ref, decb_ref, dmat_ref, qdec_ref, kdf_ref):
    rows = _row_index((RET_CHUNK, RET_CHUNK))
    rel = rows - lax.broadcasted_iota(jnp.int32, (RET_CHUNK, RET_CHUNK), 1).astype(F32)
    rows_k = _row_index((RET_CHUNK, RET_DK))
    for h in range(RET_HEADS):
        lgf = _log_sigmoid(decf_ref[h:h + 1, :])
        lgb = _log_sigmoid(decb_ref[h:h + 1, :])
        dmat_ref[h] = jnp.where(rel >= 0, jnp.exp(jnp.maximum(rel, 0.0) * lgf),
                                jnp.exp(jnp.maximum(-rel, 0.0) * lgb))
        lgf_k, lgb_k = lgf[:, :RET_DK], lgb[:, :RET_DK]
        qdec_ref[h, :, :RET_DK] = jnp.exp((rows_k + 1.0) * lgf_k)
        qdec_ref[h, :, RET_DK:] = jnp.exp((RET_CHUNK - rows_k) * lgb_k)
        kdf_ref[h] = jnp.exp((RET_CHUNK - 1.0 - rows_k) * lgf_k)


def _main_kernel(x_ref, prew_ref, postw_ref, retnw_ref, win_ref, wrb_hbm_ref, wab_hbm_ref, wo_hbm_ref, inv_ref,
                 decf_ref, decb_ref, sink_ref, rkv_ref, akp_ref, akc_ref, akn_ref, avp_ref,
                 avc_ref, avn_ref, sb_ref, kvf0_ref, km_ref, vmt_ref, out_ref,
                 sf_ref, cosb_ref, sinb_ref, bias_ref, u_ref, rq_ref, aq_ref, or_ref, oa_ref, grg_ref, gag_ref,
                 ggr_ref, gga_ref, lhs_r_ref, lhs_a_ref, mix_ref, dmat_ref, qdec_ref, kdf_ref, wmain_ref, wrb_ref,
                 wab_ref, wo_ref, stage_ref, sem_ref, *, cpt):
    t = pl.program_id(1)
    tm = cpt * CHUNK

    @pl.when((pl.program_id(0) == 0) & (t == 0))
    def _():
        jobs = [(win_ref, col, wmain_ref, i * W_CHUNK) for i, col in enumerate(MAIN_SRC)]
        for src, dst in ((wrb_hbm_ref, wrb_ref), (wab_hbm_ref, wab_ref), (wo_hbm_ref, wo_ref)):
            jobs += [(src, c, dst, c) for c in range(0, D_MODEL, W_CHUNK)]
        _load_weights_bf16(jobs, stage_ref, sem_ref)
        _fill_rope_base(inv_ref[...], cosb_ref, sinb_ref)
        _fill_band_bias(bias_ref)
        _fill_decay_tables(decf_ref, decb_ref, dmat_ref, qdec_ref, kdf_ref)

    @pl.when(t == 0)
    def _():
        sf_ref[...] = kvf0_ref[...]

    for r0 in range(0, tm, CHUNK):
        u_ref[r0:r0 + CHUNK] = _rms_norm(x_ref[r0:r0 + CHUNK], prew_ref[...]).astype(BF16)
    cos, sin = _rope_tables_from_base(N_META + t * tm, inv_ref[...], cosb_ref, sinb_ref)

    def proj(name, c0):
        col = MAIN_OFF[name] + c0
        return _dot(u_ref[...], wmain_ref[:, col:col + COL_BLOCK])

    heads_per_block = COL_BLOCK // ATT_HD
    for c0 in range(0, RET_QK, COL_BLOCK):
        z = proj("rq", c0)
        for i in range(heads_per_block):
            c = c0 + i * RET_DK
            rq_ref[:, c:c + RET_DK] = _rope(z[:, i * RET_DK:(i + 1) * RET_DK], cos, sin)
    for c0 in range(0, ATT_Q, COL_BLOCK):
        z = proj("aq", c0)
        for i in range(heads_per_block):
            c = c0 + i * ATT_HD
            aq_ref[:, c:c + ATT_HD] = (_rope(z[:, i * ATT_HD:(i + 1) * ATT_HD], cos, sin) * (ATT_HD ** -0.5)
                                       ).astype(BF16)

    gate_jobs = (("rg", jax.nn.silu, grg_ref), ("ag", jax.nn.silu, gag_ref),
                 ("gr", jax.nn.sigmoid, ggr_ref), ("ga", jax.nn.sigmoid, gga_ref))
    gate_blocks = [(j, c) for j in range(len(gate_jobs)) for c in range(0, D_MODEL, COL_BLOCK)]
    assert len(gate_blocks) % cpt == 0

    def emit_gate_blocks(n):
        for _ in range(n):
            j, c = gate_blocks.pop(0)
            name, act, ref = gate_jobs[j]
            ref[:, c:c + COL_BLOCK] = act(proj(name, c))

    cdf = [jnp.exp(RET_CHUNK * _log_sigmoid(decf_ref[h:h + 1, :])) for h in range(RET_HEADS)]

    k_cat = jnp.concatenate([akp_ref[...], akc_ref[...], akn_ref[...]], axis=0)
    vt_cat = jnp.concatenate([avp_ref[...], avc_ref[...], avn_ref[...]], axis=1)
    sink_rows = [jnp.concatenate([sink_ref[g * ATT_REP + r:g * ATT_REP + r + 1, :] for r in range(ATT_REP)],
                                 axis=1) for g in range(ATT_GROUPS)]

    def retention(rc):
        r0 = rc * RET_CHUNK
        rs = slice(r0, r0 + RET_CHUNK)
        for h in range(RET_HEADS):
            q = rq_ref[rs, h * RET_DK:(h + 1) * RET_DK]
            k = rkv_ref[rs, h * RET_DK:(h + 1) * RET_DK]
            v = rkv_ref[rs, RET_QK + h * RET_DV:RET_QK + (h + 1) * RET_DV]
            s = _dot_nt(q.astype(BF16), k) * dmat_ref[h]
            sf = sf_ref[h]
            q_cross = (jnp.concatenate([q, q], axis=1) * qdec_ref[h]).astype(BF16)
            s_cross = jnp.concatenate([sf.astype(BF16), sb_ref[rc, h]], axis=0)
            o = _dot(s.astype(BF16), v) + _dot(q_cross, s_cross)
            sf_ref[h] = cdf[h] * sf + _dot_tn((k.astype(F32) * kdf_ref[h]).astype(BF16), v)
            mu = jnp.mean(o, axis=-1, keepdims=True)
            d = o - mu
            var = jnp.mean(d * d, axis=-1, keepdims=True)
            or_ref[rs, h * RET_DV:(h + 1) * RET_DV] = d * lax.rsqrt(var + EPS)

    gates_per_pair = len(gate_blocks) // (cpt * ATT_GROUPS)
    pairs = [(lc, g) for lc in range(cpt) for g in range(ATT_GROUPS)]
    probs, denoms = {}, {}
    for lc, g in pairs:
        r0 = lc * CHUNK
        chunk = t * cpt + lc
        if g == 0 and r0 % RET_CHUNK == 0:
            retention(r0 // RET_CHUNK)
        gs = slice(g * ATT_HD, (g + 1) * ATT_HD)
        qs = jnp.concatenate([aq_ref[r0:r0 + CHUNK, (g * ATT_REP + r) * ATT_HD:(g * ATT_REP + r + 1) * ATT_HD]
                              for r in range(ATT_REP)], axis=0)
        k_all = jnp.concatenate([k_cat[r0:r0 + N_BAND, gs], km_ref[:, gs]], axis=0)
        s = _dot_nt(k_all, qs) + bias_ref[...]
        emit_gate_blocks(gates_per_pair)
        parts = [s[0:CHUNK], s[CHUNK:2 * CHUNK], s[2 * CHUNK:N_BAND], s[N_BAND:]]
        if lc == 0:
            parts[0] = jnp.where(chunk > 0, parts[0], NEG_INF)
        if lc == cpt - 1:
            parts[2] = jnp.where(chunk < N_CHUNKS - 1, parts[2], NEG_INF)
        s = jnp.concatenate(parts, axis=0)
        sk = sink_rows[g]
        m = jnp.maximum(jnp.max(s, axis=0, keepdims=True), sk)
        p = jnp.exp(s - m)
        denoms[lc, g] = jnp.sum(p, axis=0, keepdims=True) + jnp.exp(sk - m)
        probs[lc, g] = p.astype(BF16)
    for lc, g in pairs:
        r0 = lc * CHUNK
        gs = slice(g * ATT_HD, (g + 1) * ATT_HD)
        vt_all = jnp.concatenate([vt_cat[gs, r0:r0 + N_BAND], vmt_ref[gs, :]], axis=1)
        ot = _dot(vt_all, probs[lc, g]) / denoms[lc, g]
        for r in range(ATT_REP):
            c = (g * ATT_REP + r) * ATT_HD
            oa_ref[r0:r0 + CHUNK, c:c + ATT_HD] = ot[:, r * CHUNK:(r + 1) * CHUNK].T

    for r0 in range(0, tm, CHUNK):
        rs = slice(r0, r0 + CHUNK)
        lhs_r_ref[rs] = (or_ref[rs] * retnw_ref[...] * grg_ref[rs]).astype(BF16)
        lhs_a_ref[rs] = (oa_ref[rs] * gag_ref[rs]).astype(BF16)
    for c0 in range(0, D_MODEL, COL_BLOCK):
        cs = slice(c0, c0 + COL_BLOCK)
        y_r = _dot(lhs_r_ref[...], wrb_ref[:, cs])
        y_a = _dot(lhs_a_ref[...], wab_ref[:, cs])
        mix_ref[:, cs] = (ggr_ref[:, cs] * y_r + gga_ref[:, cs] * y_a).astype(BF16)
    for c0 in range(0, D_MODEL, COL_BLOCK):
        cs = slice(c0, c0 + COL_BLOCK)
        out_ref[:, cs] = _dot(mix_ref[...], wo_ref[:, cs])
    for r0 in range(0, tm, CHUNK):
        rs = slice(r0, r0 + CHUNK)
        out_ref[rs] = x_ref[rs] + _rms_norm(out_ref[rs], postw_ref[...])


def _resident(shape, index=None):
    nd = len(shape)
    index = (0,) * nd if index is None else index
    return pl.BlockSpec(shape, lambda *_: index, pipeline_mode=pl.Buffered(1))


def kernel(x, meta_tokens, pre_norm_w, w_in, ret_decay_fwd, ret_decay_bwd, ret_norm_w, w_ret_branch, attn_sink,
           w_attn_branch, w_out, post_norm_w):
    B = x.shape[0]
    assert x.shape == (B, SEQ, D_MODEL) and pre_norm_w.shape[0] == 1 and w_in.shape == (1, D_MODEL, D_IN)
    cpt = CHUNKS_PER_TILE
    tm = cpt * CHUNK
    nt = N_CHUNKS // cpt

    w_in = w_in.astype(F32)
    w_rb, w_ab, w_o = (w.astype(F32) for w in (w_ret_branch, w_attn_branch, w_out))
    pre_w = pre_norm_w.astype(F32)
    post_w = post_norm_w.astype(F32)
    ret_nw = ret_norm_w.astype(F32)
    half = ATT_HD // 2
    inv = ROPE_THETA ** (-jnp.arange(half, dtype=F32) * 2.0 / ATT_HD)
    inv = jnp.concatenate([inv, inv])[None, :]
    dec_f = jnp.broadcast_to(ret_decay_fwd[0].astype(F32)[:, None], (RET_HEADS, RET_DV))
    dec_b = jnp.broadcast_to(ret_decay_bwd[0].astype(F32)[:, None], (RET_HEADS, RET_DV))
    sink = jnp.broadcast_to(attn_sink[0].astype(F32)[:, None], (ATT_HEADS, ATT_HD))

    params = pltpu.CompilerParams(dimension_semantics=("arbitrary", "arbitrary"),
                                  vmem_limit_bytes=V7X_VMEM_LIMIT_BYTES)
    rope_scratch = [pltpu.VMEM((tm, ATT_HD), F32), pltpu.VMEM((tm, ATT_HD), F32)]
    state_scratch = pltpu.VMEM((RET_HEADS, RET_DK, RET_DV), F32)
    weight_stage = [pltpu.VMEM((2, D_MODEL, W_CHUNK), F32), pltpu.SemaphoreType.DMA((2,))]
    hbm = pl.BlockSpec(memory_space=pl.ANY)

    def win_window(off, width):
        assert off % width == 0
        return _resident((None, D_MODEL, width), (0, 0, off // width))

    kvf0, km, vmt = pl.pallas_call(
        _meta_kernel,
        grid=(1,),
        in_specs=[_resident((N_META, D_MODEL)), _resident((1, D_MODEL)), win_window(OFF_RK, RET_QK),
                  win_window(OFF_RV, RET_V), win_window(OFF_AK, 2 * ATT_KV), _resident((1, ATT_HD)),
                  _resident((RET_HEADS, RET_DV))],
        out_specs=(pl.BlockSpec((RET_HEADS, RET_DK, RET_DV), lambda i: (0, 0, 0)),
                   pl.BlockSpec((N_META, ATT_KV), lambda i: (0, 0)),
                   pl.BlockSpec((ATT_KV, N_META), lambda i: (0, 0))),
        out_shape=(jax.ShapeDtypeStruct((RET_HEADS, RET_DK, RET_DV), F32),
                   jax.ShapeDtypeStruct((N_META, ATT_KV), BF16),
                   jax.ShapeDtypeStruct((ATT_KV, N_META), BF16)),
        compiler_params=pltpu.CompilerParams(dimension_semantics=("arbitrary",),
                                             vmem_limit_bytes=V7X_VMEM_LIMIT_BYTES),
        name="meta",
    )(meta_tokens.astype(F32), pre_w, w_in, w_in, w_in, inv, dec_f)

    kcpt = KV_CHUNKS_PER_TILE
    ktm, knt = kcpt * CHUNK, N_CHUNKS // kcpt
    rkv, ak, avt, sb = pl.pallas_call(
        functools.partial(_kv_kernel, cpt=kcpt),
        grid=(B, knt),
        in_specs=[pl.BlockSpec((None, ktm, D_MODEL), lambda b, t: (b, knt - 1 - t, 0)), _resident((1, D_MODEL)), hbm,
                  _resident((1, ATT_HD)), _resident((RET_HEADS, RET_DV))],
        out_specs=(
            pl.BlockSpec((None, ktm, RKV_COLS), lambda b, t: (b, knt - 1 - t, 0)),
            pl.BlockSpec((None, ktm, ATT_KV), lambda b, t: (b, knt - 1 - t, 0)),
            pl.BlockSpec((None, ATT_KV, ktm), lambda b, t: (b, 0, knt - 1 - t)),
            pl.BlockSpec((None, ktm // RET_CHUNK, RET_HEADS, RET_DK, RET_DV), lambda b, t: (b, knt - 1 - t, 0, 0, 0)),
        ),
        out_shape=(jax.ShapeDtypeStruct((B, SEQ, RKV_COLS), BF16),
                   jax.ShapeDtypeStruct((B, SEQ, ATT_KV), BF16),
                   jax.ShapeDtypeStruct((B, ATT_KV, SEQ), BF16),
                   jax.ShapeDtypeStruct((B, N_RET_CHUNKS, RET_HEADS, RET_DK, RET_DV), BF16)),
        scratch_shapes=[state_scratch, pltpu.VMEM((ktm, ATT_HD), F32), pltpu.VMEM((ktm, ATT_HD), F32)] + [pltpu.VMEM((D_MODEL, KV_COLS), BF16)] + weight_stage,
        compiler_params=params,
        name="kv",
    )(x, pre_w, w_in, inv, dec_b)

    prev_chunk = lambda t: jnp.maximum(t * cpt - 1, 0)
    next_chunk = lambda t: jnp.minimum((t + 1) * cpt, N_CHUNKS - 1)
    out = pl.pallas_call(
        functools.partial(_main_kernel, cpt=cpt),
        grid=(B, nt),
        in_specs=[
            pl.BlockSpec((None, tm, D_MODEL), lambda b, t: (b, t, 0)),
            _resident((1, D_MODEL)),
            _resident((1, D_MODEL)),
            _resident((1, RET_V)),
            hbm, hbm, hbm, hbm,
            _resident((1, ATT_HD)),
            _resident((RET_HEADS, RET_DV)),
            _resident((RET_HEADS, RET_DV)),
            _resident((ATT_HEADS, ATT_HD)),
            pl.BlockSpec((None, tm, RKV_COLS), lambda b, t: (b, t, 0)),
            pl.BlockSpec((None, CHUNK, ATT_KV), lambda b, t: (b, prev_chunk(t), 0)),
            pl.BlockSpec((None, tm, ATT_KV), lambda b, t: (b, t, 0)),
            pl.BlockSpec((None, CHUNK, ATT_KV), lambda b, t: (b, next_chunk(t), 0)),
            pl.BlockSpec((None, ATT_KV, CHUNK), lambda b, t: (b, 0, prev_chunk(t))),
            pl.BlockSpec((None, ATT_KV, tm), lambda b, t: (b, 0, t)),
            pl.BlockSpec((None, ATT_KV, CHUNK), lambda b, t: (b, 0, next_chunk(t))),
            pl.BlockSpec((None, tm // RET_CHUNK, RET_HEADS, RET_DK, RET_DV), lambda b, t: (b, t, 0, 0, 0)),
            _resident((RET_HEADS, RET_DK, RET_DV)),
            _resident((N_META, ATT_KV)),
            _resident((ATT_KV, N_META)),
        ],
        out_specs=pl.BlockSpec((None, tm, D_MODEL), lambda b, t: (b, t, 0)),
        out_shape=jax.ShapeDtypeStruct((B, SEQ, D_MODEL), x.dtype),
        scratch_shapes=(
            [state_scratch] + rope_scratch
            + [pltpu.VMEM((N_KEYS, Q_ROWS), F32),
               pltpu.VMEM((tm, D_MODEL), BF16),
               pltpu.VMEM((tm, RET_QK), F32),
               pltpu.VMEM((tm, ATT_Q), BF16),
               pltpu.VMEM((tm, RET_V), F32),
               pltpu.VMEM((tm, ATT_Q), F32)]
            + [pltpu.VMEM((tm, D_MODEL), F32)] * 4
            + [pltpu.VMEM((tm, D_MODEL), BF16)] * 3
            + [pltpu.VMEM((RET_HEADS, RET_CHUNK, RET_CHUNK), F32),
               pltpu.VMEM((RET_HEADS, RET_CHUNK, 2 * RET_DK), F32),
               pltpu.VMEM((RET_HEADS, RET_CHUNK, RET_DK), F32)]
            + [pltpu.VMEM((D_MODEL, MAIN_COLS), BF16)]
            + [pltpu.VMEM((D_MODEL, D_MODEL), BF16)] * 3
            + weight_stage
        ),
        compiler_params=params,
        name="main",
    )(x, pre_w, post_w, ret_nw, w_in, w_rb, w_ab, w_o, inv, dec_f, dec_b, sink,
      rkv, ak, ak, ak, avt, avt, avt, sb, kvf0, km, vmt)
    return out
```

```python
import functools

import jax
import jax.numpy as jnp
from jax import lax
from jax.experimental import pallas as pl
from jax.experimental.pallas import tpu as pltpu

D_MODEL = 1024
SEQ = 8192
N_META = 16
CHUNK = 128
RET_HEADS = 4
RET_DK = 128
RET_DV = 256
ATT_HEADS = 8
ATT_GROUPS = 2
ATT_REP = ATT_HEADS // ATT_GROUPS
ATT_HD = 128
ROPE_THETA = 10000.0
EPS = 1e-6
NEG_INF = -1e30
RET_QK = RET_HEADS * RET_DK
RET_V = RET_HEADS * RET_DV
ATT_Q = ATT_HEADS * ATT_HD
ATT_KV = ATT_GROUPS * ATT_HD
D_IN = 2 * RET_QK + 2 * RET_V + 2 * ATT_Q + 2 * ATT_KV + 2 * D_MODEL
N_CHUNKS = SEQ // CHUNK
N_BAND = 3 * CHUNK
N_KEYS = N_BAND + N_META
Q_ROWS = ATT_REP * CHUNK

OFF_RQ = 0
OFF_RK = OFF_RQ + RET_QK
OFF_RV = OFF_RK + RET_QK
OFF_RG = OFF_RV + RET_V
OFF_AQ = OFF_RG + RET_V
OFF_AK = OFF_AQ + ATT_Q
OFF_AV = OFF_AK + ATT_KV
OFF_AG = OFF_AV + ATT_KV
OFF_GR = OFF_AG + ATT_Q
OFF_GA = OFF_GR + D_MODEL
RKV_COLS = RET_QK + RET_V

W_CHUNK = 512
KV_SRC = tuple(range(OFF_RK, OFF_RG, W_CHUNK)) + tuple(range(OFF_AK, OFF_AG, W_CHUNK))
KV_RK, KV_RV, KV_AKV = 0, RET_QK, RET_QK + RET_V
KV_COLS = len(KV_SRC) * W_CHUNK
MAIN_SEGMENTS = (("rq", OFF_RQ, RET_QK), ("aq", OFF_AQ, ATT_Q), ("rg", OFF_RG, RET_V), ("ag", OFF_AG, ATT_Q),
                 ("gr", OFF_GR, D_MODEL), ("ga", OFF_GA, D_MODEL))
MAIN_SRC = tuple(c for _, off, width in MAIN_SEGMENTS for c in range(off, off + width, W_CHUNK))
MAIN_OFF = {}
_o = 0
for _name, _, _width in MAIN_SEGMENTS:
    MAIN_OFF[_name] = _o
    _o += _width
MAIN_COLS = _o

CHUNKS_PER_TILE = 4
KV_CHUNKS_PER_TILE = 8
RET_CHUNK = 2 * CHUNK
N_RET_CHUNKS = SEQ // RET_CHUNK
COL_BLOCK = 256
V7X_VMEM_LIMIT_BYTES = 56 * 1024 * 1024

F32 = jnp.float32
BF16 = jnp.bfloat16


def _rms_norm(x, w):
    return x * lax.rsqrt(jnp.mean(x * x, axis=-1, keepdims=True) + EPS) * w


def _log_sigmoid(x):
    return jnp.minimum(x, 0.0) - jnp.log(1.0 + jnp.exp(-jnp.abs(x)))


def _sign_fold(sin):
    lane = lax.broadcasted_iota(jnp.int32, sin.shape, 1)
    return jnp.where(lane < ATT_HD // 2, -sin, sin)


def _rope_tables(rows, inv):
    ang = lax.broadcasted_iota(jnp.int32, (rows, ATT_HD), 0).astype(F32) * inv
    return jnp.cos(ang), _sign_fold(jnp.sin(ang))


def _fill_rope_base(inv, cosb_ref, sinb_ref):
    ang = lax.broadcasted_iota(jnp.int32, cosb_ref.shape, 0).astype(F32) * inv
    cosb_ref[...] = jnp.cos(ang)
    sinb_ref[...] = jnp.sin(ang)


def _rope_tables_from_base(pos0, inv, cosb_ref, sinb_ref):
    base = pos0.astype(F32) * inv
    ca, sa = jnp.cos(base), jnp.sin(base)
    cb, sb = cosb_ref[...], sinb_ref[...]
    return ca * cb - sa * sb, _sign_fold(sa * cb + ca * sb)


def _rope(t, cos, sin_signed):
    return t * cos + pltpu.roll(t, ATT_HD // 2, axis=1) * sin_signed


def _dot(a, b):
    return jnp.dot(a, b, preferred_element_type=F32)


def _dot_nt(a, b):
    return lax.dot_general(a, b, (((1,), (1,)), ((), ())), preferred_element_type=F32)


def _dot_tn(a, b):
    return lax.dot_general(a, b, (((0,), (0,)), ((), ())), preferred_element_type=F32)


def _row_index(shape):
    return lax.broadcasted_iota(jnp.int32, shape, 0).astype(F32)


def _weight_chunk_copy(src_ref, col, stage_ref, sem_ref, slot):
    return pltpu.make_async_copy(src_ref.at[0, :, pl.ds(col, W_CHUNK)], stage_ref.at[slot], sem_ref.at[slot])


def _load_weights_bf16(jobs, stage_ref, sem_ref):
    copies = [_weight_chunk_copy(src, col, stage_ref, sem_ref, i % 2) for i, (src, col, _, _) in enumerate(jobs)]
    copies[0].start()
    for i, (_, _, dst, dcol) in enumerate(jobs):
        if i + 1 < len(jobs):
            copies[i + 1].start()
        copies[i].wait()
        dst[:, dcol:dcol + W_CHUNK] = stage_ref[i % 2].astype(BF16)


def _meta_kernel(meta_ref, prew_ref, wrk_ref, wrv_ref, wakv_ref, inv_ref, decf_ref, kvf0_ref, km_ref, vmt_ref):
    u = _rms_norm(meta_ref[...], prew_ref[...]).astype(BF16)
    zk, zv, za = (_dot(u, w[...].astype(BF16)) for w in (wrk_ref, wrv_ref, wakv_ref))
    cos, sin = _rope_tables(N_META, inv_ref[...])
    kdec_rows = (N_META - 1) - _row_index((N_META, RET_DK))
    for h in range(RET_HEADS):
        lg = _log_sigmoid(decf_ref[h:h + 1, :RET_DK])
        k = _rope(zk[:, h * RET_DK:(h + 1) * RET_DK], cos, sin) * (RET_DK ** -0.5)
        k = (k * jnp.exp(kdec_rows * lg)).astype(BF16)
        v = zv[:, h * RET_DV:(h + 1) * RET_DV].astype(BF16)
        kvf0_ref[h] = _dot_tn(k, v)
    for g in range(ATT_GROUPS):
        km_ref[:, g * ATT_HD:(g + 1) * ATT_HD] = _rope(za[:, g * ATT_HD:(g + 1) * ATT_HD], cos, sin).astype(BF16)
    vmt_ref[...] = za[:, ATT_KV:].T.astype(BF16)


def _kv_kernel(x_ref, prew_ref, win_ref, inv_ref, decb_ref, rkv_ref, ak_ref, avt_ref, sb_ref,
               state_ref, cosb_ref, sinb_ref, wkv_ref, stage_ref, sem_ref, *, cpt):
    t = pl.program_id(1)
    tile = pl.num_programs(1) - 1 - t
    tm = cpt * CHUNK

    @pl.when((pl.program_id(0) == 0) & (t == 0))
    def _():
        _load_weights_bf16([(win_ref, col, wkv_ref, i * W_CHUNK) for i, col in enumerate(KV_SRC)],
                           stage_ref, sem_ref)
        _fill_rope_base(inv_ref[...], cosb_ref, sinb_ref)

    @pl.when(t == 0)
    def _():
        state_ref[...] = jnp.zeros_like(state_ref)

    u = _rms_norm(x_ref[...], prew_ref[...]).astype(BF16)
    zk = _dot(u, wkv_ref[:, KV_RK:KV_RV])
    zv = _dot(u, wkv_ref[:, KV_RV:KV_AKV])
    za = _dot(u, wkv_ref[:, KV_AKV:])
    cos, sin = _rope_tables_from_base(N_META + tile * tm, inv_ref[...], cosb_ref, sinb_ref)
    rk = [_rope(zk[:, h * RET_DK:(h + 1) * RET_DK], cos, sin) * (RET_DK ** -0.5) for h in range(RET_HEADS)]
    for h in range(RET_HEADS):
        rkv_ref[:, h * RET_DK:(h + 1) * RET_DK] = rk[h].astype(BF16)
    rkv_ref[:, RET_QK:] = zv.astype(BF16)
    for g in range(ATT_GROUPS):
        ak_ref[:, g * ATT_HD:(g + 1) * ATT_HD] = _rope(za[:, g * ATT_HD:(g + 1) * ATT_HD], cos, sin).astype(BF16)
    avt_ref[...] = za[:, ATT_KV:].T.astype(BF16)

    rows_k = _row_index((RET_CHUNK, RET_DK))
    for h in range(RET_HEADS):
        lg = _log_sigmoid(decb_ref[h:h + 1, :])
        kdec = jnp.exp(rows_k * lg[:, :RET_DK])
        cdec = jnp.exp(RET_CHUNK * lg)
        for rc in reversed(range(tm // RET_CHUNK)):
            r0 = rc * RET_CHUNK
            state = state_ref[h]
            sb_ref[rc, h] = state.astype(BF16)
            k = (rk[h][r0:r0 + RET_CHUNK] * kdec).astype(BF16)
            v = zv[r0:r0 + RET_CHUNK, h * RET_DV:(h + 1) * RET_DV].astype(BF16)
            state_ref[h] = cdec * state + _dot_tn(k, v)


def _fill_band_bias(bias_ref):
    kk = lax.broadcasted_iota(jnp.int32, bias_ref.shape, 0)
    qi = lax.broadcasted_iota(jnp.int32, bias_ref.shape, 1) & (CHUNK - 1)
    visible = (kk >= N_BAND) | ((kk >= qi) & (kk <= qi + 2 * CHUNK))
    bias_ref[...] = jnp.where(visible, 0.0, NEG_INF)


def _fill_decay_tables(decf_ref, decb_ref, dmat_ref, qdec_ref, kdf_ref):
    rows = _row_index((RET_CHUNK, RET_CHUNK))
    rel = rows - lax.broadcasted_iota(jnp.int32, (RET_CHUNK, RET_CHUNK), 1).astype(F32)
    rows_k = _row_index((RET_CHUNK, RET_DK))
    for h in range(RET_HEADS):
        lgf = _log_sigmoid(decf_ref[h:h + 1, :])
        lgb = _log_sigmoid(decb_ref[h:h + 1, :])
        dmat_ref[h] = jnp.where(rel >= 0, jnp.exp(jnp.maximum(rel, 0.0) * lgf),
                                jnp.exp(jnp.maximum(-rel, 0.0) * lgb))
        lgf_k, lgb_k = lgf[:, :RET_DK], lgb[:, :RET_DK]
        qdec_ref[h, :, :RET_DK] = jnp.exp((rows_k + 1.0) * lgf_k)
        qdec_ref[h, :, RET_DK:] = jnp.exp((RET_CHUNK - rows_k) * lgb_k)
        kdf_ref[h] = jnp.exp((RET_CHUNK - 1.0 - rows_k) * lgf_k)


def _main_kernel(x_ref, prew_ref, postw_ref, retnw_ref, win_ref, wrb_hbm_ref, wab_hbm_ref, wo_hbm_ref, inv_ref,
                 decf_ref, decb_ref, sink_ref, rkv_ref, akp_ref, akc_ref, akn_ref, avp_ref,
                 avc_ref, avn_ref, sb_ref, kvf0_ref, km_ref, vmt_ref, out_ref,
                 sf_ref, cosb_ref, sinb_ref, bias_ref, u_ref, rq_ref, aq_ref, or_ref, oa_ref, grg_ref, gag_ref,
                 ggr_ref, gga_ref, lhs_r_ref, lhs_a_ref, mix_ref, dmat_ref, qdec_ref, kdf_ref, wmain_ref, wrb_ref,
                 wab_ref, wo_ref, stage_ref, sem_ref, *, cpt):
    t = pl.program_id(1)
    tm = cpt * CHUNK

    @pl.when((pl.program_id(0) == 0) & (t == 0))
    def _():
        jobs = [(win_ref, col, wmain_ref, i * W_CHUNK) for i, col in enumerate(MAIN_SRC)]
        for src, dst in ((wrb_hbm_ref, wrb_ref), (wab_hbm_ref, wab_ref), (wo_hbm_ref, wo_ref)):
            jobs += [(src, c, dst, c) for c in range(0, D_MODEL, W_CHUNK)]
        _load_weights_bf16(jobs, stage_ref, sem_ref)
        _fill_rope_base(inv_ref[...], cosb_ref, sinb_ref)
        _fill_band_bias(bias_ref)
        _fill_decay_tables(decf_ref, decb_ref, dmat_ref, qdec_ref, kdf_ref)

    @pl.when(t == 0)
    def _():
        sf_ref[...] = kvf0_ref[...]

    for r0 in range(0, tm, CHUNK):
        u_ref[r0:r0 + CHUNK] = _rms_norm(x_ref[r0:r0 + CHUNK], prew_ref[...]).astype(BF16)
    cos, sin = _rope_tables_from_base(N_META + t * tm, inv_ref[...], cosb_ref, sinb_ref)

    def proj(name, c0):
        col = MAIN_OFF[name] + c0
        return _dot(u_ref[...], wmain_ref[:, col:col + COL_BLOCK])

    heads_per_block = COL_BLOCK // ATT_HD
    for c0 in range(0, RET_QK, COL_BLOCK):
        z = proj("rq", c0)
        for i in range(heads_per_block):
            c = c0 + i * RET_DK
            rq_ref[:, c:c + RET_DK] = _rope(z[:, i * RET_DK:(i + 1) * RET_DK], cos, sin)
    for c0 in range(0, ATT_Q, COL_BLOCK):
        z = proj("aq", c0)
        for i in range(heads_per_block):
            c = c0 + i * ATT_HD
            aq_ref[:, c:c + ATT_HD] = (_rope(z[:, i * ATT_HD:(i + 1) * ATT_HD], cos, sin) * (ATT_HD ** -0.5)
                                       ).astype(BF16)

    gate_jobs = (("rg", jax.nn.silu, grg_ref), ("ag", jax.nn.silu, gag_ref),
                 ("gr", jax.nn.sigmoid, ggr_ref), ("ga", jax.nn.sigmoid, gga_ref))
    gate_blocks = [(j, c) for j in range(len(gate_jobs)) for c in range(0, D_MODEL, COL_BLOCK)]
    assert len(gate_blocks) % cpt == 0

    def emit_gate_blocks(n):
        for _ in range(n):
            j, c = gate_blocks.pop(0)
            name, act, ref = gate_jobs[j]
            ref[:, c:c + COL_BLOCK] = act(proj(name, c))

    cdf = [jnp.exp(RET_CHUNK * _log_sigmoid(decf_ref[h:h + 1, :])) for h in range(RET_HEADS)]

    k_cat = jnp.concatenate([akp_ref[...], akc_ref[...], akn_ref[...]], axis=0)
    vt_cat = jnp.concatenate([avp_ref[...], avc_ref[...], avn_ref[...]], axis=1)
    sink_rows = [jnp.concatenate([sink_ref[g * ATT_REP + r:g * ATT_REP + r + 1, :] for r in range(ATT_REP)],
                                 axis=1) for g in range(ATT_GROUPS)]

    def retention(rc):
        r0 = rc * RET_CHUNK
        rs = slice(r0, r0 + RET_CHUNK)
        for h in range(RET_HEADS):
            q = rq_ref[rs, h * RET_DK:(h + 1) * RET_DK]
            k = rkv_ref[rs, h * RET_DK:(h + 1) * RET_DK]
            v = rkv_ref[rs, RET_QK + h * RET_DV:RET_QK + (h + 1) * RET_DV]
            s = _dot_nt(q.astype(BF16), k) * dmat_ref[h]
            sf = sf_ref[h]
            q_cross = (jnp.concatenate([q, q], axis=1) * qdec_ref[h]).astype(BF16)
            s_cross = jnp.concatenate([sf.astype(BF16), sb_ref[rc, h]], axis=0)
            o = _dot(s.astype(BF16), v) + _dot(q_cross, s_cross)
            sf_ref[h] = cdf[h] * sf + _dot_tn((k.astype(F32) * kdf_ref[h]).astype(BF16), v)
            mu = jnp.mean(o, axis=-1, keepdims=True)
            d = o - mu
            var = jnp.mean(d * d, axis=-1, keepdims=True)
            or_ref[rs, h * RET_DV:(h + 1) * RET_DV] = d * lax.rsqrt(var + EPS)

    for rc in range(tm // RET_CHUNK):
        retention(rc)

    gates_per_pair = len(gate_blocks) // (cpt * ATT_GROUPS)
    pairs = [(lc, g) for lc in range(cpt) for g in range(ATT_GROUPS)]
    probs, denoms = {}, {}
    for lc, g in pairs:
        r0 = lc * CHUNK
        chunk = t * cpt + lc
        gs = slice(g * ATT_HD, (g + 1) * ATT_HD)
        qs = jnp.concatenate([aq_ref[r0:r0 + CHUNK, (g * ATT_REP + r) * ATT_HD:(g * ATT_REP + r + 1) * ATT_HD]
                              for r in range(ATT_REP)], axis=0)
        k_all = jnp.concatenate([k_cat[r0:r0 + N_BAND, gs], km_ref[:, gs]], axis=0)
        emit_gate_blocks(gates_per_pair)
        s = _dot_nt(k_all, qs) + bias_ref[...]
        parts =[s[0:CHUNK], s[CHUNK:2 * CHUNK], s[2 * CHUNK:N_BAND], s[N_BAND:]]
        if lc == 0:
            parts[0] = jnp.where(chunk > 0, parts[0], NEG_INF)
        if lc == cpt - 1:
            parts[2] = jnp.where(chunk < N_CHUNKS - 1, parts[2], NEG_INF)
        s = jnp.concatenate(parts, axis=0)
        sk = sink_rows[g]
        m = jnp.maximum(jnp.max(s, axis=0, keepdims=True), sk)
        p = jnp.exp(s - m)
        denoms[lc, g] = jnp.sum(p, axis=0, keepdims=True) + jnp.exp(sk - m)
        probs[lc, g] = p.astype(BF16)
    for lc, g in pairs:
        r0 = lc * CHUNK
        gs = slice(g * ATT_HD, (g + 1) * ATT_HD)
        vt_all = jnp.concatenate([vt_cat[gs, r0:r0 + N_BAND], vmt_ref[gs, :]], axis=1)
        ot = _dot(vt_all, probs[lc, g]) / denoms[lc, g]
        for r in range(ATT_REP):
            c = (g * ATT_REP + r) * ATT_HD
            oa_ref[r0:r0 + CHUNK, c:c + ATT_HD] = ot[:, r * CHUNK:(r + 1) * CHUNK].T

    for r0 in range(0, tm, CHUNK):
        rs = slice(r0, r0 + CHUNK)
        lhs_r_ref[rs] = (or_ref[rs] * retnw_ref[...] * grg_ref[rs]).astype(BF16)
        lhs_a_ref[rs] = (oa_ref[rs] * gag_ref[rs]).astype(BF16)
    for c0 in range(0, D_MODEL, COL_BLOCK):
        cs = slice(c0, c0 + COL_BLOCK)
        y_r = _dot(lhs_r_ref[...], wrb_ref[:, cs])
        y_a = _dot(lhs_a_ref[...], wab_ref[:, cs])
        mix_ref[:, cs] = (ggr_ref[:, cs] * y_r + gga_ref[:, cs] * y_a).astype(BF16)
    for c0 in range(0, D_MODEL, COL_BLOCK):
        cs = slice(c0, c0 + COL_BLOCK)
        out_ref[:, cs] = _dot(mix_ref[...], wo_ref[:, cs])
    for r0 in range(0, tm, CHUNK):
        rs = slice(r0, r0 + CHUNK)
        out_ref[rs] = x_ref[rs] + _rms_norm(out_ref[rs], postw_ref[...])


def _resident(shape, index=None):
    nd = len(shape)
    index = (0,) * nd if index is None else index
    return pl.BlockSpec(shape, lambda *_: index, pipeline_mode=pl.Buffered(1))


def kernel(x, meta_tokens, pre_norm_w, w_in, ret_decay_fwd, ret_decay_bwd, ret_norm_w, w_ret_branch, attn_sink,
           w_attn_branch, w_out, post_norm_w):
    B = x.shape[0]
    assert x.shape == (B, SEQ, D_MODEL) and pre_norm_w.shape[0] == 1 and w_in.shape == (1, D_MODEL, D_IN)
    cpt = CHUNKS_PER_TILE
    tm = cpt * CHUNK
    nt = N_CHUNKS // cpt

    w_in = w_in.astype(F32)
    w_rb, w_ab, w_o = (w.astype(F32) for w in (w_ret_branch, w_attn_branch, w_out))
    pre_w = pre_norm_w.astype(F32)
    post_w = post_norm_w.astype(F32)
    ret_nw = ret_norm_w.astype(F32)
    half = ATT_HD // 2
    inv = ROPE_THETA ** (-jnp.arange(half, dtype=F32) * 2.0 / ATT_HD)
    inv = jnp.concatenate([inv, inv])[None, :]
    dec_f = jnp.broadcast_to(ret_decay_fwd[0].astype(F32)[:, None], (RET_HEADS, RET_DV))
    dec_b = jnp.broadcast_to(ret_decay_bwd[0].astype(F32)[:, None], (RET_HEADS, RET_DV))
    sink = jnp.broadcast_to(attn_sink[0].astype(F32)[:, None], (ATT_HEADS, ATT_HD))

    params = pltpu.CompilerParams(dimension_semantics=("arbitrary", "arbitrary"),
                                  vmem_limit_bytes=V7X_VMEM_LIMIT_BYTES)
    rope_scratch = [pltpu.VMEM((tm, ATT_HD), F32), pltpu.VMEM((tm, ATT_HD), F32)]
    state_scratch = pltpu.VMEM((RET_HEADS, RET_DK, RET_DV), F32)
    weight_stage = [pltpu.VMEM((2, D_MODEL, W_CHUNK), F32), pltpu.SemaphoreType.DMA((2,))]
    hbm = pl.BlockSpec(memory_space=pl.ANY)

    def win_window(off, width):
        assert off % width == 0
        return _resident((None, D_MODEL, width), (0, 0, off // width))

    kvf0, km, vmt = pl.pallas_call(
        _meta_kernel,
        grid=(1,),
        in_specs=[_resident((N_META, D_MODEL)), _resident((1, D_MODEL)), win_window(OFF_RK, RET_QK),
                  win_window(OFF_RV, RET_V), win_window(OFF_AK, 2 * ATT_KV), _resident((1, ATT_HD)),
                  _resident((RET_HEADS, RET_DV))],
        out_specs=(pl.BlockSpec((RET_HEADS, RET_DK, RET_DV), lambda i: (0, 0, 0)),
                   pl.BlockSpec((N_META, ATT_KV), lambda i: (0, 0)),
                   pl.BlockSpec((ATT_KV, N_META), lambda i: (0, 0))),
        out_shape=(jax.ShapeDtypeStruct((RET_HEADS, RET_DK, RET_DV), F32),
                   jax.ShapeDtypeStruct((N_META, ATT_KV), BF16),
                   jax.ShapeDtypeStruct((ATT_KV, N_META), BF16)),
        compiler_params=pltpu.CompilerParams(dimension_semantics=("arbitrary",),
                                             vmem_limit_bytes=V7X_VMEM_LIMIT_BYTES),
        name="meta",
    )(meta_tokens.astype(F32), pre_w, w_in, w_in, w_in, inv, dec_f)

    kcpt = KV_CHUNKS_PER_TILE
    ktm, knt = kcpt * CHUNK, N_CHUNKS // kcpt
    rkv, ak, avt, sb = pl.pallas_call(
        functools.partial(_kv_kernel, cpt=kcpt),
        grid=(B, knt),
        in_specs=[pl.BlockSpec((None, ktm, D_MODEL), lambda b, t: (b, knt - 1 - t, 0)), _resident((1, D_MODEL)), hbm,
                  _resident((1, ATT_HD)), _resident((RET_HEADS, RET_DV))],
        out_specs=(
            pl.BlockSpec((None, ktm, RKV_COLS), lambda b, t: (b, knt - 1 - t, 0)),
            pl.BlockSpec((None, ktm, ATT_KV), lambda b, t: (b, knt - 1 - t, 0)),
            pl.BlockSpec((None, ATT_KV, ktm), lambda b, t: (b, 0, knt - 1 - t)),
            pl.BlockSpec((None, ktm // RET_CHUNK, RET_HEADS, RET_DK, RET_DV), lambda b, t: (b, knt - 1 - t, 0, 0, 0)),
        ),
        out_shape=(jax.ShapeDtypeStruct((B, SEQ, RKV_COLS), BF16),
                   jax.ShapeDtypeStruct((B, SEQ, ATT_KV), BF16),
                   jax.ShapeDtypeStruct((B, ATT_KV, SEQ), BF16),
                   jax.ShapeDtypeStruct((B, N_RET_CHUNKS, RET_HEADS, RET_DK, RET_DV), BF16)),
        scratch_shapes=[state_scratch, pltpu.VMEM((ktm, ATT_HD), F32), pltpu.VMEM((ktm, ATT_HD), F32)] + [pltpu.VMEM((D_MODEL, KV_COLS), BF16)] + weight_stage,
        compiler_params=params,
        name="kv",
    )(x, pre_w, w_in, inv, dec_b)

    prev_chunk = lambda t: jnp.maximum(t * cpt - 1, 0)
    next_chunk = lambda t: jnp.minimum((t + 1) * cpt, N_CHUNKS - 1)
    out = pl.pallas_call(
        functools.partial(_main_kernel, cpt=cpt),
        grid=(B, nt),
        in_specs=[
            pl.BlockSpec((None, tm, D_MODEL), lambda b, t: (b, t, 0)),
            _resident((1, D_MODEL)),
            _resident((1, D_MODEL)),
            _resident((1, RET_V)),
            hbm, hbm, hbm, hbm,
            _resident((1, ATT_HD)),
            _resident((RET_HEADS, RET_DV)),
            _resident((RET_HEADS, RET_DV)),
            _resident((ATT_HEADS, ATT_HD)),
            pl.BlockSpec((None, tm, RKV_COLS), lambda b, t: (b, t, 0)),
            pl.BlockSpec((None, CHUNK, ATT_KV), lambda b, t: (b, prev_chunk(t), 0)),
            pl.BlockSpec((None, tm, ATT_KV), lambda b, t: (b, t, 0)),
            pl.BlockSpec((None, CHUNK, ATT_KV), lambda b, t: (b, next_chunk(t), 0)),
            pl.BlockSpec((None, ATT_KV, CHUNK), lambda b, t: (b, 0, prev_chunk(t))),
            pl.BlockSpec((None, ATT_KV, tm), lambda b, t: (b, 0, t)),
            pl.BlockSpec((None, ATT_KV, CHUNK), lambda b, t: (b, 0, next_chunk(t))),
            pl.BlockSpec((None, tm // RET_CHUNK, RET_HEADS, RET_DK, RET_DV), lambda b, t: (b, t, 0, 0, 0)),
            _resident((RET_HEADS, RET_DK, RET_DV)),
            _resident((N_META, ATT_KV)),
            _resident((ATT_KV, N_META)),
        ],
        out_specs=pl.BlockSpec((None, tm, D_MODEL), lambda b, t: (b, t, 0)),
        out_shape=jax.ShapeDtypeStruct((B, SEQ, D_MODEL), x.dtype),
        scratch_shapes=(
            [state_scratch] + rope_scratch
            + [pltpu.VMEM((N_KEYS, Q_ROWS), F32),
               pltpu.VMEM((tm, D_MODEL), BF16),
               pltpu.VMEM((tm, RET_QK), F32),
               pltpu.VMEM((tm, ATT_Q), BF16),
               pltpu.VMEM((tm, RET_V), F32),
               pltpu.VMEM((tm, ATT_Q), F32)]
            + [pltpu.VMEM((tm, D_MODEL), F32)] * 4
            + [pltpu.VMEM((tm, D_MODEL), BF16)] * 3
            + [pltpu.VMEM((RET_HEADS, RET_CHUNK, RET_CHUNK), F32),
               pltpu.VMEM((RET_HEADS, RET_CHUNK, 2 * RET_DK), F32),
               pltpu.VMEM((RET_HEADS, RET_CHUNK, RET_DK), F32)]
            + [pltpu.VMEM((D_MODEL, MAIN_COLS), BF16)]
            + [pltpu.VMEM((D_MODEL, D_MODEL), BF16)] * 3
            + weight_stage
        ),
        compiler_params=params,
        name="main",
    )(x, pre_w, post_w, ret_nw, w_in, w_rb, w_ab, w_o, inv, dec_f, dec_b, sink,
      rkv, ak, ak, ak, avt, avt, avt, sb, kvf0, km, vmt)
    return out
```

```python
import functools

import jax
import jax.numpy as jnp
from jax import lax
from jax.experimental import pallas as pl
from jax.experimental.pallas import tpu as pltpu

D_MODEL = 1024
SEQ = 8192
N_META = 16
CHUNK = 128
RET_HEADS = 4
RET_DK = 128
RET_DV = 256
ATT_HEADS = 8
ATT_GROUPS = 2
ATT_REP = ATT_HEADS // ATT_GROUPS
ATT_HD = 128
ROPE_THETA = 10000.0
EPS = 1e-6
NEG_INF = -1e30
RET_QK = RET_HEADS * RET_DK
RET_V = RET_HEADS * RET_DV
ATT_Q = ATT_HEADS * ATT_HD
ATT_KV = ATT_GROUPS * ATT_HD
D_IN = 2 * RET_QK + 2 * RET_V + 2 * ATT_Q + 2 * ATT_KV + 2 * D_MODEL
N_CHUNKS = SEQ // CHUNK
N_BAND = 3 * CHUNK
N_KEYS = N_BAND + N_META
Q_ROWS = ATT_REP * CHUNK

OFF_RQ = 0
OFF_RK = OFF_RQ + RET_QK
OFF_RV = OFF_RK + RET_QK
OFF_RG = OFF_RV + RET_V
OFF_AQ = OFF_RG + RET_V
OFF_AK = OFF_AQ + ATT_Q
OFF_AV = OFF_AK + ATT_KV
OFF_AG = OFF_AV + ATT_KV
OFF_GR = OFF_AG + ATT_Q
OFF_GA = OFF_GR + D_MODEL
RKV_COLS = RET_QK + RET_V

W_CHUNK = 512
KV_SRC = tuple(range(OFF_RK, OFF_RG, W_CHUNK)) + tuple(range(OFF_AK, OFF_AG, W_CHUNK))
KV_RK, KV_RV, KV_AKV = 0, RET_QK, RET_QK + RET_V
KV_COLS = len(KV_SRC) * W_CHUNK
MAIN_SEGMENTS = (("rq", OFF_RQ, RET_QK), ("aq", OFF_AQ, ATT_Q), ("rg", OFF_RG, RET_V), ("ag", OFF_AG, ATT_Q),
                 ("gr", OFF_GR, D_MODEL), ("ga", OFF_GA, D_MODEL))
MAIN_SRC = tuple(c for _, off, width in MAIN_SEGMENTS for c in range(off, off + width, W_CHUNK))
MAIN_OFF = {}
_o = 0
for _name, _, _width in MAIN_SEGMENTS:
    MAIN_OFF[_name] = _o
    _o += _width
MAIN_COLS = _o

CHUNKS_PER_TILE = 4
KV_CHUNKS_PER_TILE = 8
RET_CHUNK = 2 * CHUNK
N_RET_CHUNKS = SEQ // RET_CHUNK
COL_BLOCK = 256
V7X_VMEM_LIMIT_BYTES = 56 * 1024 * 1024

F32 = jnp.float32
BF16 = jnp.bfloat16


def _rms_norm(x, w):
    return x * lax.rsqrt(jnp.mean(x * x, axis=-1, keepdims=True) + EPS) * w


def _log_sigmoid(x):
    return jnp.minimum(x, 0.0) - jnp.log(1.0 + jnp.exp(-jnp.abs(x)))


def _sign_fold(sin):
    lane = lax.broadcasted_iota(jnp.int32, sin.shape, 1)
    return jnp.where(lane < ATT_HD // 2, -sin, sin)


def _rope_tables(rows, inv):
    ang = lax.broadcasted_iota(jnp.int32, (rows, ATT_HD), 0).astype(F32) * inv
    return jnp.cos(ang), _sign_fold(jnp.sin(ang))


def _fill_rope_base(inv, cosb_ref, sinb_ref):
    ang = lax.broadcasted_iota(jnp.int32, cosb_ref.shape, 0).astype(F32) * inv
    cosb_ref[...] = jnp.cos(ang)
    sinb_ref[...] = jnp.sin(ang)


def _rope_tables_from_base(pos0, inv, cosb_ref, sinb_ref):
    base = pos0.astype(F32) * inv
    ca, sa = jnp.cos(base), jnp.sin(base)
    cb, sb = cosb_ref[...], sinb_ref[...]
    return ca * cb - sa * sb, _sign_fold(sa * cb + ca * sb)


def _rope(t, cos, sin_signed):
    return t * cos + pltpu.roll(t, ATT_HD // 2, axis=1) * sin_signed


def _dot(a, b):
    return jnp.dot(a, b, preferred_element_type=F32)


def _dot_nt(a, b):
    return lax.dot_general(a, b, (((1,), (1,)), ((), ())), preferred_element_type=F32)


def _dot_tn(a, b):
    return lax.dot_general(a, b, (((0,), (0,)), ((), ())), preferred_element_type=F32)


def _row_index(shape):
    return lax.broadcasted_iota(jnp.int32, shape, 0).astype(F32)


def _weight_chunk_copy(src_ref, col, stage_ref, sem_ref, slot):
    return pltpu.make_async_copy(src_ref.at[0, :, pl.ds(col, W_CHUNK)], stage_ref.at[slot], sem_ref.at[slot])


def _load_weights_bf16(jobs, stage_ref, sem_ref):
    copies = [_weight_chunk_copy(src, col, stage_ref, sem_ref, i % 2) for i, (src, col, _, _) in enumerate(jobs)]
    copies[0].start()
    for i, (_, _, dst, dcol) in enumerate(jobs):
        if i + 1 < len(jobs):
            copies[i + 1].start()
        copies[i].wait()
        dst[:, dcol:dcol + W_CHUNK] = stage_ref[i % 2].astype(BF16)


def _meta_kernel(meta_ref, prew_ref, wrk_ref, wrv_ref, wakv_ref, inv_ref, decf_ref, kvf0_ref, km_ref, vmt_ref):
    u = _rms_norm(meta_ref[...], prew_ref[...]).astype(BF16)
    zk, zv, za = (_dot(u, w[...].astype(BF16)) for w in (wrk_ref, wrv_ref, wakv_ref))
    cos, sin = _rope_tables(N_META, inv_ref[...])
    kdec_rows = (N_META - 1) - _row_index((N_META, RET_DK))
    for h in range(RET_HEADS):
        lg = _log_sigmoid(decf_ref[h:h + 1, :RET_DK])
        k = _rope(zk[:, h * RET_DK:(h + 1) * RET_DK], cos, sin) * (RET_DK ** -0.5)
        k = (k * jnp.exp(kdec_rows * lg)).astype(BF16)
        v = zv[:, h * RET_DV:(h + 1) * RET_DV].astype(BF16)
        kvf0_ref[h] = _dot_tn(k, v)
    for g in range(ATT_GROUPS):
        km_ref[:, g * ATT_HD:(g + 1) * ATT_HD] = _rope(za[:, g * ATT_HD:(g + 1) * ATT_HD], cos, sin).astype(BF16)
    vmt_ref[...] = za[:, ATT_KV:].T.astype(BF16)


def _kv_kernel(x_ref, prew_ref, win_ref, inv_ref, decb_ref, rkv_ref, ak_ref, avt_ref, sb_ref,
               state_ref, cosb_ref, sinb_ref, wkv_ref, stage_ref, sem_ref, *, cpt):
    t = pl.program_id(1)
    tile = pl.num_programs(1) - 1 - t
    tm = cpt * CHUNK

    @pl.when((pl.program_id(0) == 0) & (t == 0))
    def _():
        _load_weights_bf16([(win_ref, col, wkv_ref, i * W_CHUNK) for i, col in enumerate(KV_SRC)],
                           stage_ref, sem_ref)
        _fill_rope_base(inv_ref[...], cosb_ref, sinb_ref)

    @pl.when(t == 0)
    def _():
        state_ref[...] = jnp.zeros_like(state_ref)

    u = _rms_norm(x_ref[...], prew_ref[...]).astype(BF16)
    za = _dot(u, wkv_ref[:, KV_AKV:])
    zk = _dot(u, wkv_ref[:, KV_RK:KV_RV])
    zv = _dot(u, wkv_ref[:, KV_RV:KV_AKV])
    cos, sin = _rope_tables_from_base(N_META + tile * tm, inv_ref[...], cosb_ref, sinb_ref)
    for g in range(ATT_GROUPS):
        ak_ref[:, g * ATT_HD:(g + 1) * ATT_HD] = _rope(za[:, g * ATT_HD:(g + 1) * ATT_HD], cos, sin).astype(BF16)
    avt_ref[...] = za[:, ATT_KV:].T.astype(BF16)
    rk = [_rope(zk[:, h * RET_DK:(h + 1) * RET_DK], cos, sin) * (RET_DK ** -0.5) for h in range(RET_HEADS)]
    for h in range(RET_HEADS):
        rkv_ref[:, h * RET_DK:(h + 1) * RET_DK] = rk[h].astype(BF16)
    rkv_ref[:, RET_QK:] = zv.astype(BF16)

    rows_k = _row_index((RET_CHUNK, RET_DK))
    for h in range(RET_HEADS):
        lg = _log_sigmoid(decb_ref[h:h + 1, :])
        kdec = jnp.exp(rows_k * lg[:, :RET_DK])
        cdec = jnp.exp(RET_CHUNK * lg)
        for rc in reversed(range(tm // RET_CHUNK)):
            r0 = rc * RET_CHUNK
            state = state_ref[h]
            sb_ref[rc, h] = state.astype(BF16)
            k = (rk[h][r0:r0 + RET_CHUNK] * kdec).astype(BF16)
            v = zv[r0:r0 + RET_CHUNK, h * RET_DV:(h + 1) * RET_DV].astype(BF16)
            state_ref[h] = cdec * state + _dot_tn(k, v)


def _fill_band_bias(bias_ref):
    kk = lax.broadcasted_iota(jnp.int32, bias_ref.shape, 0)
    qi = lax.broadcasted_iota(jnp.int32, bias_ref.shape, 1) & (CHUNK - 1)
    visible = (kk >= N_BAND) | ((kk >= qi) & (kk <= qi + 2 * CHUNK))
    bias_ref[...] = jnp.where(visible, 0.0, NEG_INF)


def _fill_decay_tables(decf_ref, decb_ref, dmat_ref, qdec_ref, kdf_ref):
    rows = _row_index((RET_CHUNK, RET_CHUNK))
    rel = rows - lax.broadcasted_iota(jnp.int32, (RET_CHUNK, RET_CHUNK), 1).astype(F32)
    rows_k = _row_index((RET_CHUNK, RET_DK))
    for h in range(RET_HEADS):
        lgf = _log_sigmoid(decf_ref[h:h + 1, :])
        lgb = _log_sigmoid(decb_ref[h:h + 1, :])
        dmat_ref[h] = jnp.where(rel >= 0, jnp.exp(jnp.maximum(rel, 0.0) * lgf),
                                jnp.exp(jnp.maximum(-rel, 0.0) * lgb))
        lgf_k, lgb_k = lgf[:, :RET_DK], lgb[:, :RET_DK]
        qdec_ref[h, :, :RET_DK] = jnp.exp((rows_k + 1.0) * lgf_k)
        qdec_ref[h, :, RET_DK:] = jnp.exp((RET_CHUNK - rows_k) * lgb_k)
        kdf_ref[h] = jnp.exp((RET_CHUNK - 1.0 - rows_k) * lgf_k)


def _main_kernel(x_ref, prew_ref, postw_ref, retnw_ref, win_ref, wrb_hbm_ref, wab_hbm_ref, wo_hbm_ref, inv_ref,
                 decf_ref, decb_ref, sink_ref, rkv_ref, akp_ref, akc_ref, akn_ref, avp_ref,
                 avc_ref, avn_ref, sb_ref, kvf0_ref, km_ref, vmt_ref, out_ref,
                 sf_ref, cosb_ref, sinb_ref, bias_ref, u_ref, rq_ref, aq_ref, or_ref, oa_ref, grg_ref, gag_ref,
                 ggr_ref, gga_ref, lhs_r_ref, lhs_a_ref, mix_ref, dmat_ref, qdec_ref, kdf_ref, wmain_ref, wrb_ref,
                 wab_ref, wo_ref, stage_ref, sem_ref, *, cpt):
    t = pl.program_id(1)
    tm = cpt * CHUNK

    @pl.when((pl.program_id(0) == 0) & (t == 0))
    def _():
        jobs = [(win_ref, col, wmain_ref, i * W_CHUNK) for i, col in enumerate(MAIN_SRC)]
        for src, dst in ((wrb_hbm_ref, wrb_ref), (wab_hbm_ref, wab_ref), (wo_hbm_ref, wo_ref)):
            jobs += [(src, c, dst, c) for c in range(0, D_MODEL, W_CHUNK)]
        _load_weights_bf16(jobs, stage_ref, sem_ref)
        _fill_rope_base(inv_ref[...], cosb_ref, sinb_ref)
        _fill_band_bias(bias_ref)
        _fill_decay_tables(decf_ref, decb_ref, dmat_ref, qdec_ref, kdf_ref)

    @pl.when(t == 0)
    def _():
        sf_ref[...] = kvf0_ref[...]

    for r0 in range(0, tm, CHUNK):
        u_ref[r0:r0 + CHUNK] = _rms_norm(x_ref[r0:r0 + CHUNK], prew_ref[...]).astype(BF16)
    cos, sin = _rope_tables_from_base(N_META + t * tm, inv_ref[...], cosb_ref, sinb_ref)

    def proj(name, c0):
        col = MAIN_OFF[name] + c0
        return _dot(u_ref[...], wmain_ref[:, col:col + COL_BLOCK])

    heads_per_block = COL_BLOCK // ATT_HD
    for c0 in range(0, RET_QK, COL_BLOCK):
        z = proj("rq", c0)
        for i in range(heads_per_block):
            c = c0 + i * RET_DK
            rq_ref[:, c:c + RET_DK] = _rope(z[:, i * RET_DK:(i + 1) * RET_DK], cos, sin)
    for c0 in range(0, ATT_Q, COL_BLOCK):
        z = proj("aq", c0)
        for i in range(heads_per_block):
            c = c0 + i * ATT_HD
            aq_ref[:, c:c + ATT_HD] = (_rope(z[:, i * ATT_HD:(i + 1) * ATT_HD], cos, sin) * (ATT_HD ** -0.5)
                                       ).astype(BF16)

    gate_jobs = (("rg", jax.nn.silu, grg_ref), ("ag", jax.nn.silu, gag_ref),
                 ("gr", jax.nn.sigmoid, ggr_ref), ("ga", jax.nn.sigmoid, gga_ref))
    gate_blocks = [(j, c) for j in range(len(gate_jobs)) for c in range(0, D_MODEL, COL_BLOCK)]
    assert len(gate_blocks) % cpt == 0

    def emit_gate_blocks(n):
        for _ in range(n):
            j, c = gate_blocks.pop(0)
            name, act, ref = gate_jobs[j]
            ref[:, c:c + COL_BLOCK] = act(proj(name, c))

    cdf = [jnp.exp(RET_CHUNK * _log_sigmoid(decf_ref[h:h + 1, :])) for h in range(RET_HEADS)]

    k_cat = jnp.concatenate([akp_ref[...], akc_ref[...], akn_ref[...]], axis=0)
    vt_cat = jnp.concatenate([avp_ref[...], avc_ref[...], avn_ref[...]], axis=1)
    sink_rows = [jnp.concatenate([sink_ref[g * ATT_REP + r:g * ATT_REP + r + 1, :] for r in range(ATT_REP)],
                                 axis=1) for g in range(ATT_GROUPS)]

    def retention(rc):
        r0 = rc * RET_CHUNK
        rs = slice(r0, r0 + RET_CHUNK)
        for h in range(RET_HEADS):
            q = rq_ref[rs, h * RET_DK:(h + 1) * RET_DK]
            k = rkv_ref[rs, h * RET_DK:(h + 1) * RET_DK]
            v = rkv_ref[rs, RET_QK + h * RET_DV:RET_QK + (h + 1) * RET_DV]
            s = _dot_nt(q.astype(BF16), k) * dmat_ref[h]
            sf = sf_ref[h]
            q_cross = (jnp.concatenate([q, q], axis=1) * qdec_ref[h]).astype(BF16)
            s_cross = jnp.concatenate([sf.astype(BF16), sb_ref[rc, h]], axis=0)
            o = _dot(s.astype(BF16), v) + _dot(q_cross, s_cross)
            sf_ref[h] = cdf[h] * sf + _dot_tn((k.astype(F32) * kdf_ref[h]).astype(BF16), v)
            mu = jnp.mean(o, axis=-1, keepdims=True)
            d = o - mu
            var = jnp.mean(d * d, axis=-1, keepdims=True)
            or_ref[rs, h * RET_DV:(h + 1) * RET_DV] = d * lax.rsqrt(var + EPS)

    for rc in range(tm // RET_CHUNK):
        retention(rc)

    gates_per_pair = len(gate_blocks) // (cpt * ATT_GROUPS)
    pairs = [(lc, g) for lc in range(cpt) for g in range(ATT_GROUPS)]
    probs, denoms = {}, {}
    for lc, g in pairs:
        r0 = lc * CHUNK
        chunk = t * cpt + lc
        gs = slice(g * ATT_HD, (g + 1) * ATT_HD)
        qs = jnp.concatenate([aq_ref[r0:r0 + CHUNK, (g * ATT_REP + r) * ATT_HD:(g * ATT_REP + r + 1) * ATT_HD]
                              for r in range(ATT_REP)], axis=0)
        k_all = jnp.concatenate([k_cat[r0:r0 + N_BAND, gs], km_ref[:, gs]], axis=0)
        emit_gate_blocks(gates_per_pair)
        s = _dot_nt(k_all, qs) + bias_ref[...]
        parts =[s[0:CHUNK], s[CHUNK:2 * CHUNK], s[2 * CHUNK:N_BAND], s[N_BAND:]]
        if lc == 0:
            parts[0] = jnp.where(chunk > 0, parts[0], NEG_INF)
        if lc == cpt - 1:
            parts[2] = jnp.where(chunk < N_CHUNKS - 1, parts[2], NEG_INF)
        s = jnp.concatenate(parts, axis=0)
        sk = sink_rows[g]
        m = jnp.maximum(jnp.max(s, axis=0, keepdims=True), sk)
        p = jnp.exp(s - m)
        denoms[lc, g] = jnp.sum(p, axis=0, keepdims=True) + jnp.exp(sk - m)
        probs[lc, g] = p.astype(BF16)
    for lc, g in pairs:
        r0 = lc * CHUNK
        gs = slice(g * ATT_HD, (g + 1) * ATT_HD)
        vt_all = jnp.concatenate([vt_cat[gs, r0:r0 + N_BAND], vmt_ref[gs, :]], axis=1)
        ot = _dot(vt_all, probs[lc, g]) / denoms[lc, g]
        for r in range(ATT_REP):
            c = (g * ATT_REP + r) * ATT_HD
            oa_ref[r0:r0 + CHUNK, c:c + ATT_HD] = ot[:, r * CHUNK:(r + 1) * CHUNK].T

    for r0 in range(0, tm, CHUNK):
        rs = slice(r0, r0 + CHUNK)
        lhs_r_ref[rs] = (or_ref[rs] * retnw_ref[...] * grg_ref[rs]).astype(BF16)
        lhs_a_ref[rs] = (oa_ref[rs] * gag_ref[rs]).astype(BF16)
    for c0 in range(0, D_MODEL, COL_BLOCK):
        cs = slice(c0, c0 + COL_BLOCK)
        y_r = _dot(lhs_r_ref[...], wrb_ref[:, cs])
        y_a = _dot(lhs_a_ref[...], wab_ref[:, cs])
        mix_ref[:, cs] = (ggr_ref[:, cs] * y_r + gga_ref[:, cs] * y_a).astype(BF16)
    for c0 in range(0, D_MODEL, COL_BLOCK):
        cs = slice(c0, c0 + COL_BLOCK)
        out_ref[:, cs] = _dot(mix_ref[...], wo_ref[:, cs])
    for r0 in range(0, tm, CHUNK):
        rs = slice(r0, r0 + CHUNK)
        out_ref[rs] = x_ref[rs] + _rms_norm(out_ref[rs], postw_ref[...])


def _resident(shape, index=None):
    nd = len(shape)
    index = (0,) * nd if index is None else index
    return pl.BlockSpec(shape, lambda *_: index, pipeline_mode=pl.Buffered(1))


def kernel(x, meta_tokens, pre_norm_w, w_in, ret_decay_fwd, ret_decay_bwd, ret_norm_w, w_ret_branch, attn_sink,
           w_attn_branch, w_out, post_norm_w):
    B = x.shape[0]
    assert x.shape == (B, SEQ, D_MODEL) and pre_norm_w.shape[0] == 1 and w_in.shape == (1, D_MODEL, D_IN)
    cpt = CHUNKS_PER_TILE
    tm = cpt * CHUNK
    nt = N_CHUNKS // cpt

    w_in = w_in.astype(F32)
    w_rb, w_ab, w_o = (w.astype(F32) for w in (w_ret_branch, w_attn_branch, w_out))
    pre_w = pre_norm_w.astype(F32)
    post_w = post_norm_w.astype(F32)
    ret_nw = ret_norm_w.astype(F32)
    half = ATT_HD // 2
    inv = ROPE_THETA ** (-jnp.arange(half, dtype=F32) * 2.0 / ATT_HD)
    inv = jnp.concatenate([inv, inv])[None, :]
    dec_f = jnp.broadcast_to(ret_decay_fwd[0].astype(F32)[:, None], (RET_HEADS, RET_DV))
    dec_b = jnp.broadcast_to(ret_decay_bwd[0].astype(F32)[:, None], (RET_HEADS, RET_DV))
    sink = jnp.broadcast_to(attn_sink[0].astype(F32)[:, None], (ATT_HEADS, ATT_HD))

    params = pltpu.CompilerParams(dimension_semantics=("arbitrary", "arbitrary"),
                                  vmem_limit_bytes=V7X_VMEM_LIMIT_BYTES)
    rope_scratch = [pltpu.VMEM((tm, ATT_HD), F32), pltpu.VMEM((tm, ATT_HD), F32)]
    state_scratch = pltpu.VMEM((RET_HEADS, RET_DK, RET_DV), F32)
    weight_stage = [pltpu.VMEM((2, D_MODEL, W_CHUNK), F32), pltpu.SemaphoreType.DMA((2,))]
    hbm = pl.BlockSpec(memory_space=pl.ANY)

    def win_window(off, width):
        assert off % width == 0
        return _resident((None, D_MODEL, width), (0, 0, off // width))

    kvf0, km, vmt = pl.pallas_call(
        _meta_kernel,
        grid=(1,),
        in_specs=[_resident((N_META, D_MODEL)), _resident((1, D_MODEL)), win_window(OFF_RK, RET_QK),
                  win_window(OFF_RV, RET_V), win_window(OFF_AK, 2 * ATT_KV), _resident((1, ATT_HD)),
                  _resident((RET_HEADS, RET_DV))],
        out_specs=(pl.BlockSpec((RET_HEADS, RET_DK, RET_DV), lambda i: (0, 0, 0)),
                   pl.BlockSpec((N_META, ATT_KV), lambda i: (0, 0)),
                   pl.BlockSpec((ATT_KV, N_META), lambda i: (0, 0))),
        out_shape=(jax.ShapeDtypeStruct((RET_HEADS, RET_DK, RET_DV), F32),
                   jax.ShapeDtypeStruct((N_META, ATT_KV), BF16),
                   jax.ShapeDtypeStruct((ATT_KV, N_META), BF16)),
        compiler_params=pltpu.CompilerParams(dimension_semantics=("arbitrary",),
                                             vmem_limit_bytes=V7X_VMEM_LIMIT_BYTES),
        name="meta",
    )(meta_tokens.astype(F32), pre_w, w_in, w_in, w_in, inv, dec_f)

    kcpt = KV_CHUNKS_PER_TILE
    ktm, knt = kcpt * CHUNK, N_CHUNKS // kcpt
    rkv, ak, avt, sb = pl.pallas_call(
        functools.partial(_kv_kernel, cpt=kcpt),
        grid=(B, knt),
        in_specs=[pl.BlockSpec((None, ktm, D_MODEL), lambda b, t: (b, knt - 1 - t, 0)), _resident((1, D_MODEL)), hbm,
                  _resident((1, ATT_HD)), _resident((RET_HEADS, RET_DV))],
        out_specs=(
            pl.BlockSpec((None, ktm, RKV_COLS), lambda b, t: (b, knt - 1 - t, 0)),
            pl.BlockSpec((None, ktm, ATT_KV), lambda b, t: (b, knt - 1 - t, 0)),
            pl.BlockSpec((None, ATT_KV, ktm), lambda b, t: (b, 0, knt - 1 - t)),
            pl.BlockSpec((None, ktm // RET_CHUNK, RET_HEADS, RET_DK, RET_DV), lambda b, t: (b, knt - 1 - t, 0, 0, 0)),
        ),
        out_shape=(jax.ShapeDtypeStruct((B, SEQ, RKV_COLS), BF16),
                   jax.ShapeDtypeStruct((B, SEQ, ATT_KV), BF16),
                   jax.ShapeDtypeStruct((B, ATT_KV, SEQ), BF16),
                   jax.ShapeDtypeStruct((B, N_RET_CHUNKS, RET_HEADS, RET_DK, RET_DV), BF16)),
        scratch_shapes=[state_scratch, pltpu.VMEM((ktm, ATT_HD), F32), pltpu.VMEM((ktm, ATT_HD), F32)] + [pltpu.VMEM((D_MODEL, KV_COLS), BF16)] + weight_stage,
        compiler_params=params,
        name="kv",
    )(x, pre_w, w_in, inv, dec_b)

    prev_chunk = lambda t: jnp.maximum(t * cpt - 1, 0)
    next_chunk = lambda t: jnp.minimum((t + 1) * cpt, N_CHUNKS - 1)
    out = pl.pallas_call(
        functools.partial(_main_kernel, cpt=cpt),
        grid=(B, nt),
        in_specs=[
            pl.BlockSpec((None, tm, D_MODEL), lambda b, t: (b, t, 0)),
            _resident((1, D_MODEL)),
            _resident((1, D_MODEL)),
            _resident((1, RET_V)),
            hbm, hbm, hbm, hbm,
            _resident((1, ATT_HD)),
            _resident((RET_HEADS, RET_DV)),
            _resident((RET_HEADS, RET_DV)),
            _resident((ATT_HEADS, ATT_HD)),
            pl.BlockSpec((None, tm, RKV_COLS), lambda b, t: (b, t, 0)),
            pl.BlockSpec((None, CHUNK, ATT_KV), lambda b, t: (b, prev_chunk(t), 0)),
            pl.BlockSpec((None, tm, ATT_KV), lambda b, t: (b, t, 0)),
            pl.BlockSpec((None, CHUNK, ATT_KV), lambda b, t: (b, next_chunk(t), 0)),
            pl.BlockSpec((None, ATT_KV, CHUNK), lambda b, t: (b, 0, prev_chunk(t))),
            pl.BlockSpec((None, ATT_KV, tm), lambda b, t: (b, 0, t)),
            pl.BlockSpec((None, ATT_KV, CHUNK), lambda b, t: (b, 0, next_chunk(t))),
            pl.BlockSpec((None, tm // RET_CHUNK, RET_HEADS, RET_DK, RET_DV), lambda b, t: (b, t, 0, 0, 0)),
            _resident((RET_HEADS, RET_DK, RET_DV)),
            _resident((N_META, ATT_KV)),
            _resident((ATT_KV, N_META)),
        ],
        out_specs=pl.BlockSpec((None, tm, D_MODEL), lambda b, t: (b, t, 0)),
        out_shape=jax.ShapeDtypeStruct((B, SEQ, D_MODEL), x.dtype),
        scratch_shapes=(
            [state_scratch] + rope_scratch
            + [pltpu.VMEM((N_KEYS, Q_ROWS), F32),
               pltpu.VMEM((tm, D_MODEL), BF16),
               pltpu.VMEM((tm, RET_QK), F32),
               pltpu.VMEM((tm, ATT_Q), BF16),
               pltpu.VMEM((tm, RET_V), F32),
               pltpu.VMEM((tm, ATT_Q), F32)]
            + [pltpu.VMEM((tm, D_MODEL), F32)] * 4
            + [pltpu.VMEM((tm, D_MODEL), BF16)] * 3
            + [pltpu.VMEM((RET_HEADS, RET_CHUNK, RET_CHUNK), F32),
               pltpu.VMEM((RET_HEADS, RET_CHUNK, 2 * RET_DK), F32),
               pltpu.VMEM((RET_HEADS, RET_CHUNK, RET_DK), F32)]
            + [pltpu.VMEM((D_MODEL, MAIN_COLS), BF16)]
            + [pltpu.VMEM((D_MODEL, D_MODEL), BF16)] * 3
            + weight_stage
        ),
        compiler_params=params,
        name="main",
    )(x, pre_w, post_w, ret_nw, w_in, w_rb, w_ab, w_o, inv, dec_f, dec_b, sink,
      rkv, ak, ak, ak, avt, avt, avt, sb, kvf0, km, vmt)
    return out
```

```python
import functools

import jax
import jax.numpy as jnp
from jax import lax
from jax.experimental import pallas as pl
from jax.experimental.pallas import tpu as pltpu

D_MODEL = 1024
SEQ = 8192
N_META = 16
CHUNK = 128
RET_HEADS = 4
RET_DK = 128
RET_DV = 256
ATT_HEADS = 8
ATT_GROUPS = 2
ATT_REP = ATT_HEADS // ATT_GROUPS
ATT_HD = 128
ROPE_THETA = 10000.0
EPS = 1e-6
NEG_INF = -1e30
RET_QK = RET_HEADS * RET_DK
RET_V = RET_HEADS * RET_DV
ATT_Q = ATT_HEADS * ATT_HD
ATT_KV = ATT_GROUPS * ATT_HD
D_IN = 2 * RET_QK + 2 * RET_V + 2 * ATT_Q + 2 * ATT_KV + 2 * D_MODEL
N_CHUNKS = SEQ // CHUNK
N_BAND = 3 * CHUNK
N_KEYS = N_BAND + N_META
Q_ROWS = ATT_REP * CHUNK

OFF_RQ = 0
OFF_RK = OFF_RQ + RET_QK
OFF_RV = OFF_RK + RET_QK
OFF_RG = OFF_RV + RET_V
OFF_AQ = OFF_RG + RET_V
OFF_AK = OFF_AQ + ATT_Q
OFF_AV = OFF_AK + ATT_KV
OFF_AG = OFF_AV + ATT_KV
OFF_GR = OFF_AG + ATT_Q
OFF_GA = OFF_GR + D_MODEL
RKV_COLS = RET_QK + RET_V

W_CHUNK = 512
KV_SRC = tuple(range(OFF_RK, OFF_RG, W_CHUNK)) + tuple(range(OFF_AK, OFF_AG, W_CHUNK))
KV_RK, KV_RV, KV_AKV = 0, RET_QK, RET_QK + RET_V
KV_COLS = len(KV_SRC) * W_CHUNK
MAIN_SEGMENTS = (("rq", OFF_RQ, RET_QK), ("aq", OFF_AQ, ATT_Q), ("rg", OFF_RG, RET_V), ("ag", OFF_AG, ATT_Q),
                 ("gr", OFF_GR, D_MODEL), ("ga", OFF_GA, D_MODEL))
MAIN_SRC = tuple(c for _, off, width in MAIN_SEGMENTS for c in range(off, off + width, W_CHUNK))
MAIN_OFF = {}
_o = 0
for _name, _, _width in MAIN_SEGMENTS:
    MAIN_OFF[_name] = _o
    _o += _width
MAIN_COLS = _o

CHUNKS_PER_TILE = 4
KV_CHUNKS_PER_TILE = 8
RET_CHUNK = 2 * CHUNK
N_RET_CHUNKS = SEQ // RET_CHUNK
COL_BLOCK = 256
SUBLANES = 8
V7X_VMEM_LIMIT_BYTES = 56 * 1024 * 1024

F32 = jnp.float32
BF16 = jnp.bfloat16


def _rms_norm(x, w):
    return x * lax.rsqrt(jnp.mean(x * x, axis=-1, keepdims=True) + EPS) * w


def _log_sigmoid(x):
    return jnp.minimum(x, 0.0) - jnp.log(1.0 + jnp.exp(-jnp.abs(x)))


def _sign_fold(sin):
    lane = lax.broadcasted_iota(jnp.int32, sin.shape, 1)
    return jnp.where(lane < ATT_HD // 2, -sin, sin)


def _rope_tables(rows, inv):
    ang = lax.broadcasted_iota(jnp.int32, (rows, ATT_HD), 0).astype(F32) * inv
    return jnp.cos(ang), _sign_fold(jnp.sin(ang))


def _fill_rope_base(inv, cosb_ref, sinb_ref):
    ang = lax.broadcasted_iota(jnp.int32, cosb_ref.shape, 0).astype(F32) * inv
    cosb_ref[...] = jnp.cos(ang)
    sinb_ref[...] = jnp.sin(ang)


def _rope_tables_from_base(pos0, inv, cosb_ref, sinb_ref):
    base = pos0.astype(F32) * inv
    ca, sa = jnp.cos(base), jnp.sin(base)
    cb, sb = cosb_ref[...], sinb_ref[...]
    return ca * cb - sa * sb, _sign_fold(sa * cb + ca * sb)


def _rope(t, cos, sin_signed):
    return t * cos + pltpu.roll(t, ATT_HD // 2, axis=1) * sin_signed


def _dot(a, b):
    return jnp.dot(a, b, preferred_element_type=F32)


def _dot_nt(a, b):
    return lax.dot_general(a, b, (((1,), (1,)), ((), ())), preferred_element_type=F32)


def _dot_tn(a, b):
    return lax.dot_general(a, b, (((0,), (0,)), ((), ())), preferred_element_type=F32)


def _row_index(shape):
    return lax.broadcasted_iota(jnp.int32, shape, 0).astype(F32)


def _weight_chunk_copy(src_ref, col, stage_ref, sem_ref, slot):
    return pltpu.make_async_copy(src_ref.at[0, :, pl.ds(col, W_CHUNK)], stage_ref.at[slot], sem_ref.at[slot])


def _load_weights_bf16(jobs, stage_ref, sem_ref):
    copies = [_weight_chunk_copy(src, col, stage_ref, sem_ref, i % 2) for i, (src, col, _, _) in enumerate(jobs)]
    copies[0].start()
    for i, (_, _, dst, dcol) in enumerate(jobs):
        if i + 1 < len(jobs):
            copies[i + 1].start()
        copies[i].wait()
        dst[:, dcol:dcol + W_CHUNK] = stage_ref[i % 2].astype(BF16)


def _meta_block(meta_ref, prew_ref, wkv_ref, inv_ref, decf_ref, kvf0_ref, km_ref, vmt_ref):
    u = _rms_norm(meta_ref[...], prew_ref[...]).astype(BF16)
    zk = _dot(u, wkv_ref[:, KV_RK:KV_RV])
    zv = _dot(u, wkv_ref[:, KV_RV:KV_AKV])
    za = _dot(u, wkv_ref[:, KV_AKV:])
    cos, sin = _rope_tables(N_META, inv_ref[...])
    kdec_rows = (N_META - 1) - _row_index((N_META, RET_DK))
    for h in range(RET_HEADS):
        lg = _log_sigmoid(decf_ref[h:h + 1, :RET_DK])
        k = _rope(zk[:, h * RET_DK:(h + 1) * RET_DK], cos, sin) * (RET_DK ** -0.5)
        k = (k * jnp.exp(kdec_rows * lg)).astype(BF16)
        v = zv[:, h * RET_DV:(h + 1) * RET_DV].astype(BF16)
        kvf0_ref[h] = _dot_tn(k, v)
    for g in range(ATT_GROUPS):
        km_ref[:, g * ATT_HD:(g + 1) * ATT_HD] = _rope(za[:, g * ATT_HD:(g + 1) * ATT_HD], cos, sin).astype(BF16)
    vmt_ref[...] = za[:, ATT_KV:].T.astype(BF16)


def _kv_kernel(x_ref, meta_ref, prew_ref, win_ref, inv_ref, decf_ref, decb_ref, rkv_ref, ak_ref, avt_ref, sb_ref,
               kvf0_ref, km_ref, vmt_ref, state_ref, cosb_ref, sinb_ref, wkv_ref, stage_ref, sem_ref, *, cpt):
    t = pl.program_id(1)
    tile = pl.num_programs(1) - 1 - t
    tm = cpt * CHUNK

    @pl.when((pl.program_id(0) == 0) & (t == 0))
    def _():
        _load_weights_bf16([(win_ref, col, wkv_ref, i * W_CHUNK) for i, col in enumerate(KV_SRC)],
                           stage_ref, sem_ref)
        _fill_rope_base(inv_ref[...], cosb_ref, sinb_ref)
        _meta_block(meta_ref, prew_ref, wkv_ref, inv_ref, decf_ref, kvf0_ref, km_ref, vmt_ref)

    @pl.when(t == 0)
    def _():
        state_ref[...] = jnp.zeros_like(state_ref)

    u = _rms_norm(x_ref[...], prew_ref[...]).astype(BF16)
    za = _dot(u, wkv_ref[:, KV_AKV:])
    zk = _dot(u, wkv_ref[:, KV_RK:KV_RV])
    zv = _dot(u, wkv_ref[:, KV_RV:KV_AKV])
    cos, sin = _rope_tables_from_base(N_META + tile * tm, inv_ref[...], cosb_ref, sinb_ref)
    for g in range(ATT_GROUPS):
        ak_ref[:, g * ATT_HD:(g + 1) * ATT_HD] = _rope(za[:, g * ATT_HD:(g + 1) * ATT_HD], cos, sin).astype(BF16)
    avt_ref[...] = za[:, ATT_KV:].T.astype(BF16)
    rk = [_rope(zk[:, h * RET_DK:(h + 1) * RET_DK], cos, sin) * (RET_DK ** -0.5) for h in range(RET_HEADS)]
    for h in range(RET_HEADS):
        rkv_ref[:, h * RET_DK:(h + 1) * RET_DK] = rk[h].astype(BF16)
    rkv_ref[:, RET_QK:] = zv.astype(BF16)

    rows_k = _row_index((RET_CHUNK, RET_DK))
    for h in range(RET_HEADS):
        lg = _log_sigmoid(decb_ref[h:h + 1, :])
        kdec = jnp.exp(rows_k * lg[:, :RET_DK])
        cdec = jnp.exp(RET_CHUNK * lg)
        for rc in reversed(range(tm // RET_CHUNK)):
            r0 = rc * RET_CHUNK
            state = state_ref[h]
            sb_ref[rc, h] = state.astype(BF16)
            k = (rk[h][r0:r0 + RET_CHUNK] * kdec).astype(BF16)
            v = zv[r0:r0 + RET_CHUNK, h * RET_DV:(h + 1) * RET_DV].astype(BF16)
            state_ref[h] = cdec * state + _dot_tn(k, v)


def _fill_band_bias(bias_ref):
    kk = lax.broadcasted_iota(jnp.int32, bias_ref.shape, 0)
    qi = lax.broadcasted_iota(jnp.int32, bias_ref.shape, 1) & (CHUNK - 1)
    visible = (kk >= N_BAND) | ((kk >= qi) & (kk <= qi + 2 * CHUNK))
    bias_ref[...] = jnp.where(visible, 0.0, NEG_INF)


def _fill_decay_tables(decf_ref, decb_ref, dmat_ref, qdec_ref, kdf_ref):
    rows = _row_index((RET_CHUNK, RET_CHUNK))
    rel = rows - lax.broadcasted_iota(jnp.int32, (RET_CHUNK, RET_CHUNK), 1).astype(F32)
    rows_k = _row_index((RET_CHUNK, RET_DK))
    for h in range(RET_HEADS):
        lgf = _log_sigmoid(decf_ref[h:h + 1, :])
        lgb = _log_sigmoid(decb_ref[h:h + 1, :])
        dmat_ref[h] = jnp.where(rel >= 0, jnp.exp(jnp.maximum(rel, 0.0) * lgf),
                                jnp.exp(jnp.maximum(-rel, 0.0) * lgb))
        lgf_k, lgb_k = lgf[:, :RET_DK], lgb[:, :RET_DK]
        qdec_ref[h, :, :RET_DK] = jnp.exp((rows_k + 1.0) * lgf_k)
        qdec_ref[h, :, RET_DK:] = jnp.exp((RET_CHUNK - rows_k) * lgb_k)
        kdf_ref[h] = jnp.exp((RET_CHUNK - 1.0 - rows_k) * lgf_k)


def _main_kernel(x_ref, prew_ref, postw_ref, retnw_ref, win_ref, wrb_hbm_ref, wab_hbm_ref, wo_hbm_ref, inv_ref,
                 decf_ref, decb_ref, sink_ref, rkv_ref, akp_ref, akc_ref, akn_ref, avp_ref,
                 avc_ref, avn_ref, sb_ref, kvf0_ref, km_ref, vmt_ref, out_ref,
                 sf_ref, cosb_ref, sinb_ref, bias_ref, u_ref, rq_ref, aq_ref, or_ref, oa_ref, grg_ref, gag_ref,
                 ggr_ref, gga_ref, lhs_r_ref, lhs_a_ref, mix_ref, dmat_ref, qdec_ref, kdf_ref, wmain_ref, wrb_ref,
                 wab_ref, wo_ref, stage_ref, sem_ref, *, cpt):
    t = pl.program_id(1)
    tm = cpt * CHUNK

    @pl.when((pl.program_id(0) == 0) & (t == 0))
    def _():
        jobs = [(win_ref, col, wmain_ref, i * W_CHUNK) for i, col in enumerate(MAIN_SRC)]
        for src, dst in ((wrb_hbm_ref, wrb_ref), (wab_hbm_ref, wab_ref), (wo_hbm_ref, wo_ref)):
            jobs += [(src, c, dst, c) for c in range(0, D_MODEL, W_CHUNK)]
        _load_weights_bf16(jobs, stage_ref, sem_ref)
        _fill_rope_base(inv_ref[...], cosb_ref, sinb_ref)
        _fill_band_bias(bias_ref)
        _fill_decay_tables(decf_ref, decb_ref, dmat_ref, qdec_ref, kdf_ref)

    @pl.when(t == 0)
    def _():
        sf_ref[...] = kvf0_ref[...]

    for r0 in range(0, tm, CHUNK):
        u_ref[r0:r0 + CHUNK] = _rms_norm(x_ref[r0:r0 + CHUNK], prew_ref[...]).astype(BF16)
    cos, sin = _rope_tables_from_base(N_META + t * tm, inv_ref[...], cosb_ref, sinb_ref)

    def proj(name, c0):
        col = MAIN_OFF[name] + c0
        return _dot(u_ref[...], wmain_ref[:, col:col + COL_BLOCK])

    heads_per_block = COL_BLOCK // ATT_HD
    for c0 in range(0, RET_QK, COL_BLOCK):
        z = proj("rq", c0)
        for i in range(heads_per_block):
            c = c0 + i * RET_DK
            rq_ref[:, c:c + RET_DK] = _rope(z[:, i * RET_DK:(i + 1) * RET_DK], cos, sin)
    for c0 in range(0, ATT_Q, COL_BLOCK):
        z = proj("aq", c0)
        for i in range(heads_per_block):
            c = c0 + i * ATT_HD
            aq_ref[:, c:c + ATT_HD] = (_rope(z[:, i * ATT_HD:(i + 1) * ATT_HD], cos, sin) * (ATT_HD ** -0.5)
                                       ).astype(BF16)

    gate_jobs = (("rg", jax.nn.silu, grg_ref), ("ag", jax.nn.silu, gag_ref),
                 ("gr", jax.nn.sigmoid, ggr_ref), ("ga", jax.nn.sigmoid, gga_ref))
    gate_blocks = [(j, c) for j in range(len(gate_jobs)) for c in range(0, D_MODEL, COL_BLOCK)]
    assert len(gate_blocks) % cpt == 0

    def emit_gate_blocks(n):
        for _ in range(n):
            j, c = gate_blocks.pop(0)
            name, act, ref = gate_jobs[j]
            ref[:, c:c + COL_BLOCK] = act(proj(name, c))

    cdf = [jnp.exp(RET_CHUNK * _log_sigmoid(decf_ref[h:h + 1, :])) for h in range(RET_HEADS)]

    k_cat = jnp.concatenate([akp_ref[...], akc_ref[...], akn_ref[...]], axis=0)
    vt_cat = jnp.concatenate([avp_ref[...], avc_ref[...], avn_ref[...]], axis=1)
    sink_rows = [jnp.concatenate([sink_ref[g * ATT_REP + r:g * ATT_REP + r + 1, :] for r in range(ATT_REP)],
                                 axis=1) for g in range(ATT_GROUPS)]

    def retention(rc):
        r0 = rc * RET_CHUNK
        rs = slice(r0, r0 + RET_CHUNK)
        for h in range(RET_HEADS):
            q = rq_ref[rs, h * RET_DK:(h + 1) * RET_DK]
            k = rkv_ref[rs, h * RET_DK:(h + 1) * RET_DK]
            v = rkv_ref[rs, RET_QK + h * RET_DV:RET_QK + (h + 1) * RET_DV]
            s = _dot_nt(q.astype(BF16), k) * dmat_ref[h]
            sf = sf_ref[h]
            q_cross = (jnp.concatenate([q, q], axis=1) * qdec_ref[h]).astype(BF16)
            s_cross = jnp.concatenate([sf.astype(BF16), sb_ref[rc, h]], axis=0)
            o = _dot(s.astype(BF16), v) + _dot(q_cross, s_cross)
            sf_ref[h] = cdf[h] * sf + _dot_tn((k.astype(F32) * kdf_ref[h]).astype(BF16), v)
            mu = jnp.mean(o, axis=-1, keepdims=True)
            d = o - mu
            var = jnp.mean(d * d, axis=-1, keepdims=True)
            or_ref[rs, h * RET_DV:(h + 1) * RET_DV] = d * lax.rsqrt(var + EPS)

    for rc in range(tm // RET_CHUNK):
        retention(rc)

    gates_per_pair = len(gate_blocks) // (cpt * ATT_GROUPS)
    pairs = [(lc, g) for lc in range(cpt) for g in range(ATT_GROUPS)]
    probs, denoms = {}, {}
    for lc, g in pairs:
        r0 = lc * CHUNK
        chunk = t * cpt + lc
        gs = slice(g * ATT_HD, (g + 1) * ATT_HD)
        qs = jnp.concatenate([aq_ref[r0:r0 + CHUNK, (g * ATT_REP + r) * ATT_HD:(g * ATT_REP + r + 1) * ATT_HD]
                              for r in range(ATT_REP)], axis=0)
        k_all = jnp.concatenate([k_cat[r0:r0 + N_BAND, gs], km_ref[:, gs]], axis=0)
        emit_gate_blocks(gates_per_pair)
        s = _dot_nt(k_all, qs) + bias_ref[...]
        parts =[s[0:CHUNK], s[CHUNK:2 * CHUNK], s[2 * CHUNK:N_BAND], s[N_BAND:]]
        if lc == 0:
            parts[0] = jnp.where(chunk > 0, parts[0], NEG_INF)
        if lc == cpt - 1:
            parts[2] = jnp.where(chunk < N_CHUNKS - 1, parts[2], NEG_INF)
        s = jnp.concatenate(parts, axis=0)
        sk = sink_rows[g]
        m = jnp.maximum(jnp.max(s, axis=0, keepdims=True), sk)
        p = jnp.exp(s - m)
        denoms[lc, g] = jnp.sum(p, axis=0, keepdims=True) + jnp.exp(sk - m)
        probs[lc, g] = p.astype(BF16)
    for lc, g in pairs:
        r0 = lc * CHUNK
        gs = slice(g * ATT_HD, (g + 1) * ATT_HD)
        vt_all = jnp.concatenate([vt_cat[gs, r0:r0 + N_BAND], vmt_ref[gs, :]], axis=1)
        ot = _dot(vt_all, probs[lc, g]) / denoms[lc, g]
        for r in range(ATT_REP):
            c = (g * ATT_REP + r) * ATT_HD
            oa_ref[r0:r0 + CHUNK, c:c + ATT_HD] = ot[:, r * CHUNK:(r + 1) * CHUNK].T

    for r0 in range(0, tm, CHUNK):
        rs = slice(r0, r0 + CHUNK)
        lhs_r_ref[rs] = (or_ref[rs] * retnw_ref[...] * grg_ref[rs]).astype(BF16)
        lhs_a_ref[rs] = (oa_ref[rs] * gag_ref[rs]).astype(BF16)
    for c0 in range(0, D_MODEL, COL_BLOCK):
        cs = slice(c0, c0 + COL_BLOCK)
        y_r = _dot(lhs_r_ref[...], wrb_ref[:, cs])
        y_a = _dot(lhs_a_ref[...], wab_ref[:, cs])
        mix_ref[:, cs] = (ggr_ref[:, cs] * y_r + gga_ref[:, cs] * y_a).astype(BF16)
    for c0 in range(0, D_MODEL, COL_BLOCK):
        cs = slice(c0, c0 + COL_BLOCK)
        out_ref[:, cs] = _dot(mix_ref[...], wo_ref[:, cs])
    for r0 in range(0, tm, CHUNK):
        rs = slice(r0, r0 + CHUNK)
        out_ref[rs] = x_ref[rs] + _rms_norm(out_ref[rs], postw_ref[...])


def _resident(shape, index=None):
    nd = len(shape)
    index = (0,) * nd if index is None else index
    return pl.BlockSpec(shape, lambda *_: index, pipeline_mode=pl.Buffered(1))


def kernel(x, meta_tokens, pre_norm_w, w_in, ret_decay_fwd, ret_decay_bwd, ret_norm_w, w_ret_branch, attn_sink,
           w_attn_branch, w_out, post_norm_w):
    B = x.shape[0]
    assert x.shape == (B, SEQ, D_MODEL) and pre_norm_w.shape[0] == 1 and w_in.shape == (1, D_MODEL, D_IN)
    cpt = CHUNKS_PER_TILE
    tm = cpt * CHUNK
    nt = N_CHUNKS // cpt

    w_in = w_in.astype(F32)
    w_rb, w_ab, w_o = (w.astype(F32) for w in (w_ret_branch, w_attn_branch, w_out))
    pre_w = pre_norm_w.astype(F32)
    post_w = post_norm_w.astype(F32)
    ret_nw = ret_norm_w.astype(F32)
    half = ATT_HD // 2
    inv = ROPE_THETA ** (-jnp.arange(half, dtype=F32) * 2.0 / ATT_HD)
    inv = jnp.concatenate([inv, inv])[None, :]
    assert ATT_HEADS == SUBLANES and RET_HEADS <= SUBLANES
    pad = jnp.zeros((SUBLANES - RET_HEADS,), F32)
    per_head = jnp.concatenate([ret_decay_fwd[0].astype(F32), pad, ret_decay_bwd[0].astype(F32), pad,
                                attn_sink[0].astype(F32)])
    per_head = jnp.broadcast_to(per_head[:, None], (2 * SUBLANES + ATT_HEADS, RET_DV))
    dec_f_spec = _resident((SUBLANES, RET_DV), (0, 0))
    dec_b_spec = _resident((SUBLANES, RET_DV), (1, 0))
    sink_spec = _resident((ATT_HEADS, ATT_HD), (2, 0))

    params = pltpu.CompilerParams(dimension_semantics=("arbitrary", "arbitrary"),
                                  vmem_limit_bytes=V7X_VMEM_LIMIT_BYTES)
    rope_scratch = [pltpu.VMEM((tm, ATT_HD), F32), pltpu.VMEM((tm, ATT_HD), F32)]
    state_scratch = pltpu.VMEM((RET_HEADS, RET_DK, RET_DV), F32)
    weight_stage = [pltpu.VMEM((2, D_MODEL, W_CHUNK), F32), pltpu.SemaphoreType.DMA((2,))]
    hbm = pl.BlockSpec(memory_space=pl.ANY)

    kcpt = KV_CHUNKS_PER_TILE
    ktm, knt = kcpt * CHUNK, N_CHUNKS // kcpt
    rkv, ak, avt, sb, kvf0, km, vmt = pl.pallas_call(
        functools.partial(_kv_kernel, cpt=kcpt),
        grid=(B, knt),
        in_specs=[pl.BlockSpec((None, ktm, D_MODEL), lambda b, t: (b, knt - 1 - t, 0)), _resident((N_META, D_MODEL)),
                  _resident((1, D_MODEL)), hbm, _resident((1, ATT_HD)), dec_f_spec, dec_b_spec],
        out_specs=(
            pl.BlockSpec((None, ktm, RKV_COLS), lambda b, t: (b, knt - 1 - t, 0)),
            pl.BlockSpec((None, ktm, ATT_KV), lambda b, t: (b, knt - 1 - t, 0)),
            pl.BlockSpec((None, ATT_KV, ktm), lambda b, t: (b, 0, knt - 1 - t)),
            pl.BlockSpec((None, ktm // RET_CHUNK, RET_HEADS, RET_DK, RET_DV), lambda b, t: (b, knt - 1 - t, 0, 0, 0)),
            pl.BlockSpec((RET_HEADS, RET_DK, RET_DV), lambda b, t: (0, 0, 0)),
            pl.BlockSpec((N_META, ATT_KV), lambda b, t: (0, 0)),
            pl.BlockSpec((ATT_KV, N_META), lambda b, t: (0, 0)),
        ),
        out_shape=(jax.ShapeDtypeStruct((B, SEQ, RKV_COLS), BF16),
                   jax.ShapeDtypeStruct((B, SEQ, ATT_KV), BF16),
                   jax.ShapeDtypeStruct((B, ATT_KV, SEQ), BF16),
                   jax.ShapeDtypeStruct((B, N_RET_CHUNKS, RET_HEADS, RET_DK, RET_DV), BF16),
                   jax.ShapeDtypeStruct((RET_HEADS, RET_DK, RET_DV), F32),
                   jax.ShapeDtypeStruct((N_META, ATT_KV), BF16),
                   jax.ShapeDtypeStruct((ATT_KV, N_META), BF16)),
        scratch_shapes=[state_scratch, pltpu.VMEM((ktm, ATT_HD), F32), pltpu.VMEM((ktm, ATT_HD), F32),
                        pltpu.VMEM((D_MODEL, KV_COLS), BF16)] + weight_stage,
        compiler_params=params,
        name="kv",
    )(x, meta_tokens.astype(F32), pre_w, w_in, inv, per_head, per_head)

    prev_chunk = lambda t: jnp.maximum(t * cpt - 1, 0)
    next_chunk = lambda t: jnp.minimum((t + 1) * cpt, N_CHUNKS - 1)
    out = pl.pallas_call(
        functools.partial(_main_kernel, cpt=cpt),
        grid=(B, nt),
        in_specs=[
            pl.BlockSpec((None, tm, D_MODEL), lambda b, t: (b, t, 0)),
            _resident((1, D_MODEL)),
            _resident((1, D_MODEL)),
            _resident((1, RET_V)),
            hbm, hbm, hbm, hbm,
            _resident((1, ATT_HD)),
            dec_f_spec,
            dec_b_spec,
            sink_spec,
            pl.BlockSpec((None, tm, RKV_COLS), lambda b, t: (b, t, 0)),
            pl.BlockSpec((None, CHUNK, ATT_KV), lambda b, t: (b, prev_chunk(t), 0)),
            pl.BlockSpec((None, tm, ATT_KV), lambda b, t: (b, t, 0)),
            pl.BlockSpec((None, CHUNK, ATT_KV), lambda b, t: (b, next_chunk(t), 0)),
            pl.BlockSpec((None, ATT_KV, CHUNK), lambda b, t: (b, 0, prev_chunk(t))),
            pl.BlockSpec((None, ATT_KV, tm), lambda b, t: (b, 0, t)),
            pl.BlockSpec((None, ATT_KV, CHUNK), lambda b, t: (b, 0, next_chunk(t))),
            pl.BlockSpec((None, tm // RET_CHUNK, RET_HEADS, RET_DK, RET_DV), lambda b, t: (b, t, 0, 0, 0)),
            _resident((RET_HEADS, RET_DK, RET_DV)),
            _resident((N_META, ATT_KV)),
            _resident((ATT_KV, N_META)),
        ],
        out_specs=pl.BlockSpec((None, tm, D_MODEL), lambda b, t: (b, t, 0)),
        out_shape=jax.ShapeDtypeStruct((B, SEQ, D_MODEL), x.dtype),
        scratch_shapes=(
            [state_scratch] + rope_scratch
            + [pltpu.VMEM((N_KEYS, Q_ROWS), F32),
               pltpu.VMEM((tm, D_MODEL), BF16),
               pltpu.VMEM((tm, RET_QK), F32),
               pltpu.VMEM((tm, ATT_Q), BF16),
               pltpu.VMEM((tm, RET_V), F32),
               pltpu.VMEM((tm, ATT_Q), F32)]
            + [pltpu.VMEM((tm, D_MODEL), F32)] * 4
            + [pltpu.VMEM((tm, D_MODEL), BF16)] * 3
            + [pltpu.VMEM((RET_HEADS, RET_CHUNK, RET_CHUNK), F32),
               pltpu.VMEM((RET_HEADS, RET_CHUNK, 2 * RET_DK), F32),
               pltpu.VMEM((RET_HEADS, RET_CHUNK, RET_DK), F32)]
            + [pltpu.VMEM((D_MODEL, MAIN_COLS), BF16)]
            + [pltpu.VMEM((D_MODEL, D_MODEL), BF16)] * 3
            + weight_stage
        ),
        compiler_params=params,
        name="main",
    )(x, pre_w, post_w, ret_nw, w_in, w_rb, w_ab, w_o, inv, per_head, per_head, per_head,
      rkv, ak, ak, ak, avt, avt, avt, sb, kvf0, km, vmt)
    return out
```

```python
import functools

import jax
import jax.numpy as jnp
from jax import lax
from jax.experimental import pallas as pl
from jax.experimental.pallas import tpu as pltpu

D_MODEL = 1024
SEQ = 8192
N_META = 16
CHUNK = 128
RET_HEADS = 4
RET_DK = 128
RET_DV = 256
ATT_HEADS = 8
ATT_GROUPS = 2
ATT_REP = ATT_HEADS // ATT_GROUPS
ATT_HD = 128
ROPE_THETA = 10000.0
EPS = 1e-6
NEG_INF = -1e30
RET_QK = RET_HEADS * RET_DK
RET_V = RET_HEADS * RET_DV
ATT_Q = ATT_HEADS * ATT_HD
ATT_KV = ATT_GROUPS * ATT_HD
D_IN = 2 * RET_QK + 2 * RET_V + 2 * ATT_Q + 2 * ATT_KV + 2 * D_MODEL
N_CHUNKS = SEQ // CHUNK
N_BAND = 3 * CHUNK
N_KEYS = N_BAND + N_META
Q_ROWS = ATT_REP * CHUNK

OFF_RQ = 0
OFF_RK = OFF_RQ + RET_QK
OFF_RV = OFF_RK + RET_QK
OFF_RG = OFF_RV + RET_V
OFF_AQ = OFF_RG + RET_V
OFF_AK = OFF_AQ + ATT_Q
OFF_AV = OFF_AK + ATT_KV
OFF_AG = OFF_AV + ATT_KV
OFF_GR = OFF_AG + ATT_Q
OFF_GA = OFF_GR + D_MODEL
RKV_COLS = RET_QK + RET_V

W_CHUNK = 512
KV_SRC = tuple(range(OFF_RK, OFF_RG, W_CHUNK)) + tuple(range(OFF_AK, OFF_AG, W_CHUNK))
KV_RK, KV_RV, KV_AKV = 0, RET_QK, RET_QK + RET_V
KV_COLS = len(KV_SRC) * W_CHUNK
MAIN_SEGMENTS = (("rq", OFF_RQ, RET_QK), ("aq", OFF_AQ, ATT_Q), ("rg", OFF_RG, RET_V), ("ag", OFF_AG, ATT_Q),
                 ("gr", OFF_GR, D_MODEL), ("ga", OFF_GA, D_MODEL))
MAIN_SRC = tuple(c for _, off, width in MAIN_SEGMENTS for c in range(off, off + width, W_CHUNK))
MAIN_OFF = {}
_o = 0
for _name, _, _width in MAIN_SEGMENTS:
    MAIN_OFF[_name] = _o
    _o += _width
MAIN_COLS = _o

CHUNKS_PER_TILE = 4
KV_CHUNKS_PER_TILE = 8
RET_CHUNK = 2 * CHUNK
N_RET_CHUNKS = SEQ // RET_CHUNK
COL_BLOCK = 256
SUBLANES = 8
V7X_VMEM_LIMIT_BYTES = 56 * 1024 * 1024

F32 = jnp.float32
BF16 = jnp.bfloat16


def _rms_norm(x, w):
    return x * lax.rsqrt(jnp.mean(x * x, axis=-1, keepdims=True) + EPS) * w


def _log_sigmoid(x):
    return jnp.minimum(x, 0.0) - jnp.log(1.0 + jnp.exp(-jnp.abs(x)))


def _sign_fold(sin):
    lane = lax.broadcasted_iota(jnp.int32, sin.shape, 1)
    return jnp.where(lane < ATT_HD // 2, -sin, sin)


def _rope_tables(rows, inv):
    ang = lax.broadcasted_iota(jnp.int32, (rows, ATT_HD), 0).astype(F32) * inv
    return jnp.cos(ang), _sign_fold(jnp.sin(ang))


def _fill_rope_base(inv, cosb_ref, sinb_ref):
    ang = lax.broadcasted_iota(jnp.int32, cosb_ref.shape, 0).astype(F32) * inv
    cosb_ref[...] = jnp.cos(ang)
    sinb_ref[...] = jnp.sin(ang)


def _rope_tables_from_base(pos0, inv, cosb_ref, sinb_ref):
    base = pos0.astype(F32) * inv
    ca, sa = jnp.cos(base), jnp.sin(base)
    cb, sb = cosb_ref[...], sinb_ref[...]
    return ca * cb - sa * sb, _sign_fold(sa * cb + ca * sb)


def _rope(t, cos, sin_signed):
    return t * cos + pltpu.roll(t, ATT_HD // 2, axis=1) * sin_signed


def _dot(a, b):
    return jnp.dot(a, b, preferred_element_type=F32)


def _dot_nt(a, b):
    return lax.dot_general(a, b, (((1,), (1,)), ((), ())), preferred_element_type=F32)


def _dot_tn(a, b):
    return lax.dot_general(a, b, (((0,), (0,)), ((), ())), preferred_element_type=F32)


def _row_index(shape):
    return lax.broadcasted_iota(jnp.int32, shape, 0).astype(F32)


def _weight_chunk_copy(src_ref, col, stage_ref, sem_ref, slot):
    return pltpu.make_async_copy(src_ref.at[0, :, pl.ds(col, W_CHUNK)], stage_ref.at[slot], sem_ref.at[slot])


def _load_weights_bf16(jobs, stage_ref, sem_ref):
    copies = [_weight_chunk_copy(src, col, stage_ref, sem_ref, i % 2) for i, (src, col, _, _) in enumerate(jobs)]
    copies[0].start()
    for i, (_, _, dst, dcol) in enumerate(jobs):
        if i + 1 < len(jobs):
            copies[i + 1].start()
        copies[i].wait()
        dst[:, dcol:dcol + W_CHUNK] = stage_ref[i % 2].astype(BF16)


def _meta_block(meta_ref, prew_ref, wkv_ref, inv_ref, decf_ref, kvf0_ref, km_ref, vmt_ref):
    u = _rms_norm(meta_ref[...], prew_ref[...]).astype(BF16)
    zk = _dot(u, wkv_ref[:, KV_RK:KV_RV])
    zv = _dot(u, wkv_ref[:, KV_RV:KV_AKV])
    za = _dot(u, wkv_ref[:, KV_AKV:])
    cos, sin = _rope_tables(N_META, inv_ref[...])
    kdec_rows = (N_META - 1) - _row_index((N_META, RET_DK))
    for h in range(RET_HEADS):
        lg = _log_sigmoid(decf_ref[h:h + 1, :RET_DK])
        k = _rope(zk[:, h * RET_DK:(h + 1) * RET_DK], cos, sin) * (RET_DK ** -0.5)
        k = (k * jnp.exp(kdec_rows * lg)).astype(BF16)
        v = zv[:, h * RET_DV:(h + 1) * RET_DV].astype(BF16)
        kvf0_ref[h] = _dot_tn(k, v)
    for g in range(ATT_GROUPS):
        km_ref[:, g * ATT_HD:(g + 1) * ATT_HD] = _rope(za[:, g * ATT_HD:(g + 1) * ATT_HD], cos, sin).astype(BF16)
    vmt_ref[...] = za[:, ATT_KV:].T.astype(BF16)


def _kv_kernel(x_ref, meta_ref, prew_ref, win_ref, inv_ref, decf_ref, decb_ref, rkv_ref, ak_ref, avt_ref, sb_ref,
               kvf0_ref, km_ref, vmt_ref, state_ref, cosb_ref, sinb_ref, wkv_ref, stage_ref, sem_ref, *, cpt):
    t = pl.program_id(1)
    tile = pl.num_programs(1) - 1 - t
    tm = cpt * CHUNK

    @pl.when((pl.program_id(0) == 0) & (t == 0))
    def _():
        _load_weights_bf16([(win_ref, col, wkv_ref, i * W_CHUNK) for i, col in enumerate(KV_SRC)],
                           stage_ref, sem_ref)
        _fill_rope_base(inv_ref[...], cosb_ref, sinb_ref)
        _meta_block(meta_ref, prew_ref, wkv_ref, inv_ref, decf_ref, kvf0_ref, km_ref, vmt_ref)

    @pl.when(t == 0)
    def _():
        state_ref[...] = jnp.zeros_like(state_ref)

    u = _rms_norm(x_ref[...], prew_ref[...]).astype(BF16)
    za = _dot(u, wkv_ref[:, KV_AKV:])
    zk = _dot(u, wkv_ref[:, KV_RK:KV_RV])
    zv = _dot(u, wkv_ref[:, KV_RV:KV_AKV])
    cos, sin = _rope_tables_from_base(N_META + tile * tm, inv_ref[...], cosb_ref, sinb_ref)
    for g in range(ATT_GROUPS):
        ak_ref[:, g * ATT_HD:(g + 1) * ATT_HD] = _rope(za[:, g * ATT_HD:(g + 1) * ATT_HD], cos, sin).astype(BF16)
    avt_ref[...] = za[:, ATT_KV:].T.astype(BF16)
    rk = [_rope(zk[:, h * RET_DK:(h + 1) * RET_DK], cos, sin) * (RET_DK ** -0.5) for h in range(RET_HEADS)]
    for h in range(RET_HEADS):
        rkv_ref[:, h * RET_DK:(h + 1) * RET_DK] = rk[h].astype(BF16)
    rkv_ref[:, RET_QK:] = zv.astype(BF16)

    rows_k = _row_index((RET_CHUNK, RET_DK))
    for h in range(RET_HEADS):
        lg = _log_sigmoid(decb_ref[h:h + 1, :])
        kdec = jnp.exp(rows_k * lg[:, :RET_DK])
        cdec = jnp.exp(RET_CHUNK * lg)
        for rc in reversed(range(tm // RET_CHUNK)):
            r0 = rc * RET_CHUNK
            state = state_ref[h]
            sb_ref[rc, h] = state.astype(BF16)
            k = (rk[h][r0:r0 + RET_CHUNK] * kdec).astype(BF16)
            v = zv[r0:r0 + RET_CHUNK, h * RET_DV:(h + 1) * RET_DV].astype(BF16)
            state_ref[h] = cdec * state + _dot_tn(k, v)


def _fill_band_bias(bias_ref):
    kk = lax.broadcasted_iota(jnp.int32, bias_ref.shape, 0)
    qi = lax.broadcasted_iota(jnp.int32, bias_ref.shape, 1) & (CHUNK - 1)
    visible = (kk >= N_BAND) | ((kk >= qi) & (kk <= qi + 2 * CHUNK))
    bias_ref[...] = jnp.where(visible, 0.0, NEG_INF)


def _fill_decay_tables(decf_ref, decb_ref, dmat_ref, qdec_ref, kdf_ref):
    rows = _row_index((RET_CHUNK, RET_CHUNK))
    rel = rows - lax.broadcasted_iota(jnp.int32, (RET_CHUNK, RET_CHUNK), 1).astype(F32)
    rows_k = _row_index((RET_CHUNK, RET_DK))
    for h in range(RET_HEADS):
        lgf = _log_sigmoid(decf_ref[h:h + 1, :])
        lgb = _log_sigmoid(decb_ref[h:h + 1, :])
        dmat_ref[h] = jnp.where(rel >= 0, jnp.exp(jnp.maximum(rel, 0.0) * lgf),
                                jnp.exp(jnp.maximum(-rel, 0.0) * lgb))
        lgf_k, lgb_k = lgf[:, :RET_DK], lgb[:, :RET_DK]
        qdec_ref[h, :, :RET_DK] = jnp.exp((rows_k + 1.0) * lgf_k)
        qdec_ref[h, :, RET_DK:] = jnp.exp((RET_CHUNK - rows_k) * lgb_k)
        kdf_ref[h] = jnp.exp((RET_CHUNK - 1.0 - rows_k) * lgf_k)


def _main_kernel(x_ref, prew_ref, postw_ref, retnw_ref, win_ref, wrb_hbm_ref, wab_hbm_ref, wo_hbm_ref, inv_ref,
                 decf_ref, decb_ref, sink_ref, rkv_ref, akp_ref, akc_ref, akn_ref, avp_ref,
                 avc_ref, avn_ref, sb_ref, kvf0_ref, km_ref, vmt_ref, out_ref,
                 sf_ref, cosb_ref, sinb_ref, bias_ref, u_ref, rq_ref, aq_ref, or_ref, oa_ref, grg_ref, gag_ref,
                 ggr_ref, gga_ref, lhs_r_ref, lhs_a_ref, mix_ref, dmat_ref, qdec_ref, kdf_ref, wmain_ref, wrb_ref,
                 wab_ref, wo_ref, stage_ref, sem_ref, *, cpt):
    t = pl.program_id(1)
    tm = cpt * CHUNK

    @pl.when((pl.program_id(0) == 0) & (t == 0))
    def _():
        jobs = [(win_ref, col, wmain_ref, i * W_CHUNK) for i, col in enumerate(MAIN_SRC)]
        for src, dst in ((wrb_hbm_ref, wrb_ref), (wab_hbm_ref, wab_ref), (wo_hbm_ref, wo_ref)):
            jobs += [(src, c, dst, c) for c in range(0, D_MODEL, W_CHUNK)]
        _load_weights_bf16(jobs, stage_ref, sem_ref)
        _fill_rope_base(inv_ref[...], cosb_ref, sinb_ref)
        _fill_band_bias(bias_ref)
        _fill_decay_tables(decf_ref, decb_ref, dmat_ref, qdec_ref, kdf_ref)

    @pl.when(t == 0)
    def _():
        sf_ref[...] = kvf0_ref[...]

    for r0 in range(0, tm, CHUNK):
        u_ref[r0:r0 + CHUNK] = _rms_norm(x_ref[r0:r0 + CHUNK], prew_ref[...]).astype(BF16)
    cos, sin = _rope_tables_from_base(N_META + t * tm, inv_ref[...], cosb_ref, sinb_ref)

    def proj(name, c0):
        col = MAIN_OFF[name] + c0
        return _dot(u_ref[...], wmain_ref[:, col:col + COL_BLOCK])

    heads_per_block = COL_BLOCK // ATT_HD
    for c0 in range(0, RET_QK, COL_BLOCK):
        z = proj("rq", c0)
        for i in range(heads_per_block):
            c = c0 + i * RET_DK
            rq_ref[:, c:c + RET_DK] = _rope(z[:, i * RET_DK:(i + 1) * RET_DK], cos, sin)
    for c0 in range(0, ATT_Q, COL_BLOCK):
        z = proj("aq", c0)
        for i in range(heads_per_block):
            c = c0 + i * ATT_HD
            aq_ref[:, c:c + ATT_HD] = (_rope(z[:, i * ATT_HD:(i + 1) * ATT_HD], cos, sin) * (ATT_HD ** -0.5)
                                       ).astype(BF16)

    gate_jobs = (("rg", jax.nn.silu, grg_ref), ("ag", jax.nn.silu, gag_ref),
                 ("gr", jax.nn.sigmoid, ggr_ref), ("ga", jax.nn.sigmoid, gga_ref))
    gate_blocks = [(j, c) for j in range(len(gate_jobs)) for c in range(0, D_MODEL, COL_BLOCK)]
    assert len(gate_blocks) % cpt == 0

    def emit_gate_blocks(n):
        for _ in range(n):
            j, c = gate_blocks.pop(0)
            name, act, ref = gate_jobs[j]
            ref[:, c:c + COL_BLOCK] = act(proj(name, c))

    cdf = [jnp.exp(RET_CHUNK * _log_sigmoid(decf_ref[h:h + 1, :])) for h in range(RET_HEADS)]

    k_cat = jnp.concatenate([akp_ref[...], akc_ref[...], akn_ref[...]], axis=0)
    vt_cat = jnp.concatenate([avp_ref[...], avc_ref[...], avn_ref[...]], axis=1)
    sink_rows = [jnp.concatenate([sink_ref[g * ATT_REP + r:g * ATT_REP + r + 1, :] for r in range(ATT_REP)],
                                 axis=1) for g in range(ATT_GROUPS)]

    def retention(rc):
        r0 = rc * RET_CHUNK
        rs = slice(r0, r0 + RET_CHUNK)
        for h in range(RET_HEADS):
            q = rq_ref[rs, h * RET_DK:(h + 1) * RET_DK]
            k = rkv_ref[rs, h * RET_DK:(h + 1) * RET_DK]
            v = rkv_ref[rs, RET_QK + h * RET_DV:RET_QK + (h + 1) * RET_DV]
            s = _dot_nt(q.astype(BF16), k) * dmat_ref[h]
            sf = sf_ref[h]
            q_cross = (jnp.concatenate([q, q], axis=1) * qdec_ref[h]).astype(BF16)
            s_cross = jnp.concatenate([sf.astype(BF16), sb_ref[rc, h]], axis=0)
            o = _dot(s.astype(BF16), v) + _dot(q_cross, s_cross)
            sf_ref[h] = cdf[h] * sf + _dot_tn((k.astype(F32) * kdf_ref[h]).astype(BF16), v)
            mu = jnp.mean(o, axis=-1, keepdims=True)
            d = o - mu
            var = jnp.mean(d * d, axis=-1, keepdims=True)
            or_ref[rs, h * RET_DV:(h + 1) * RET_DV] = d * lax.rsqrt(var + EPS)

    for rc in range(tm // RET_CHUNK):
        retention(rc)

    gates_per_pair = len(gate_blocks) // (cpt * ATT_GROUPS)
    pairs = [(lc, g) for lc in range(cpt) for g in range(ATT_GROUPS)]
    probs, denoms = {}, {}
    for lc, g in pairs:
        r0 = lc * CHUNK
        chunk = t * cpt + lc
        gs = slice(g * ATT_HD, (g + 1) * ATT_HD)
        qs = jnp.concatenate([aq_ref[r0:r0 + CHUNK, (g * ATT_REP + r) * ATT_HD:(g * ATT_REP + r + 1) * ATT_HD]
                              for r in range(ATT_REP)], axis=0)
        k_all = jnp.concatenate([k_cat[r0:r0 + N_BAND, gs], km_ref[:, gs]], axis=0)
        emit_gate_blocks(gates_per_pair)
        s = _dot_nt(k_all, qs) + bias_ref[...]
        parts = [s[0:CHUNK], s[CHUNK:2 * CHUNK], s[2 * CHUNK:N_BAND], s[N_BAND:]]
        if lc == 0:
            parts[0] = jnp.where(chunk > 0, parts[0], NEG_INF)
        if lc == cpt - 1:
            parts[2] = jnp.where(chunk < N_CHUNKS - 1, parts[2], NEG_INF)
        s = jnp.concatenate(parts, axis=0)
        sk = sink_rows[g]
        m = jnp.maximum(jnp.max(s, axis=0, keepdims=True), sk)
        p = jnp.exp(s - m)
        denoms[lc, g] = jnp.sum(p, axis=0, keepdims=True) + jnp.exp(sk - m)
        probs[lc, g] = p.astype(BF16)
    for lc, g in pairs:
        r0 = lc * CHUNK
        gs = slice(g * ATT_HD, (g + 1) * ATT_HD)
        vt_all = jnp.concatenate([vt_cat[gs, r0:r0 + N_BAND], vmt_ref[gs, :]], axis=1)
        ot = _dot(vt_all, probs[lc, g]) / denoms[lc, g]
        for r in range(ATT_REP):
            c = (g * ATT_REP + r) * ATT_HD
            oa_ref[r0:r0 + CHUNK, c:c + ATT_HD] = ot[:, r * CHUNK:(r + 1) * CHUNK].T

    for r0 in range(0, tm, CHUNK):
        rs = slice(r0, r0 + CHUNK)
        lhs_r_ref[rs] = (or_ref[rs] * retnw_ref[...] * grg_ref[rs]).astype(BF16)
        lhs_a_ref[rs] = (oa_ref[rs] * gag_ref[rs]).astype(BF16)
    for c0 in range(0, D_MODEL, COL_BLOCK):
        cs = slice(c0, c0 + COL_BLOCK)
        y_r = _dot(lhs_r_ref[...], wrb_ref[:, cs])
        y_a = _dot(lhs_a_ref[...], wab_ref[:, cs])
        mix_ref[:, cs] = (ggr_ref[:, cs] * y_r + gga_ref[:, cs] * y_a).astype(BF16)
    for c0 in range(0, D_MODEL, COL_BLOCK):
        cs = slice(c0, c0 + COL_BLOCK)
        out_ref[:, cs] = _dot(mix_ref[...], wo_ref[:, cs])
    for r0 in range(0, tm, CHUNK):
        rs = slice(r0, r0 + CHUNK)
        out_ref[rs] = x_ref[rs] + _rms_norm(out_ref[rs], postw_ref[...])


def _resident(shape, index=None):
    nd = len(shape)
    index = (0,) * nd if index is None else index
    return pl.BlockSpec(shape, lambda *_: index, pipeline_mode=pl.Buffered(1))


def kernel(x, meta_tokens, pre_norm_w, w_in, ret_decay_fwd, ret_decay_bwd, ret_norm_w, w_ret_branch, attn_sink,
           w_attn_branch, w_out, post_norm_w):
    B = x.shape[0]
    assert x.shape == (B, SEQ, D_MODEL) and pre_norm_w.shape[0] == 1 and w_in.shape == (1, D_MODEL, D_IN)
    cpt = CHUNKS_PER_TILE
    tm = cpt * CHUNK
    nt = N_CHUNKS // cpt

    w_in = w_in.astype(F32)
    w_rb, w_ab, w_o = (w.astype(F32) for w in (w_ret_branch, w_attn_branch, w_out))
    pre_w = pre_norm_w.astype(F32)
    post_w = post_norm_w.astype(F32)
    ret_nw = ret_norm_w.astype(F32)
    half = ATT_HD // 2
    inv = ROPE_THETA ** (-jnp.arange(half, dtype=F32) * 2.0 / ATT_HD)
    inv = jnp.concatenate([inv, inv])[None, :]
    assert ATT_HEADS == SUBLANES and RET_HEADS <= SUBLANES
    pad = jnp.zeros((SUBLANES - RET_HEADS,), F32)
    per_head = jnp.concatenate([ret_decay_fwd[0].astype(F32), pad, ret_decay_bwd[0].astype(F32), pad,
                                attn_sink[0].astype(F32)])
    per_head = jnp.broadcast_to(per_head[:, None], (2 * SUBLANES + ATT_HEADS, RET_DV))
    dec_f_spec = _resident((SUBLANES, RET_DV), (0, 0))
    dec_b_spec = _resident((SUBLANES, RET_DV), (1, 0))
    sink_spec = _resident((ATT_HEADS, ATT_HD), (2, 0))

    params = pltpu.CompilerParams(dimension_semantics=("arbitrary", "arbitrary"),
                                  vmem_limit_bytes=V7X_VMEM_LIMIT_BYTES)
    rope_scratch = [pltpu.VMEM((tm, ATT_HD), F32), pltpu.VMEM((tm, ATT_HD), F32)]
    state_scratch = pltpu.VMEM((RET_HEADS, RET_DK, RET_DV), F32)
    weight_stage = [pltpu.VMEM((2, D_MODEL, W_CHUNK), F32), pltpu.SemaphoreType.DMA((2,))]
    hbm = pl.BlockSpec(memory_space=pl.ANY)

    kcpt = KV_CHUNKS_PER_TILE
    ktm, knt = kcpt * CHUNK, N_CHUNKS // kcpt
    rkv, ak, avt, sb, kvf0, km, vmt = pl.pallas_call(
        functools.partial(_kv_kernel, cpt=kcpt),
        grid=(B, knt),
        in_specs=[pl.BlockSpec((None, ktm, D_MODEL), lambda b, t: (b, knt - 1 - t, 0)), _resident((N_META, D_MODEL)),
                  _resident((1, D_MODEL)), hbm, _resident((1, ATT_HD)), dec_f_spec, dec_b_spec],
        out_specs=(
            pl.BlockSpec((None, ktm, RKV_COLS), lambda b, t: (b, knt - 1 - t, 0)),
            pl.BlockSpec((None, ktm, ATT_KV), lambda b, t: (b, knt - 1 - t, 0)),
            pl.BlockSpec((None, ATT_KV, ktm), lambda b, t: (b, 0, knt - 1 - t)),
            pl.BlockSpec((None, ktm // RET_CHUNK, RET_HEADS, RET_DK, RET_DV), lambda b, t: (b, knt - 1 - t, 0, 0, 0)),
            pl.BlockSpec((RET_HEADS, RET_DK, RET_DV), lambda b, t: (0, 0, 0)),
            pl.BlockSpec((N_META, ATT_KV), lambda b, t: (0, 0)),
            pl.BlockSpec((ATT_KV, N_META), lambda b, t: (0, 0)),
        ),
        out_shape=(jax.ShapeDtypeStruct((B, SEQ, RKV_COLS), BF16),
                   jax.ShapeDtypeStruct((B, SEQ, ATT_KV), BF16),
                   jax.ShapeDtypeStruct((B, ATT_KV, SEQ), BF16),
                   jax.ShapeDtypeStruct((B, N_RET_CHUNKS, RET_HEADS, RET_DK, RET_DV), BF16),
                   jax.ShapeDtypeStruct((RET_HEADS, RET_DK, RET_DV), F32),
                   jax.ShapeDtypeStruct((N_META, ATT_KV), BF16),
                   jax.ShapeDtypeStruct((ATT_KV, N_META), BF16)),
        scratch_shapes=[state_scratch, pltpu.VMEM((ktm, ATT_HD), F32), pltpu.VMEM((ktm, ATT_HD), F32),
                        pltpu.VMEM((D_MODEL, KV_COLS), BF16)] + weight_stage,
        compiler_params=params,
        name="kv",
    )(x, meta_tokens.astype(F32), pre_w, w_in, inv, per_head, per_head)

    prev_chunk = lambda t: jnp.maximum(t * cpt - 1, 0)
    next_chunk = lambda t: jnp.minimum((t + 1) * cpt, N_CHUNKS - 1)
    out = pl.pallas_call(
        functools.partial(_main_kernel, cpt=cpt),
        grid=(B, nt),
        in_specs=[
            pl.BlockSpec((None, tm, D_MODEL), lambda b, t: (b, t, 0)),
            _resident((1, D_MODEL)),
            _resident((1, D_MODEL)),
            _resident((1, RET_V)),
            hbm, hbm, hbm, hbm,
            _resident((1, ATT_HD)),
            dec_f_spec,
            dec_b_spec,
            sink_spec,
            pl.BlockSpec((None, tm, RKV_COLS), lambda b, t: (b, t, 0)),
            pl.BlockSpec((None, CHUNK, ATT_KV), lambda b, t: (b, prev_chunk(t), 0)),
            pl.BlockSpec((None, tm, ATT_KV), lambda b, t: (b, t, 0)),
            pl.BlockSpec((None, CHUNK, ATT_KV), lambda b, t: (b, next_chunk(t), 0)),
            pl.BlockSpec((None, ATT_KV, CHUNK), lambda b, t: (b, 0, prev_chunk(t))),
            pl.BlockSpec((None, ATT_KV, tm), lambda b, t: (b, 0, t)),
            pl.BlockSpec((None, ATT_KV, CHUNK), lambda b, t: (b, 0, next_chunk(t))),
            pl.BlockSpec((None, tm // RET_CHUNK, RET_HEADS, RET_DK, RET_DV), lambda b, t: (b, t, 0, 0, 0)),
            _resident((RET_HEADS, RET_DK, RET_DV)),
            _resident((N_META, ATT_KV)),
            _resident((ATT_KV, N_META)),
        ],
        out_specs=pl.BlockSpec((None, tm, D_MODEL), lambda b, t: (b, t, 0)),
        out_shape=jax.ShapeDtypeStruct((B, SEQ, D_MODEL), x.dtype),
        scratch_shapes=(
            [state_scratch] + rope_scratch
            + [pltpu.VMEM((N_KEYS, Q_ROWS), F32),
               pltpu.VMEM((tm, D_MODEL), BF16),
               pltpu.VMEM((tm, RET_QK), F32),
               pltpu.VMEM((tm, ATT_Q), BF16),
               pltpu.VMEM((tm, RET_V), F32),
               pltpu.VMEM((tm, ATT_Q), F32)]
            + [pltpu.VMEM((tm, D_MODEL), F32)] * 4
            + [pltpu.VMEM((tm, D_MODEL), BF16)] * 3
            + [pltpu.VMEM((RET_HEADS, RET_CHUNK, RET_CHUNK), F32),
               pltpu.VMEM((RET_HEADS, RET_CHUNK, 2 * RET_DK), F32),
               pltpu.VMEM((RET_HEADS, RET_CHUNK, RET_DK), F32)]
            + [pltpu.VMEM((D_MODEL, MAIN_COLS), BF16)]
            + [pltpu.VMEM((D_MODEL, D_MODEL), BF16)] * 3
            + weight_stage
        ),
        compiler_params=params,
        name="main",
    )(x, pre_w, post_w, ret_nw, w_in, w_rb, w_ab, w_o, inv, per_head, per_head, per_head,
      rkv, ak, ak, ak, avt, avt, avt, sb, kvf0, km, vmt)
    return out
```

```python
import functools

import jax
import jax.numpy as jnp
from jax import lax
from jax.experimental import pallas as pl
from jax.experimental.pallas import tpu as pltpu

D_MODEL = 1024
SEQ = 8192
N_META = 16
CHUNK = 128
RET_HEADS = 4
RET_DK = 128
RET_DV = 256
ATT_HEADS = 8
ATT_GROUPS = 2
ATT_REP = ATT_HEADS // ATT_GROUPS
ATT_HD = 128
ROPE_THETA = 10000.0
EPS = 1e-6
NEG_INF = -1e30
RET_QK = RET_HEADS * RET_DK
RET_V = RET_HEADS * RET_DV
ATT_Q = ATT_HEADS * ATT_HD
ATT_KV = ATT_GROUPS * ATT_HD
D_IN = 2 * RET_QK + 2 * RET_V + 2 * ATT_Q + 2 * ATT_KV + 2 * D_MODEL
N_CHUNKS = SEQ // CHUNK
N_BAND = 3 * CHUNK
N_KEYS = N_BAND + N_META
Q_ROWS = ATT_REP * CHUNK

OFF_RQ = 0
OFF_RK = OFF_RQ + RET_QK
OFF_RV = OFF_RK + RET_QK
OFF_RG = OFF_RV + RET_V
OFF_AQ = OFF_RG + RET_V
OFF_AK = OFF_AQ + ATT_Q
OFF_AV = OFF_AK + ATT_KV
OFF_AG = OFF_AV + ATT_KV
OFF_GR = OFF_AG + ATT_Q
OFF_GA = OFF_GR + D_MODEL
RKV_COLS = RET_QK + RET_V

W_CHUNK = 512
KV_SRC = tuple(range(OFF_RK, OFF_RG, W_CHUNK)) + tuple(range(OFF_AK, OFF_AG, W_CHUNK))
KV_RK, KV_RV, KV_AKV = 0, RET_QK, RET_QK + RET_V
KV_COLS = len(KV_SRC) * W_CHUNK
MAIN_SEGMENTS = (("rq", OFF_RQ, RET_QK), ("aq", OFF_AQ, ATT_Q), ("rg", OFF_RG, RET_V), ("ag", OFF_AG, ATT_Q),
                 ("gr", OFF_GR, D_MODEL), ("ga", OFF_GA, D_MODEL))
MAIN_SRC = tuple(c for _, off, width in MAIN_SEGMENTS for c in range(off, off + width, W_CHUNK))
MAIN_OFF = {}
_o = 0
for _name, _, _width in MAIN_SEGMENTS:
    MAIN_OFF[_name] = _o
    _o += _width
MAIN_COLS = _o

CHUNKS_PER_TILE = 4
KV_CHUNKS_PER_TILE = 8
RET_CHUNK = 2 * CHUNK
N_RET_CHUNKS = SEQ // RET_CHUNK
COL_BLOCK = 256
SUBLANES = 8
V7X_VMEM_LIMIT_BYTES = 60 * 1024 * 1024

F32 = jnp.float32
BF16 = jnp.bfloat16


def _rms_norm(x, w):
    return x * lax.rsqrt(jnp.mean(x * x, axis=-1, keepdims=True) + EPS) * w


def _log_sigmoid(x):
    return jnp.minimum(x, 0.0) - jnp.log(1.0 + jnp.exp(-jnp.abs(x)))


def _sign_fold(sin):
    lane = lax.broadcasted_iota(jnp.int32, sin.shape, 1)
    return jnp.where(lane < ATT_HD // 2, -sin, sin)


def _rope_tables(rows, inv):
    ang = lax.broadcasted_iota(jnp.int32, (rows, ATT_HD), 0).astype(F32) * inv
    return jnp.cos(ang), _sign_fold(jnp.sin(ang))


def _fill_rope_base(inv, cosb_ref, sinb_ref):
    ang = lax.broadcasted_iota(jnp.int32, cosb_ref.shape, 0).astype(F32) * inv
    cosb_ref[...] = jnp.cos(ang)
    sinb_ref[...] = jnp.sin(ang)


def _rope_tables_from_base(pos0, inv, cosb_ref, sinb_ref):
    base = pos0.astype(F32) * inv
    ca, sa = jnp.cos(base), jnp.sin(base)
    cb, sb = cosb_ref[...], sinb_ref[...]
    return ca * cb - sa * sb, _sign_fold(sa * cb + ca * sb)


def _rope(t, cos, sin_signed):
    return t * cos + pltpu.roll(t, ATT_HD // 2, axis=1) * sin_signed


def _dot(a, b):
    return jnp.dot(a, b, preferred_element_type=F32)


def _dot_nt(a, b):
    return lax.dot_general(a, b, (((1,), (1,)), ((), ())), preferred_element_type=F32)


def _dot_tn(a, b):
    return lax.dot_general(a, b, (((0,), (0,)), ((), ())), preferred_element_type=F32)


def _row_index(shape):
    return lax.broadcasted_iota(jnp.int32, shape, 0).astype(F32)


def _weight_chunk_copy(src_ref, col, stage_ref, sem_ref, slot):
    return pltpu.make_async_copy(src_ref.at[0, :, pl.ds(col, W_CHUNK)], stage_ref.at[slot], sem_ref.at[slot])


def _load_weights_bf16(jobs, stage_ref, sem_ref):
    copies = [_weight_chunk_copy(src, col, stage_ref, sem_ref, i % 2) for i, (src, col, _, _) in enumerate(jobs)]
    copies[0].start()
    for i, (_, _, dst, dcol) in enumerate(jobs):
        if i + 1 < len(jobs):
            copies[i + 1].start()
        copies[i].wait()
        dst[:, dcol:dcol + W_CHUNK] = stage_ref[i % 2].astype(BF16)


def _meta_block(meta_ref, prew_ref, wkv_ref, inv_ref, decf_ref, kvf0_ref, km_ref, vmt_ref):
    u = _rms_norm(meta_ref[...], prew_ref[...]).astype(BF16)
    zk = _dot(u, wkv_ref[:, KV_RK:KV_RV])
    zv = _dot(u, wkv_ref[:, KV_RV:KV_AKV])
    za = _dot(u, wkv_ref[:, KV_AKV:])
    cos, sin = _rope_tables(N_META, inv_ref[...])
    kdec_rows = (N_META - 1) - _row_index((N_META, RET_DK))
    for h in range(RET_HEADS):
        lg = _log_sigmoid(decf_ref[h:h + 1, :RET_DK])
        k = _rope(zk[:, h * RET_DK:(h + 1) * RET_DK], cos, sin) * (RET_DK ** -0.5)
        k = (k * jnp.exp(kdec_rows * lg)).astype(BF16)
        v = zv[:, h * RET_DV:(h + 1) * RET_DV].astype(BF16)
        kvf0_ref[h] = _dot_tn(k, v)
    for g in range(ATT_GROUPS):
        km_ref[:, g * ATT_HD:(g + 1) * ATT_HD] = _rope(za[:, g * ATT_HD:(g + 1) * ATT_HD], cos, sin).astype(BF16)
    vmt_ref[...] = za[:, ATT_KV:].T.astype(BF16)


def _kv_kernel(x_ref, meta_ref, prew_ref, win_ref, inv_ref, decf_ref, decb_ref, rkv_ref, ak_ref, avt_ref, sb_ref,
               kvf0_ref, km_ref, vmt_ref, state_ref, cosb_ref, sinb_ref, wkv_ref, stage_ref, sem_ref, *, cpt):
    t = pl.program_id(1)
    tile = pl.num_programs(1) - 1 - t
    tm = cpt * CHUNK

    @pl.when((pl.program_id(0) == 0) & (t == 0))
    def _():
        _load_weights_bf16([(win_ref, col, wkv_ref, i * W_CHUNK) for i, col in enumerate(KV_SRC)],
                           stage_ref, sem_ref)
        _fill_rope_base(inv_ref[...], cosb_ref, sinb_ref)
        _meta_block(meta_ref, prew_ref, wkv_ref, inv_ref, decf_ref, kvf0_ref, km_ref, vmt_ref)

    @pl.when(t == 0)
    def _():
        state_ref[...] = jnp.zeros_like(state_ref)

    u = _rms_norm(x_ref[...], prew_ref[...]).astype(BF16)
    za = _dot(u, wkv_ref[:, KV_AKV:])
    zk = _dot(u, wkv_ref[:, KV_RK:KV_RV])
    zv = _dot(u, wkv_ref[:, KV_RV:KV_AKV])
    cos, sin = _rope_tables_from_base(N_META + tile * tm, inv_ref[...], cosb_ref, sinb_ref)
    for g in range(ATT_GROUPS):
        ak_ref[:, g * ATT_HD:(g + 1) * ATT_HD] = _rope(za[:, g * ATT_HD:(g + 1) * ATT_HD], cos, sin).astype(BF16)
    avt_ref[...] = za[:, ATT_KV:].T.astype(BF16)
    rk = [_rope(zk[:, h * RET_DK:(h + 1) * RET_DK], cos, sin) * (RET_DK ** -0.5) for h in range(RET_HEADS)]
    for h in range(RET_HEADS):
        rkv_ref[:, h * RET_DK:(h + 1) * RET_DK] = rk[h].astype(BF16)
    rkv_ref[:, RET_QK:] = zv.astype(BF16)

    rows_k = _row_index((RET_CHUNK, RET_DK))
    for h in range(RET_HEADS):
        lg = _log_sigmoid(decb_ref[h:h + 1, :])
        kdec = jnp.exp(rows_k * lg[:, :RET_DK])
        cdec = jnp.exp(RET_CHUNK * lg)
        for rc in reversed(range(tm // RET_CHUNK)):
            r0 = rc * RET_CHUNK
            state = state_ref[h]
            sb_ref[rc, h] = state.astype(BF16)
            k = (rk[h][r0:r0 + RET_CHUNK] * kdec).astype(BF16)
            v = zv[r0:r0 + RET_CHUNK, h * RET_DV:(h + 1) * RET_DV].astype(BF16)
            state_ref[h] = cdec * state + _dot_tn(k, v)


def _fill_band_bias(bias_ref):
    kk = lax.broadcasted_iota(jnp.int32, bias_ref.shape, 0)
    qi = lax.broadcasted_iota(jnp.int32, bias_ref.shape, 1) & (CHUNK - 1)
    visible = (kk >= N_BAND) | ((kk >= qi) & (kk <= qi + 2 * CHUNK))
    bias_ref[...] = jnp.where(visible, 0.0, NEG_INF)


def _fill_decay_tables(decf_ref, decb_ref, dmat_ref, qdec_ref, kdf_ref):
    rows = _row_index((RET_CHUNK, RET_CHUNK))
    rel = rows - lax.broadcasted_iota(jnp.int32, (RET_CHUNK, RET_CHUNK), 1).astype(F32)
    rows_k = _row_index((RET_CHUNK, RET_DK))
    for h in range(RET_HEADS):
        lgf = _log_sigmoid(decf_ref[h:h + 1, :])
        lgb = _log_sigmoid(decb_ref[h:h + 1, :])
        dmat_ref[h] = jnp.where(rel >= 0, jnp.exp(jnp.maximum(rel, 0.0) * lgf),
                                jnp.exp(jnp.maximum(-rel, 0.0) * lgb))
        lgf_k, lgb_k = lgf[:, :RET_DK], lgb[:, :RET_DK]
        qdec_ref[h, :, :RET_DK] = jnp.exp((rows_k + 1.0) * lgf_k)
        qdec_ref[h, :, RET_DK:] = jnp.exp((RET_CHUNK - rows_k) * lgb_k)
        kdf_ref[h] = jnp.exp((RET_CHUNK - 1.0 - rows_k) * lgf_k)


def _main_kernel(x_ref, xp_ref, prew_ref, postw_ref, retnw_ref, win_ref, wrb_hbm_ref, wab_hbm_ref, wo_hbm_ref, inv_ref,
                 decf_ref, decb_ref, sink_ref, rkv_ref, akp_ref, akc_ref, akn_ref, avp_ref,
                 avc_ref, avn_ref, sb_ref, kvf0_ref, km_ref, vmt_ref, out_ref,
                 sf_ref, cosb_ref, sinb_ref, bias_ref, u_ref, rq_ref, aq_ref, or_ref, oa_ref, grg_ref, gag_ref,
                 ggr_ref, gga_ref, lhs_r_ref, lhs_a_ref, mix_ref, dmat_ref, qdec_ref, kdf_ref, wmain_ref, wrb_ref,
                 wab_ref, wo_ref, stage_ref, sem_ref, outp_ref, *, cpt, n_tiles, tiles_per_batch):
    step = pl.program_id(0)
    t = step % tiles_per_batch
    tm = cpt * CHUNK

    def finish_previous_tile():
        for r0 in range(0, tm, CHUNK):
            rs = slice(r0, r0 + CHUNK)
            out_ref[rs] = xp_ref[rs] + _rms_norm(outp_ref[rs], postw_ref[...])

    @pl.when(step == 0)
    def _():
        outp_ref[...] = jnp.zeros_like(outp_ref)
        jobs = [(win_ref, col, wmain_ref, i * W_CHUNK) for i, col in enumerate(MAIN_SRC)]
        for src, dst in ((wrb_hbm_ref, wrb_ref), (wab_hbm_ref, wab_ref), (wo_hbm_ref, wo_ref)):
            jobs += [(src, c, dst, c) for c in range(0, D_MODEL, W_CHUNK)]
        _load_weights_bf16(jobs, stage_ref, sem_ref)
        _fill_rope_base(inv_ref[...], cosb_ref, sinb_ref)
        _fill_band_bias(bias_ref)
        _fill_decay_tables(decf_ref, decb_ref, dmat_ref, qdec_ref, kdf_ref)

    @pl.when((t == 0) & (step < n_tiles))
    def _():
        sf_ref[...] = kvf0_ref[...]

    @pl.when(step == n_tiles)
    def _():
        finish_previous_tile()

    @pl.when(step < n_tiles)
    def _():
        finish_previous_tile()
        _main_tile(t, cpt, x_ref, prew_ref, retnw_ref, inv_ref, decf_ref, sink_ref, rkv_ref, akp_ref, akc_ref, akn_ref,
                   avp_ref, avc_ref, avn_ref, sb_ref, km_ref, vmt_ref, sf_ref, cosb_ref, sinb_ref, bias_ref, u_ref,
                   rq_ref, aq_ref, or_ref, oa_ref, grg_ref, gag_ref, ggr_ref, gga_ref, lhs_r_ref, lhs_a_ref, mix_ref,
                   dmat_ref, qdec_ref, kdf_ref, wmain_ref, wrb_ref, wab_ref, wo_ref, outp_ref)


def _main_tile(t, cpt, x_ref, prew_ref, retnw_ref, inv_ref, decf_ref, sink_ref, rkv_ref, akp_ref, akc_ref, akn_ref,
               avp_ref, avc_ref, avn_ref, sb_ref, km_ref, vmt_ref, sf_ref, cosb_ref, sinb_ref, bias_ref, u_ref,
               rq_ref, aq_ref, or_ref, oa_ref, grg_ref, gag_ref, ggr_ref, gga_ref, lhs_r_ref, lhs_a_ref, mix_ref,
               dmat_ref, qdec_ref, kdf_ref, wmain_ref, wrb_ref, wab_ref, wo_ref, outp_ref):
    tm = cpt * CHUNK
    for r0 in range(0, tm, CHUNK):
        u_ref[r0:r0 + CHUNK] = _rms_norm(x_ref[r0:r0 + CHUNK], prew_ref[...]).astype(BF16)
    cos, sin = _rope_tables_from_base(N_META + t * tm, inv_ref[...], cosb_ref, sinb_ref)

    def proj(name, c0):
        col = MAIN_OFF[name] + c0
        return _dot(u_ref[...], wmain_ref[:, col:col + COL_BLOCK])

    heads_per_block = COL_BLOCK // ATT_HD
    for c0 in range(0, RET_QK, COL_BLOCK):
        z = proj("rq", c0)
        for i in range(heads_per_block):
            c = c0 + i * RET_DK
            rq_ref[:, c:c + RET_DK] = _rope(z[:, i * RET_DK:(i + 1) * RET_DK], cos, sin)
    for c0 in range(0, ATT_Q, COL_BLOCK):
        z = proj("aq", c0)
        for i in range(heads_per_block):
            c = c0 + i * ATT_HD
            aq_ref[:, c:c + ATT_HD] = (_rope(z[:, i * ATT_HD:(i + 1) * ATT_HD], cos, sin) * (ATT_HD ** -0.5)
                                       ).astype(BF16)

    gate_jobs = (("rg", jax.nn.silu, grg_ref), ("ag", jax.nn.silu, gag_ref),
                 ("gr", jax.nn.sigmoid, ggr_ref), ("ga", jax.nn.sigmoid, gga_ref))
    gate_blocks = [(j, c) for j in range(len(gate_jobs)) for c in range(0, D_MODEL, COL_BLOCK)]
    assert len(gate_blocks) % cpt == 0

    def emit_gate_blocks(n):
        for _ in range(n):
            j, c = gate_blocks.pop(0)
            name, act, ref = gate_jobs[j]
            ref[:, c:c + COL_BLOCK] = act(proj(name, c))

    cdf = [jnp.exp(RET_CHUNK * _log_sigmoid(decf_ref[h:h + 1, :])) for h in range(RET_HEADS)]

    k_cat = jnp.concatenate([akp_ref[...], akc_ref[...], akn_ref[...]], axis=0)
    vt_cat = jnp.concatenate([avp_ref[...], avc_ref[...], avn_ref[...]], axis=1)
    sink_rows = [jnp.concatenate([sink_ref[g * ATT_REP + r:g * ATT_REP + r + 1, :] for r in range(ATT_REP)],
                                 axis=1) for g in range(ATT_GROUPS)]

    def retention(rc):
        r0 = rc * RET_CHUNK
        rs = slice(r0, r0 + RET_CHUNK)
        for h in range(RET_HEADS):
            q = rq_ref[rs, h * RET_DK:(h + 1) * RET_DK]
            k = rkv_ref[rs, h * RET_DK:(h + 1) * RET_DK]
            v = rkv_ref[rs, RET_QK + h * RET_DV:RET_QK + (h + 1) * RET_DV]
            s = _dot_nt(q.astype(BF16), k) * dmat_ref[h]
            sf = sf_ref[h]
            q_cross = (jnp.concatenate([q, q], axis=1) * qdec_ref[h]).astype(BF16)
            s_cross = jnp.concatenate([sf.astype(BF16), sb_ref[rc, h]], axis=0)
            o = _dot(s.astype(BF16), v) + _dot(q_cross, s_cross)
            sf_ref[h] = cdf[h] * sf + _dot_tn((k.astype(F32) * kdf_ref[h]).astype(BF16), v)
            mu = jnp.mean(o, axis=-1, keepdims=True)
            d = o - mu
            var = jnp.mean(d * d, axis=-1, keepdims=True)
            or_ref[rs, h * RET_DV:(h + 1) * RET_DV] = d * lax.rsqrt(var + EPS)

    for rc in range(tm // RET_CHUNK):
        retention(rc)

    gates_per_pair = len(gate_blocks) // (cpt * ATT_GROUPS)
    pairs = [(lc, g) for lc in range(cpt) for g in range(ATT_GROUPS)]
    probs, denoms = {}, {}
    for lc, g in pairs:
        r0 = lc * CHUNK
        chunk = t * cpt + lc
        gs = slice(g * ATT_HD, (g + 1) * ATT_HD)
        qs = jnp.concatenate([aq_ref[r0:r0 + CHUNK, (g * ATT_REP + r) * ATT_HD:(g * ATT_REP + r + 1) * ATT_HD]
                              for r in range(ATT_REP)], axis=0)
        k_all = jnp.concatenate([k_cat[r0:r0 + N_BAND, gs], km_ref[:, gs]], axis=0)
        emit_gate_blocks(gates_per_pair)
        s = _dot_nt(k_all, qs) + bias_ref[...]
        parts = [s[0:CHUNK], s[CHUNK:2 * CHUNK], s[2 * CHUNK:N_BAND], s[N_BAND:]]
        if lc == 0:
            parts[0] = jnp.where(chunk > 0, parts[0], NEG_INF)
        if lc == cpt - 1:
            parts[2] = jnp.where(chunk < N_CHUNKS - 1, parts[2], NEG_INF)
        s = jnp.concatenate(parts, axis=0)
        sk = sink_rows[g]
        m = jnp.maximum(jnp.max(s, axis=0, keepdims=True), sk)
        p = jnp.exp(s - m)
        denoms[lc, g] = jnp.sum(p, axis=0, keepdims=True) + jnp.exp(sk - m)
        probs[lc, g] = p.astype(BF16)
    for lc, g in pairs:
        r0 = lc * CHUNK
        gs = slice(g * ATT_HD, (g + 1) * ATT_HD)
        vt_all = jnp.concatenate([vt_cat[gs, r0:r0 + N_BAND], vmt_ref[gs, :]], axis=1)
        ot = _dot(vt_all, probs[lc, g]) / denoms[lc, g]
        for r in range(ATT_REP):
            c = (g * ATT_REP + r) * ATT_HD
            oa_ref[r0:r0 + CHUNK, c:c + ATT_HD] = ot[:, r * CHUNK:(r + 1) * CHUNK].T

    for r0 in range(0, tm, CHUNK):
        rs = slice(r0, r0 + CHUNK)
        lhs_r_ref[rs] = (or_ref[rs] * retnw_ref[...] * grg_ref[rs]).astype(BF16)
        lhs_a_ref[rs] = (oa_ref[rs] * gag_ref[rs]).astype(BF16)
    for c0 in range(0, D_MODEL, COL_BLOCK):
        cs = slice(c0, c0 + COL_BLOCK)
        y_r = _dot(lhs_r_ref[...], wrb_ref[:, cs])
        y_a = _dot(lhs_a_ref[...], wab_ref[:, cs])
        mix_ref[:, cs] = (ggr_ref[:, cs] * y_r + gga_ref[:, cs] * y_a).astype(BF16)
    for c0 in range(0, D_MODEL, COL_BLOCK):
        cs = slice(c0, c0 + COL_BLOCK)
        outp_ref[:, cs] = _dot(mix_ref[...], wo_ref[:, cs])


def _resident(shape, index=None):
    nd = len(shape)
    index = (0,) * nd if index is None else index
    return pl.BlockSpec(shape, lambda *_: index, pipeline_mode=pl.Buffered(1))


def kernel(x, meta_tokens, pre_norm_w, w_in, ret_decay_fwd, ret_decay_bwd, ret_norm_w, w_ret_branch, attn_sink,
           w_attn_branch, w_out, post_norm_w):
    B = x.shape[0]
    assert x.shape == (B, SEQ, D_MODEL) and pre_norm_w.shape[0] == 1 and w_in.shape == (1, D_MODEL, D_IN)
    cpt = CHUNKS_PER_TILE
    tm = cpt * CHUNK
    nt = N_CHUNKS // cpt

    w_in = w_in.astype(F32)
    w_rb, w_ab, w_o = (w.astype(F32) for w in (w_ret_branch, w_attn_branch, w_out))
    pre_w = pre_norm_w.astype(F32)
    post_w = post_norm_w.astype(F32)
    ret_nw = ret_norm_w.astype(F32)
    half = ATT_HD // 2
    inv = ROPE_THETA ** (-jnp.arange(half, dtype=F32) * 2.0 / ATT_HD)
    inv = jnp.concatenate([inv, inv])[None, :]
    assert ATT_HEADS == SUBLANES and RET_HEADS <= SUBLANES
    pad = jnp.zeros((SUBLANES - RET_HEADS,), F32)
    per_head = jnp.concatenate([ret_decay_fwd[0].astype(F32), pad, ret_decay_bwd[0].astype(F32), pad,
                                attn_sink[0].astype(F32)])
    per_head = jnp.broadcast_to(per_head[:, None], (2 * SUBLANES + ATT_HEADS, RET_DV))
    dec_f_spec = _resident((SUBLANES, RET_DV), (0, 0))
    dec_b_spec = _resident((SUBLANES, RET_DV), (1, 0))
    sink_spec = _resident((ATT_HEADS, ATT_HD), (2, 0))

    params = pltpu.CompilerParams(dimension_semantics=("arbitrary", "arbitrary"),
                                  vmem_limit_bytes=V7X_VMEM_LIMIT_BYTES)
    rope_scratch = [pltpu.VMEM((tm, ATT_HD), F32), pltpu.VMEM((tm, ATT_HD), F32)]
    state_scratch = pltpu.VMEM((RET_HEADS, RET_DK, RET_DV), F32)
    weight_stage = [pltpu.VMEM((2, D_MODEL, W_CHUNK), F32), pltpu.SemaphoreType.DMA((2,))]
    hbm = pl.BlockSpec(memory_space=pl.ANY)

    kcpt = KV_CHUNKS_PER_TILE
    ktm, knt = kcpt * CHUNK, N_CHUNKS // kcpt
    rkv, ak, avt, sb, kvf0, km, vmt = pl.pallas_call(
        functools.partial(_kv_kernel, cpt=kcpt),
        grid=(B, knt),
        in_specs=[pl.BlockSpec((None, ktm, D_MODEL), lambda b, t: (b, knt - 1 - t, 0)), _resident((N_META, D_MODEL)),
                  _resident((1, D_MODEL)), hbm, _resident((1, ATT_HD)), dec_f_spec, dec_b_spec],
        out_specs=(
            pl.BlockSpec((None, ktm, RKV_COLS), lambda b, t: (b, knt - 1 - t, 0)),
            pl.BlockSpec((None, ktm, ATT_KV), lambda b, t: (b, knt - 1 - t, 0)),
            pl.BlockSpec((None, ATT_KV, ktm), lambda b, t: (b, 0, knt - 1 - t)),
            pl.BlockSpec((None, ktm // RET_CHUNK, RET_HEADS, RET_DK, RET_DV), lambda b, t: (b, knt - 1 - t, 0, 0, 0)),
            pl.BlockSpec((RET_HEADS, RET_DK, RET_DV), lambda b, t: (0, 0, 0)),
            pl.BlockSpec((N_META, ATT_KV), lambda b, t: (0, 0)),
            pl.BlockSpec((ATT_KV, N_META), lambda b, t: (0, 0)),
        ),
        out_shape=(jax.ShapeDtypeStruct((B, SEQ, RKV_COLS), BF16),
                   jax.ShapeDtypeStruct((B, SEQ, ATT_KV), BF16),
                   jax.ShapeDtypeStruct((B, ATT_KV, SEQ), BF16),
                   jax.ShapeDtypeStruct((B, N_RET_CHUNKS, RET_HEADS, RET_DK, RET_DV), BF16),
                   jax.ShapeDtypeStruct((RET_HEADS, RET_DK, RET_DV), F32),
                   jax.ShapeDtypeStruct((N_META, ATT_KV), BF16),
                   jax.ShapeDtypeStruct((ATT_KV, N_META), BF16)),
        scratch_shapes=[state_scratch, pltpu.VMEM((ktm, ATT_HD), F32), pltpu.VMEM((ktm, ATT_HD), F32),
                        pltpu.VMEM((D_MODEL, KV_COLS), BF16)] + weight_stage,
        compiler_params=params,
        name="kv",
    )(x, meta_tokens.astype(F32), pre_w, w_in, inv, per_head, per_head)

    n_tiles = B * nt

    def tile(s):
        s = jnp.clip(s, 0, n_tiles - 1)
        return s // nt, s % nt

    def rows(width, shift=0):
        return pl.BlockSpec((None, tm, width), lambda s: (*tile(s + shift), 0))

    def chunk_rows(width, pick):
        def index(s):
            b, t = tile(s)
            return b, pick(t), 0
        return pl.BlockSpec((None, CHUNK, width), index)

    def chunk_cols(pick):
        def index(s):
            b, t = tile(s)
            return b, 0, pick(t)
        return pl.BlockSpec((None, ATT_KV, CHUNK), index)

    prev_chunk = lambda t: jnp.maximum(t * cpt - 1, 0)
    next_chunk = lambda t: jnp.minimum((t + 1) * cpt, N_CHUNKS - 1)
    out = pl.pallas_call(
        functools.partial(_main_kernel, cpt=cpt, n_tiles=n_tiles, tiles_per_batch=nt),
        grid=(n_tiles + 1,),
        in_specs=[
            rows(D_MODEL),
            rows(D_MODEL, -1),
            _resident((1, D_MODEL)),
            _resident((1, D_MODEL)),
            _resident((1, RET_V)),
            hbm, hbm, hbm, hbm,
            _resident((1, ATT_HD)),
            dec_f_spec,
            dec_b_spec,
            sink_spec,
            rows(RKV_COLS),
            chunk_rows(ATT_KV, prev_chunk),
            rows(ATT_KV),
            chunk_rows(ATT_KV, next_chunk),
            chunk_cols(prev_chunk),
            pl.BlockSpec((None, ATT_KV, tm), lambda s: (tile(s)[0], 0, tile(s)[1])),
            chunk_cols(next_chunk),
            pl.BlockSpec((None, tm // RET_CHUNK, RET_HEADS, RET_DK, RET_DV), lambda s: (*tile(s), 0, 0, 0)),
            _resident((RET_HEADS, RET_DK, RET_DV)),
            _resident((N_META, ATT_KV)),
            _resident((ATT_KV, N_META)),
        ],
        out_specs=rows(D_MODEL, -1),
        out_shape=jax.ShapeDtypeStruct((B, SEQ, D_MODEL), x.dtype),
        scratch_shapes=(
            [state_scratch] + rope_scratch
            + [pltpu.VMEM((N_KEYS, Q_ROWS), F32),
               pltpu.VMEM((tm, D_MODEL), BF16),
               pltpu.VMEM((tm, RET_QK), F32),
               pltpu.VMEM((tm, ATT_Q), BF16),
               pltpu.VMEM((tm, RET_V), F32),
               pltpu.VMEM((tm, ATT_Q), F32)]
            + [pltpu.VMEM((tm, D_MODEL), F32)] * 4
            + [pltpu.VMEM((tm, D_MODEL), BF16)] * 3
            + [pltpu.VMEM((RET_HEADS, RET_CHUNK, RET_CHUNK), F32),
               pltpu.VMEM((RET_HEADS, RET_CHUNK, 2 * RET_DK), F32),
               pltpu.VMEM((RET_HEADS, RET_CHUNK, RET_DK), F32)]
            + [pltpu.VMEM((D_MODEL, MAIN_COLS), BF16)]
            + [pltpu.VMEM((D_MODEL, D_MODEL), BF16)] * 3
            + weight_stage
            + [pltpu.VMEM((tm, D_MODEL), F32)]
        ),
        compiler_params=pltpu.CompilerParams(dimension_semantics=("arbitrary",),
                                             vmem_limit_bytes=V7X_VMEM_LIMIT_BYTES),
        name="main",
    )(x, x, pre_w, post_w, ret_nw, w_in, w_rb, w_ab, w_o, inv, per_head, per_head, per_head,
      rkv, ak, ak, ak, avt, avt, avt, sb, kvf0, km, vmt)
    return out
```

```python
import functools

import jax
import jax.numpy as jnp
from jax import lax
from jax.experimental import pallas as pl
from jax.experimental.pallas import tpu as pltpu

D_MODEL = 1024
SEQ = 8192
N_META = 16
CHUNK = 128
RET_HEADS = 4
RET_DK = 128
RET_DV = 256
ATT_HEADS = 8
ATT_GROUPS = 2
ATT_REP = ATT_HEADS // ATT_GROUPS
ATT_HD = 128
ROPE_THETA = 10000.0
EPS = 1e-6
NEG_INF = -1e30
LOG2_E = 1.4426950408889634
RET_QK = RET_HEADS * RET_DK
RET_V = RET_HEADS * RET_DV
ATT_Q = ATT_HEADS * ATT_HD
ATT_KV = ATT_GROUPS * ATT_HD
D_IN = 2 * RET_QK + 2 * RET_V + 2 * ATT_Q + 2 * ATT_KV + 2 * D_MODEL
N_CHUNKS = SEQ // CHUNK
N_BAND = 3 * CHUNK
N_KEYS = N_BAND + N_META
Q_ROWS = ATT_REP * CHUNK

OFF_RQ = 0
OFF_RK = OFF_RQ + RET_QK
OFF_RV = OFF_RK + RET_QK
OFF_RG = OFF_RV + RET_V
OFF_AQ = OFF_RG + RET_V
OFF_AK = OFF_AQ + ATT_Q
OFF_AV = OFF_AK + ATT_KV
OFF_AG = OFF_AV + ATT_KV
OFF_GR = OFF_AG + ATT_Q
OFF_GA = OFF_GR + D_MODEL
RKV_COLS = RET_QK + RET_V

W_CHUNK = 512
KV_SRC = tuple(range(OFF_RK, OFF_RG, W_CHUNK)) + tuple(range(OFF_AK, OFF_AG, W_CHUNK))
KV_RK, KV_RV, KV_AKV = 0, RET_QK, RET_QK + RET_V
KV_COLS = len(KV_SRC) * W_CHUNK
MAIN_SEGMENTS = (("rq", OFF_RQ, RET_QK), ("aq", OFF_AQ, ATT_Q), ("rg", OFF_RG, RET_V), ("ag", OFF_AG, ATT_Q),
                 ("gr", OFF_GR, D_MODEL), ("ga", OFF_GA, D_MODEL))
MAIN_SRC = tuple(c for _, off, width in MAIN_SEGMENTS for c in range(off, off + width, W_CHUNK))
MAIN_OFF = {}
_o = 0
for _name, _, _width in MAIN_SEGMENTS:
    MAIN_OFF[_name] = _o
    _o += _width
MAIN_COLS = _o

CHUNKS_PER_TILE = 4
KV_CHUNKS_PER_TILE = 8
RET_CHUNK = 2 * CHUNK
N_RET_CHUNKS = SEQ // RET_CHUNK
COL_BLOCK = 256
SUBLANES = 8
V7X_VMEM_LIMIT_BYTES = 56 * 1024 * 1024

F32 = jnp.float32
BF16 = jnp.bfloat16


def _rms_norm(x, w):
    return x * lax.rsqrt(jnp.mean(x * x, axis=-1, keepdims=True) + EPS) * w


def _log_sigmoid(x):
    return jnp.minimum(x, 0.0) - jnp.log(1.0 + jnp.exp(-jnp.abs(x)))


def _sign_fold(sin):
    lane = lax.broadcasted_iota(jnp.int32, sin.shape, 1)
    return jnp.where(lane < ATT_HD // 2, -sin, sin)


def _rope_tables(rows, inv):
    ang = lax.broadcasted_iota(jnp.int32, (rows, ATT_HD), 0).astype(F32) * inv
    return jnp.cos(ang), _sign_fold(jnp.sin(ang))


def _fill_rope_base(inv, cosb_ref, sinb_ref):
    ang = lax.broadcasted_iota(jnp.int32, cosb_ref.shape, 0).astype(F32) * inv
    cosb_ref[...] = jnp.cos(ang)
    sinb_ref[...] = jnp.sin(ang)


def _rope_tables_from_base(pos0, inv, cosb_ref, sinb_ref):
    base = pos0.astype(F32) * inv
    ca, sa = jnp.cos(base), jnp.sin(base)
    cb, sb = cosb_ref[...], sinb_ref[...]
    return ca * cb - sa * sb, _sign_fold(sa * cb + ca * sb)


def _sigmoid(x):
    return 0.5 * jnp.tanh(0.5 * x) + 0.5


def _silu(x):
    return x * _sigmoid(x)


def _rope(t, cos, sin_signed):
    return t * cos + pltpu.roll(t, ATT_HD // 2, axis=1) * sin_signed


def _dot(a, b):
    return jnp.dot(a, b, preferred_element_type=F32)


def _dot_nt(a, b):
    return lax.dot_general(a, b, (((1,), (1,)), ((), ())), preferred_element_type=F32)


def _dot_tn(a, b):
    return lax.dot_general(a, b, (((0,), (0,)), ((), ())), preferred_element_type=F32)


def _row_index(shape):
    return lax.broadcasted_iota(jnp.int32, shape, 0).astype(F32)


def _weight_chunk_copy(src_ref, col, stage_ref, sem_ref, slot):
    return pltpu.make_async_copy(src_ref.at[0, :, pl.ds(col, W_CHUNK)], stage_ref.at[slot], sem_ref.at[slot])


def _load_weights_bf16(jobs, stage_ref, sem_ref):
    copies = [_weight_chunk_copy(src, col, stage_ref, sem_ref, i % 2) for i, (src, col, _, _) in enumerate(jobs)]
    copies[0].start()
    for i, (_, _, dst, dcol) in enumerate(jobs):
        if i + 1 < len(jobs):
            copies[i + 1].start()
        copies[i].wait()
        dst[:, dcol:dcol + W_CHUNK] = stage_ref[i % 2].astype(BF16)


def _meta_block(meta_ref, prew_ref, wkv_ref, inv_ref, decf_ref, kvf0_ref, km_ref, vmt_ref):
    u = _rms_norm(meta_ref[...], prew_ref[...]).astype(BF16)
    zk = _dot(u, wkv_ref[:, KV_RK:KV_RV])
    zv = _dot(u, wkv_ref[:, KV_RV:KV_AKV])
    za = _dot(u, wkv_ref[:, KV_AKV:])
    cos, sin = _rope_tables(N_META, inv_ref[...])
    kdec_rows = (N_META - 1) - _row_index((N_META, RET_DK))
    for h in range(RET_HEADS):
        lg = _log_sigmoid(decf_ref[h:h + 1, :RET_DK])
        k = _rope(zk[:, h * RET_DK:(h + 1) * RET_DK], cos, sin) * (RET_DK ** -0.5)
        k = (k * jnp.exp(kdec_rows * lg)).astype(BF16)
        v = zv[:, h * RET_DV:(h + 1) * RET_DV].astype(BF16)
        kvf0_ref[h] = _dot_tn(k, v)
    for g in range(ATT_GROUPS):
        km_ref[:, g * ATT_HD:(g + 1) * ATT_HD] = _rope(za[:, g * ATT_HD:(g + 1) * ATT_HD], cos, sin).astype(BF16)
    vmt_ref[...] = za[:, ATT_KV:].T.astype(BF16)


def _kv_kernel(x_ref, meta_ref, prew_ref, win_ref, inv_ref, decf_ref, decb_ref, rkv_ref, ak_ref, avt_ref, sb_ref,
               kvf0_ref, km_ref, vmt_ref, state_ref, cosb_ref, sinb_ref, wkv_ref, stage_ref, sem_ref, *, cpt):
    t = pl.program_id(1)
    tile = pl.num_programs(1) - 1 - t
    tm = cpt * CHUNK

    @pl.when((pl.program_id(0) == 0) & (t == 0))
    def _():
        _load_weights_bf16([(win_ref, col, wkv_ref, i * W_CHUNK) for i, col in enumerate(KV_SRC)],
                           stage_ref, sem_ref)
        _fill_rope_base(inv_ref[...], cosb_ref, sinb_ref)
        _meta_block(meta_ref, prew_ref, wkv_ref, inv_ref, decf_ref, kvf0_ref, km_ref, vmt_ref)

    @pl.when(t == 0)
    def _():
        state_ref[...] = jnp.zeros_like(state_ref)

    u = _rms_norm(x_ref[...], prew_ref[...]).astype(BF16)
    za = _dot(u, wkv_ref[:, KV_AKV:])
    zk = _dot(u, wkv_ref[:, KV_RK:KV_RV])
    zv = _dot(u, wkv_ref[:, KV_RV:KV_AKV])
    cos, sin = _rope_tables_from_base(N_META + tile * tm, inv_ref[...], cosb_ref, sinb_ref)
    for g in range(ATT_GROUPS):
        ak_ref[:, g * ATT_HD:(g + 1) * ATT_HD] = _rope(za[:, g * ATT_HD:(g + 1) * ATT_HD], cos, sin).astype(BF16)
    avt_ref[...] = za[:, ATT_KV:].T.astype(BF16)
    rk = [_rope(zk[:, h * RET_DK:(h + 1) * RET_DK], cos, sin) * (RET_DK ** -0.5) for h in range(RET_HEADS)]
    for h in range(RET_HEADS):
        rkv_ref[:, h * RET_DK:(h + 1) * RET_DK] = rk[h].astype(BF16)
    rkv_ref[:, RET_QK:] = zv.astype(BF16)

    rows_k = _row_index((RET_CHUNK, RET_DK))
    for h in range(RET_HEADS):
        lg = _log_sigmoid(decb_ref[h:h + 1, :])
        kdec = jnp.exp(rows_k * lg[:, :RET_DK])
        cdec = jnp.exp(RET_CHUNK * lg)
        for rc in reversed(range(tm // RET_CHUNK)):
            r0 = rc * RET_CHUNK
            state = state_ref[h]
            sb_ref[rc, h] = state.astype(BF16)
            k = (rk[h][r0:r0 + RET_CHUNK] * kdec).astype(BF16)
            v = zv[r0:r0 + RET_CHUNK, h * RET_DV:(h + 1) * RET_DV].astype(BF16)
            state_ref[h] = cdec * state + _dot_tn(k, v)


def _fill_band_bias(bias_ref):
    kk = lax.broadcasted_iota(jnp.int32, bias_ref.shape, 0)
    qi = lax.broadcasted_iota(jnp.int32, bias_ref.shape, 1) & (CHUNK - 1)
    visible = (kk >= N_BAND) | ((kk >= qi) & (kk <= qi + 2 * CHUNK))
    bias_ref[...] = jnp.where(visible, 0.0, NEG_INF)


def _fill_decay_tables(decf_ref, decb_ref, dmat_ref, qdec_ref, kdf_ref):
    rows = _row_index((RET_CHUNK, RET_CHUNK))
    rel = rows - lax.broadcasted_iota(jnp.int32, (RET_CHUNK, RET_CHUNK), 1).astype(F32)
    rows_k = _row_index((RET_CHUNK, RET_DK))
    for h in range(RET_HEADS):
        lgf = _log_sigmoid(decf_ref[h:h + 1, :])
        lgb = _log_sigmoid(decb_ref[h:h + 1, :])
        dmat_ref[h] = jnp.where(rel >= 0, jnp.exp(jnp.maximum(rel, 0.0) * lgf),
                                jnp.exp(jnp.maximum(-rel, 0.0) * lgb))
        lgf_k, lgb_k = lgf[:, :RET_DK], lgb[:, :RET_DK]
        qdec_ref[h, :, :RET_DK] = jnp.exp((rows_k + 1.0) * lgf_k)
        qdec_ref[h, :, RET_DK:] = jnp.exp((RET_CHUNK - rows_k) * lgb_k)
        kdf_ref[h] = jnp.exp((RET_CHUNK - 1.0 - rows_k) * lgf_k)


def _main_kernel(x_ref, prew_ref, postw_ref, retnw_ref, win_ref, wrb_hbm_ref, wab_hbm_ref, wo_hbm_ref, inv_ref,
                 decf_ref, decb_ref, sink_ref, rkv_ref, akp_ref, akc_ref, akn_ref, avp_ref,
                 avc_ref, avn_ref, sb_ref, kvf0_ref, km_ref, vmt_ref, out_ref,
                 sf_ref, cosb_ref, sinb_ref, bias_ref, u_ref, rq_ref, aq_ref, or_ref, oa_ref, grg_ref, gag_ref,
                 ggr_ref, gga_ref, lhs_r_ref, lhs_a_ref, mix_ref, dmat_ref, qdec_ref, kdf_ref, wmain_ref, wrb_ref,
                 wab_ref, wo_ref, stage_ref, sem_ref, *, cpt):
    t = pl.program_id(1)
    tm = cpt * CHUNK

    @pl.when((pl.program_id(0) == 0) & (t == 0))
    def _():
        jobs = [(win_ref, col, wmain_ref, i * W_CHUNK) for i, col in enumerate(MAIN_SRC)]
        for src, dst in ((wrb_hbm_ref, wrb_ref), (wab_hbm_ref, wab_ref), (wo_hbm_ref, wo_ref)):
            jobs += [(src, c, dst, c) for c in range(0, D_MODEL, W_CHUNK)]
        _load_weights_bf16(jobs, stage_ref, sem_ref)
        _fill_rope_base(inv_ref[...], cosb_ref, sinb_ref)
        _fill_band_bias(bias_ref)
        _fill_decay_tables(decf_ref, decb_ref, dmat_ref, qdec_ref, kdf_ref)

    @pl.when(t == 0)
    def _():
        sf_ref[...] = kvf0_ref[...]

    for r0 in range(0, tm, CHUNK):
        u_ref[r0:r0 + CHUNK] = _rms_norm(x_ref[r0:r0 + CHUNK], prew_ref[...]).astype(BF16)
    cos, sin = _rope_tables_from_base(N_META + t * tm, inv_ref[...], cosb_ref, sinb_ref)

    def proj(name, c0):
        col = MAIN_OFF[name] + c0
        return _dot(u_ref[...], wmain_ref[:, col:col + COL_BLOCK])

    heads_per_block = COL_BLOCK // ATT_HD
    for c0 in range(0, RET_QK, COL_BLOCK):
        z = proj("rq", c0)
        for i in range(heads_per_block):
            c = c0 + i * RET_DK
            rq_ref[:, c:c + RET_DK] = _rope(z[:, i * RET_DK:(i + 1) * RET_DK], cos, sin)
    for c0 in range(0, ATT_Q, COL_BLOCK):
        z = proj("aq", c0)
        for i in range(heads_per_block):
            c = c0 + i * ATT_HD
            aq_ref[:, c:c + ATT_HD] = (_rope(z[:, i * ATT_HD:(i + 1) * ATT_HD], cos, sin)
                                       * (ATT_HD ** -0.5 * LOG2_E)).astype(BF16)

    gate_jobs = (("rg", _silu, grg_ref), ("ag", _silu, gag_ref), ("gr", _sigmoid, ggr_ref), ("ga", _sigmoid, gga_ref))
    gate_blocks = [(j, c) for j in range(len(gate_jobs)) for c in range(0, D_MODEL, COL_BLOCK)]
    assert len(gate_blocks) % cpt == 0

    def emit_gate_blocks(n):
        for _ in range(n):
            j, c = gate_blocks.pop(0)
            name, act, ref = gate_jobs[j]
            ref[:, c:c + COL_BLOCK] = act(proj(name, c))

    cdf = [jnp.exp(RET_CHUNK * _log_sigmoid(decf_ref[h:h + 1, :])) for h in range(RET_HEADS)]

    k_cat = jnp.concatenate([akp_ref[...], akc_ref[...], akn_ref[...]], axis=0)
    vt_cat = jnp.concatenate([avp_ref[...], avc_ref[...], avn_ref[...]], axis=1)
    sink_rows = [jnp.concatenate([sink_ref[g * ATT_REP + r:g * ATT_REP + r + 1, :] for r in range(ATT_REP)],
                                 axis=1) * LOG2_E for g in range(ATT_GROUPS)]

    def retention(rc):
        r0 = rc * RET_CHUNK
        rs = slice(r0, r0 + RET_CHUNK)
        for h in range(RET_HEADS):
            q = rq_ref[rs, h * RET_DK:(h + 1) * RET_DK]
            k = rkv_ref[rs, h * RET_DK:(h + 1) * RET_DK]
            v = rkv_ref[rs, RET_QK + h * RET_DV:RET_QK + (h + 1) * RET_DV]
            s = _dot_nt(q.astype(BF16), k) * dmat_ref[h]
            sf = sf_ref[h]
            q_cross = (jnp.concatenate([q, q], axis=1) * qdec_ref[h]).astype(BF16)
            s_cross = jnp.concatenate([sf.astype(BF16), sb_ref[rc, h]], axis=0)
            o = _dot(s.astype(BF16), v) + _dot(q_cross, s_cross)
            sf_ref[h] = cdf[h] * sf + _dot_tn((k.astype(F32) * kdf_ref[h]).astype(BF16), v)
            mu = jnp.mean(o, axis=-1, keepdims=True)
            d = o - mu
            var = jnp.mean(d * d, axis=-1, keepdims=True)
            or_ref[rs, h * RET_DV:(h + 1) * RET_DV] = d * lax.rsqrt(var + EPS)

    for rc in range(tm // RET_CHUNK):
        retention(rc)

    gates_per_pair = len(gate_blocks) // (cpt * ATT_GROUPS)
    pairs = [(lc, g) for lc in range(cpt) for g in range(ATT_GROUPS)]
    probs, denoms = {}, {}
    for lc, g in pairs:
        r0 = lc * CHUNK
        chunk = t * cpt + lc
        gs = slice(g * ATT_HD, (g + 1) * ATT_HD)
        qs = jnp.concatenate([aq_ref[r0:r0 + CHUNK, (g * ATT_REP + r) * ATT_HD:(g * ATT_REP + r + 1) * ATT_HD]
                              for r in range(ATT_REP)], axis=0)
        k_all = jnp.concatenate([k_cat[r0:r0 + N_BAND, gs], km_ref[:, gs]], axis=0)
        emit_gate_blocks(gates_per_pair)
        s = _dot_nt(k_all, qs) + bias_ref[...]
        parts = [s[0:CHUNK], s[CHUNK:2 * CHUNK], s[2 * CHUNK:N_BAND], s[N_BAND:]]
        if lc == 0:
            parts[0] = jnp.where(chunk > 0, parts[0], NEG_INF)
        if lc == cpt - 1:
            parts[2] = jnp.where(chunk < N_CHUNKS - 1, parts[2], NEG_INF)
        s = jnp.concatenate(parts, axis=0)
        sk = sink_rows[g]
        m = jnp.maximum(jnp.max(s, axis=0, keepdims=True), sk)
        p = jnp.exp2(s - m)
        denoms[lc, g] = jnp.sum(p, axis=0, keepdims=True) + jnp.exp2(sk - m)
        probs[lc, g] = p.astype(BF16)
    for lc, g in pairs:
        r0 = lc * CHUNK
        gs = slice(g * ATT_HD, (g + 1) * ATT_HD)
        vt_all = jnp.concatenate([vt_cat[gs, r0:r0 + N_BAND], vmt_ref[gs, :]], axis=1)
        ot = _dot(vt_all, probs[lc, g]) / denoms[lc, g]
        for r in range(ATT_REP):
            c = (g * ATT_REP + r) * ATT_HD
            oa_ref[r0:r0 + CHUNK, c:c + ATT_HD] = ot[:, r * CHUNK:(r + 1) * CHUNK].T

    for r0 in range(0, tm, CHUNK):
        rs = slice(r0, r0 + CHUNK)
        lhs_r_ref[rs] = (or_ref[rs] * retnw_ref[...] * grg_ref[rs]).astype(BF16)
        lhs_a_ref[rs] = (oa_ref[rs] * gag_ref[rs]).astype(BF16)
    for c0 in range(0, D_MODEL, COL_BLOCK):
        cs = slice(c0, c0 + COL_BLOCK)
        y_r = _dot(lhs_r_ref[...], wrb_ref[:, cs])
        y_a = _dot(lhs_a_ref[...], wab_ref[:, cs])
        mix_ref[:, cs] = (ggr_ref[:, cs] * y_r + gga_ref[:, cs] * y_a).astype(BF16)
    for c0 in range(0, D_MODEL, COL_BLOCK):
        cs = slice(c0, c0 + COL_BLOCK)
        out_ref[:, cs] = _dot(mix_ref[...], wo_ref[:, cs])
    for r0 in range(0, tm, CHUNK):
        rs = slice(r0, r0 + CHUNK)
        out_ref[rs] = x_ref[rs] + _rms_norm(out_ref[rs], postw_ref[...])


def _resident(shape, index=None):
    nd = len(shape)
    index = (0,) * nd if index is None else index
    return pl.BlockSpec(shape, lambda *_: index, pipeline_mode=pl.Buffered(1))


def kernel(x, meta_tokens, pre_norm_w, w_in, ret_decay_fwd, ret_decay_bwd, ret_norm_w, w_ret_branch, attn_sink,
           w_attn_branch, w_out, post_norm_w):
    B = x.shape[0]
    assert x.shape == (B, SEQ, D_MODEL) and pre_norm_w.shape[0] == 1 and w_in.shape == (1, D_MODEL, D_IN)
    cpt = CHUNKS_PER_TILE
    tm = cpt * CHUNK
    nt = N_CHUNKS // cpt

    w_in = w_in.astype(F32)
    w_rb, w_ab, w_o = (w.astype(F32) for w in (w_ret_branch, w_attn_branch, w_out))
    pre_w = pre_norm_w.astype(F32)
    post_w = post_norm_w.astype(F32)
    ret_nw = ret_norm_w.astype(F32)
    half = ATT_HD // 2
    inv = ROPE_THETA ** (-jnp.arange(half, dtype=F32) * 2.0 / ATT_HD)
    inv = jnp.concatenate([inv, inv])[None, :]
    assert ATT_HEADS == SUBLANES and RET_HEADS <= SUBLANES
    pad = jnp.zeros((SUBLANES - RET_HEADS,), F32)
    per_head = jnp.concatenate([ret_decay_fwd[0].astype(F32), pad, ret_decay_bwd[0].astype(F32), pad,
                                attn_sink[0].astype(F32)])
    per_head = jnp.broadcast_to(per_head[:, None], (2 * SUBLANES + ATT_HEADS, RET_DV))
    dec_f_spec = _resident((SUBLANES, RET_DV), (0, 0))
    dec_b_spec = _resident((SUBLANES, RET_DV), (1, 0))
    sink_spec = _resident((ATT_HEADS, ATT_HD), (2, 0))

    params = pltpu.CompilerParams(dimension_semantics=("arbitrary", "arbitrary"),
                                  vmem_limit_bytes=V7X_VMEM_LIMIT_BYTES)
    rope_scratch = [pltpu.VMEM((tm, ATT_HD), F32), pltpu.VMEM((tm, ATT_HD), F32)]
    state_scratch = pltpu.VMEM((RET_HEADS, RET_DK, RET_DV), F32)
    weight_stage = [pltpu.VMEM((2, D_MODEL, W_CHUNK), F32), pltpu.SemaphoreType.DMA((2,))]
    hbm = pl.BlockSpec(memory_space=pl.ANY)

    kcpt = KV_CHUNKS_PER_TILE
    ktm, knt = kcpt * CHUNK, N_CHUNKS // kcpt
    rkv, ak, avt, sb, kvf0, km, vmt = pl.pallas_call(
        functools.partial(_kv_kernel, cpt=kcpt),
        grid=(B, knt),
        in_specs=[pl.BlockSpec((None, ktm, D_MODEL), lambda b, t: (b, knt - 1 - t, 0)), _resident((N_META, D_MODEL)),
                  _resident((1, D_MODEL)), hbm, _resident((1, ATT_HD)), dec_f_spec, dec_b_spec],
        out_specs=(
            pl.BlockSpec((None, ktm, RKV_COLS), lambda b, t: (b, knt - 1 - t, 0)),
            pl.BlockSpec((None, ktm, ATT_KV), lambda b, t: (b, knt - 1 - t, 0)),
            pl.BlockSpec((None, ATT_KV, ktm), lambda b, t: (b, 0, knt - 1 - t)),
            pl.BlockSpec((None, ktm // RET_CHUNK, RET_HEADS, RET_DK, RET_DV), lambda b, t: (b, knt - 1 - t, 0, 0, 0)),
            pl.BlockSpec((RET_HEADS, RET_DK, RET_DV), lambda b, t: (0, 0, 0)),
            pl.BlockSpec((N_META, ATT_KV), lambda b, t: (0, 0)),
            pl.BlockSpec((ATT_KV, N_META), lambda b, t: (0, 0)),
        ),
        out_shape=(jax.ShapeDtypeStruct((B, SEQ, RKV_COLS), BF16),
                   jax.ShapeDtypeStruct((B, SEQ, ATT_KV), BF16),
                   jax.ShapeDtypeStruct((B, ATT_KV, SEQ), BF16),
                   jax.ShapeDtypeStruct((B, N_RET_CHUNKS, RET_HEADS, RET_DK, RET_DV), BF16),
                   jax.ShapeDtypeStruct((RET_HEADS, RET_DK, RET_DV), F32),
                   jax.ShapeDtypeStruct((N_META, ATT_KV), BF16),
                   jax.ShapeDtypeStruct((ATT_KV, N_META), BF16)),
        scratch_shapes=[state_scratch, pltpu.VMEM((ktm, ATT_HD), F32), pltpu.VMEM((ktm, ATT_HD), F32),
                        pltpu.VMEM((D_MODEL, KV_COLS), BF16)] + weight_stage,
        compiler_params=params,
        name="kv",
    )(x, meta_tokens.astype(F32), pre_w, w_in, inv, per_head, per_head)

    prev_chunk = lambda t: jnp.maximum(t * cpt - 1, 0)
    next_chunk = lambda t: jnp.minimum((t + 1) * cpt, N_CHUNKS - 1)
    out = pl.pallas_call(
        functools.partial(_main_kernel, cpt=cpt),
        grid=(B, nt),
        in_specs=[
            pl.BlockSpec((None, tm, D_MODEL), lambda b, t: (b, t, 0)),
            _resident((1, D_MODEL)),
            _resident((1, D_MODEL)),
            _resident((1, RET_V)),
            hbm, hbm, hbm, hbm,
            _resident((1, ATT_HD)),
            dec_f_spec,
            dec_b_spec,
            sink_spec,
            pl.BlockSpec((None, tm, RKV_COLS), lambda b, t: (b, t, 0)),
            pl.BlockSpec((None, CHUNK, ATT_KV), lambda b, t: (b, prev_chunk(t), 0)),
            pl.BlockSpec((None, tm, ATT_KV), lambda b, t: (b, t, 0)),
            pl.BlockSpec((None, CHUNK, ATT_KV), lambda b, t: (b, next_chunk(t), 0)),
            pl.BlockSpec((None, ATT_KV, CHUNK), lambda b, t: (b, 0, prev_chunk(t))),
            pl.BlockSpec((None, ATT_KV, tm), lambda b, t: (b, 0, t)),
            pl.BlockSpec((None, ATT_KV, CHUNK), lambda b, t: (b, 0, next_chunk(t))),
            pl.BlockSpec((None, tm // RET_CHUNK, RET_HEADS, RET_DK, RET_DV), lambda b, t: (b, t, 0, 0, 0)),
            _resident((RET_HEADS, RET_DK, RET_DV)),
            _resident((N_META, ATT_KV)),
            _resident((ATT_KV, N_META)),
        ],
        out_specs=pl.BlockSpec((None, tm, D_MODEL), lambda b, t: (b, t, 0)),
        out_shape=jax.ShapeDtypeStruct((B, SEQ, D_MODEL), x.dtype),
        scratch_shapes=(
            [state_scratch] + rope_scratch
            + [pltpu.VMEM((N_KEYS, Q_ROWS), F32),
               pltpu.VMEM((tm, D_MODEL), BF16),
               pltpu.VMEM((tm, RET_QK), F32),
               pltpu.VMEM((tm, ATT_Q), BF16),
               pltpu.VMEM((tm, RET_V), F32),
               pltpu.VMEM((tm, ATT_Q), F32)]
            + [pltpu.VMEM((tm, D_MODEL), F32)] * 4
            + [pltpu.VMEM((tm, D_MODEL), BF16)] * 3
            + [pltpu.VMEM((RET_HEADS, RET_CHUNK, RET_CHUNK), F32),
               pltpu.VMEM((RET_HEADS, RET_CHUNK, 2 * RET_DK), F32),
               pltpu.VMEM((RET_HEADS, RET_CHUNK, RET_DK), F32)]
            + [pltpu.VMEM((D_MODEL, MAIN_COLS), BF16)]
            + [pltpu.VMEM((D_MODEL, D_MODEL), BF16)] * 3
            + weight_stage
        ),
        compiler_params=params,
        name="main",
    )(x, pre_w, post_w, ret_nw, w_in, w_rb, w_ab, w_o, inv, per_head, per_head, per_head,
      rkv, ak, ak, ak, avt, avt, avt, sb, kvf0, km, vmt)
    return out
```

```python
import functools

import jax
import jax.numpy as jnp
from jax import lax
from jax.experimental import pallas as pl
from jax.experimental.pallas import tpu as pltpu

D_MODEL = 1024
SEQ = 8192
N_META = 16
CHUNK = 128
RET_HEADS = 4
RET_DK = 128
RET_DV = 256
ATT_HEADS = 8
ATT_GROUPS = 2
ATT_REP = ATT_HEADS // ATT_GROUPS
ATT_HD = 128
ROPE_THETA = 10000.0
EPS = 1e-6
NEG_INF = -1e30
LOG2_E = 1.4426950408889634
RET_QK = RET_HEADS * RET_DK
RET_V = RET_HEADS * RET_DV
ATT_Q = ATT_HEADS * ATT_HD
ATT_KV = ATT_GROUPS * ATT_HD
D_IN = 2 * RET_QK + 2 * RET_V + 2 * ATT_Q + 2 * ATT_KV + 2 * D_MODEL
N_CHUNKS = SEQ // CHUNK
N_BAND = 3 * CHUNK
N_KEYS = N_BAND + N_META
Q_ROWS = ATT_REP * CHUNK

OFF_RQ = 0
OFF_RK = OFF_RQ + RET_QK
OFF_RV = OFF_RK + RET_QK
OFF_RG = OFF_RV + RET_V
OFF_AQ = OFF_RG + RET_V
OFF_AK = OFF_AQ + ATT_Q
OFF_AV = OFF_AK + ATT_KV
OFF_AG = OFF_AV + ATT_KV
OFF_GR = OFF_AG + ATT_Q
OFF_GA = OFF_GR + D_MODEL
RKV_COLS = RET_QK + RET_V

W_CHUNK = 512
KV_SRC = tuple(range(OFF_RK, OFF_RG, W_CHUNK)) + tuple(range(OFF_AK, OFF_AG, W_CHUNK))
KV_RK, KV_RV, KV_AKV = 0, RET_QK, RET_QK + RET_V
KV_COLS = len(KV_SRC) * W_CHUNK
MAIN_SEGMENTS = (("rq", OFF_RQ, RET_QK), ("aq", OFF_AQ, ATT_Q), ("rg", OFF_RG, RET_V), ("ag", OFF_AG, ATT_Q),
                 ("gr", OFF_GR, D_MODEL), ("ga", OFF_GA, D_MODEL))
MAIN_SRC = tuple(c for _, off, width in MAIN_SEGMENTS for c in range(off, off + width, W_CHUNK))
MAIN_OFF = {}
_o = 0
for _name, _, _width in MAIN_SEGMENTS:
    MAIN_OFF[_name] = _o
    _o += _width
MAIN_COLS = _o

CHUNKS_PER_TILE = 4
KV_CHUNKS_PER_TILE = 8
RET_CHUNK = 2 * CHUNK
N_RET_CHUNKS = SEQ // RET_CHUNK
COL_BLOCK = 256
SUBLANES = 8
V7X_VMEM_LIMIT_BYTES = 56 * 1024 * 1024

F32 = jnp.float32
BF16 = jnp.bfloat16


def _rms_norm(x, w):
    return x * lax.rsqrt(jnp.mean(x * x, axis=-1, keepdims=True) + EPS) * w


def _log_sigmoid(x):
    return jnp.minimum(x, 0.0) - jnp.log(1.0 + jnp.exp(-jnp.abs(x)))


def _sign_fold(sin):
    lane = lax.broadcasted_iota(jnp.int32, sin.shape, 1)
    return jnp.where(lane < ATT_HD // 2, -sin, sin)


def _rope_tables(rows, inv):
    ang = lax.broadcasted_iota(jnp.int32, (rows, ATT_HD), 0).astype(F32) * inv
    return jnp.cos(ang), _sign_fold(jnp.sin(ang))


def _fill_rope_base(inv, cosb_ref, sinb_ref):
    ang = lax.broadcasted_iota(jnp.int32, (CHUNK, ATT_HD), 0).astype(F32) * inv
    cb, sb = jnp.cos(ang), jnp.sin(ang)
    for r0 in range(0, cosb_ref.shape[0], CHUNK):
        base = float(r0) * inv
        ca, sa = jnp.cos(base), jnp.sin(base)
        cosb_ref[r0:r0 + CHUNK] = ca * cb - sa * sb
        sinb_ref[r0:r0 + CHUNK] = sa * cb + ca * sb


def _rope_tables_from_base(pos0, inv, cosb_ref, sinb_ref):
    base = pos0.astype(F32) * inv
    ca, sa = jnp.cos(base), jnp.sin(base)
    cb, sb = cosb_ref[...], sinb_ref[...]
    return ca * cb - sa * sb, _sign_fold(sa * cb + ca * sb)


def _sigmoid(x):
    return 0.5 * jnp.tanh(0.5 * x) + 0.5


def _silu(x):
    return x * _sigmoid(x)


def _rope(t, cos, sin_signed):
    return t * cos + pltpu.roll(t, ATT_HD // 2, axis=1) * sin_signed


def _dot(a, b):
    return jnp.dot(a, b, preferred_element_type=F32)


def _dot_nt(a, b):
    return lax.dot_general(a, b, (((1,), (1,)), ((), ())), preferred_element_type=F32)


def _dot_tn(a, b):
    return lax.dot_general(a, b, (((0,), (0,)), ((), ())), preferred_element_type=F32)


def _row_index(shape):
    return lax.broadcasted_iota(jnp.int32, shape, 0).astype(F32)


def _weight_chunk_copy(src_ref, col, stage_ref, sem_ref, slot):
    return pltpu.make_async_copy(src_ref.at[0, :, pl.ds(col, W_CHUNK)], stage_ref.at[slot], sem_ref.at[slot])


def _load_weights_bf16(jobs, stage_ref, sem_ref):
    copies = [_weight_chunk_copy(src, col, stage_ref, sem_ref, i % 2) for i, (src, col, _, _) in enumerate(jobs)]
    copies[0].start()
    for i, (_, _, dst, dcol) in enumerate(jobs):
        if i + 1 < len(jobs):
            copies[i + 1].start()
        copies[i].wait()
        dst[:, dcol:dcol + W_CHUNK] = stage_ref[i % 2].astype(BF16)


def _meta_block(meta_ref, prew_ref, wkv_ref, inv_ref, decf_ref, kvf0_ref, km_ref, vmt_ref):
    u = _rms_norm(meta_ref[...], prew_ref[...]).astype(BF16)
    zk = _dot(u, wkv_ref[:, KV_RK:KV_RV])
    zv = _dot(u, wkv_ref[:, KV_RV:KV_AKV])
    za = _dot(u, wkv_ref[:, KV_AKV:])
    cos, sin = _rope_tables(N_META, inv_ref[...])
    kdec_rows = (N_META - 1) - _row_index((N_META, RET_DK))
    for h in range(RET_HEADS):
        lg = _log_sigmoid(decf_ref[h:h + 1, :RET_DK])
        k = _rope(zk[:, h * RET_DK:(h + 1) * RET_DK], cos, sin) * (RET_DK ** -0.5)
        k = (k * jnp.exp(kdec_rows * lg)).astype(BF16)
        v = zv[:, h * RET_DV:(h + 1) * RET_DV].astype(BF16)
        kvf0_ref[h] = _dot_tn(k, v)
    for g in range(ATT_GROUPS):
        km_ref[:, g * ATT_HD:(g + 1) * ATT_HD] = _rope(za[:, g * ATT_HD:(g + 1) * ATT_HD], cos, sin).astype(BF16)
    vmt_ref[...] = za[:, ATT_KV:].T.astype(BF16)


def _kv_kernel(x_ref, meta_ref, prew_ref, win_ref, inv_ref, decf_ref, decb_ref, rkv_ref, ak_ref, avt_ref, sb_ref,
               kvf0_ref, km_ref, vmt_ref, state_ref, cosb_ref, sinb_ref, wkv_ref, stage_ref, sem_ref, *, cpt):
    t = pl.program_id(1)
    tile = pl.num_programs(1) - 1 - t
    tm = cpt * CHUNK

    @pl.when((pl.program_id(0) == 0) & (t == 0))
    def _():
        _load_weights_bf16([(win_ref, col, wkv_ref, i * W_CHUNK) for i, col in enumerate(KV_SRC)],
                           stage_ref, sem_ref)
        _fill_rope_base(inv_ref[...], cosb_ref, sinb_ref)
        _meta_block(meta_ref, prew_ref, wkv_ref, inv_ref, decf_ref, kvf0_ref, km_ref, vmt_ref)

    @pl.when(t == 0)
    def _():
        state_ref[...] = jnp.zeros_like(state_ref)

    u = _rms_norm(x_ref[...], prew_ref[...]).astype(BF16)
    za = _dot(u, wkv_ref[:, KV_AKV:])
    zk = _dot(u, wkv_ref[:, KV_RK:KV_RV])
    zv = _dot(u, wkv_ref[:, KV_RV:KV_AKV])
    cos, sin = _rope_tables_from_base(N_META + tile * tm, inv_ref[...], cosb_ref, sinb_ref)
    for g in range(ATT_GROUPS):
        ak_ref[:, g * ATT_HD:(g + 1) * ATT_HD] = _rope(za[:, g * ATT_HD:(g + 1) * ATT_HD], cos, sin).astype(BF16)
    avt_ref[...] = za[:, ATT_KV:].T.astype(BF16)
    rk = [_rope(zk[:, h * RET_DK:(h + 1) * RET_DK], cos, sin) * (RET_DK ** -0.5) for h in range(RET_HEADS)]
    for h in range(RET_HEADS):
        rkv_ref[:, h * RET_DK:(h + 1) * RET_DK] = rk[h].astype(BF16)
    rkv_ref[:, RET_QK:] = zv.astype(BF16)

    rows_k = _row_index((RET_CHUNK, RET_DK))
    for h in range(RET_HEADS):
        lg = _log_sigmoid(decb_ref[h:h + 1, :])
        kdec = jnp.exp(rows_k * lg[:, :RET_DK])
        cdec = jnp.exp(RET_CHUNK * lg)
        for rc in reversed(range(tm // RET_CHUNK)):
            r0 = rc * RET_CHUNK
            state = state_ref[h]
            sb_ref[rc, h] = state.astype(BF16)
            k = (rk[h][r0:r0 + RET_CHUNK] * kdec).astype(BF16)
            v = zv[r0:r0 + RET_CHUNK, h * RET_DV:(h + 1) * RET_DV].astype(BF16)
            state_ref[h] = cdec * state + _dot_tn(k, v)


def _fill_band_bias(bias_ref):
    kk = lax.broadcasted_iota(jnp.int32, bias_ref.shape, 0)
    qi = lax.broadcasted_iota(jnp.int32, bias_ref.shape, 1) & (CHUNK - 1)
    visible = (kk >= N_BAND) | ((kk >= qi) & (kk <= qi + 2 * CHUNK))
    bias_ref[...] = jnp.where(visible, 0.0, NEG_INF)


def _fill_decay_tables(decf_ref, decb_ref, dmat_ref, qdec_ref, kdf_ref):
    rows = _row_index((RET_CHUNK, RET_CHUNK))
    rel = rows - lax.broadcasted_iota(jnp.int32, (RET_CHUNK, RET_CHUNK), 1).astype(F32)
    rows_k = _row_index((RET_CHUNK, RET_DK))
    for h in range(RET_HEADS):
        lgf = _log_sigmoid(decf_ref[h:h + 1, :])
        lgb = _log_sigmoid(decb_ref[h:h + 1, :])
        dmat_ref[h] = jnp.where(rel >= 0, jnp.exp(jnp.maximum(rel, 0.0) * lgf),
                                jnp.exp(jnp.maximum(-rel, 0.0) * lgb))
        lgf_k, lgb_k = lgf[:, :RET_DK], lgb[:, :RET_DK]
        qdec_ref[h, :, :RET_DK] = jnp.exp((rows_k + 1.0) * lgf_k)
        qdec_ref[h, :, RET_DK:] = jnp.exp((RET_CHUNK - rows_k) * lgb_k)
        kdf_ref[h] = jnp.exp((RET_CHUNK - 1.0 - rows_k) * lgf_k)


def _main_kernel(x_ref, prew_ref, postw_ref, retnw_ref, win_ref, wrb_hbm_ref, wab_hbm_ref, wo_hbm_ref, inv_ref,
                 decf_ref, decb_ref, sink_ref, rkv_ref, akp_ref, akc_ref, akn_ref, avp_ref,
                 avc_ref, avn_ref, sb_ref, kvf0_ref, km_ref, vmt_ref, out_ref,
                 sf_ref, cosb_ref, sinb_ref, bias_ref, u_ref, rq_ref, aq_ref, or_ref, oa_ref, grg_ref, gag_ref,
                 ggr_ref, gga_ref, lhs_r_ref, lhs_a_ref, mix_ref, dmat_ref, qdec_ref, kdf_ref, wmain_ref, wrb_ref,
                 wab_ref, wo_ref, stage_ref, sem_ref, *, cpt):
    t = pl.program_id(1)
    tm = cpt * CHUNK

    @pl.when((pl.program_id(0) == 0) & (t == 0))
    def _():
        jobs = [(win_ref, col, wmain_ref, i * W_CHUNK) for i, col in enumerate(MAIN_SRC)]
        for src, dst in ((wrb_hbm_ref, wrb_ref), (wab_hbm_ref, wab_ref), (wo_hbm_ref, wo_ref)):
            jobs += [(src, c, dst, c) for c in range(0, D_MODEL, W_CHUNK)]
        _load_weights_bf16(jobs, stage_ref, sem_ref)
        _fill_rope_base(inv_ref[...], cosb_ref, sinb_ref)
        _fill_band_bias(bias_ref)
        _fill_decay_tables(decf_ref, decb_ref, dmat_ref, qdec_ref, kdf_ref)

    @pl.when(t == 0)
    def _():
        sf_ref[...] = kvf0_ref[...]

    for r0 in range(0, tm, CHUNK):
        u_ref[r0:r0 + CHUNK] = _rms_norm(x_ref[r0:r0 + CHUNK], prew_ref[...]).astype(BF16)
    cos, sin = _rope_tables_from_base(N_META + t * tm, inv_ref[...], cosb_ref, sinb_ref)

    def proj(name, c0):
        col = MAIN_OFF[name] + c0
        return _dot(u_ref[...], wmain_ref[:, col:col + COL_BLOCK])

    heads_per_block = COL_BLOCK // ATT_HD
    for c0 in range(0, RET_QK, COL_BLOCK):
        z = proj("rq", c0)
        for i in range(heads_per_block):
            c = c0 + i * RET_DK
            rq_ref[:, c:c + RET_DK] = _rope(z[:, i * RET_DK:(i + 1) * RET_DK], cos, sin)
    for c0 in range(0, ATT_Q, COL_BLOCK):
        z = proj("aq", c0)
        for i in range(heads_per_block):
            c = c0 + i * ATT_HD
            aq_ref[:, c:c + ATT_HD] = (_rope(z[:, i * ATT_HD:(i + 1) * ATT_HD], cos, sin)
                                       * (ATT_HD ** -0.5 * LOG2_E)).astype(BF16)

    gate_jobs = (("rg", _silu, grg_ref), ("ag", _silu, gag_ref), ("gr", _sigmoid, ggr_ref), ("ga", _sigmoid, gga_ref))
    gate_blocks = [(j, c) for j in range(len(gate_jobs)) for c in range(0, D_MODEL, COL_BLOCK)]
    assert len(gate_blocks) % cpt == 0

    def emit_gate_blocks(n):
        for _ in range(n):
            j, c = gate_blocks.pop(0)
            name, act, ref = gate_jobs[j]
            ref[:, c:c + COL_BLOCK] = act(proj(name, c))

    cdf = [jnp.exp(RET_CHUNK * _log_sigmoid(decf_ref[h:h + 1, :])) for h in range(RET_HEADS)]

    k_cat = jnp.concatenate([akp_ref[...], akc_ref[...], akn_ref[...]], axis=0)
    vt_cat = jnp.concatenate([avp_ref[...], avc_ref[...], avn_ref[...]], axis=1)
    sink_rows = [jnp.concatenate([sink_ref[g * ATT_REP + r:g * ATT_REP + r + 1, :] for r in range(ATT_REP)],
                                 axis=1) * LOG2_E for g in range(ATT_GROUPS)]

    def retention(rc):
        r0 = rc * RET_CHUNK
        rs = slice(r0, r0 + RET_CHUNK)
        for h in range(RET_HEADS):
            q = rq_ref[rs, h * RET_DK:(h + 1) * RET_DK]
            k = rkv_ref[rs, h * RET_DK:(h + 1) * RET_DK]
            v = rkv_ref[rs, RET_QK + h * RET_DV:RET_QK + (h + 1) * RET_DV]
            s = _dot_nt(q.astype(BF16), k) * dmat_ref[h]
            sf = sf_ref[h]
            q_cross = (jnp.concatenate([q, q], axis=1) * qdec_ref[h]).astype(BF16)
            s_cross = jnp.concatenate([sf.astype(BF16), sb_ref[rc, h]], axis=0)
            o = _dot(s.astype(BF16), v) + _dot(q_cross, s_cross)
            sf_ref[h] = cdf[h] * sf + _dot_tn((k.astype(F32) * kdf_ref[h]).astype(BF16), v)
            mu = jnp.mean(o, axis=-1, keepdims=True)
            d = o - mu
            var = jnp.mean(d * d, axis=-1, keepdims=True)
            or_ref[rs, h * RET_DV:(h + 1) * RET_DV] = d * lax.rsqrt(var + EPS)

    for rc in range(tm // RET_CHUNK):
        retention(rc)

    gates_per_pair = len(gate_blocks) // (cpt * ATT_GROUPS)
    pairs = [(lc, g) for lc in range(cpt) for g in range(ATT_GROUPS)]
    probs, denoms = {}, {}
    for lc, g in pairs:
        r0 = lc * CHUNK
        chunk = t * cpt + lc
        gs = slice(g * ATT_HD, (g + 1) * ATT_HD)
        qs = jnp.concatenate([aq_ref[r0:r0 + CHUNK, (g * ATT_REP + r) * ATT_HD:(g * ATT_REP + r + 1) * ATT_HD]
                              for r in range(ATT_REP)], axis=0)
        k_all = jnp.concatenate([k_cat[r0:r0 + N_BAND, gs], km_ref[:, gs]], axis=0)
        emit_gate_blocks(gates_per_pair)
        s = _dot_nt(k_all, qs) + bias_ref[...]
        parts = [s[0:CHUNK], s[CHUNK:2 * CHUNK], s[2 * CHUNK:N_BAND], s[N_BAND:]]
        if lc == 0:
            parts[0] = jnp.where(chunk > 0, parts[0], NEG_INF)
        if lc == cpt - 1:
            parts[2] = jnp.where(chunk < N_CHUNKS - 1, parts[2], NEG_INF)
        s = jnp.concatenate(parts, axis=0)
        sk = sink_rows[g]
        m = jnp.maximum(jnp.max(s, axis=0, keepdims=True), sk)
        p = jnp.exp2(s - m)
        denoms[lc, g] = jnp.sum(p, axis=0, keepdims=True) + jnp.exp2(sk - m)
        probs[lc, g] = p.astype(BF16)
    for lc, g in pairs:
        r0 = lc * CHUNK
        gs = slice(g * ATT_HD, (g + 1) * ATT_HD)
        vt_all = jnp.concatenate([vt_cat[gs, r0:r0 + N_BAND], vmt_ref[gs, :]], axis=1)
        ot = _dot(vt_all, probs[lc, g]) / denoms[lc, g]
        for r in range(ATT_REP):
            c = (g * ATT_REP + r) * ATT_HD
            oa_ref[r0:r0 + CHUNK, c:c + ATT_HD] = ot[:, r * CHUNK:(r + 1) * CHUNK].T

    for r0 in range(0, tm, CHUNK):
        rs = slice(r0, r0 + CHUNK)
        lhs_r_ref[rs] = (or_ref[rs] * retnw_ref[...] * grg_ref[rs]).astype(BF16)
        lhs_a_ref[rs] = (oa_ref[rs] * gag_ref[rs]).astype(BF16)
    for c0 in range(0, D_MODEL, COL_BLOCK):
        cs = slice(c0, c0 + COL_BLOCK)
        y_r = _dot(lhs_r_ref[...], wrb_ref[:, cs])
        y_a = _dot(lhs_a_ref[...], wab_ref[:, cs])
        mix_ref[:, cs] = (ggr_ref[:, cs] * y_r + gga_ref[:, cs] * y_a).astype(BF16)
    for c0 in range(0, D_MODEL, COL_BLOCK):
        cs = slice(c0, c0 + COL_BLOCK)
        out_ref[:, cs] = _dot(mix_ref[...], wo_ref[:, cs])
    for r0 in range(0, tm, CHUNK):
        rs = slice(r0, r0 + CHUNK)
        out_ref[rs] = x_ref[rs] + _rms_norm(out_ref[rs], postw_ref[...])


def _resident(shape, index=None):
    nd = len(shape)
    index = (0,) * nd if index is None else index
    return pl.BlockSpec(shape, lambda *_: index, pipeline_mode=pl.Buffered(1))


def kernel(x, meta_tokens, pre_norm_w, w_in, ret_decay_fwd, ret_decay_bwd, ret_norm_w, w_ret_branch, attn_sink,
           w_attn_branch, w_out, post_norm_w):
    B = x.shape[0]
    assert x.shape == (B, SEQ, D_MODEL) and pre_norm_w.shape[0] == 1 and w_in.shape == (1, D_MODEL, D_IN)
    cpt = CHUNKS_PER_TILE
    tm = cpt * CHUNK
    nt = N_CHUNKS // cpt

    w_in = w_in.astype(F32)
    w_rb, w_ab, w_o = (w.astype(F32) for w in (w_ret_branch, w_attn_branch, w_out))
    pre_w = pre_norm_w.astype(F32)
    post_w = post_norm_w.astype(F32)
    ret_nw = ret_norm_w.astype(F32)
    half = ATT_HD // 2
    inv = ROPE_THETA ** (-jnp.arange(half, dtype=F32) * 2.0 / ATT_HD)
    inv = jnp.concatenate([inv, inv])[None, :]
    assert ATT_HEADS == SUBLANES and RET_HEADS <= SUBLANES
    pad = jnp.zeros((SUBLANES - RET_HEADS,), F32)
    per_head = jnp.concatenate([ret_decay_fwd[0].astype(F32), pad, ret_decay_bwd[0].astype(F32), pad,
                                attn_sink[0].astype(F32)])
    per_head = jnp.broadcast_to(per_head[:, None], (2 * SUBLANES + ATT_HEADS, RET_DV))
    dec_f_spec = _resident((SUBLANES, RET_DV), (0, 0))
    dec_b_spec = _resident((SUBLANES, RET_DV), (1, 0))
    sink_spec = _resident((ATT_HEADS, ATT_HD), (2, 0))

    params = pltpu.CompilerParams(dimension_semantics=("arbitrary", "arbitrary"),
                                  vmem_limit_bytes=V7X_VMEM_LIMIT_BYTES)
    rope_scratch = [pltpu.VMEM((tm, ATT_HD), F32), pltpu.VMEM((tm, ATT_HD), F32)]
    state_scratch = pltpu.VMEM((RET_HEADS, RET_DK, RET_DV), F32)
    weight_stage = [pltpu.VMEM((2, D_MODEL, W_CHUNK), F32), pltpu.SemaphoreType.DMA((2,))]
    hbm = pl.BlockSpec(memory_space=pl.ANY)

    kcpt = KV_CHUNKS_PER_TILE
    ktm, knt = kcpt * CHUNK, N_CHUNKS // kcpt
    rkv, ak, avt, sb, kvf0, km, vmt = pl.pallas_call(
        functools.partial(_kv_kernel, cpt=kcpt),
        grid=(B, knt),
        in_specs=[pl.BlockSpec((None, ktm, D_MODEL), lambda b, t: (b, knt - 1 - t, 0)), _resident((N_META, D_MODEL)),
                  _resident((1, D_MODEL)), hbm, _resident((1, ATT_HD)), dec_f_spec, dec_b_spec],
        out_specs=(
            pl.BlockSpec((None, ktm, RKV_COLS), lambda b, t: (b, knt - 1 - t, 0)),
            pl.BlockSpec((None, ktm, ATT_KV), lambda b, t: (b, knt - 1 - t, 0)),
            pl.BlockSpec((None, ATT_KV, ktm), lambda b, t: (b, 0, knt - 1 - t)),
            pl.BlockSpec((None, ktm // RET_CHUNK, RET_HEADS, RET_DK, RET_DV), lambda b, t: (b, knt - 1 - t, 0, 0, 0)),
            pl.BlockSpec((RET_HEADS, RET_DK, RET_DV), lambda b, t: (0, 0, 0)),
            pl.BlockSpec((N_META, ATT_KV), lambda b, t: (0, 0)),
            pl.BlockSpec((ATT_KV, N_META), lambda b, t: (0, 0)),
        ),
        out_shape=(jax.ShapeDtypeStruct((B, SEQ, RKV_COLS), BF16),
                   jax.ShapeDtypeStruct((B, SEQ, ATT_KV), BF16),
                   jax.ShapeDtypeStruct((B, ATT_KV, SEQ), BF16),
                   jax.ShapeDtypeStruct((B, N_RET_CHUNKS, RET_HEADS, RET_DK, RET_DV), BF16),
                   jax.ShapeDtypeStruct((RET_HEADS, RET_DK, RET_DV), F32),
                   jax.ShapeDtypeStruct((N_META, ATT_KV), BF16),
                   jax.ShapeDtypeStruct((ATT_KV, N_META), BF16)),
        scratch_shapes=[state_scratch, pltpu.VMEM((ktm, ATT_HD), F32), pltpu.VMEM((ktm, ATT_HD), F32),
                        pltpu.VMEM((D_MODEL, KV_COLS), BF16)] + weight_stage,
        compiler_params=params,
        name="kv",
    )(x, meta_tokens.astype(F32), pre_w, w_in, inv, per_head, per_head)

    prev_chunk = lambda t: jnp.maximum(t * cpt - 1, 0)
    next_chunk = lambda t: jnp.minimum((t + 1) * cpt, N_CHUNKS - 1)
    out = pl.pallas_call(
        functools.partial(_main_kernel, cpt=cpt),
        grid=(B, nt),
        in_specs=[
            pl.BlockSpec((None, tm, D_MODEL), lambda b, t: (b, t, 0)),
            _resident((1, D_MODEL)),
            _resident((1, D_MODEL)),
            _resident((1, RET_V)),
            hbm, hbm, hbm, hbm,
            _resident((1, ATT_HD)),
            dec_f_spec,
            dec_b_spec,
            sink_spec,
            pl.BlockSpec((None, tm, RKV_COLS), lambda b, t: (b, t, 0)),
            pl.BlockSpec((None, CHUNK, ATT_KV), lambda b, t: (b, prev_chunk(t), 0)),
            pl.BlockSpec((None, tm, ATT_KV), lambda b, t: (b, t, 0)),
            pl.BlockSpec((None, CHUNK, ATT_KV), lambda b, t: (b, next_chunk(t), 0)),
            pl.BlockSpec((None, ATT_KV, CHUNK), lambda b, t: (b, 0, prev_chunk(t))),
            pl.BlockSpec((None, ATT_KV, tm), lambda b, t: (b, 0, t)),
            pl.BlockSpec((None, ATT_KV, CHUNK), lambda b, t: (b, 0, next_chunk(t))),
            pl.BlockSpec((None, tm // RET_CHUNK, RET_HEADS, RET_DK, RET_DV), lambda b, t: (b, t, 0, 0, 0)),
            _resident((RET_HEADS, RET_DK, RET_DV)),
            _resident((N_META, ATT_KV)),
            _resident((ATT_KV, N_META)),
        ],
        out_specs=pl.BlockSpec((None, tm, D_MODEL), lambda b, t: (b, t, 0)),
        out_shape=jax.ShapeDtypeStruct((B, SEQ, D_MODEL), x.dtype),
        scratch_shapes=(
            [state_scratch] + rope_scratch
            + [pltpu.VMEM((N_KEYS, Q_ROWS), F32),
               pltpu.VMEM((tm, D_MODEL), BF16),
               pltpu.VMEM((tm, RET_QK), F32),
               pltpu.VMEM((tm, ATT_Q), BF16),
               pltpu.VMEM((tm, RET_V), F32),
               pltpu.VMEM((tm, ATT_Q), F32)]
            + [pltpu.VMEM((tm, D_MODEL), F32)] * 4
            + [pltpu.VMEM((tm, D_MODEL), BF16)] * 3
            + [pltpu.VMEM((RET_HEADS, RET_CHUNK, RET_CHUNK), F32),
               pltpu.VMEM((RET_HEADS, RET_CHUNK, 2 * RET_DK), F32),
               pltpu.VMEM((RET_HEADS, RET_CHUNK, RET_DK), F32)]
            + [pltpu.VMEM((D_MODEL, MAIN_COLS), BF16)]
            + [pltpu.VMEM((D_MODEL, D_MODEL), BF16)] * 3
            + weight_stage
        ),
        compiler_params=params,
        name="main",
    )(x, pre_w, post_w, ret_nw, w_in, w_rb, w_ab, w_o, inv, per_head, per_head, per_head,
      rkv, ak, ak, ak, avt, avt, avt, sb, kvf0, km, vmt)
    return out
```

```python
import functools

import jax
import jax.numpy as jnp
from jax import lax
from jax.experimental import pallas as pl
from jax.experimental.pallas import tpu as pltpu

D_MODEL = 1024
SEQ = 8192
N_META = 16
CHUNK = 128
RET_HEADS = 4
RET_DK = 128
RET_DV = 256
ATT_HEADS = 8
ATT_GROUPS = 2
ATT_REP = ATT_HEADS // ATT_GROUPS
ATT_HD = 128
ROPE_THETA = 10000.0
EPS = 1e-6
NEG_INF = -1e30
LOG2_E = 1.4426950408889634
RET_QK = RET_HEADS * RET_DK
RET_V = RET_HEADS * RET_DV
ATT_Q = ATT_HEADS * ATT_HD
ATT_KV = ATT_GROUPS * ATT_HD
D_IN = 2 * RET_QK + 2 * RET_V + 2 * ATT_Q + 2 * ATT_KV + 2 * D_MODEL
N_CHUNKS = SEQ // CHUNK
N_BAND = 3 * CHUNK
N_KEYS = N_BAND + N_META
Q_ROWS = ATT_REP * CHUNK

OFF_RQ = 0
OFF_RK = OFF_RQ + RET_QK
OFF_RV = OFF_RK + RET_QK
OFF_RG = OFF_RV + RET_V
OFF_AQ = OFF_RG + RET_V
OFF_AK = OFF_AQ + ATT_Q
OFF_AV = OFF_AK + ATT_KV
OFF_AG = OFF_AV + ATT_KV
OFF_GR = OFF_AG + ATT_Q
OFF_GA = OFF_GR + D_MODEL
RKV_COLS = RET_QK + RET_V

W_CHUNK = 512
KV_SRC = tuple(range(OFF_RK, OFF_RG, W_CHUNK)) + tuple(range(OFF_AK, OFF_AG, W_CHUNK))
KV_RK, KV_RV, KV_AKV = 0, RET_QK, RET_QK + RET_V
KV_COLS = len(KV_SRC) * W_CHUNK
MAIN_SEGMENTS = (("rq", OFF_RQ, RET_QK), ("aq", OFF_AQ, ATT_Q), ("rg", OFF_RG, RET_V), ("ag", OFF_AG, ATT_Q),
                 ("gr", OFF_GR, D_MODEL), ("ga", OFF_GA, D_MODEL))
MAIN_SRC = tuple(c for _, off, width in MAIN_SEGMENTS for c in range(off, off + width, W_CHUNK))
MAIN_OFF = {}
_o = 0
for _name, _, _width in MAIN_SEGMENTS:
    MAIN_OFF[_name] = _o
    _o += _width
MAIN_COLS = _o

CHUNKS_PER_TILE = 4
KV_CHUNKS_PER_TILE = 8
RET_CHUNK = 2 * CHUNK
N_RET_CHUNKS = SEQ // RET_CHUNK
COL_BLOCK = 256
SUBLANES = 8
V7X_VMEM_LIMIT_BYTES = 56 * 1024 * 1024

F32 = jnp.float32
BF16 = jnp.bfloat16


def _rms_norm(x, w):
    return x * lax.rsqrt(jnp.mean(x * x, axis=-1, keepdims=True) + EPS) * w


def _log_sigmoid(x):
    return jnp.minimum(x, 0.0) - jnp.log(1.0 + jnp.exp(-jnp.abs(x)))


def _sign_fold(sin):
    lane = lax.broadcasted_iota(jnp.int32, sin.shape, 1)
    return jnp.where(lane < ATT_HD // 2, -sin, sin)


def _rope_tables(rows, inv):
    ang = lax.broadcasted_iota(jnp.int32, (rows, ATT_HD), 0).astype(F32) * inv
    return jnp.cos(ang), _sign_fold(jnp.sin(ang))


def _fill_rope_base(inv, cosb_ref, sinb_ref):
    ang = lax.broadcasted_iota(jnp.int32, (CHUNK, ATT_HD), 0).astype(F32) * inv
    cb, sb = jnp.cos(ang), jnp.sin(ang)
    for r0 in range(0, cosb_ref.shape[0], CHUNK):
        base = float(r0) * inv
        ca, sa = jnp.cos(base), jnp.sin(base)
        cosb_ref[r0:r0 + CHUNK] = ca * cb - sa * sb
        sinb_ref[r0:r0 + CHUNK] = sa * cb + ca * sb


def _rope_tables_from_base(pos0, inv, cosb_ref, sinb_ref):
    base = pos0.astype(F32) * inv
    ca, sa = jnp.cos(base), jnp.sin(base)
    cb, sb = cosb_ref[...], sinb_ref[...]
    return ca * cb - sa * sb, _sign_fold(sa * cb + ca * sb)


def _sigmoid(x):
    return 0.5 * jnp.tanh(0.5 * x) + 0.5


def _silu(x):
    return x * _sigmoid(x)


def _rope(t, cos, sin_signed):
    return t * cos + pltpu.roll(t, ATT_HD // 2, axis=1) * sin_signed


def _dot(a, b):
    return jnp.dot(a, b, preferred_element_type=F32)


def _dot_nt(a, b):
    return lax.dot_general(a, b, (((1,), (1,)), ((), ())), preferred_element_type=F32)


def _dot_tn(a, b):
    return lax.dot_general(a, b, (((0,), (0,)), ((), ())), preferred_element_type=F32)


def _row_index(shape):
    return lax.broadcasted_iota(jnp.int32, shape, 0).astype(F32)


def _weight_chunk_copy(src_ref, col, stage_ref, sem_ref, slot):
    return pltpu.make_async_copy(src_ref.at[0, :, pl.ds(col, W_CHUNK)], stage_ref.at[slot], sem_ref.at[slot])


def _load_weights_bf16(jobs, stage_ref, sem_ref, fillers=()):
    assert len(fillers) <= len(jobs)
    copies = [_weight_chunk_copy(src, col, stage_ref, sem_ref, i % 2) for i, (src, col, _, _) in enumerate(jobs)]
    copies[0].start()
    for i, (_, _, dst, dcol) in enumerate(jobs):
        if i + 1 < len(jobs):
            copies[i + 1].start()
        if i < len(fillers):
            fillers[i]()
        copies[i].wait()
        dst[:, dcol:dcol + W_CHUNK] = stage_ref[i % 2].astype(BF16)


def _meta_block(meta_ref, prew_ref, wkv_ref, inv_ref, decf_ref, kvf0_ref, km_ref, vmt_ref):
    u = _rms_norm(meta_ref[...], prew_ref[...]).astype(BF16)
    zk = _dot(u, wkv_ref[:, KV_RK:KV_RV])
    zv = _dot(u, wkv_ref[:, KV_RV:KV_AKV])
    za = _dot(u, wkv_ref[:, KV_AKV:])
    cos, sin = _rope_tables(N_META, inv_ref[...])
    kdec_rows = (N_META - 1) - _row_index((N_META, RET_DK))
    for h in range(RET_HEADS):
        lg = _log_sigmoid(decf_ref[h:h + 1, :RET_DK])
        k = _rope(zk[:, h * RET_DK:(h + 1) * RET_DK], cos, sin) * (RET_DK ** -0.5)
        k = (k * jnp.exp(kdec_rows * lg)).astype(BF16)
        v = zv[:, h * RET_DV:(h + 1) * RET_DV].astype(BF16)
        kvf0_ref[h] = _dot_tn(k, v)
    for g in range(ATT_GROUPS):
        km_ref[:, g * ATT_HD:(g + 1) * ATT_HD] = _rope(za[:, g * ATT_HD:(g + 1) * ATT_HD], cos, sin).astype(BF16)
    vmt_ref[...] = za[:, ATT_KV:].T.astype(BF16)


def _kv_kernel(x_ref, meta_ref, prew_ref, win_ref, inv_ref, decf_ref, decb_ref, rkv_ref, ak_ref, avt_ref, sb_ref,
               kvf0_ref, km_ref, vmt_ref, state_ref, cosb_ref, sinb_ref, wkv_ref, stage_ref, sem_ref, *, cpt):
    t = pl.program_id(1)
    tile = pl.num_programs(1) - 1 - t
    tm = cpt * CHUNK

    @pl.when((pl.program_id(0) == 0) & (t == 0))
    def _():
        _load_weights_bf16([(win_ref, col, wkv_ref, i * W_CHUNK) for i, col in enumerate(KV_SRC)],
                           stage_ref, sem_ref,
                           fillers=(functools.partial(_fill_rope_base, inv_ref[...], cosb_ref, sinb_ref),))
        _meta_block(meta_ref, prew_ref, wkv_ref, inv_ref, decf_ref, kvf0_ref, km_ref, vmt_ref)

    @pl.when(t == 0)
    def _():
        state_ref[...] = jnp.zeros_like(state_ref)

    u = _rms_norm(x_ref[...], prew_ref[...]).astype(BF16)
    za = _dot(u, wkv_ref[:, KV_AKV:])
    zk = _dot(u, wkv_ref[:, KV_RK:KV_RV])
    zv = _dot(u, wkv_ref[:, KV_RV:KV_AKV])
    cos, sin = _rope_tables_from_base(N_META + tile * tm, inv_ref[...], cosb_ref, sinb_ref)
    for g in range(ATT_GROUPS):
        ak_ref[:, g * ATT_HD:(g + 1) * ATT_HD] = _rope(za[:, g * ATT_HD:(g + 1) * ATT_HD], cos, sin).astype(BF16)
    avt_ref[...] = za[:, ATT_KV:].T.astype(BF16)
    rk = [_rope(zk[:, h * RET_DK:(h + 1) * RET_DK], cos, sin) * (RET_DK ** -0.5) for h in range(RET_HEADS)]
    for h in range(RET_HEADS):
        rkv_ref[:, h * RET_DK:(h + 1) * RET_DK] = rk[h].astype(BF16)
    rkv_ref[:, RET_QK:] = zv.astype(BF16)

    rows_k = _row_index((RET_CHUNK, RET_DK))
    for h in range(RET_HEADS):
        lg = _log_sigmoid(decb_ref[h:h + 1, :])
        kdec = jnp.exp(rows_k * lg[:, :RET_DK])
        cdec = jnp.exp(RET_CHUNK * lg)
        for rc in reversed(range(tm // RET_CHUNK)):
            r0 = rc * RET_CHUNK
            state = state_ref[h]
            sb_ref[rc, h] = state.astype(BF16)
            k = (rk[h][r0:r0 + RET_CHUNK] * kdec).astype(BF16)
            v = zv[r0:r0 + RET_CHUNK, h * RET_DV:(h + 1) * RET_DV].astype(BF16)
            state_ref[h] = cdec * state + _dot_tn(k, v)


def _fill_band_bias(bias_ref):
    kk = lax.broadcasted_iota(jnp.int32, bias_ref.shape, 0)
    qi = lax.broadcasted_iota(jnp.int32, bias_ref.shape, 1) & (CHUNK - 1)
    visible = (kk >= N_BAND) | ((kk >= qi) & (kk <= qi + 2 * CHUNK))
    bias_ref[...] = jnp.where(visible, 0.0, NEG_INF)


def _fill_decay_tables(decf_ref, decb_ref, dmat_ref, qdec_ref, kdf_ref):
    rows = _row_index((RET_CHUNK, RET_CHUNK))
    rel = rows - lax.broadcasted_iota(jnp.int32, (RET_CHUNK, RET_CHUNK), 1).astype(F32)
    rows_k = _row_index((RET_CHUNK, RET_DK))
    for h in range(RET_HEADS):
        lgf = _log_sigmoid(decf_ref[h:h + 1, :])
        lgb = _log_sigmoid(decb_ref[h:h + 1, :])
        dmat_ref[h] = jnp.where(rel >= 0, jnp.exp(jnp.maximum(rel, 0.0) * lgf),
                                jnp.exp(jnp.maximum(-rel, 0.0) * lgb))
        lgf_k, lgb_k = lgf[:, :RET_DK], lgb[:, :RET_DK]
        qdec_ref[h, :, :RET_DK] = jnp.exp((rows_k + 1.0) * lgf_k)
        qdec_ref[h, :, RET_DK:] = jnp.exp((RET_CHUNK - rows_k) * lgb_k)
        kdf_ref[h] = jnp.exp((RET_CHUNK - 1.0 - rows_k) * lgf_k)


def _main_kernel(x_ref, prew_ref, postw_ref, retnw_ref, win_ref, wrb_hbm_ref, wab_hbm_ref, wo_hbm_ref, inv_ref,
                 decf_ref, decb_ref, sink_ref, rkv_ref, akp_ref, akc_ref, akn_ref, avp_ref,
                 avc_ref, avn_ref, sb_ref, kvf0_ref, km_ref, vmt_ref, out_ref,
                 sf_ref, cosb_ref, sinb_ref, bias_ref, u_ref, rq_ref, aq_ref, or_ref, oa_ref, grg_ref, gag_ref,
                 ggr_ref, gga_ref, lhs_r_ref, lhs_a_ref, mix_ref, dmat_ref, qdec_ref, kdf_ref, wmain_ref, wrb_ref,
                 wab_ref, wo_ref, stage_ref, sem_ref, *, cpt):
    t = pl.program_id(1)
    tm = cpt * CHUNK

    @pl.when((pl.program_id(0) == 0) & (t == 0))
    def _():
        jobs = [(win_ref, col, wmain_ref, i * W_CHUNK) for i, col in enumerate(MAIN_SRC)]
        for src, dst in ((wrb_hbm_ref, wrb_ref), (wab_hbm_ref, wab_ref), (wo_hbm_ref, wo_ref)):
            jobs += [(src, c, dst, c) for c in range(0, D_MODEL, W_CHUNK)]
        _load_weights_bf16(jobs, stage_ref, sem_ref, fillers=(
            functools.partial(_fill_rope_base, inv_ref[...], cosb_ref, sinb_ref),
            functools.partial(_fill_band_bias, bias_ref),
            functools.partial(_fill_decay_tables, decf_ref, decb_ref, dmat_ref, qdec_ref, kdf_ref)))

    @pl.when(t == 0)
    def _():
        sf_ref[...] = kvf0_ref[...]

    for r0 in range(0, tm, CHUNK):
        u_ref[r0:r0 + CHUNK] = _rms_norm(x_ref[r0:r0 + CHUNK], prew_ref[...]).astype(BF16)
    cos, sin = _rope_tables_from_base(N_META + t * tm, inv_ref[...], cosb_ref, sinb_ref)

    def proj(name, c0):
        col = MAIN_OFF[name] + c0
        return _dot(u_ref[...], wmain_ref[:, col:col + COL_BLOCK])

    heads_per_block = COL_BLOCK // ATT_HD
    for c0 in range(0, RET_QK, COL_BLOCK):
        z = proj("rq", c0)
        for i in range(heads_per_block):
            c = c0 + i * RET_DK
            rq_ref[:, c:c + RET_DK] = _rope(z[:, i * RET_DK:(i + 1) * RET_DK], cos, sin)
    for c0 in range(0, ATT_Q, COL_BLOCK):
        z = proj("aq", c0)
        for i in range(heads_per_block):
            c = c0 + i * ATT_HD
            aq_ref[:, c:c + ATT_HD] = (_rope(z[:, i * ATT_HD:(i + 1) * ATT_HD], cos, sin)
                                       * (ATT_HD ** -0.5 * LOG2_E)).astype(BF16)

    gate_jobs = (("rg", _silu, grg_ref), ("ag", _silu, gag_ref), ("gr", _sigmoid, ggr_ref), ("ga", _sigmoid, gga_ref))
    gate_blocks = [(j, c) for j in range(len(gate_jobs)) for c in range(0, D_MODEL, COL_BLOCK)]
    assert len(gate_blocks) % cpt == 0

    def emit_gate_blocks(n):
        for _ in range(n):
            j, c = gate_blocks.pop(0)
            name, act, ref = gate_jobs[j]
            ref[:, c:c + COL_BLOCK] = act(proj(name, c))

    cdf = [jnp.exp(RET_CHUNK * _log_sigmoid(decf_ref[h:h + 1, :])) for h in range(RET_HEADS)]

    k_cat = jnp.concatenate([akp_ref[...], akc_ref[...], akn_ref[...]], axis=0)
    vt_cat = jnp.concatenate([avp_ref[...], avc_ref[...], avn_ref[...]], axis=1)
    sink_rows = [jnp.concatenate([sink_ref[g * ATT_REP + r:g * ATT_REP + r + 1, :] for r in range(ATT_REP)],
                                 axis=1) * LOG2_E for g in range(ATT_GROUPS)]

    def retention(rc):
        r0 = rc * RET_CHUNK
        rs = slice(r0, r0 + RET_CHUNK)
        for h in range(RET_HEADS):
            q = rq_ref[rs, h * RET_DK:(h + 1) * RET_DK]
            k = rkv_ref[rs, h * RET_DK:(h + 1) * RET_DK]
            v = rkv_ref[rs, RET_QK + h * RET_DV:RET_QK + (h + 1) * RET_DV]
            s = _dot_nt(q.astype(BF16), k) * dmat_ref[h]
            sf = sf_ref[h]
            q_cross = (jnp.concatenate([q, q], axis=1) * qdec_ref[h]).astype(BF16)
            s_cross = jnp.concatenate([sf.astype(BF16), sb_ref[rc, h]], axis=0)
            o = _dot(s.astype(BF16), v) + _dot(q_cross, s_cross)
            sf_ref[h] = cdf[h] * sf + _dot_tn((k.astype(F32) * kdf_ref[h]).astype(BF16), v)
            mu = jnp.mean(o, axis=-1, keepdims=True)
            d = o - mu
            var = jnp.mean(d * d, axis=-1, keepdims=True)
            or_ref[rs, h * RET_DV:(h + 1) * RET_DV] = d * lax.rsqrt(var + EPS)

    for rc in range(tm // RET_CHUNK):
        retention(rc)

    gates_per_pair = len(gate_blocks) // (cpt * ATT_GROUPS)
    pairs = [(lc, g) for lc in range(cpt) for g in range(ATT_GROUPS)]
    probs, denoms = {}, {}
    for lc, g in pairs:
        r0 = lc * CHUNK
        chunk = t * cpt + lc
        gs = slice(g * ATT_HD, (g + 1) * ATT_HD)
        qs = jnp.concatenate([aq_ref[r0:r0 + CHUNK, (g * ATT_REP + r) * ATT_HD:(g * ATT_REP + r + 1) * ATT_HD]
                              for r in range(ATT_REP)], axis=0)
        k_all = jnp.concatenate([k_cat[r0:r0 + N_BAND, gs], km_ref[:, gs]], axis=0)
        emit_gate_blocks(gates_per_pair)
        s = _dot_nt(k_all, qs) + bias_ref[...]
        parts = [s[0:CHUNK], s[CHUNK:2 * CHUNK], s[2 * CHUNK:N_BAND], s[N_BAND:]]
        if lc == 0:
            parts[0] = jnp.where(chunk > 0, parts[0], NEG_INF)
        if lc == cpt - 1:
            parts[2] = jnp.where(chunk < N_CHUNKS - 1, parts[2], NEG_INF)
        s = jnp.concatenate(parts, axis=0)
        sk = sink_rows[g]
        m = jnp.maximum(jnp.max(s, axis=0, keepdims=True), sk)
        p = jnp.exp2(s - m)
        denoms[lc, g] = jnp.sum(p, axis=0, keepdims=True) + jnp.exp2(sk - m)
        probs[lc, g] = p.astype(BF16)
    for lc, g in pairs:
        r0 = lc * CHUNK
        gs = slice(g * ATT_HD, (g + 1) * ATT_HD)
        vt_all = jnp.concatenate([vt_cat[gs, r0:r0 + N_BAND], vmt_ref[gs, :]], axis=1)
        ot = _dot(vt_all, probs[lc, g]) / denoms[lc, g]
        for r in range(ATT_REP):
            c = (g * ATT_REP + r) * ATT_HD
            oa_ref[r0:r0 + CHUNK, c:c + ATT_HD] = ot[:, r * CHUNK:(r + 1) * CHUNK].T

    for r0 in range(0, tm, CHUNK):
        rs = slice(r0, r0 + CHUNK)
        lhs_r_ref[rs] = (or_ref[rs] * retnw_ref[...] * grg_ref[rs]).astype(BF16)
        lhs_a_ref[rs] = (oa_ref[rs] * gag_ref[rs]).astype(BF16)
    for c0 in range(0, D_MODEL, COL_BLOCK):
        cs = slice(c0, c0 + COL_BLOCK)
        y_r = _dot(lhs_r_ref[...], wrb_ref[:, cs])
        y_a = _dot(lhs_a_ref[...], wab_ref[:, cs])
        mix_ref[:, cs] = (ggr_ref[:, cs] * y_r + gga_ref[:, cs] * y_a).astype(BF16)
    for c0 in range(0, D_MODEL, COL_BLOCK):
        cs = slice(c0, c0 + COL_BLOCK)
        out_ref[:, cs] = _dot(mix_ref[...], wo_ref[:, cs])
    for r0 in range(0, tm, CHUNK):
        rs = slice(r0, r0 + CHUNK)
        out_ref[rs] = x_ref[rs] + _rms_norm(out_ref[rs], postw_ref[...])


def _resident(shape, index=None):
    nd = len(shape)
    index = (0,) * nd if index is None else index
    return pl.BlockSpec(shape, lambda *_: index, pipeline_mode=pl.Buffered(1))


def kernel(x, meta_tokens, pre_norm_w, w_in, ret_decay_fwd, ret_decay_bwd, ret_norm_w, w_ret_branch, attn_sink,
           w_attn_branch, w_out, post_norm_w):
    B = x.shape[0]
    assert x.shape == (B, SEQ, D_MODEL) and pre_norm_w.shape[0] == 1 and w_in.shape == (1, D_MODEL, D_IN)
    cpt = CHUNKS_PER_TILE
    tm = cpt * CHUNK
    nt = N_CHUNKS // cpt

    w_in = w_in.astype(F32)
    w_rb, w_ab, w_o = (w.astype(F32) for w in (w_ret_branch, w_attn_branch, w_out))
    pre_w = pre_norm_w.astype(F32)
    post_w = post_norm_w.astype(F32)
    ret_nw = ret_norm_w.astype(F32)
    half = ATT_HD // 2
    inv = ROPE_THETA ** (-jnp.arange(half, dtype=F32) * 2.0 / ATT_HD)
    inv = jnp.concatenate([inv, inv])[None, :]
    assert ATT_HEADS == SUBLANES and RET_HEADS <= SUBLANES
    pad = jnp.zeros((SUBLANES - RET_HEADS,), F32)
    per_head = jnp.concatenate([ret_decay_fwd[0].astype(F32), pad, ret_decay_bwd[0].astype(F32), pad,
                                attn_sink[0].astype(F32)])
    per_head = jnp.broadcast_to(per_head[:, None], (2 * SUBLANES + ATT_HEADS, RET_DV))
    dec_f_spec = _resident((SUBLANES, RET_DV), (0, 0))
    dec_b_spec = _resident((SUBLANES, RET_DV), (1, 0))
    sink_spec = _resident((ATT_HEADS, ATT_HD), (2, 0))

    params = pltpu.CompilerParams(dimension_semantics=("arbitrary", "arbitrary"),
                                  vmem_limit_bytes=V7X_VMEM_LIMIT_BYTES)
    rope_scratch = [pltpu.VMEM((tm, ATT_HD), F32), pltpu.VMEM((tm, ATT_HD), F32)]
    state_scratch = pltpu.VMEM((RET_HEADS, RET_DK, RET_DV), F32)
    weight_stage = [pltpu.VMEM((2, D_MODEL, W_CHUNK), F32), pltpu.SemaphoreType.DMA((2,))]
    hbm = pl.BlockSpec(memory_space=pl.ANY)

    kcpt = KV_CHUNKS_PER_TILE
    ktm, knt = kcpt * CHUNK, N_CHUNKS // kcpt
    rkv, ak, avt, sb, kvf0, km, vmt = pl.pallas_call(
        functools.partial(_kv_kernel, cpt=kcpt),
        grid=(B, knt),
        in_specs=[pl.BlockSpec((None, ktm, D_MODEL), lambda b, t: (b, knt - 1 - t, 0)), _resident((N_META, D_MODEL)),
                  _resident((1, D_MODEL)), hbm, _resident((1, ATT_HD)), dec_f_spec, dec_b_spec],
        out_specs=(
            pl.BlockSpec((None, ktm, RKV_COLS), lambda b, t: (b, knt - 1 - t, 0)),
            pl.BlockSpec((None, ktm, ATT_KV), lambda b, t: (b, knt - 1 - t, 0)),
            pl.BlockSpec((None, ATT_KV, ktm), lambda b, t: (b, 0, knt - 1 - t)),
            pl.BlockSpec((None, ktm // RET_CHUNK, RET_HEADS, RET_DK, RET_DV), lambda b, t: (b, knt - 1 - t, 0, 0, 0)),
            pl.BlockSpec((RET_HEADS, RET_DK, RET_DV), lambda b, t: (0, 0, 0)),
            pl.BlockSpec((N_META, ATT_KV), lambda b, t: (0, 0)),
            pl.BlockSpec((ATT_KV, N_META), lambda b, t: (0, 0)),
        ),
        out_shape=(jax.ShapeDtypeStruct((B, SEQ, RKV_COLS), BF16),
                   jax.ShapeDtypeStruct((B, SEQ, ATT_KV), BF16),
                   jax.ShapeDtypeStruct((B, ATT_KV, SEQ), BF16),
                   jax.ShapeDtypeStruct((B, N_RET_CHUNKS, RET_HEADS, RET_DK, RET_DV), BF16),
                   jax.ShapeDtypeStruct((RET_HEADS, RET_DK, RET_DV), F32),
                   jax.ShapeDtypeStruct((N_META, ATT_KV), BF16),
                   jax.ShapeDtypeStruct((ATT_KV, N_META), BF16)),
        scratch_shapes=[state_scratch, pltpu.VMEM((ktm, ATT_HD), F32), pltpu.VMEM((ktm, ATT_HD), F32),
                        pltpu.VMEM((D_MODEL, KV_COLS), BF16)] + weight_stage,
        compiler_params=params,
        name="kv",
    )(x, meta_tokens.astype(F32), pre_w, w_in, inv, per_head, per_head)

    prev_chunk = lambda t: jnp.maximum(t * cpt - 1, 0)
    next_chunk = lambda t: jnp.minimum((t + 1) * cpt, N_CHUNKS - 1)
    out = pl.pallas_call(
        functools.partial(_main_kernel, cpt=cpt),
        grid=(B, nt),
        in_specs=[
            pl.BlockSpec((None, tm, D_MODEL), lambda b, t: (b, t, 0)),
            _resident((1, D_MODEL)),
            _resident((1, D_MODEL)),
            _resident((1, RET_V)),
            hbm, hbm, hbm, hbm,
            _resident((1, ATT_HD)),
            dec_f_spec,
            dec_b_spec,
            sink_spec,
            pl.BlockSpec((None, tm, RKV_COLS), lambda b, t: (b, t, 0)),
            pl.BlockSpec((None, CHUNK, ATT_KV), lambda b, t: (b, prev_chunk(t), 0)),
            pl.BlockSpec((None, tm, ATT_KV), lambda b, t: (b, t, 0)),
            pl.BlockSpec((None, CHUNK, ATT_KV), lambda b, t: (b, next_chunk(t), 0)),
            pl.BlockSpec((None, ATT_KV, CHUNK), lambda b, t: (b, 0, prev_chunk(t))),
            pl.BlockSpec((None, ATT_KV, tm), lambda b, t: (b, 0, t)),
            pl.BlockSpec((None, ATT_KV, CHUNK), lambda b, t: (b, 0, next_chunk(t))),
            pl.BlockSpec((None, tm // RET_CHUNK, RET_HEADS, RET_DK, RET_DV), lambda b, t: (b, t, 0, 0, 0)),
            _resident((RET_HEADS, RET_DK, RET_DV)),
            _resident((N_META, ATT_KV)),
            _resident((ATT_KV, N_META)),
        ],
        out_specs=pl.BlockSpec((None, tm, D_MODEL), lambda b, t: (b, t, 0)),
        out_shape=jax.ShapeDtypeStruct((B, SEQ, D_MODEL), x.dtype),
        scratch_shapes=(
            [state_scratch] + rope_scratch
            + [pltpu.VMEM((N_KEYS, Q_ROWS), F32),
               pltpu.VMEM((tm, D_MODEL), BF16),
               pltpu.VMEM((tm, RET_QK), F32),
               pltpu.VMEM((tm, ATT_Q), BF16),
               pltpu.VMEM((tm, RET_V), F32),
               pltpu.VMEM((tm, ATT_Q), F32)]
            + [pltpu.VMEM((tm, D_MODEL), F32)] * 4
            + [pltpu.VMEM((tm, D_MODEL), BF16)] * 3
            + [pltpu.VMEM((RET_HEADS, RET_CHUNK, RET_CHUNK), F32),
               pltpu.VMEM((RET_HEADS, RET_CHUNK, 2 * RET_DK), F32),
               pltpu.VMEM((RET_HEADS, RET_CHUNK, RET_DK), F32)]
            + [pltpu.VMEM((D_MODEL, MAIN_COLS), BF16)]
            + [pltpu.VMEM((D_MODEL, D_MODEL), BF16)] * 3
            + weight_stage
        ),
        compiler_params=params,
        name="main",
    )(x, pre_w, post_w, ret_nw, w_in, w_rb, w_ab, w_o, inv, per_head, per_head, per_head,
      rkv, ak, ak, ak, avt, avt, avt, sb, kvf0, km, vmt)
    return out
```

```python
import functools

import jax
import jax.numpy as jnp
from jax import lax
from jax.experimental import pallas as pl
from jax.experimental.pallas import tpu as pltpu

D_MODEL = 1024
SEQ = 8192
N_META = 16
CHUNK = 128
RET_HEADS = 4
RET_DK = 128
RET_DV = 256
ATT_HEADS = 8
ATT_GROUPS = 2
ATT_REP = ATT_HEADS // ATT_GROUPS
ATT_HD = 128
ROPE_THETA = 10000.0
EPS = 1e-6
NEG_INF = -1e30
LOG2_E = 1.4426950408889634
RET_QK = RET_HEADS * RET_DK
RET_V = RET_HEADS * RET_DV
ATT_Q = ATT_HEADS * ATT_HD
ATT_KV = ATT_GROUPS * ATT_HD
D_IN = 2 * RET_QK + 2 * RET_V + 2 * ATT_Q + 2 * ATT_KV + 2 * D_MODEL
N_CHUNKS = SEQ // CHUNK
N_BAND = 3 * CHUNK
N_KEYS = N_BAND + N_META
Q_ROWS = ATT_REP * CHUNK

OFF_RQ = 0
OFF_RK = OFF_RQ + RET_QK
OFF_RV = OFF_RK + RET_QK
OFF_RG = OFF_RV + RET_V
OFF_AQ = OFF_RG + RET_V
OFF_AK = OFF_AQ + ATT_Q
OFF_AV = OFF_AK + ATT_KV
OFF_AG = OFF_AV + ATT_KV
OFF_GR = OFF_AG + ATT_Q
OFF_GA = OFF_GR + D_MODEL
RKV_COLS = RET_QK + RET_V

W_CHUNK = 512
KV_SRC = tuple(range(OFF_RK, OFF_RG, W_CHUNK)) + tuple(range(OFF_AK, OFF_AG, W_CHUNK))
KV_RK, KV_RV, KV_AKV = 0, RET_QK, RET_QK + RET_V
KV_COLS = len(KV_SRC) * W_CHUNK
MAIN_SEGMENTS = (("rq", OFF_RQ, RET_QK), ("aq", OFF_AQ, ATT_Q), ("rg", OFF_RG, RET_V), ("ag", OFF_AG, ATT_Q),
                 ("gr", OFF_GR, D_MODEL), ("ga", OFF_GA, D_MODEL))
MAIN_SRC = tuple(c for _, off, width in MAIN_SEGMENTS for c in range(off, off + width, W_CHUNK))
MAIN_OFF = {}
_o = 0
for _name, _, _width in MAIN_SEGMENTS:
    MAIN_OFF[_name] = _o
    _o += _width
MAIN_COLS = _o

CHUNKS_PER_TILE = 4
KV_CHUNKS_PER_TILE = 8
RET_CHUNK = 2 * CHUNK
N_RET_CHUNKS = SEQ // RET_CHUNK
COL_BLOCK = 256
SUBLANES = 8
V7X_VMEM_LIMIT_BYTES = 56 * 1024 * 1024

F32 = jnp.float32
BF16 = jnp.bfloat16


def _rms_norm(x, w):
    return x * lax.rsqrt(jnp.mean(x * x, axis=-1, keepdims=True) + EPS) * w


def _log_sigmoid(x):
    return jnp.minimum(x, 0.0) - jnp.log(1.0 + jnp.exp(-jnp.abs(x)))


def _sign_fold(sin):
    lane = lax.broadcasted_iota(jnp.int32, sin.shape, 1)
    return jnp.where(lane < ATT_HD // 2, -sin, sin)


def _rope_tables(rows, inv):
    ang = lax.broadcasted_iota(jnp.int32, (rows, ATT_HD), 0).astype(F32) * inv
    return jnp.cos(ang), _sign_fold(jnp.sin(ang))


def _fill_rope_base(inv, cosb_ref, sinb_ref):
    ang = lax.broadcasted_iota(jnp.int32, (CHUNK, ATT_HD), 0).astype(F32) * inv
    cb, sb = jnp.cos(ang), jnp.sin(ang)
    for r0 in range(0, cosb_ref.shape[0], CHUNK):
        base = float(r0) * inv
        ca, sa = jnp.cos(base), jnp.sin(base)
        cosb_ref[r0:r0 + CHUNK] = ca * cb - sa * sb
        sinb_ref[r0:r0 + CHUNK] = sa * cb + ca * sb


def _rope_tables_from_base(pos0, inv, cosb_ref, sinb_ref):
    base = pos0.astype(F32) * inv
    ca, sa = jnp.cos(base), jnp.sin(base)
    cb, sb = cosb_ref[...], sinb_ref[...]
    return ca * cb - sa * sb, _sign_fold(sa * cb + ca * sb)


def _sigmoid(x):
    return 0.5 * jnp.tanh(0.5 * x) + 0.5


def _silu(x):
    return x * _sigmoid(x)


def _rope(t, cos, sin_signed):
    return t * cos + pltpu.roll(t, ATT_HD // 2, axis=1) * sin_signed


def _dot(a, b):
    return jnp.dot(a, b, preferred_element_type=F32)


def _dot_nt(a, b):
    return lax.dot_general(a, b, (((1,), (1,)), ((), ())), preferred_element_type=F32)


def _dot_tn(a, b):
    return lax.dot_general(a, b, (((0,), (0,)), ((), ())), preferred_element_type=F32)


def _row_index(shape):
    return lax.broadcasted_iota(jnp.int32, shape, 0).astype(F32)


def _weight_chunk_copy(src_ref, col, stage_ref, sem_ref, slot):
    return pltpu.make_async_copy(src_ref.at[0, :, pl.ds(col, W_CHUNK)], stage_ref.at[slot], sem_ref.at[slot])


def _load_weights_bf16(jobs, stage_ref, sem_ref):
    copies = [_weight_chunk_copy(src, col, stage_ref, sem_ref, i % 2) for i, (src, col, _, _) in enumerate(jobs)]
    copies[0].start()
    for i, (_, _, dst, dcol) in enumerate(jobs):
        if i + 1 < len(jobs):
            copies[i + 1].start()
        copies[i].wait()
        dst[:, dcol:dcol + W_CHUNK] = stage_ref[i % 2].astype(BF16)


def _meta_block(meta_ref, prew_ref, wkv_ref, inv_ref, decf_ref, kvf0_ref, km_ref, vmt_ref):
    u = _rms_norm(meta_ref[...], prew_ref[...]).astype(BF16)
    zk = _dot(u, wkv_ref[:, KV_RK:KV_RV])
    zv = _dot(u, wkv_ref[:, KV_RV:KV_AKV])
    za = _dot(u, wkv_ref[:, KV_AKV:])
    cos, sin = _rope_tables(N_META, inv_ref[...])
    kdec_rows = (N_META - 1) - _row_index((N_META, RET_DK))
    for h in range(RET_HEADS):
        lg = _log_sigmoid(decf_ref[h:h + 1, :RET_DK])
        k = _rope(zk[:, h * RET_DK:(h + 1) * RET_DK], cos, sin) * (RET_DK ** -0.5)
        k = (k * jnp.exp(kdec_rows * lg)).astype(BF16)
        v = zv[:, h * RET_DV:(h + 1) * RET_DV].astype(BF16)
        kvf0_ref[h] = _dot_tn(k, v)
    for g in range(ATT_GROUPS):
        km_ref[:, g * ATT_HD:(g + 1) * ATT_HD] = _rope(za[:, g * ATT_HD:(g + 1) * ATT_HD], cos, sin).astype(BF16)
    vmt_ref[...] = za[:, ATT_KV:].T.astype(BF16)


def _kv_kernel(x_ref, meta_ref, prew_ref, win_ref, inv_ref, decf_ref, decb_ref, win_rows_ref, wrb_rows_ref,
               wab_rows_ref, wo_rows_ref, rkv_ref, ak_ref, avt_ref, sb_ref, kvf0_ref, km_ref, vmt_ref, wmain_out_ref,
               wrb_out_ref, wab_out_ref, wo_out_ref, state_ref, cosb_ref, sinb_ref, wkv_ref, stage_ref, sem_ref, *,
               cpt):
    t = pl.program_id(1)
    tile = pl.num_programs(1) - 1 - t
    tm = cpt * CHUNK

    for i, col in enumerate(MAIN_SRC):
        wmain_out_ref[:, i * W_CHUNK:(i + 1) * W_CHUNK] = win_rows_ref[:, col:col + W_CHUNK].astype(BF16)
    for src, dst in ((wrb_rows_ref, wrb_out_ref), (wab_rows_ref, wab_out_ref), (wo_rows_ref, wo_out_ref)):
        dst[...] = src[...].astype(BF16)

    @pl.when((pl.program_id(0) == 0) & (t == 0))
    def _():
        _load_weights_bf16([(win_ref, col, wkv_ref, i * W_CHUNK) for i, col in enumerate(KV_SRC)],
                           stage_ref, sem_ref)
        _fill_rope_base(inv_ref[...], cosb_ref, sinb_ref)
        _meta_block(meta_ref, prew_ref, wkv_ref, inv_ref, decf_ref, kvf0_ref, km_ref, vmt_ref)

    @pl.when(t == 0)
    def _():
        state_ref[...] = jnp.zeros_like(state_ref)

    u = _rms_norm(x_ref[...], prew_ref[...]).astype(BF16)
    za = _dot(u, wkv_ref[:, KV_AKV:])
    zk = _dot(u, wkv_ref[:, KV_RK:KV_RV])
    zv = _dot(u, wkv_ref[:, KV_RV:KV_AKV])
    cos, sin = _rope_tables_from_base(N_META + tile * tm, inv_ref[...], cosb_ref, sinb_ref)
    for g in range(ATT_GROUPS):
        ak_ref[:, g * ATT_HD:(g + 1) * ATT_HD] = _rope(za[:, g * ATT_HD:(g + 1) * ATT_HD], cos, sin).astype(BF16)
    avt_ref[...] = za[:, ATT_KV:].T.astype(BF16)
    rk = [_rope(zk[:, h * RET_DK:(h + 1) * RET_DK], cos, sin) * (RET_DK ** -0.5) for h in range(RET_HEADS)]
    for h in range(RET_HEADS):
        rkv_ref[:, h * RET_DK:(h + 1) * RET_DK] = rk[h].astype(BF16)
    rkv_ref[:, RET_QK:] = zv.astype(BF16)

    rows_k = _row_index((RET_CHUNK, RET_DK))
    for h in range(RET_HEADS):
        lg = _log_sigmoid(decb_ref[h:h + 1, :])
        kdec = jnp.exp(rows_k * lg[:, :RET_DK])
        cdec = jnp.exp(RET_CHUNK * lg)
        for rc in reversed(range(tm // RET_CHUNK)):
            r0 = rc * RET_CHUNK
            state = state_ref[h]
            sb_ref[rc, h] = state.astype(BF16)
            k = (rk[h][r0:r0 + RET_CHUNK] * kdec).astype(BF16)
            v = zv[r0:r0 + RET_CHUNK, h * RET_DV:(h + 1) * RET_DV].astype(BF16)
            state_ref[h] = cdec * state + _dot_tn(k, v)


def _fill_band_bias(bias_ref):
    kk = lax.broadcasted_iota(jnp.int32, bias_ref.shape, 0)
    qi = lax.broadcasted_iota(jnp.int32, bias_ref.shape, 1) & (CHUNK - 1)
    visible = (kk >= N_BAND) | ((kk >= qi) & (kk <= qi + 2 * CHUNK))
    bias_ref[...] = jnp.where(visible, 0.0, NEG_INF)


def _fill_decay_tables(decf_ref, decb_ref, dmat_ref, qdec_ref, kdf_ref):
    rows = _row_index((RET_CHUNK, RET_CHUNK))
    rel = rows - lax.broadcasted_iota(jnp.int32, (RET_CHUNK, RET_CHUNK), 1).astype(F32)
    rows_k = _row_index((RET_CHUNK, RET_DK))
    for h in range(RET_HEADS):
        lgf = _log_sigmoid(decf_ref[h:h + 1, :])
        lgb = _log_sigmoid(decb_ref[h:h + 1, :])
        dmat_ref[h] = jnp.where(rel >= 0, jnp.exp(jnp.maximum(rel, 0.0) * lgf),
                                jnp.exp(jnp.maximum(-rel, 0.0) * lgb))
        lgf_k, lgb_k = lgf[:, :RET_DK], lgb[:, :RET_DK]
        qdec_ref[h, :, :RET_DK] = jnp.exp((rows_k + 1.0) * lgf_k)
        qdec_ref[h, :, RET_DK:] = jnp.exp((RET_CHUNK - rows_k) * lgb_k)
        kdf_ref[h] = jnp.exp((RET_CHUNK - 1.0 - rows_k) * lgf_k)


def _main_kernel(x_ref, prew_ref, postw_ref, retnw_ref, wmain_ref, wrb_ref, wab_ref, wo_ref, inv_ref,
                 decf_ref, decb_ref, sink_ref, rkv_ref, akp_ref, akc_ref, akn_ref, avp_ref,
                 avc_ref, avn_ref, sb_ref, kvf0_ref, km_ref, vmt_ref, out_ref,
                 sf_ref, cosb_ref, sinb_ref, bias_ref, u_ref, rq_ref, aq_ref, or_ref, oa_ref, grg_ref, gag_ref,
                 ggr_ref, gga_ref, lhs_r_ref, lhs_a_ref, mix_ref, dmat_ref, qdec_ref, kdf_ref, *, cpt):
    t = pl.program_id(1)
    tm = cpt * CHUNK

    @pl.when((pl.program_id(0) == 0) & (t == 0))
    def _():
        _fill_rope_base(inv_ref[...], cosb_ref, sinb_ref)
        _fill_band_bias(bias_ref)
        _fill_decay_tables(decf_ref, decb_ref, dmat_ref, qdec_ref, kdf_ref)

    @pl.when(t == 0)
    def _():
        sf_ref[...] = kvf0_ref[...]

    for r0 in range(0, tm, CHUNK):
        u_ref[r0:r0 + CHUNK] = _rms_norm(x_ref[r0:r0 + CHUNK], prew_ref[...]).astype(BF16)
    cos, sin = _rope_tables_from_base(N_META + t * tm, inv_ref[...], cosb_ref, sinb_ref)

    def proj(name, c0):
        col = MAIN_OFF[name] + c0
        return _dot(u_ref[...], wmain_ref[:, col:col + COL_BLOCK])

    heads_per_block = COL_BLOCK // ATT_HD
    for c0 in range(0, RET_QK, COL_BLOCK):
        z = proj("rq", c0)
        for i in range(heads_per_block):
            c = c0 + i * RET_DK
            rq_ref[:, c:c + RET_DK] = _rope(z[:, i * RET_DK:(i + 1) * RET_DK], cos, sin)
    for c0 in range(0, ATT_Q, COL_BLOCK):
        z = proj("aq", c0)
        for i in range(heads_per_block):
            c = c0 + i * ATT_HD
            aq_ref[:, c:c + ATT_HD] = (_rope(z[:, i * ATT_HD:(i + 1) * ATT_HD], cos, sin)
                                       * (ATT_HD ** -0.5 * LOG2_E)).astype(BF16)

    gate_jobs = (("rg", _silu, grg_ref), ("ag", _silu, gag_ref), ("gr", _sigmoid, ggr_ref), ("ga", _sigmoid, gga_ref))
    gate_blocks = [(j, c) for j in range(len(gate_jobs)) for c in range(0, D_MODEL, COL_BLOCK)]
    assert len(gate_blocks) % cpt == 0

    def emit_gate_blocks(n):
        for _ in range(n):
            j, c = gate_blocks.pop(0)
            name, act, ref = gate_jobs[j]
            ref[:, c:c + COL_BLOCK] = act(proj(name, c))

    cdf = [jnp.exp(RET_CHUNK * _log_sigmoid(decf_ref[h:h + 1, :])) for h in range(RET_HEADS)]

    k_cat = jnp.concatenate([akp_ref[...], akc_ref[...], akn_ref[...]], axis=0)
    vt_cat = jnp.concatenate([avp_ref[...], avc_ref[...], avn_ref[...]], axis=1)
    sink_rows = [jnp.concatenate([sink_ref[g * ATT_REP + r:g * ATT_REP + r + 1, :] for r in range(ATT_REP)],
                                 axis=1) * LOG2_E for g in range(ATT_GROUPS)]

    def retention(rc):
        r0 = rc * RET_CHUNK
        rs = slice(r0, r0 + RET_CHUNK)
        for h in range(RET_HEADS):
            q = rq_ref[rs, h * RET_DK:(h + 1) * RET_DK]
            k = rkv_ref[rs, h * RET_DK:(h + 1) * RET_DK]
            v = rkv_ref[rs, RET_QK + h * RET_DV:RET_QK + (h + 1) * RET_DV]
            s = _dot_nt(q.astype(BF16), k) * dmat_ref[h]
            sf = sf_ref[h]
            q_cross = (jnp.concatenate([q, q], axis=1) * qdec_ref[h]).astype(BF16)
            s_cross = jnp.concatenate([sf.astype(BF16), sb_ref[rc, h]], axis=0)
            o = _dot(s.astype(BF16), v) + _dot(q_cross, s_cross)
            sf_ref[h] = cdf[h] * sf + _dot_tn((k.astype(F32) * kdf_ref[h]).astype(BF16), v)
            mu = jnp.mean(o, axis=-1, keepdims=True)
            d = o - mu
            var = jnp.mean(d * d, axis=-1, keepdims=True)
            or_ref[rs, h * RET_DV:(h + 1) * RET_DV] = d * lax.rsqrt(var + EPS)

    for rc in range(tm // RET_CHUNK):
        retention(rc)

    gates_per_pair = len(gate_blocks) // (cpt * ATT_GROUPS)
    pairs = [(lc, g) for lc in range(cpt) for g in range(ATT_GROUPS)]
    probs, denoms = {}, {}
    for lc, g in pairs:
        r0 = lc * CHUNK
        chunk = t * cpt + lc
        gs = slice(g * ATT_HD, (g + 1) * ATT_HD)
        qs = jnp.concatenate([aq_ref[r0:r0 + CHUNK, (g * ATT_REP + r) * ATT_HD:(g * ATT_REP + r + 1) * ATT_HD]
                              for r in range(ATT_REP)], axis=0)
        k_all = jnp.concatenate([k_cat[r0:r0 + N_BAND, gs], km_ref[:, gs]], axis=0)
        emit_gate_blocks(gates_per_pair)
        s = _dot_nt(k_all, qs) + bias_ref[...]
        parts = [s[0:CHUNK], s[CHUNK:2 * CHUNK], s[2 * CHUNK:N_BAND], s[N_BAND:]]
        if lc == 0:
            parts[0] = jnp.where(chunk > 0, parts[0], NEG_INF)
        if lc == cpt - 1:
            parts[2] = jnp.where(chunk < N_CHUNKS - 1, parts[2], NEG_INF)
        s = jnp.concatenate(parts, axis=0)
        sk = sink_rows[g]
        m = jnp.maximum(jnp.max(s, axis=0, keepdims=True), sk)
        p = jnp.exp2(s - m)
        denoms[lc, g] = jnp.sum(p, axis=0, keepdims=True) + jnp.exp2(sk - m)
        probs[lc, g] = p.astype(BF16)
    for lc, g in pairs:
        r0 = lc * CHUNK
        gs = slice(g * ATT_HD, (g + 1) * ATT_HD)
        vt_all = jnp.concatenate([vt_cat[gs, r0:r0 + N_BAND], vmt_ref[gs, :]], axis=1)
        ot = _dot(vt_all, probs[lc, g]) / denoms[lc, g]
        for r in range(ATT_REP):
            c = (g * ATT_REP + r) * ATT_HD
            oa_ref[r0:r0 + CHUNK, c:c + ATT_HD] = ot[:, r * CHUNK:(r + 1) * CHUNK].T

    for r0 in range(0, tm, CHUNK):
        rs = slice(r0, r0 + CHUNK)
        lhs_r_ref[rs] = (or_ref[rs] * retnw_ref[...] * grg_ref[rs]).astype(BF16)
        lhs_a_ref[rs] = (oa_ref[rs] * gag_ref[rs]).astype(BF16)
    for c0 in range(0, D_MODEL, COL_BLOCK):
        cs = slice(c0, c0 + COL_BLOCK)
        y_r = _dot(lhs_r_ref[...], wrb_ref[:, cs])
        y_a = _dot(lhs_a_ref[...], wab_ref[:, cs])
        mix_ref[:, cs] = (ggr_ref[:, cs] * y_r + gga_ref[:, cs] * y_a).astype(BF16)
    for c0 in range(0, D_MODEL, COL_BLOCK):
        cs = slice(c0, c0 + COL_BLOCK)
        out_ref[:, cs] = _dot(mix_ref[...], wo_ref[:, cs])
    for r0 in range(0, tm, CHUNK):
        rs = slice(r0, r0 + CHUNK)
        out_ref[rs] = x_ref[rs] + _rms_norm(out_ref[rs], postw_ref[...])


def _resident(shape, index=None):
    nd = len(shape)
    index = (0,) * nd if index is None else index
    return pl.BlockSpec(shape, lambda *_: index, pipeline_mode=pl.Buffered(1))


def kernel(x, meta_tokens, pre_norm_w, w_in, ret_decay_fwd, ret_decay_bwd, ret_norm_w, w_ret_branch, attn_sink,
           w_attn_branch, w_out, post_norm_w):
    B = x.shape[0]
    assert x.shape == (B, SEQ, D_MODEL) and pre_norm_w.shape[0] == 1 and w_in.shape == (1, D_MODEL, D_IN)
    cpt = CHUNKS_PER_TILE
    tm = cpt * CHUNK
    nt = N_CHUNKS // cpt

    w_in = w_in.astype(F32)
    w_rb, w_ab, w_o = (w.astype(F32) for w in (w_ret_branch, w_attn_branch, w_out))
    pre_w = pre_norm_w.astype(F32)
    post_w = post_norm_w.astype(F32)
    ret_nw = ret_norm_w.astype(F32)
    half = ATT_HD // 2
    inv = ROPE_THETA ** (-jnp.arange(half, dtype=F32) * 2.0 / ATT_HD)
    inv = jnp.concatenate([inv, inv])[None, :]
    assert ATT_HEADS == SUBLANES and RET_HEADS <= SUBLANES
    pad = jnp.zeros((SUBLANES - RET_HEADS,), F32)
    per_head = jnp.concatenate([ret_decay_fwd[0].astype(F32), pad, ret_decay_bwd[0].astype(F32), pad,
                                attn_sink[0].astype(F32)])
    per_head = jnp.broadcast_to(per_head[:, None], (2 * SUBLANES + ATT_HEADS, RET_DV))
    dec_f_spec = _resident((SUBLANES, RET_DV), (0, 0))
    dec_b_spec = _resident((SUBLANES, RET_DV), (1, 0))
    sink_spec = _resident((ATT_HEADS, ATT_HD), (2, 0))

    params = pltpu.CompilerParams(dimension_semantics=("arbitrary", "arbitrary"),
                                  vmem_limit_bytes=V7X_VMEM_LIMIT_BYTES)
    rope_scratch = [pltpu.VMEM((tm, ATT_HD), F32), pltpu.VMEM((tm, ATT_HD), F32)]
    state_scratch = pltpu.VMEM((RET_HEADS, RET_DK, RET_DV), F32)
    weight_stage = [pltpu.VMEM((2, D_MODEL, W_CHUNK), F32), pltpu.SemaphoreType.DMA((2,))]
    hbm = pl.BlockSpec(memory_space=pl.ANY)

    kcpt = KV_CHUNKS_PER_TILE
    ktm, knt = kcpt * CHUNK, N_CHUNKS // kcpt
    slab = D_MODEL // (B * knt)
    assert slab * B * knt == D_MODEL and slab % (2 * SUBLANES) == 0

    def slab_in(width):
        return pl.BlockSpec((None, slab, width), lambda b, t: (0, b * knt + t, 0))

    def slab_out(width):
        return pl.BlockSpec((slab, width), lambda b, t: (b * knt + t, 0))

    rkv, ak, avt, sb, kvf0, km, vmt, w_main_b, w_rb_b, w_ab_b, w_o_b = pl.pallas_call(
        functools.partial(_kv_kernel, cpt=kcpt),
        grid=(B, knt),
        in_specs=[pl.BlockSpec((None, ktm, D_MODEL), lambda b, t: (b, knt - 1 - t, 0)), _resident((N_META, D_MODEL)),
                  _resident((1, D_MODEL)), hbm, _resident((1, ATT_HD)), dec_f_spec, dec_b_spec,
                  slab_in(D_IN), slab_in(D_MODEL), slab_in(D_MODEL), slab_in(D_MODEL)],
        out_specs=(
            pl.BlockSpec((None, ktm, RKV_COLS), lambda b, t: (b, knt - 1 - t, 0)),
            pl.BlockSpec((None, ktm, ATT_KV), lambda b, t: (b, knt - 1 - t, 0)),
            pl.BlockSpec((None, ATT_KV, ktm), lambda b, t: (b, 0, knt - 1 - t)),
            pl.BlockSpec((None, ktm // RET_CHUNK, RET_HEADS, RET_DK, RET_DV), lambda b, t: (b, knt - 1 - t, 0, 0, 0)),
            pl.BlockSpec((RET_HEADS, RET_DK, RET_DV), lambda b, t: (0, 0, 0)),
            pl.BlockSpec((N_META, ATT_KV), lambda b, t: (0, 0)),
            pl.BlockSpec((ATT_KV, N_META), lambda b, t: (0, 0)),
            slab_out(MAIN_COLS), slab_out(D_MODEL), slab_out(D_MODEL), slab_out(D_MODEL),
        ),
        out_shape=(jax.ShapeDtypeStruct((B, SEQ, RKV_COLS), BF16),
                   jax.ShapeDtypeStruct((B, SEQ, ATT_KV), BF16),
                   jax.ShapeDtypeStruct((B, ATT_KV, SEQ), BF16),
                   jax.ShapeDtypeStruct((B, N_RET_CHUNKS, RET_HEADS, RET_DK, RET_DV), BF16),
                   jax.ShapeDtypeStruct((RET_HEADS, RET_DK, RET_DV), F32),
                   jax.ShapeDtypeStruct((N_META, ATT_KV), BF16),
                   jax.ShapeDtypeStruct((ATT_KV, N_META), BF16),
                   jax.ShapeDtypeStruct((D_MODEL, MAIN_COLS), BF16),
                   jax.ShapeDtypeStruct((D_MODEL, D_MODEL), BF16),
                   jax.ShapeDtypeStruct((D_MODEL, D_MODEL), BF16),
                   jax.ShapeDtypeStruct((D_MODEL, D_MODEL), BF16)),
        scratch_shapes=[state_scratch, pltpu.VMEM((ktm, ATT_HD), F32), pltpu.VMEM((ktm, ATT_HD), F32),
                        pltpu.VMEM((D_MODEL, KV_COLS), BF16)] + weight_stage,
        compiler_params=params,
        name="kv",
    )(x, meta_tokens.astype(F32), pre_w, w_in, inv, per_head, per_head, w_in, w_rb, w_ab, w_o)

    prev_chunk = lambda t: jnp.maximum(t * cpt - 1, 0)
    next_chunk = lambda t: jnp.minimum((t + 1) * cpt, N_CHUNKS - 1)
    out = pl.pallas_call(
        functools.partial(_main_kernel, cpt=cpt),
        grid=(B, nt),
        in_specs=[
            pl.BlockSpec((None, tm, D_MODEL), lambda b, t: (b, t, 0)),
            _resident((1, D_MODEL)),
            _resident((1, D_MODEL)),
            _resident((1, RET_V)),
            _resident((D_MODEL, MAIN_COLS)),
            _resident((RET_V, D_MODEL)),
            _resident((ATT_Q, D_MODEL)),
            _resident((D_MODEL, D_MODEL)),
            _resident((1, ATT_HD)),
            dec_f_spec,
            dec_b_spec,
            sink_spec,
            pl.BlockSpec((None, tm, RKV_COLS), lambda b, t: (b, t, 0)),
            pl.BlockSpec((None, CHUNK, ATT_KV), lambda b, t: (b, prev_chunk(t), 0)),
            pl.BlockSpec((None, tm, ATT_KV), lambda b, t: (b, t, 0)),
            pl.BlockSpec((None, CHUNK, ATT_KV), lambda b, t: (b, next_chunk(t), 0)),
            pl.BlockSpec((None, ATT_KV, CHUNK), lambda b, t: (b, 0, prev_chunk(t))),
            pl.BlockSpec((None, ATT_KV, tm), lambda b, t: (b, 0, t)),
            pl.BlockSpec((None, ATT_KV, CHUNK), lambda b, t: (b, 0, next_chunk(t))),
            pl.BlockSpec((None, tm // RET_CHUNK, RET_HEADS, RET_DK, RET_DV), lambda b, t: (b, t, 0, 0, 0)),
            _resident((RET_HEADS, RET_DK, RET_DV)),
            _resident((N_META, ATT_KV)),
            _resident((ATT_KV, N_META)),
        ],
        out_specs=pl.BlockSpec((None, tm, D_MODEL), lambda b, t: (b, t, 0)),
        out_shape=jax.ShapeDtypeStruct((B, SEQ, D_MODEL), x.dtype),
        scratch_shapes=(
            [state_scratch] + rope_scratch
            + [pltpu.VMEM((N_KEYS, Q_ROWS), F32),
               pltpu.VMEM((tm, D_MODEL), BF16),
               pltpu.VMEM((tm, RET_QK), F32),
               pltpu.VMEM((tm, ATT_Q), BF16),
               pltpu.VMEM((tm, RET_V), F32),
               pltpu.VMEM((tm, ATT_Q), F32)]
            + [pltpu.VMEM((tm, D_MODEL), F32)] * 4
            + [pltpu.VMEM((tm, D_MODEL), BF16)] * 3
            + [pltpu.VMEM((RET_HEADS, RET_CHUNK, RET_CHUNK), F32),
               pltpu.VMEM((RET_HEADS, RET_CHUNK, 2 * RET_DK), F32),
               pltpu.VMEM((RET_HEADS, RET_CHUNK, RET_DK), F32)]
        ),
        compiler_params=params,
        name="main",
    )(x, pre_w, post_w, ret_nw, w_main_b, w_rb_b, w_ab_b, w_o_b, inv, per_head, per_head, per_head,
      rkv, ak, ak, ak, avt, avt, avt, sb, kvf0, km, vmt)
    return out
```

```python
import functools

import jax
import jax.numpy as jnp
from jax import lax
from jax.experimental import pallas as pl
from jax.experimental.pallas import tpu as pltpu

D_MODEL = 1024
SEQ = 8192
N_META = 16
CHUNK = 128
RET_HEADS = 4
RET_DK = 128
RET_DV = 256
ATT_HEADS = 8
ATT_GROUPS = 2
ATT_REP = ATT_HEADS // ATT_GROUPS
ATT_HD = 128
ROPE_THETA = 10000.0
EPS = 1e-6
NEG_INF = -1e30
LOG2_E = 1.4426950408889634
RET_QK = RET_HEADS * RET_DK
RET_V = RET_HEADS * RET_DV
ATT_Q = ATT_HEADS * ATT_HD
ATT_KV = ATT_GROUPS * ATT_HD
D_IN = 2 * RET_QK + 2 * RET_V + 2 * ATT_Q + 2 * ATT_KV + 2 * D_MODEL
N_CHUNKS = SEQ // CHUNK
N_BAND = 3 * CHUNK
N_KEYS = N_BAND + N_META
Q_ROWS = ATT_REP * CHUNK

OFF_RQ = 0
OFF_RK = OFF_RQ + RET_QK
OFF_RV = OFF_RK + RET_QK
OFF_RG = OFF_RV + RET_V
OFF_AQ = OFF_RG + RET_V
OFF_AK = OFF_AQ + ATT_Q
OFF_AV = OFF_AK + ATT_KV
OFF_AG = OFF_AV + ATT_KV
OFF_GR = OFF_AG + ATT_Q
OFF_GA = OFF_GR + D_MODEL
RKV_COLS = RET_QK + RET_V

W_CHUNK = 512
KV_SRC = tuple(range(OFF_RK, OFF_RG, W_CHUNK)) + tuple(range(OFF_AK, OFF_AG, W_CHUNK))
KV_RK, KV_RV, KV_AKV = 0, RET_QK, RET_QK + RET_V
KV_COLS = len(KV_SRC) * W_CHUNK
MAIN_SEGMENTS = (("rq", OFF_RQ, RET_QK), ("aq", OFF_AQ, ATT_Q), ("rg", OFF_RG, RET_V), ("ag", OFF_AG, ATT_Q),
                 ("gr", OFF_GR, D_MODEL), ("ga", OFF_GA, D_MODEL))
MAIN_SRC = tuple(c for _, off, width in MAIN_SEGMENTS for c in range(off, off + width, W_CHUNK))
MAIN_OFF = {}
_o = 0
for _name, _, _width in MAIN_SEGMENTS:
    MAIN_OFF[_name] = _o
    _o += _width
MAIN_COLS = _o

CHUNKS_PER_TILE = 4
KV_CHUNKS_PER_TILE = 8
RET_CHUNK = 2 * CHUNK
N_RET_CHUNKS = SEQ // RET_CHUNK
COL_BLOCK = 256
SUBLANES = 8
V7X_VMEM_LIMIT_BYTES = 56 * 1024 * 1024

F32 = jnp.float32
BF16 = jnp.bfloat16


def _rms_norm(x, w):
    return x * lax.rsqrt(jnp.mean(x * x, axis=-1, keepdims=True) + EPS) * w


def _log_sigmoid(x):
    return jnp.minimum(x, 0.0) - jnp.log(1.0 + jnp.exp(-jnp.abs(x)))


def _sign_fold(sin):
    lane = lax.broadcasted_iota(jnp.int32, sin.shape, 1)
    return jnp.where(lane < ATT_HD // 2, -sin, sin)


def _rope_tables(rows, inv):
    ang = lax.broadcasted_iota(jnp.int32, (rows, ATT_HD), 0).astype(F32) * inv
    return jnp.cos(ang), _sign_fold(jnp.sin(ang))


def _fill_rope_base(inv, cosb_ref, sinb_ref):
    ang = lax.broadcasted_iota(jnp.int32, (CHUNK, ATT_HD), 0).astype(F32) * inv
    cb, sb = jnp.cos(ang), jnp.sin(ang)
    for r0 in range(0, cosb_ref.shape[0], CHUNK):
        base = float(r0) * inv
        ca, sa = jnp.cos(base), jnp.sin(base)
        cosb_ref[r0:r0 + CHUNK] = ca * cb - sa * sb
        sinb_ref[r0:r0 + CHUNK] = sa * cb + ca * sb


def _rope_tables_from_base(pos0, inv, cosb_ref, sinb_ref):
    base = pos0.astype(F32) * inv
    ca, sa = jnp.cos(base), jnp.sin(base)
    cb, sb = cosb_ref[...], sinb_ref[...]
    return ca * cb - sa * sb, _sign_fold(sa * cb + ca * sb)


def _sigmoid(x):
    return 0.5 * jnp.tanh(0.5 * x) + 0.5


def _silu(x):
    return x * _sigmoid(x)


def _rope(t, cos, sin_signed):
    return t * cos + pltpu.roll(t, ATT_HD // 2, axis=1) * sin_signed


def _dot(a, b):
    return jnp.dot(a, b, preferred_element_type=F32)


def _dot_nt(a, b):
    return lax.dot_general(a, b, (((1,), (1,)), ((), ())), preferred_element_type=F32)


def _dot_tn(a, b):
    return lax.dot_general(a, b, (((0,), (0,)), ((), ())), preferred_element_type=F32)


def _row_index(shape):
    return lax.broadcasted_iota(jnp.int32, shape, 0).astype(F32)


def _weight_chunk_copy(src_ref, col, stage_ref, sem_ref, slot):
    return pltpu.make_async_copy(src_ref.at[0, :, pl.ds(col, W_CHUNK)], stage_ref.at[slot], sem_ref.at[slot])


def _load_weights_bf16(jobs, stage_ref, sem_ref):
    copies = [_weight_chunk_copy(src, col, stage_ref, sem_ref, i % 2) for i, (src, col, _, _) in enumerate(jobs)]
    copies[0].start()
    for i, (_, _, dst, dcol) in enumerate(jobs):
        if i + 1 < len(jobs):
            copies[i + 1].start()
        copies[i].wait()
        dst[:, dcol:dcol + W_CHUNK] = stage_ref[i % 2].astype(BF16)


def _meta_block(meta_ref, prew_ref, wkv_ref, inv_ref, decf_ref, kvf0_ref, km_ref, vmt_ref):
    u = _rms_norm(meta_ref[...], prew_ref[...]).astype(BF16)
    zk = _dot(u, wkv_ref[:, KV_RK:KV_RV])
    zv = _dot(u, wkv_ref[:, KV_RV:KV_AKV])
    za = _dot(u, wkv_ref[:, KV_AKV:])
    cos, sin = _rope_tables(N_META, inv_ref[...])
    kdec_rows = (N_META - 1) - _row_index((N_META, RET_DK))
    for h in range(RET_HEADS):
        lg = _log_sigmoid(decf_ref[h:h + 1, :RET_DK])
        k = _rope(zk[:, h * RET_DK:(h + 1) * RET_DK], cos, sin) * (RET_DK ** -0.5)
        k = (k * jnp.exp(kdec_rows * lg)).astype(BF16)
        v = zv[:, h * RET_DV:(h + 1) * RET_DV].astype(BF16)
        kvf0_ref[h] = _dot_tn(k, v)
    for g in range(ATT_GROUPS):
        km_ref[:, g * ATT_HD:(g + 1) * ATT_HD] = _rope(za[:, g * ATT_HD:(g + 1) * ATT_HD], cos, sin).astype(BF16)
    vmt_ref[...] = za[:, ATT_KV:].T.astype(BF16)


def _kv_kernel(x_ref, meta_ref, prew_ref, win_ref, inv_ref, decf_ref, decb_ref, win_rows_ref, wrb_rows_ref,
               wab_rows_ref, wo_rows_ref, rkv_ref, ak_ref, avt_ref, sb_ref, kvf0_ref, km_ref, vmt_ref, wmain_out_ref,
               wrb_out_ref, wab_out_ref, wo_out_ref, state_ref, cosb_ref, sinb_ref, wkv_ref, stage_ref, sem_ref, *,
               cpt):
    t = pl.program_id(1)
    tile = pl.num_programs(1) - 1 - t
    tm = cpt * CHUNK

    @pl.when((pl.program_id(0) == 0) & (t == 0))
    def _():
        _load_weights_bf16([(win_ref, col, wkv_ref, i * W_CHUNK) for i, col in enumerate(KV_SRC)],
                           stage_ref, sem_ref)
        _fill_rope_base(inv_ref[...], cosb_ref, sinb_ref)
        _meta_block(meta_ref, prew_ref, wkv_ref, inv_ref, decf_ref, kvf0_ref, km_ref, vmt_ref)

    @pl.when(t == 0)
    def _():
        state_ref[...] = jnp.zeros_like(state_ref)

    u = _rms_norm(x_ref[...], prew_ref[...]).astype(BF16)
    for i, col in enumerate(MAIN_SRC):
        wmain_out_ref[:, i * W_CHUNK:(i + 1) * W_CHUNK] = win_rows_ref[:, col:col + W_CHUNK].astype(BF16)
    for src, dst in ((wrb_rows_ref, wrb_out_ref), (wab_rows_ref, wab_out_ref), (wo_rows_ref, wo_out_ref)):
        dst[...] = src[...].astype(BF16)
    za = _dot(u, wkv_ref[:, KV_AKV:])
    zk = _dot(u, wkv_ref[:, KV_RK:KV_RV])
    zv = _dot(u, wkv_ref[:, KV_RV:KV_AKV])
    cos, sin = _rope_tables_from_base(N_META + tile * tm, inv_ref[...], cosb_ref, sinb_ref)
    for g in range(ATT_GROUPS):
        ak_ref[:, g * ATT_HD:(g + 1) * ATT_HD] = _rope(za[:, g * ATT_HD:(g + 1) * ATT_HD], cos, sin).astype(BF16)
    avt_ref[...] = za[:, ATT_KV:].T.astype(BF16)
    rk = [_rope(zk[:, h * RET_DK:(h + 1) * RET_DK], cos, sin) * (RET_DK ** -0.5) for h in range(RET_HEADS)]
    for h in range(RET_HEADS):
        rkv_ref[:, h * RET_DK:(h + 1) * RET_DK] = rk[h].astype(BF16)
    rkv_ref[:, RET_QK:] = zv.astype(BF16)

    rows_k = _row_index((RET_CHUNK, RET_DK))
    for h in range(RET_HEADS):
        lg = _log_sigmoid(decb_ref[h:h + 1, :])
        kdec = jnp.exp(rows_k * lg[:, :RET_DK])
        cdec = jnp.exp(RET_CHUNK * lg)
        for rc in reversed(range(tm // RET_CHUNK)):
            r0 = rc * RET_CHUNK
            state = state_ref[h]
            sb_ref[rc, h] = state.astype(BF16)
            k = (rk[h][r0:r0 + RET_CHUNK] * kdec).astype(BF16)
            v = zv[r0:r0 + RET_CHUNK, h * RET_DV:(h + 1) * RET_DV].astype(BF16)
            state_ref[h] = cdec * state + _dot_tn(k, v)


def _fill_band_bias(bias_ref):
    kk = lax.broadcasted_iota(jnp.int32, bias_ref.shape, 0)
    qi = lax.broadcasted_iota(jnp.int32, bias_ref.shape, 1) & (CHUNK - 1)
    visible = (kk >= N_BAND) | ((kk >= qi) & (kk <= qi + 2 * CHUNK))
    bias_ref[...] = jnp.where(visible, 0.0, NEG_INF)


def _fill_decay_tables(decf_ref, decb_ref, dmat_ref, qdec_ref, kdf_ref):
    rows = _row_index((RET_CHUNK, RET_CHUNK))
    rel = rows - lax.broadcasted_iota(jnp.int32, (RET_CHUNK, RET_CHUNK), 1).astype(F32)
    rows_k = _row_index((RET_CHUNK, RET_DK))
    for h in range(RET_HEADS):
        lgf = _log_sigmoid(decf_ref[h:h + 1, :])
        lgb = _log_sigmoid(decb_ref[h:h + 1, :])
        dmat_ref[h] = jnp.where(rel >= 0, jnp.exp(jnp.maximum(rel, 0.0) * lgf),
                                jnp.exp(jnp.maximum(-rel, 0.0) * lgb))
        lgf_k, lgb_k = lgf[:, :RET_DK], lgb[:, :RET_DK]
        qdec_ref[h, :, :RET_DK] = jnp.exp((rows_k + 1.0) * lgf_k)
        qdec_ref[h, :, RET_DK:] = jnp.exp((RET_CHUNK - rows_k) * lgb_k)
        kdf_ref[h] = jnp.exp((RET_CHUNK - 1.0 - rows_k) * lgf_k)


def _main_kernel(x_ref, prew_ref, postw_ref, retnw_ref, wmain_ref, wrb_ref, wab_ref, wo_ref, inv_ref,
                 decf_ref, decb_ref, sink_ref, rkv_ref, akp_ref, akc_ref, akn_ref, avp_ref,
                 avc_ref, avn_ref, sb_ref, kvf0_ref, km_ref, vmt_ref, out_ref,
                 sf_ref, cosb_ref, sinb_ref, bias_ref, u_ref, rq_ref, aq_ref, or_ref, oa_ref, grg_ref, gag_ref,
                 ggr_ref, gga_ref, lhs_r_ref, lhs_a_ref, mix_ref, dmat_ref, qdec_ref, kdf_ref, *, cpt):
    t = pl.program_id(1)
    tm = cpt * CHUNK

    @pl.when((pl.program_id(0) == 0) & (t == 0))
    def _():
        _fill_rope_base(inv_ref[...], cosb_ref, sinb_ref)
        _fill_band_bias(bias_ref)
        _fill_decay_tables(decf_ref, decb_ref, dmat_ref, qdec_ref, kdf_ref)

    @pl.when(t == 0)
    def _():
        sf_ref[...] = kvf0_ref[...]

    for r0 in range(0, tm, CHUNK):
        u_ref[r0:r0 + CHUNK] = _rms_norm(x_ref[r0:r0 + CHUNK], prew_ref[...]).astype(BF16)
    cos, sin = _rope_tables_from_base(N_META + t * tm, inv_ref[...], cosb_ref, sinb_ref)

    def proj(name, c0):
        col = MAIN_OFF[name] + c0
        return _dot(u_ref[...], wmain_ref[:, col:col + COL_BLOCK])

    heads_per_block = COL_BLOCK // ATT_HD
    for c0 in range(0, RET_QK, COL_BLOCK):
        z = proj("rq", c0)
        for i in range(heads_per_block):
            c = c0 + i * RET_DK
            rq_ref[:, c:c + RET_DK] = _rope(z[:, i * RET_DK:(i + 1) * RET_DK], cos, sin)
    for c0 in range(0, ATT_Q, COL_BLOCK):
        z = proj("aq", c0)
        for i in range(heads_per_block):
            c = c0 + i * ATT_HD
            aq_ref[:, c:c + ATT_HD] = (_rope(z[:, i * ATT_HD:(i + 1) * ATT_HD], cos, sin)
                                       * (ATT_HD ** -0.5 * LOG2_E)).astype(BF16)

    gate_jobs = (("rg", _silu, grg_ref), ("ag", _silu, gag_ref), ("gr", _sigmoid, ggr_ref), ("ga", _sigmoid, gga_ref))
    gate_blocks = [(j, c) for j in range(len(gate_jobs)) for c in range(0, D_MODEL, COL_BLOCK)]
    assert len(gate_blocks) % cpt == 0

    def emit_gate_blocks(n):
        for _ in range(n):
            j, c = gate_blocks.pop(0)
            name, act, ref = gate_jobs[j]
            ref[:, c:c + COL_BLOCK] = act(proj(name, c))

    cdf = [jnp.exp(RET_CHUNK * _log_sigmoid(decf_ref[h:h + 1, :])) for h in range(RET_HEADS)]

    k_cat = jnp.concatenate([akp_ref[...], akc_ref[...], akn_ref[...]], axis=0)
    vt_cat = jnp.concatenate([avp_ref[...], avc_ref[...], avn_ref[...]], axis=1)
    sink_rows = [jnp.concatenate([sink_ref[g * ATT_REP + r:g * ATT_REP + r + 1, :] for r in range(ATT_REP)],
                                 axis=1) * LOG2_E for g in range(ATT_GROUPS)]

    def retention(rc):
        r0 = rc * RET_CHUNK
        rs = slice(r0, r0 + RET_CHUNK)
        for h in range(RET_HEADS):
            q = rq_ref[rs, h * RET_DK:(h + 1) * RET_DK]
            k = rkv_ref[rs, h * RET_DK:(h + 1) * RET_DK]
            v = rkv_ref[rs, RET_QK + h * RET_DV:RET_QK + (h + 1) * RET_DV]
            s = _dot_nt(q.astype(BF16), k) * dmat_ref[h]
            sf = sf_ref[h]
            q_cross = (jnp.concatenate([q, q], axis=1) * qdec_ref[h]).astype(BF16)
            s_cross = jnp.concatenate([sf.astype(BF16), sb_ref[rc, h]], axis=0)
            o = _dot(s.astype(BF16), v) + _dot(q_cross, s_cross)
            sf_ref[h] = cdf[h] * sf + _dot_tn((k.astype(F32) * kdf_ref[h]).astype(BF16), v)
            mu = jnp.mean(o, axis=-1, keepdims=True)
            d = o - mu
            var = jnp.mean(d * d, axis=-1, keepdims=True)
            or_ref[rs, h * RET_DV:(h + 1) * RET_DV] = d * lax.rsqrt(var + EPS)

    for rc in range(tm // RET_CHUNK):
        retention(rc)

    gates_per_pair = len(gate_blocks) // (cpt * ATT_GROUPS)
    pairs = [(lc, g) for lc in range(cpt) for g in range(ATT_GROUPS)]
    probs, denoms = {}, {}
    for lc, g in pairs:
        r0 = lc * CHUNK
        chunk = t * cpt + lc
        gs = slice(g * ATT_HD, (g + 1) * ATT_HD)
        qs = jnp.concatenate([aq_ref[r0:r0 + CHUNK, (g * ATT_REP + r) * ATT_HD:(g * ATT_REP + r + 1) * ATT_HD]
                              for r in range(ATT_REP)], axis=0)
        k_all = jnp.concatenate([k_cat[r0:r0 + N_BAND, gs], km_ref[:, gs]], axis=0)
        emit_gate_blocks(gates_per_pair)
        s = _dot_nt(k_all, qs) + bias_ref[...]
        parts = [s[0:CHUNK], s[CHUNK:2 * CHUNK], s[2 * CHUNK:N_BAND], s[N_BAND:]]
        if lc == 0:
            parts[0] = jnp.where(chunk > 0, parts[0], NEG_INF)
        if lc == cpt - 1:
            parts[2] = jnp.where(chunk < N_CHUNKS - 1, parts[2], NEG_INF)
        s = jnp.concatenate(parts, axis=0)
        sk = sink_rows[g]
        m = jnp.maximum(jnp.max(s, axis=0, keepdims=True), sk)
        p = jnp.exp2(s - m)
        denoms[lc, g] = jnp.sum(p, axis=0, keepdims=True) + jnp.exp2(sk - m)
        probs[lc, g] = p.astype(BF16)
    for lc, g in pairs:
        r0 = lc * CHUNK
        gs = slice(g * ATT_HD, (g + 1) * ATT_HD)
        vt_all = jnp.concatenate([vt_cat[gs, r0:r0 + N_BAND], vmt_ref[gs, :]], axis=1)
        ot = _dot(vt_all, probs[lc, g]) / denoms[lc, g]
        for r in range(ATT_REP):
            c = (g * ATT_REP + r) * ATT_HD
            oa_ref[r0:r0 + CHUNK, c:c + ATT_HD] = ot[:, r * CHUNK:(r + 1) * CHUNK].T

    for r0 in range(0, tm, CHUNK):
        rs = slice(r0, r0 + CHUNK)
        lhs_r_ref[rs] = (or_ref[rs] * retnw_ref[...] * grg_ref[rs]).astype(BF16)
        lhs_a_ref[rs] = (oa_ref[rs] * gag_ref[rs]).astype(BF16)
    for c0 in range(0, D_MODEL, COL_BLOCK):
        cs = slice(c0, c0 + COL_BLOCK)
        y_r = _dot(lhs_r_ref[...], wrb_ref[:, cs])
        y_a = _dot(lhs_a_ref[...], wab_ref[:, cs])
        mix_ref[:, cs] = (ggr_ref[:, cs] * y_r + gga_ref[:, cs] * y_a).astype(BF16)
    for c0 in range(0, D_MODEL, COL_BLOCK):
        cs = slice(c0, c0 + COL_BLOCK)
        out_ref[:, cs] = _dot(mix_ref[...], wo_ref[:, cs])
    for r0 in range(0, tm, CHUNK):
        rs = slice(r0, r0 + CHUNK)
        out_ref[rs] = x_ref[rs] + _rms_norm(out_ref[rs], postw_ref[...])


def _resident(shape, index=None):
    nd = len(shape)
    index = (0,) * nd if index is None else index
    return pl.BlockSpec(shape, lambda *_: index, pipeline_mode=pl.Buffered(1))


def kernel(x, meta_tokens, pre_norm_w, w_in, ret_decay_fwd, ret_decay_bwd, ret_norm_w, w_ret_branch, attn_sink,
           w_attn_branch, w_out, post_norm_w):
    B = x.shape[0]
    assert x.shape == (B, SEQ, D_MODEL) and pre_norm_w.shape[0] == 1 and w_in.shape == (1, D_MODEL, D_IN)
    cpt = CHUNKS_PER_TILE
    tm = cpt * CHUNK
    nt = N_CHUNKS // cpt

    w_in = w_in.astype(F32)
    w_rb, w_ab, w_o = (w.astype(F32) for w in (w_ret_branch, w_attn_branch, w_out))
    pre_w = pre_norm_w.astype(F32)
    post_w = post_norm_w.astype(F32)
    ret_nw = ret_norm_w.astype(F32)
    half = ATT_HD // 2
    inv = ROPE_THETA ** (-jnp.arange(half, dtype=F32) * 2.0 / ATT_HD)
    inv = jnp.concatenate([inv, inv])[None, :]
    assert ATT_HEADS == SUBLANES and RET_HEADS <= SUBLANES
    pad = jnp.zeros((SUBLANES - RET_HEADS,), F32)
    per_head = jnp.concatenate([ret_decay_fwd[0].astype(F32), pad, ret_decay_bwd[0].astype(F32), pad,
                                attn_sink[0].astype(F32)])
    per_head = jnp.broadcast_to(per_head[:, None], (2 * SUBLANES + ATT_HEADS, RET_DV))
    dec_f_spec = _resident((SUBLANES, RET_DV), (0, 0))
    dec_b_spec = _resident((SUBLANES, RET_DV), (1, 0))
    sink_spec = _resident((ATT_HEADS, ATT_HD), (2, 0))

    params = pltpu.CompilerParams(dimension_semantics=("arbitrary", "arbitrary"),
                                  vmem_limit_bytes=V7X_VMEM_LIMIT_BYTES)
    rope_scratch = [pltpu.VMEM((tm, ATT_HD), F32), pltpu.VMEM((tm, ATT_HD), F32)]
    state_scratch = pltpu.VMEM((RET_HEADS, RET_DK, RET_DV), F32)
    weight_stage = [pltpu.VMEM((2, D_MODEL, W_CHUNK), F32), pltpu.SemaphoreType.DMA((2,))]
    hbm = pl.BlockSpec(memory_space=pl.ANY)

    kcpt = KV_CHUNKS_PER_TILE
    ktm, knt = kcpt * CHUNK, N_CHUNKS // kcpt
    slab = D_MODEL // (B * knt)
    assert slab * B * knt == D_MODEL and slab % (2 * SUBLANES) == 0

    def slab_in(width):
        return pl.BlockSpec((None, slab, width), lambda b, t: (0, b * knt + t, 0))

    def slab_out(width):
        return pl.BlockSpec((slab, width), lambda b, t: (b * knt + t, 0))

    rkv, ak, avt, sb, kvf0, km, vmt, w_main_b, w_rb_b, w_ab_b, w_o_b = pl.pallas_call(
        functools.partial(_kv_kernel, cpt=kcpt),
        grid=(B, knt),
        in_specs=[pl.BlockSpec((None, ktm, D_MODEL), lambda b, t: (b, knt - 1 - t, 0)), _resident((N_META, D_MODEL)),
                  _resident((1, D_MODEL)), hbm, _resident((1, ATT_HD)), dec_f_spec, dec_b_spec,
                  slab_in(D_IN), slab_in(D_MODEL), slab_in(D_MODEL), slab_in(D_MODEL)],
        out_specs=(
            pl.BlockSpec((None, ktm, RKV_COLS), lambda b, t: (b, knt - 1 - t, 0)),
            pl.BlockSpec((None, ktm, ATT_KV), lambda b, t: (b, knt - 1 - t, 0)),
            pl.BlockSpec((None, ATT_KV, ktm), lambda b, t: (b, 0, knt - 1 - t)),
            pl.BlockSpec((None, ktm // RET_CHUNK, RET_HEADS, RET_DK, RET_DV), lambda b, t: (b, knt - 1 - t, 0, 0, 0)),
            pl.BlockSpec((RET_HEADS, RET_DK, RET_DV), lambda b, t: (0, 0, 0)),
            pl.BlockSpec((N_META, ATT_KV), lambda b, t: (0, 0)),
            pl.BlockSpec((ATT_KV, N_META), lambda b, t: (0, 0)),
            slab_out(MAIN_COLS), slab_out(D_MODEL), slab_out(D_MODEL), slab_out(D_MODEL),
        ),
        out_shape=(jax.ShapeDtypeStruct((B, SEQ, RKV_COLS), BF16),
                   jax.ShapeDtypeStruct((B, SEQ, ATT_KV), BF16),
                   jax.ShapeDtypeStruct((B, ATT_KV, SEQ), BF16),
                   jax.ShapeDtypeStruct((B, N_RET_CHUNKS, RET_HEADS, RET_DK, RET_DV), BF16),
                   jax.ShapeDtypeStruct((RET_HEADS, RET_DK, RET_DV), F32),
                   jax.ShapeDtypeStruct((N_META, ATT_KV), BF16),
                   jax.ShapeDtypeStruct((ATT_KV, N_META), BF16),
                   jax.ShapeDtypeStruct((D_MODEL, MAIN_COLS), BF16),
                   jax.ShapeDtypeStruct((D_MODEL, D_MODEL), BF16),
                   jax.ShapeDtypeStruct((D_MODEL, D_MODEL), BF16),
                   jax.ShapeDtypeStruct((D_MODEL, D_MODEL), BF16)),
        scratch_shapes=[state_scratch, pltpu.VMEM((ktm, ATT_HD), F32), pltpu.VMEM((ktm, ATT_HD), F32),
                        pltpu.VMEM((D_MODEL, KV_COLS), BF16)] + weight_stage,
        compiler_params=params,
        name="kv",
    )(x, meta_tokens.astype(F32), pre_w, w_in, inv, per_head, per_head, w_in, w_rb, w_ab, w_o)

    prev_chunk = lambda t: jnp.maximum(t * cpt - 1, 0)
    next_chunk = lambda t: jnp.minimum((t + 1) * cpt, N_CHUNKS - 1)
    out = pl.pallas_call(
        functools.partial(_main_kernel, cpt=cpt),
        grid=(B, nt),
        in_specs=[
            pl.BlockSpec((None, tm, D_MODEL), lambda b, t: (b, t, 0)),
            _resident((1, D_MODEL)),
            _resident((1, D_MODEL)),
            _resident((1, RET_V)),
            _resident((D_MODEL, MAIN_COLS)),
            _resident((RET_V, D_MODEL)),
            _resident((ATT_Q, D_MODEL)),
            _resident((D_MODEL, D_MODEL)),
            _resident((1, ATT_HD)),
            dec_f_spec,
            dec_b_spec,
            sink_spec,
            pl.BlockSpec((None, tm, RKV_COLS), lambda b, t: (b, t, 0)),
            pl.BlockSpec((None, CHUNK, ATT_KV), lambda b, t: (b, prev_chunk(t), 0)),
            pl.BlockSpec((None, tm, ATT_KV), lambda b, t: (b, t, 0)),
            pl.BlockSpec((None, CHUNK, ATT_KV), lambda b, t: (b, next_chunk(t), 0)),
            pl.BlockSpec((None, ATT_KV, CHUNK), lambda b, t: (b, 0, prev_chunk(t))),
            pl.BlockSpec((None, ATT_KV, tm), lambda b, t: (b, 0, t)),
            pl.BlockSpec((None, ATT_KV, CHUNK), lambda b, t: (b, 0, next_chunk(t))),
            pl.BlockSpec((None, tm // RET_CHUNK, RET_HEADS, RET_DK, RET_DV), lambda b, t: (b, t, 0, 0, 0)),
            _resident((RET_HEADS, RET_DK, RET_DV)),
            _resident((N_META, ATT_KV)),
            _resident((ATT_KV, N_META)),
        ],
        out_specs=pl.BlockSpec((None, tm, D_MODEL), lambda b, t: (b, t, 0)),
        out_shape=jax.ShapeDtypeStruct((B, SEQ, D_MODEL), x.dtype),
        scratch_shapes=(
            [state_scratch] + rope_scratch
            + [pltpu.VMEM((N_KEYS, Q_ROWS), F32),
               pltpu.VMEM((tm, D_MODEL), BF16),
               pltpu.VMEM((tm, RET_QK), F32),
               pltpu.VMEM((tm, ATT_Q), BF16),
               pltpu.VMEM((tm, RET_V), F32),
               pltpu.VMEM((tm, ATT_Q), F32)]
            + [pltpu.VMEM((tm, D_MODEL), F32)] * 4
            + [pltpu.VMEM((tm, D_MODEL), BF16)] * 3
            + [pltpu.VMEM((RET_HEADS, RET_CHUNK, RET_CHUNK), F32),
               pltpu.VMEM((RET_HEADS, RET_CHUNK, 2 * RET_DK), F32),
               pltpu.VMEM((RET_HEADS, RET_CHUNK, RET_DK), F32)]
        ),
        compiler_params=params,
        name="main",
    )(x, pre_w, post_w, ret_nw, w_main_b, w_rb_b, w_ab_b, w_o_b, inv, per_head, per_head, per_head,
      rkv, ak, ak, ak, avt, avt, avt, sb, kvf0, km, vmt)
    return out
```

```python
import functools

import jax
import jax.numpy as jnp
from jax import lax
from jax.experimental import pallas as pl
from jax.experimental.pallas import tpu as pltpu

D_MODEL = 1024
SEQ = 8192
N_META = 16
CHUNK = 128
RET_HEADS = 4
RET_DK = 128
RET_DV = 256
ATT_HEADS = 8
ATT_GROUPS = 2
ATT_REP = ATT_HEADS // ATT_GROUPS
ATT_HD = 128
ROPE_THETA = 10000.0
EPS = 1e-6
NEG_INF = -1e30
LOG2_E = 1.4426950408889634
RET_QK = RET_HEADS * RET_DK
RET_V = RET_HEADS * RET_DV
ATT_Q = ATT_HEADS * ATT_HD
ATT_KV = ATT_GROUPS * ATT_HD
D_IN = 2 * RET_QK + 2 * RET_V + 2 * ATT_Q + 2 * ATT_KV + 2 * D_MODEL
N_CHUNKS = SEQ // CHUNK
N_BAND = 3 * CHUNK
N_KEYS = N_BAND + N_META
Q_ROWS = ATT_REP * CHUNK

OFF_RQ = 0
OFF_RK = OFF_RQ + RET_QK
OFF_RV = OFF_RK + RET_QK
OFF_RG = OFF_RV + RET_V
OFF_AQ = OFF_RG + RET_V
OFF_AK = OFF_AQ + ATT_Q
OFF_AV = OFF_AK + ATT_KV
OFF_AG = OFF_AV + ATT_KV
OFF_GR = OFF_AG + ATT_Q
OFF_GA = OFF_GR + D_MODEL
RKV_COLS = RET_QK + RET_V

W_CHUNK = 512
KV_SRC = tuple(range(OFF_RK, OFF_RG, W_CHUNK)) + tuple(range(OFF_AK, OFF_AG, W_CHUNK))
KV_RK, KV_RV, KV_AKV = 0, RET_QK, RET_QK + RET_V
KV_COLS = len(KV_SRC) * W_CHUNK
MAIN_SEGMENTS = (("rq", OFF_RQ, RET_QK), ("aq", OFF_AQ, ATT_Q), ("rg", OFF_RG, RET_V), ("ag", OFF_AG, ATT_Q),
                 ("gr", OFF_GR, D_MODEL), ("ga", OFF_GA, D_MODEL))
MAIN_SRC = tuple(c for _, off, width in MAIN_SEGMENTS for c in range(off, off + width, W_CHUNK))
MAIN_OFF = {}
_o = 0
for _name, _, _width in MAIN_SEGMENTS:
    MAIN_OFF[_name] = _o
    _o += _width
MAIN_COLS = _o

CHUNKS_PER_TILE = 4
KV_CHUNKS_PER_TILE = 8
RET_CHUNK = 2 * CHUNK
N_RET_CHUNKS = SEQ // RET_CHUNK
COL_BLOCK = 256
SUBLANES = 8
V7X_VMEM_LIMIT_BYTES = 56 * 1024 * 1024

F32 = jnp.float32
BF16 = jnp.bfloat16


def _rms_norm(x, w):
    return x * lax.rsqrt(jnp.mean(x * x, axis=-1, keepdims=True) + EPS) * w


def _log_sigmoid(x):
    return jnp.minimum(x, 0.0) - jnp.log(1.0 + jnp.exp(-jnp.abs(x)))


def _sign_fold(sin):
    lane = lax.broadcasted_iota(jnp.int32, sin.shape, 1)
    return jnp.where(lane < ATT_HD // 2, -sin, sin)


def _rope_tables(rows, inv):
    ang = lax.broadcasted_iota(jnp.int32, (rows, ATT_HD), 0).astype(F32) * inv
    return jnp.cos(ang), _sign_fold(jnp.sin(ang))


def _fill_rope_base(inv, cosb_ref, sinb_ref):
    ang = lax.broadcasted_iota(jnp.int32, (CHUNK, ATT_HD), 0).astype(F32) * inv
    cb, sb = jnp.cos(ang), jnp.sin(ang)
    for r0 in range(0, cosb_ref.shape[0], CHUNK):
        base = float(r0) * inv
        ca, sa = jnp.cos(base), jnp.sin(base)
        cosb_ref[r0:r0 + CHUNK] = ca * cb - sa * sb
        sinb_ref[r0:r0 + CHUNK] = sa * cb + ca * sb


def _rope_tables_from_base(pos0, inv, cosb_ref, sinb_ref):
    base = pos0.astype(F32) * inv
    ca, sa = jnp.cos(base), jnp.sin(base)
    cb, sb = cosb_ref[...], sinb_ref[...]
    return ca * cb - sa * sb, _sign_fold(sa * cb + ca * sb)


def _sigmoid(x):
    return 0.5 * jnp.tanh(0.5 * x) + 0.5


def _silu(x):
    return x * _sigmoid(x)


def _rope(t, cos, sin_signed):
    return t * cos + pltpu.roll(t, ATT_HD // 2, axis=1) * sin_signed


def _dot(a, b):
    return jnp.dot(a, b, preferred_element_type=F32)


def _dot_nt(a, b):
    return lax.dot_general(a, b, (((1,), (1,)), ((), ())), preferred_element_type=F32)


def _dot_tn(a, b):
    return lax.dot_general(a, b, (((0,), (0,)), ((), ())), preferred_element_type=F32)


def _row_index(shape):
    return lax.broadcasted_iota(jnp.int32, shape, 0).astype(F32)


def _weight_chunk_copy(src_ref, col, stage_ref, sem_ref, slot):
    return pltpu.make_async_copy(src_ref.at[0, :, pl.ds(col, W_CHUNK)], stage_ref.at[slot], sem_ref.at[slot])


def _load_weights_bf16(jobs, stage_ref, sem_ref):
    copies = [_weight_chunk_copy(src, col, stage_ref, sem_ref, i % 2) for i, (src, col, _, _) in enumerate(jobs)]
    copies[0].start()
    for i, (_, _, dst, dcol) in enumerate(jobs):
        if i + 1 < len(jobs):
            copies[i + 1].start()
        copies[i].wait()
        dst[:, dcol:dcol + W_CHUNK] = stage_ref[i % 2].astype(BF16)


def _meta_block(meta_ref, prew_ref, wkv_ref, inv_ref, decf_ref, kvf0_ref, km_ref, vmt_ref):
    u = _rms_norm(meta_ref[...], prew_ref[...]).astype(BF16)
    zk = _dot(u, wkv_ref[:, KV_RK:KV_RV])
    zv = _dot(u, wkv_ref[:, KV_RV:KV_AKV])
    za = _dot(u, wkv_ref[:, KV_AKV:])
    cos, sin = _rope_tables(N_META, inv_ref[...])
    kdec_rows = (N_META - 1) - _row_index((N_META, RET_DK))
    for h in range(RET_HEADS):
        lg = _log_sigmoid(decf_ref[h:h + 1, :RET_DK])
        k = _rope(zk[:, h * RET_DK:(h + 1) * RET_DK], cos, sin) * (RET_DK ** -0.5)
        k = (k * jnp.exp(kdec_rows * lg)).astype(BF16)
        v = zv[:, h * RET_DV:(h + 1) * RET_DV].astype(BF16)
        kvf0_ref[h] = _dot_tn(k, v)
    for g in range(ATT_GROUPS):
        km_ref[:, g * ATT_HD:(g + 1) * ATT_HD] = _rope(za[:, g * ATT_HD:(g + 1) * ATT_HD], cos, sin).astype(BF16)
    vmt_ref[...] = za[:, ATT_KV:].T.astype(BF16)


def _kv_kernel(x_ref, meta_ref, prew_ref, win_ref, inv_ref, decf_ref, decb_ref, win_rows_ref, wrb_rows_ref,
               wab_rows_ref, wo_rows_ref, rkv_ref, ak_ref, avt_ref, sb_ref, kvf0_ref, km_ref, vmt_ref, wmain_out_ref,
               wrb_out_ref, wab_out_ref, wo_out_ref, u_out_ref, state_ref, cosb_ref, sinb_ref, wkv_ref, stage_ref,
               sem_ref, *,
               cpt):
    t = pl.program_id(1)
    tile = pl.num_programs(1) - 1 - t
    tm = cpt * CHUNK

    for i, col in enumerate(MAIN_SRC):
        wmain_out_ref[:, i * W_CHUNK:(i + 1) * W_CHUNK] = win_rows_ref[:, col:col + W_CHUNK].astype(BF16)
    for src, dst in ((wrb_rows_ref, wrb_out_ref), (wab_rows_ref, wab_out_ref), (wo_rows_ref, wo_out_ref)):
        dst[...] = src[...].astype(BF16)

    @pl.when((pl.program_id(0) == 0) & (t == 0))
    def _():
        _load_weights_bf16([(win_ref, col, wkv_ref, i * W_CHUNK) for i, col in enumerate(KV_SRC)],
                           stage_ref, sem_ref)
        _fill_rope_base(inv_ref[...], cosb_ref, sinb_ref)
        _meta_block(meta_ref, prew_ref, wkv_ref, inv_ref, decf_ref, kvf0_ref, km_ref, vmt_ref)

    @pl.when(t == 0)
    def _():
        state_ref[...] = jnp.zeros_like(state_ref)

    u = _rms_norm(x_ref[...], prew_ref[...]).astype(BF16)
    u_out_ref[...] = u
    za = _dot(u, wkv_ref[:, KV_AKV:])
    zk = _dot(u, wkv_ref[:, KV_RK:KV_RV])
    zv = _dot(u, wkv_ref[:, KV_RV:KV_AKV])
    cos, sin = _rope_tables_from_base(N_META + tile * tm, inv_ref[...], cosb_ref, sinb_ref)
    for g in range(ATT_GROUPS):
        ak_ref[:, g * ATT_HD:(g + 1) * ATT_HD] = _rope(za[:, g * ATT_HD:(g + 1) * ATT_HD], cos, sin).astype(BF16)
    avt_ref[...] = za[:, ATT_KV:].T.astype(BF16)
    rk = [_rope(zk[:, h * RET_DK:(h + 1) * RET_DK], cos, sin) * (RET_DK ** -0.5) for h in range(RET_HEADS)]
    for h in range(RET_HEADS):
        rkv_ref[:, h * RET_DK:(h + 1) * RET_DK] = rk[h].astype(BF16)
    rkv_ref[:, RET_QK:] = zv.astype(BF16)

    rows_k = _row_index((RET_CHUNK, RET_DK))
    for h in range(RET_HEADS):
        lg = _log_sigmoid(decb_ref[h:h + 1, :])
        kdec = jnp.exp(rows_k * lg[:, :RET_DK])
        cdec = jnp.exp(RET_CHUNK * lg)
        for rc in reversed(range(tm // RET_CHUNK)):
            r0 = rc * RET_CHUNK
            state = state_ref[h]
            sb_ref[rc, h] = state.astype(BF16)
            k = (rk[h][r0:r0 + RET_CHUNK] * kdec).astype(BF16)
            v = zv[r0:r0 + RET_CHUNK, h * RET_DV:(h + 1) * RET_DV].astype(BF16)
            state_ref[h] = cdec * state + _dot_tn(k, v)


def _fill_band_bias(bias_ref):
    kk = lax.broadcasted_iota(jnp.int32, bias_ref.shape, 0)
    qi = lax.broadcasted_iota(jnp.int32, bias_ref.shape, 1) & (CHUNK - 1)
    visible = (kk >= N_BAND) | ((kk >= qi) & (kk <= qi + 2 * CHUNK))
    bias_ref[...] = jnp.where(visible, 0.0, NEG_INF)


def _fill_decay_tables(decf_ref, decb_ref, dmat_ref, qdec_ref, kdf_ref):
    rows = _row_index((RET_CHUNK, RET_CHUNK))
    rel = rows - lax.broadcasted_iota(jnp.int32, (RET_CHUNK, RET_CHUNK), 1).astype(F32)
    rows_k = _row_index((RET_CHUNK, RET_DK))
    for h in range(RET_HEADS):
        lgf = _log_sigmoid(decf_ref[h:h + 1, :])
        lgb = _log_sigmoid(decb_ref[h:h + 1, :])
        dmat_ref[h] = jnp.where(rel >= 0, jnp.exp(jnp.maximum(rel, 0.0) * lgf),
                                jnp.exp(jnp.maximum(-rel, 0.0) * lgb))
        lgf_k, lgb_k = lgf[:, :RET_DK], lgb[:, :RET_DK]
        qdec_ref[h, :, :RET_DK] = jnp.exp((rows_k + 1.0) * lgf_k)
        qdec_ref[h, :, RET_DK:] = jnp.exp((RET_CHUNK - rows_k) * lgb_k)
        kdf_ref[h] = jnp.exp((RET_CHUNK - 1.0 - rows_k) * lgf_k)


def _main_kernel(x_ref, u_ref, postw_ref, retnw_ref, wmain_ref, wrb_ref, wab_ref, wo_ref, inv_ref,
                 decf_ref, decb_ref, sink_ref, rkv_ref, akp_ref, akc_ref, akn_ref, avp_ref,
                 avc_ref, avn_ref, sb_ref, kvf0_ref, km_ref, vmt_ref, out_ref,
                 sf_ref, cosb_ref, sinb_ref, bias_ref, rq_ref, aq_ref, or_ref, oa_ref, grg_ref, gag_ref,
                 ggr_ref, gga_ref, lhs_r_ref, lhs_a_ref, mix_ref, dmat_ref, qdec_ref, kdf_ref, *, cpt):
    t = pl.program_id(1)
    tm = cpt * CHUNK

    @pl.when((pl.program_id(0) == 0) & (t == 0))
    def _():
        _fill_rope_base(inv_ref[...], cosb_ref, sinb_ref)
        _fill_band_bias(bias_ref)
        _fill_decay_tables(decf_ref, decb_ref, dmat_ref, qdec_ref, kdf_ref)

    @pl.when(t == 0)
    def _():
        sf_ref[...] = kvf0_ref[...]

    cos, sin = _rope_tables_from_base(N_META + t * tm, inv_ref[...], cosb_ref, sinb_ref)

    def proj(name, c0):
        col = MAIN_OFF[name] + c0
        return _dot(u_ref[...], wmain_ref[:, col:col + COL_BLOCK])

    heads_per_block = COL_BLOCK // ATT_HD
    for c0 in range(0, RET_QK, COL_BLOCK):
        z = proj("rq", c0)
        for i in range(heads_per_block):
            c = c0 + i * RET_DK
            rq_ref[:, c:c + RET_DK] = _rope(z[:, i * RET_DK:(i + 1) * RET_DK], cos, sin)
    for c0 in range(0, ATT_Q, COL_BLOCK):
        z = proj("aq", c0)
        for i in range(heads_per_block):
            c = c0 + i * ATT_HD
            aq_ref[:, c:c + ATT_HD] = (_rope(z[:, i * ATT_HD:(i + 1) * ATT_HD], cos, sin)
                                       * (ATT_HD ** -0.5 * LOG2_E)).astype(BF16)

    gate_jobs = (("rg", _silu, grg_ref), ("ag", _silu, gag_ref), ("gr", _sigmoid, ggr_ref), ("ga", _sigmoid, gga_ref))
    gate_blocks = [(j, c) for j in range(len(gate_jobs)) for c in range(0, D_MODEL, COL_BLOCK)]
    assert len(gate_blocks) % cpt == 0

    def emit_gate_blocks(n):
        for _ in range(n):
            j, c = gate_blocks.pop(0)
            name, act, ref = gate_jobs[j]
            ref[:, c:c + COL_BLOCK] = act(proj(name, c))

    cdf = [jnp.exp(RET_CHUNK * _log_sigmoid(decf_ref[h:h + 1, :])) for h in range(RET_HEADS)]

    k_cat = jnp.concatenate([akp_ref[...], akc_ref[...], akn_ref[...]], axis=0)
    vt_cat = jnp.concatenate([avp_ref[...], avc_ref[...], avn_ref[...]], axis=1)
    sink_rows = [jnp.concatenate([sink_ref[g * ATT_REP + r:g * ATT_REP + r + 1, :] for r in range(ATT_REP)],
                                 axis=1) * LOG2_E for g in range(ATT_GROUPS)]

    def retention(rc):
        r0 = rc * RET_CHUNK
        rs = slice(r0, r0 + RET_CHUNK)
        for h in range(RET_HEADS):
            q = rq_ref[rs, h * RET_DK:(h + 1) * RET_DK]
            k = rkv_ref[rs, h * RET_DK:(h + 1) * RET_DK]
            v = rkv_ref[rs, RET_QK + h * RET_DV:RET_QK + (h + 1) * RET_DV]
            s = _dot_nt(q.astype(BF16), k) * dmat_ref[h]
            sf = sf_ref[h]
            q_cross = (jnp.concatenate([q, q], axis=1) * qdec_ref[h]).astype(BF16)
            s_cross = jnp.concatenate([sf.astype(BF16), sb_ref[rc, h]], axis=0)
            o = _dot(s.astype(BF16), v) + _dot(q_cross, s_cross)
            sf_ref[h] = cdf[h] * sf + _dot_tn((k.astype(F32) * kdf_ref[h]).astype(BF16), v)
            mu = jnp.mean(o, axis=-1, keepdims=True)
            d = o - mu
            var = jnp.mean(d * d, axis=-1, keepdims=True)
            or_ref[rs, h * RET_DV:(h + 1) * RET_DV] = d * lax.rsqrt(var + EPS)

    for rc in range(tm // RET_CHUNK):
        retention(rc)

    gates_per_pair = len(gate_blocks) // (cpt * ATT_GROUPS)
    pairs = [(lc, g) for lc in range(cpt) for g in range(ATT_GROUPS)]
    probs, denoms = {}, {}
    for lc, g in pairs:
        r0 = lc * CHUNK
        chunk = t * cpt + lc
        gs = slice(g * ATT_HD, (g + 1) * ATT_HD)
        qs = jnp.concatenate([aq_ref[r0:r0 + CHUNK, (g * ATT_REP + r) * ATT_HD:(g * ATT_REP + r + 1) * ATT_HD]
                              for r in range(ATT_REP)], axis=0)
        k_all = jnp.concatenate([k_cat[r0:r0 + N_BAND, gs], km_ref[:, gs]], axis=0)
        emit_gate_blocks(gates_per_pair)
        s = _dot_nt(k_all, qs) + bias_ref[...]
        parts = [s[0:CHUNK], s[CHUNK:2 * CHUNK], s[2 * CHUNK:N_BAND], s[N_BAND:]]
        if lc == 0:
            parts[0] = jnp.where(chunk > 0, parts[0], NEG_INF)
        if lc == cpt - 1:
            parts[2] = jnp.where(chunk < N_CHUNKS - 1, parts[2], NEG_INF)
        s = jnp.concatenate(parts, axis=0)
        sk = sink_rows[g]
        m = jnp.maximum(jnp.max(s, axis=0, keepdims=True), sk)
        p = jnp.exp2(s - m)
        denoms[lc, g] = jnp.sum(p, axis=0, keepdims=True) + jnp.exp2(sk - m)
        probs[lc, g] = p.astype(BF16)
    for lc, g in pairs:
        r0 = lc * CHUNK
        gs = slice(g * ATT_HD, (g + 1) * ATT_HD)
        vt_all = jnp.concatenate([vt_cat[gs, r0:r0 + N_BAND], vmt_ref[gs, :]], axis=1)
        ot = _dot(vt_all, probs[lc, g]) / denoms[lc, g]
        for r in range(ATT_REP):
            c = (g * ATT_REP + r) * ATT_HD
            oa_ref[r0:r0 + CHUNK, c:c + ATT_HD] = ot[:, r * CHUNK:(r + 1) * CHUNK].T

    for r0 in range(0, tm, CHUNK):
        rs = slice(r0, r0 + CHUNK)
        lhs_r_ref[rs] = (or_ref[rs] * retnw_ref[...] * grg_ref[rs]).astype(BF16)
        lhs_a_ref[rs] = (oa_ref[rs] * gag_ref[rs]).astype(BF16)
    for c0 in range(0, D_MODEL, COL_BLOCK):
        cs = slice(c0, c0 + COL_BLOCK)
        y_r = _dot(lhs_r_ref[...], wrb_ref[:, cs])
        y_a = _dot(lhs_a_ref[...], wab_ref[:, cs])
        mix_ref[:, cs] = (ggr_ref[:, cs] * y_r + gga_ref[:, cs] * y_a).astype(BF16)
    for c0 in range(0, D_MODEL, COL_BLOCK):
        cs = slice(c0, c0 + COL_BLOCK)
        out_ref[:, cs] = _dot(mix_ref[...], wo_ref[:, cs])
    for r0 in range(0, tm, CHUNK):
        rs = slice(r0, r0 + CHUNK)
        out_ref[rs] = x_ref[rs] + _rms_norm(out_ref[rs], postw_ref[...])


def _resident(shape, index=None):
    nd = len(shape)
    index = (0,) * nd if index is None else index
    return pl.BlockSpec(shape, lambda *_: index, pipeline_mode=pl.Buffered(1))


def kernel(x, meta_tokens, pre_norm_w, w_in, ret_decay_fwd, ret_decay_bwd, ret_norm_w, w_ret_branch, attn_sink,
           w_attn_branch, w_out, post_norm_w):
    B = x.shape[0]
    assert x.shape == (B, SEQ, D_MODEL) and pre_norm_w.shape[0] == 1 and w_in.shape == (1, D_MODEL, D_IN)
    cpt = CHUNKS_PER_TILE
    tm = cpt * CHUNK
    nt = N_CHUNKS // cpt

    w_in = w_in.astype(F32)
    w_rb, w_ab, w_o = (w.astype(F32) for w in (w_ret_branch, w_attn_branch, w_out))
    pre_w = pre_norm_w.astype(F32)
    post_w = post_norm_w.astype(F32)
    ret_nw = ret_norm_w.astype(F32)
    half = ATT_HD // 2
    inv = ROPE_THETA ** (-jnp.arange(half, dtype=F32) * 2.0 / ATT_HD)
    inv = jnp.concatenate([inv, inv])[None, :]
    assert ATT_HEADS == SUBLANES and RET_HEADS <= SUBLANES
    pad = jnp.zeros((SUBLANES - RET_HEADS,), F32)
    per_head = jnp.concatenate([ret_decay_fwd[0].astype(F32), pad, ret_decay_bwd[0].astype(F32), pad,
                                attn_sink[0].astype(F32)])
    per_head = jnp.broadcast_to(per_head[:, None], (2 * SUBLANES + ATT_HEADS, RET_DV))
    dec_f_spec = _resident((SUBLANES, RET_DV), (0, 0))
    dec_b_spec = _resident((SUBLANES, RET_DV), (1, 0))
    sink_spec = _resident((ATT_HEADS, ATT_HD), (2, 0))

    params = pltpu.CompilerParams(dimension_semantics=("arbitrary", "arbitrary"),
                                  vmem_limit_bytes=V7X_VMEM_LIMIT_BYTES)
    rope_scratch = [pltpu.VMEM((tm, ATT_HD), F32), pltpu.VMEM((tm, ATT_HD), F32)]
    state_scratch = pltpu.VMEM((RET_HEADS, RET_DK, RET_DV), F32)
    weight_stage = [pltpu.VMEM((2, D_MODEL, W_CHUNK), F32), pltpu.SemaphoreType.DMA((2,))]
    hbm = pl.BlockSpec(memory_space=pl.ANY)

    kcpt = KV_CHUNKS_PER_TILE
    ktm, knt = kcpt * CHUNK, N_CHUNKS // kcpt
    slab = D_MODEL // (B * knt)
    assert slab * B * knt == D_MODEL and slab % (2 * SUBLANES) == 0

    def slab_in(width):
        return pl.BlockSpec((None, slab, width), lambda b, t: (0, b * knt + t, 0))

    def slab_out(width):
        return pl.BlockSpec((slab, width), lambda b, t: (b * knt + t, 0))

    rkv, ak, avt, sb, kvf0, km, vmt, w_main_b, w_rb_b, w_ab_b, w_o_b, u_norm = pl.pallas_call(
        functools.partial(_kv_kernel, cpt=kcpt),
        grid=(B, knt),
        in_specs=[pl.BlockSpec((None, ktm, D_MODEL), lambda b, t: (b, knt - 1 - t, 0)), _resident((N_META, D_MODEL)),
                  _resident((1, D_MODEL)), hbm, _resident((1, ATT_HD)), dec_f_spec, dec_b_spec,
                  slab_in(D_IN), slab_in(D_MODEL), slab_in(D_MODEL), slab_in(D_MODEL)],
        out_specs=(
            pl.BlockSpec((None, ktm, RKV_COLS), lambda b, t: (b, knt - 1 - t, 0)),
            pl.BlockSpec((None, ktm, ATT_KV), lambda b, t: (b, knt - 1 - t, 0)),
            pl.BlockSpec((None, ATT_KV, ktm), lambda b, t: (b, 0, knt - 1 - t)),
            pl.BlockSpec((None, ktm // RET_CHUNK, RET_HEADS, RET_DK, RET_DV), lambda b, t: (b, knt - 1 - t, 0, 0, 0)),
            pl.BlockSpec((RET_HEADS, RET_DK, RET_DV), lambda b, t: (0, 0, 0)),
            pl.BlockSpec((N_META, ATT_KV), lambda b, t: (0, 0)),
            pl.BlockSpec((ATT_KV, N_META), lambda b, t: (0, 0)),
            slab_out(MAIN_COLS), slab_out(D_MODEL), slab_out(D_MODEL), slab_out(D_MODEL),
            pl.BlockSpec((None, ktm, D_MODEL), lambda b, t: (b, knt - 1 - t, 0)),
        ),
        out_shape=(jax.ShapeDtypeStruct((B, SEQ, RKV_COLS), BF16),
                   jax.ShapeDtypeStruct((B, SEQ, ATT_KV), BF16),
                   jax.ShapeDtypeStruct((B, ATT_KV, SEQ), BF16),
                   jax.ShapeDtypeStruct((B, N_RET_CHUNKS, RET_HEADS, RET_DK, RET_DV), BF16),
                   jax.ShapeDtypeStruct((RET_HEADS, RET_DK, RET_DV), F32),
                   jax.ShapeDtypeStruct((N_META, ATT_KV), BF16),
                   jax.ShapeDtypeStruct((ATT_KV, N_META), BF16),
                   jax.ShapeDtypeStruct((D_MODEL, MAIN_COLS), BF16),
                   jax.ShapeDtypeStruct((D_MODEL, D_MODEL), BF16),
                   jax.ShapeDtypeStruct((D_MODEL, D_MODEL), BF16),
                   jax.ShapeDtypeStruct((D_MODEL, D_MODEL), BF16),
                   jax.ShapeDtypeStruct((B, SEQ, D_MODEL), BF16)),
        scratch_shapes=[state_scratch, pltpu.VMEM((ktm, ATT_HD), F32), pltpu.VMEM((ktm, ATT_HD), F32),
                        pltpu.VMEM((D_MODEL, KV_COLS), BF16)] + weight_stage,
        compiler_params=params,
        name="kv",
    )(x, meta_tokens.astype(F32), pre_w, w_in, inv, per_head, per_head, w_in, w_rb, w_ab, w_o)

    prev_chunk = lambda t: jnp.maximum(t * cpt - 1, 0)
    next_chunk = lambda t: jnp.minimum((t + 1) * cpt, N_CHUNKS - 1)
    out = pl.pallas_call(
        functools.partial(_main_kernel, cpt=cpt),
        grid=(B, nt),
        in_specs=[
            pl.BlockSpec((None, tm, D_MODEL), lambda b, t: (b, t, 0)),
            pl.BlockSpec((None, tm, D_MODEL), lambda b, t: (b, t, 0)),
            _resident((1, D_MODEL)),
            _resident((1, RET_V)),
            _resident((D_MODEL, MAIN_COLS)),
            _resident((RET_V, D_MODEL)),
            _resident((ATT_Q, D_MODEL)),
            _resident((D_MODEL, D_MODEL)),
            _resident((1, ATT_HD)),
            dec_f_spec,
            dec_b_spec,
            sink_spec,
            pl.BlockSpec((None, tm, RKV_COLS), lambda b, t: (b, t, 0)),
            pl.BlockSpec((None, CHUNK, ATT_KV), lambda b, t: (b, prev_chunk(t), 0)),
            pl.BlockSpec((None, tm, ATT_KV), lambda b, t: (b, t, 0)),
            pl.BlockSpec((None, CHUNK, ATT_KV), lambda b, t: (b, next_chunk(t), 0)),
            pl.BlockSpec((None, ATT_KV, CHUNK), lambda b, t: (b, 0, prev_chunk(t))),
            pl.BlockSpec((None, ATT_KV, tm), lambda b, t: (b, 0, t)),
            pl.BlockSpec((None, ATT_KV, CHUNK), lambda b, t: (b, 0, next_chunk(t))),
            pl.BlockSpec((None, tm // RET_CHUNK, RET_HEADS, RET_DK, RET_DV), lambda b, t: (b, t, 0, 0, 0)),
            _resident((RET_HEADS, RET_DK, RET_DV)),
            _resident((N_META, ATT_KV)),
            _resident((ATT_KV, N_META)),
        ],
        out_specs=pl.BlockSpec((None, tm, D_MODEL), lambda b, t: (b, t, 0)),
        out_shape=jax.ShapeDtypeStruct((B, SEQ, D_MODEL), x.dtype),
        scratch_shapes=(
            [state_scratch] + rope_scratch
            + [pltpu.VMEM((N_KEYS, Q_ROWS), F32),
               pltpu.VMEM((tm, RET_QK), F32),
               pltpu.VMEM((tm, ATT_Q), BF16),
               pltpu.VMEM((tm, RET_V), F32),
               pltpu.VMEM((tm, ATT_Q), F32)]
            + [pltpu.VMEM((tm, D_MODEL), F32)] * 4
            + [pltpu.VMEM((tm, D_MODEL), BF16)] * 3
            + [pltpu.VMEM((RET_HEADS, RET_CHUNK, RET_CHUNK), F32),
               pltpu.VMEM((RET_HEADS, RET_CHUNK, 2 * RET_DK), F32),
               pltpu.VMEM((RET_HEADS, RET_CHUNK, RET_DK), F32)]
        ),
        compiler_params=params,
        name="main",
    )(x, u_norm, post_w, ret_nw, w_main_b, w_rb_b, w_ab_b, w_o_b, inv, per_head, per_head, per_head,
      rkv, ak, ak, ak, avt, avt, avt, sb, kvf0, km, vmt)
    return out
```

```python
import functools

import jax
import jax.numpy as jnp
from jax import lax
from jax.experimental import pallas as pl
from jax.experimental.pallas import tpu as pltpu

D_MODEL = 1024
SEQ = 8192
N_META = 16
CHUNK = 128
RET_HEADS = 4
RET_DK = 128
RET_DV = 256
ATT_HEADS = 8
ATT_GROUPS = 2
ATT_REP = ATT_HEADS // ATT_GROUPS
ATT_HD = 128
ROPE_THETA = 10000.0
EPS = 1e-6
NEG_INF = -1e30
LOG2_E = 1.4426950408889634
RET_QK = RET_HEADS * RET_DK
RET_V = RET_HEADS * RET_DV
ATT_Q = ATT_HEADS * ATT_HD
ATT_KV = ATT_GROUPS * ATT_HD
D_IN = 2 * RET_QK + 2 * RET_V + 2 * ATT_Q + 2 * ATT_KV + 2 * D_MODEL
N_CHUNKS = SEQ // CHUNK
N_BAND = 3 * CHUNK
N_KEYS = N_BAND + N_META
Q_ROWS = ATT_REP * CHUNK

OFF_RQ = 0
OFF_RK = OFF_RQ + RET_QK
OFF_RV = OFF_RK + RET_QK
OFF_RG = OFF_RV + RET_V
OFF_AQ = OFF_RG + RET_V
OFF_AK = OFF_AQ + ATT_Q
OFF_AV = OFF_AK + ATT_KV
OFF_AG = OFF_AV + ATT_KV
OFF_GR = OFF_AG + ATT_Q
OFF_GA = OFF_GR + D_MODEL
RKV_COLS = RET_QK + RET_V

W_CHUNK = 512
KV_SRC = tuple(range(OFF_RK, OFF_RG, W_CHUNK)) + tuple(range(OFF_AK, OFF_AG, W_CHUNK))
KV_RK, KV_RV, KV_AKV = 0, RET_QK, RET_QK + RET_V
KV_COLS = len(KV_SRC) * W_CHUNK
MAIN_SEGMENTS = (("rq", OFF_RQ, RET_QK), ("aq", OFF_AQ, ATT_Q), ("rg", OFF_RG, RET_V), ("ag", OFF_AG, ATT_Q),
                 ("gr", OFF_GR, D_MODEL), ("ga", OFF_GA, D_MODEL))
MAIN_SRC = tuple(c for _, off, width in MAIN_SEGMENTS for c in range(off, off + width, W_CHUNK))
MAIN_OFF = {}
_o = 0
for _name, _, _width in MAIN_SEGMENTS:
    MAIN_OFF[_name] = _o
    _o += _width
MAIN_COLS = _o

CHUNKS_PER_TILE = 4
KV_CHUNKS_PER_TILE = 8
RET_CHUNK = 2 * CHUNK
N_RET_CHUNKS = SEQ // RET_CHUNK
COL_BLOCK = 256
X_SLOTS = 3
SUBLANES = 8
V7X_VMEM_LIMIT_BYTES = 56 * 1024 * 1024

F32 = jnp.float32
BF16 = jnp.bfloat16


def _rms_norm(x, w):
    return x * lax.rsqrt(jnp.mean(x * x, axis=-1, keepdims=True) + EPS) * w


def _log_sigmoid(x):
    return jnp.minimum(x, 0.0) - jnp.log(1.0 + jnp.exp(-jnp.abs(x)))


def _sign_fold(sin):
    lane = lax.broadcasted_iota(jnp.int32, sin.shape, 1)
    return jnp.where(lane < ATT_HD // 2, -sin, sin)


def _rope_tables(rows, inv):
    ang = lax.broadcasted_iota(jnp.int32, (rows, ATT_HD), 0).astype(F32) * inv
    return jnp.cos(ang), _sign_fold(jnp.sin(ang))


def _fill_rope_base(inv, cosb_ref, sinb_ref):
    ang = lax.broadcasted_iota(jnp.int32, (CHUNK, ATT_HD), 0).astype(F32) * inv
    cb, sb = jnp.cos(ang), jnp.sin(ang)
    for r0 in range(0, cosb_ref.shape[0], CHUNK):
        base = float(r0) * inv
        ca, sa = jnp.cos(base), jnp.sin(base)
        cosb_ref[r0:r0 + CHUNK] = ca * cb - sa * sb
        sinb_ref[r0:r0 + CHUNK] = sa * cb + ca * sb


def _rope_tables_from_base(pos0, inv, cosb_ref, sinb_ref):
    base = pos0.astype(F32) * inv
    ca, sa = jnp.cos(base), jnp.sin(base)
    cb, sb = cosb_ref[...], sinb_ref[...]
    return ca * cb - sa * sb, _sign_fold(sa * cb + ca * sb)


def _sigmoid(x):
    return 0.5 * jnp.tanh(0.5 * x) + 0.5


def _silu(x):
    return x * _sigmoid(x)


def _rope(t, cos, sin_signed):
    return t * cos + pltpu.roll(t, ATT_HD // 2, axis=1) * sin_signed


def _dot(a, b):
    return jnp.dot(a, b, preferred_element_type=F32)


def _dot_nt(a, b):
    return lax.dot_general(a, b, (((1,), (1,)), ((), ())), preferred_element_type=F32)


def _dot_tn(a, b):
    return lax.dot_general(a, b, (((0,), (0,)), ((), ())), preferred_element_type=F32)


def _row_index(shape):
    return lax.broadcasted_iota(jnp.int32, shape, 0).astype(F32)


def _weight_chunk_copy(src_ref, col, stage_ref, sem_ref, slot):
    return pltpu.make_async_copy(src_ref.at[0, :, pl.ds(col, W_CHUNK)], stage_ref.at[slot], sem_ref.at[slot])


def _load_weights_bf16(jobs, stage_ref, sem_ref):
    copies = [_weight_chunk_copy(src, col, stage_ref, sem_ref, i % 2) for i, (src, col, _, _) in enumerate(jobs)]
    copies[0].start()
    for i, (_, _, dst, dcol) in enumerate(jobs):
        if i + 1 < len(jobs):
            copies[i + 1].start()
        copies[i].wait()
        dst[:, dcol:dcol + W_CHUNK] = stage_ref[i % 2].astype(BF16)


def _meta_block(meta_ref, prew_ref, wkv_ref, inv_ref, decf_ref, kvf0_ref, km_ref, vmt_ref):
    u = _rms_norm(meta_ref[...], prew_ref[...]).astype(BF16)
    zk = _dot(u, wkv_ref[:, KV_RK:KV_RV])
    zv = _dot(u, wkv_ref[:, KV_RV:KV_AKV])
    za = _dot(u, wkv_ref[:, KV_AKV:])
    cos, sin = _rope_tables(N_META, inv_ref[...])
    kdec_rows = (N_META - 1) - _row_index((N_META, RET_DK))
    for h in range(RET_HEADS):
        lg = _log_sigmoid(decf_ref[h:h + 1, :RET_DK])
        k = _rope(zk[:, h * RET_DK:(h + 1) * RET_DK], cos, sin) * (RET_DK ** -0.5)
        k = (k * jnp.exp(kdec_rows * lg)).astype(BF16)
        v = zv[:, h * RET_DV:(h + 1) * RET_DV].astype(BF16)
        kvf0_ref[h] = _dot_tn(k, v)
    for g in range(ATT_GROUPS):
        km_ref[:, g * ATT_HD:(g + 1) * ATT_HD] = _rope(za[:, g * ATT_HD:(g + 1) * ATT_HD], cos, sin).astype(BF16)
    vmt_ref[...] = za[:, ATT_KV:].T.astype(BF16)


def _kv_kernel(x_hbm_ref, meta_ref, prew_ref, win_ref, inv_ref, decf_ref, decb_ref, win_rows_ref, wrb_rows_ref,
               wab_rows_ref, wo_rows_ref, rkv_ref, ak_ref, avt_ref, sb_ref, kvf0_ref, km_ref, vmt_ref, wmain_out_ref,
               wrb_out_ref, wab_out_ref, wo_out_ref, state_ref, cosb_ref, sinb_ref, wkv_ref, stage_ref, sem_ref,
               xbuf_ref, xsem_ref, *, cpt):
    t = pl.program_id(1)
    knt = pl.num_programs(1)
    tile = knt - 1 - t
    tm = cpt * CHUNK

    step = pl.program_id(0) * knt + t
    n_steps = pl.num_programs(0) * knt

    def x_copy(s):
        slot = s % X_SLOTS
        rows = pl.ds((knt - 1 - s % knt) * tm, tm)
        return pltpu.make_async_copy(x_hbm_ref.at[s // knt, rows, :], xbuf_ref.at[slot], xsem_ref.at[slot])

    @pl.when(step == 0)
    def _():
        for s in range(X_SLOTS - 1):
            x_copy(step + s).start()

    @pl.when(step + X_SLOTS - 1 < n_steps)
    def _():
        x_copy(step + X_SLOTS - 1).start()

    for i, col in enumerate(MAIN_SRC):
        wmain_out_ref[:, i * W_CHUNK:(i + 1) * W_CHUNK] = win_rows_ref[:, col:col + W_CHUNK].astype(BF16)
    for src, dst in ((wrb_rows_ref, wrb_out_ref), (wab_rows_ref, wab_out_ref), (wo_rows_ref, wo_out_ref)):
        dst[...] = src[...].astype(BF16)

    @pl.when((pl.program_id(0) == 0) & (t == 0))
    def _():
        _load_weights_bf16([(win_ref, col, wkv_ref, i * W_CHUNK) for i, col in enumerate(KV_SRC)],
                           stage_ref, sem_ref)
        _fill_rope_base(inv_ref[...], cosb_ref, sinb_ref)
        _meta_block(meta_ref, prew_ref, wkv_ref, inv_ref, decf_ref, kvf0_ref, km_ref, vmt_ref)

    @pl.when(t == 0)
    def _():
        state_ref[...] = jnp.zeros_like(state_ref)

    x_copy(step).wait()
    u = _rms_norm(xbuf_ref[step % X_SLOTS], prew_ref[...]).astype(BF16)
    za = _dot(u, wkv_ref[:, KV_AKV:])
    zk = _dot(u, wkv_ref[:, KV_RK:KV_RV])
    zv = _dot(u, wkv_ref[:, KV_RV:KV_AKV])
    cos, sin = _rope_tables_from_base(N_META + tile * tm, inv_ref[...], cosb_ref, sinb_ref)
    for g in range(ATT_GROUPS):
        ak_ref[:, g * ATT_HD:(g + 1) * ATT_HD] = _rope(za[:, g * ATT_HD:(g + 1) * ATT_HD], cos, sin).astype(BF16)
    avt_ref[...] = za[:, ATT_KV:].T.astype(BF16)
    rk = [_rope(zk[:, h * RET_DK:(h + 1) * RET_DK], cos, sin) * (RET_DK ** -0.5) for h in range(RET_HEADS)]
    for h in range(RET_HEADS):
        rkv_ref[:, h * RET_DK:(h + 1) * RET_DK] = rk[h].astype(BF16)
    rkv_ref[:, RET_QK:] = zv.astype(BF16)

    rows_k = _row_index((RET_CHUNK, RET_DK))
    for h in range(RET_HEADS):
        lg = _log_sigmoid(decb_ref[h:h + 1, :])
        kdec = jnp.exp(rows_k * lg[:, :RET_DK])
        cdec = jnp.exp(RET_CHUNK * lg)
        for rc in reversed(range(tm // RET_CHUNK)):
            r0 = rc * RET_CHUNK
            state = state_ref[h]
            sb_ref[rc, h] = state.astype(BF16)
            k = (rk[h][r0:r0 + RET_CHUNK] * kdec).astype(BF16)
            v = zv[r0:r0 + RET_CHUNK, h * RET_DV:(h + 1) * RET_DV].astype(BF16)
            state_ref[h] = cdec * state + _dot_tn(k, v)


def _fill_band_bias(bias_ref):
    kk = lax.broadcasted_iota(jnp.int32, bias_ref.shape, 0)
    qi = lax.broadcasted_iota(jnp.int32, bias_ref.shape, 1) & (CHUNK - 1)
    visible = (kk >= N_BAND) | ((kk >= qi) & (kk <= qi + 2 * CHUNK))
    bias_ref[...] = jnp.where(visible, 0.0, NEG_INF)


def _fill_decay_tables(decf_ref, decb_ref, dmat_ref, qdec_ref, kdf_ref):
    rows = _row_index((RET_CHUNK, RET_CHUNK))
    rel = rows - lax.broadcasted_iota(jnp.int32, (RET_CHUNK, RET_CHUNK), 1).astype(F32)
    rows_k = _row_index((RET_CHUNK, RET_DK))
    for h in range(RET_HEADS):
        lgf = _log_sigmoid(decf_ref[h:h + 1, :])
        lgb = _log_sigmoid(decb_ref[h:h + 1, :])
        dmat_ref[h] = jnp.where(rel >= 0, jnp.exp(jnp.maximum(rel, 0.0) * lgf),
                                jnp.exp(jnp.maximum(-rel, 0.0) * lgb))
        lgf_k, lgb_k = lgf[:, :RET_DK], lgb[:, :RET_DK]
        qdec_ref[h, :, :RET_DK] = jnp.exp((rows_k + 1.0) * lgf_k)
        qdec_ref[h, :, RET_DK:] = jnp.exp((RET_CHUNK - rows_k) * lgb_k)
        kdf_ref[h] = jnp.exp((RET_CHUNK - 1.0 - rows_k) * lgf_k)


def _main_kernel(x_ref, prew_ref, postw_ref, retnw_ref, wmain_ref, wrb_ref, wab_ref, wo_ref, inv_ref,
                 decf_ref, decb_ref, sink_ref, rkv_ref, akp_ref, akc_ref, akn_ref, avp_ref,
                 avc_ref, avn_ref, sb_ref, kvf0_ref, km_ref, vmt_ref, out_ref,
                 sf_ref, cosb_ref, sinb_ref, bias_ref, u_ref, rq_ref, aq_ref, or_ref, oa_ref, grg_ref, gag_ref,
                 ggr_ref, gga_ref, lhs_r_ref, lhs_a_ref, mix_ref, dmat_ref, qdec_ref, kdf_ref, *, cpt):
    t = pl.program_id(1)
    tm = cpt * CHUNK

    @pl.when((pl.program_id(0) == 0) & (t == 0))
    def _():
        _fill_rope_base(inv_ref[...], cosb_ref, sinb_ref)
        _fill_band_bias(bias_ref)
        _fill_decay_tables(decf_ref, decb_ref, dmat_ref, qdec_ref, kdf_ref)

    @pl.when(t == 0)
    def _():
        sf_ref[...] = kvf0_ref[...]

    for r0 in range(0, tm, CHUNK):
        u_ref[r0:r0 + CHUNK] = _rms_norm(x_ref[r0:r0 + CHUNK], prew_ref[...]).astype(BF16)
    cos, sin = _rope_tables_from_base(N_META + t * tm, inv_ref[...], cosb_ref, sinb_ref)

    def proj(name, c0):
        col = MAIN_OFF[name] + c0
        return _dot(u_ref[...], wmain_ref[:, col:col + COL_BLOCK])

    heads_per_block = COL_BLOCK // ATT_HD
    for c0 in range(0, RET_QK, COL_BLOCK):
        z = proj("rq", c0)
        for i in range(heads_per_block):
            c = c0 + i * RET_DK
            rq_ref[:, c:c + RET_DK] = _rope(z[:, i * RET_DK:(i + 1) * RET_DK], cos, sin)
    for c0 in range(0, ATT_Q, COL_BLOCK):
        z = proj("aq", c0)
        for i in range(heads_per_block):
            c = c0 + i * ATT_HD
            aq_ref[:, c:c + ATT_HD] = (_rope(z[:, i * ATT_HD:(i + 1) * ATT_HD], cos, sin)
                                       * (ATT_HD ** -0.5 * LOG2_E)).astype(BF16)

    gate_jobs = (("rg", _silu, grg_ref), ("ag", _silu, gag_ref), ("gr", _sigmoid, ggr_ref), ("ga", _sigmoid, gga_ref))
    gate_blocks = [(j, c) for j in range(len(gate_jobs)) for c in range(0, D_MODEL, COL_BLOCK)]
    assert len(gate_blocks) % cpt == 0

    def emit_gate_blocks(n):
        for _ in range(n):
            j, c = gate_blocks.pop(0)
            name, act, ref = gate_jobs[j]
            ref[:, c:c + COL_BLOCK] = act(proj(name, c))

    cdf = [jnp.exp(RET_CHUNK * _log_sigmoid(decf_ref[h:h + 1, :])) for h in range(RET_HEADS)]

    k_cat = jnp.concatenate([akp_ref[...], akc_ref[...], akn_ref[...]], axis=0)
    vt_cat = jnp.concatenate([avp_ref[...], avc_ref[...], avn_ref[...]], axis=1)
    sink_rows = [jnp.concatenate([sink_ref[g * ATT_REP + r:g * ATT_REP + r + 1, :] for r in range(ATT_REP)],
                                 axis=1) * LOG2_E for g in range(ATT_GROUPS)]

    def retention(rc):
        r0 = rc * RET_CHUNK
        rs = slice(r0, r0 + RET_CHUNK)
        for h in range(RET_HEADS):
            q = rq_ref[rs, h * RET_DK:(h + 1) * RET_DK]
            k = rkv_ref[rs, h * RET_DK:(h + 1) * RET_DK]
            v = rkv_ref[rs, RET_QK + h * RET_DV:RET_QK + (h + 1) * RET_DV]
            s = _dot_nt(q.astype(BF16), k) * dmat_ref[h]
            sf = sf_ref[h]
            q_cross = (jnp.concatenate([q, q], axis=1) * qdec_ref[h]).astype(BF16)
            s_cross = jnp.concatenate([sf.astype(BF16), sb_ref[rc, h]], axis=0)
            o = _dot(s.astype(BF16), v) + _dot(q_cross, s_cross)
            sf_ref[h] = cdf[h] * sf + _dot_tn((k.astype(F32) * kdf_ref[h]).astype(BF16), v)
            mu = jnp.mean(o, axis=-1, keepdims=True)
            d = o - mu
            var = jnp.mean(d * d, axis=-1, keepdims=True)
            or_ref[rs, h * RET_DV:(h + 1) * RET_DV] = d * lax.rsqrt(var + EPS)

    for rc in range(tm // RET_CHUNK):
        retention(rc)

    gates_per_pair = len(gate_blocks) // (cpt * ATT_GROUPS)
    pairs = [(lc, g) for lc in range(cpt) for g in range(ATT_GROUPS)]
    probs, denoms = {}, {}
    for lc, g in pairs:
        r0 = lc * CHUNK
        chunk = t * cpt + lc
        gs = slice(g * ATT_HD, (g + 1) * ATT_HD)
        qs = jnp.concatenate([aq_ref[r0:r0 + CHUNK, (g * ATT_REP + r) * ATT_HD:(g * ATT_REP + r + 1) * ATT_HD]
                              for r in range(ATT_REP)], axis=0)
        k_all = jnp.concatenate([k_cat[r0:r0 + N_BAND, gs], km_ref[:, gs]], axis=0)
        emit_gate_blocks(gates_per_pair)
        s = _dot_nt(k_all, qs) + bias_ref[...]
        parts = [s[0:CHUNK], s[CHUNK:2 * CHUNK], s[2 * CHUNK:N_BAND], s[N_BAND:]]
        if lc == 0:
            parts[0] = jnp.where(chunk > 0, parts[0], NEG_INF)
        if lc == cpt - 1:
            parts[2] = jnp.where(chunk < N_CHUNKS - 1, parts[2], NEG_INF)
        s = jnp.concatenate(parts, axis=0)
        sk = sink_rows[g]
        m = jnp.maximum(jnp.max(s, axis=0, keepdims=True), sk)
        p = jnp.exp2(s - m)
        denoms[lc, g] = jnp.sum(p, axis=0, keepdims=True) + jnp.exp2(sk - m)
        probs[lc, g] = p.astype(BF16)
    for lc, g in pairs:
        r0 = lc * CHUNK
        gs = slice(g * ATT_HD, (g + 1) * ATT_HD)
        vt_all = jnp.concatenate([vt_cat[gs, r0:r0 + N_BAND], vmt_ref[gs, :]], axis=1)
        ot = _dot(vt_all, probs[lc, g]) / denoms[lc, g]
        for r in range(ATT_REP):
            c = (g * ATT_REP + r) * ATT_HD
            oa_ref[r0:r0 + CHUNK, c:c + ATT_HD] = ot[:, r * CHUNK:(r + 1) * CHUNK].T

    for r0 in range(0, tm, CHUNK):
        rs = slice(r0, r0 + CHUNK)
        lhs_r_ref[rs] = (or_ref[rs] * retnw_ref[...] * grg_ref[rs]).astype(BF16)
        lhs_a_ref[rs] = (oa_ref[rs] * gag_ref[rs]).astype(BF16)
    for c0 in range(0, D_MODEL, COL_BLOCK):
        cs = slice(c0, c0 + COL_BLOCK)
        y_r = _dot(lhs_r_ref[...], wrb_ref[:, cs])
        y_a = _dot(lhs_a_ref[...], wab_ref[:, cs])
        mix_ref[:, cs] = (ggr_ref[:, cs] * y_r + gga_ref[:, cs] * y_a).astype(BF16)
    for c0 in range(0, D_MODEL, COL_BLOCK):
        cs = slice(c0, c0 + COL_BLOCK)
        out_ref[:, cs] = _dot(mix_ref[...], wo_ref[:, cs])
    for r0 in range(0, tm, CHUNK):
        rs = slice(r0, r0 + CHUNK)
        out_ref[rs] = x_ref[rs] + _rms_norm(out_ref[rs], postw_ref[...])


def _resident(shape, index=None):
    nd = len(shape)
    index = (0,) * nd if index is None else index
    return pl.BlockSpec(shape, lambda *_: index, pipeline_mode=pl.Buffered(1))


def kernel(x, meta_tokens, pre_norm_w, w_in, ret_decay_fwd, ret_decay_bwd, ret_norm_w, w_ret_branch, attn_sink,
           w_attn_branch, w_out, post_norm_w):
    B = x.shape[0]
    assert x.shape == (B, SEQ, D_MODEL) and pre_norm_w.shape[0] == 1 and w_in.shape == (1, D_MODEL, D_IN)
    cpt = CHUNKS_PER_TILE
    tm = cpt * CHUNK
    nt = N_CHUNKS // cpt

    w_in = w_in.astype(F32)
    w_rb, w_ab, w_o = (w.astype(F32) for w in (w_ret_branch, w_attn_branch, w_out))
    pre_w = pre_norm_w.astype(F32)
    post_w = post_norm_w.astype(F32)
    ret_nw = ret_norm_w.astype(F32)
    half = ATT_HD // 2
    inv = ROPE_THETA ** (-jnp.arange(half, dtype=F32) * 2.0 / ATT_HD)
    inv = jnp.concatenate([inv, inv])[None, :]
    assert ATT_HEADS == SUBLANES and RET_HEADS <= SUBLANES
    pad = jnp.zeros((SUBLANES - RET_HEADS,), F32)
    per_head = jnp.concatenate([ret_decay_fwd[0].astype(F32), pad, ret_decay_bwd[0].astype(F32), pad,
                                attn_sink[0].astype(F32)])
    per_head = jnp.broadcast_to(per_head[:, None], (2 * SUBLANES + ATT_HEADS, RET_DV))
    dec_f_spec = _resident((SUBLANES, RET_DV), (0, 0))
    dec_b_spec = _resident((SUBLANES, RET_DV), (1, 0))
    sink_spec = _resident((ATT_HEADS, ATT_HD), (2, 0))

    params = pltpu.CompilerParams(dimension_semantics=("arbitrary", "arbitrary"),
                                  vmem_limit_bytes=V7X_VMEM_LIMIT_BYTES)
    rope_scratch = [pltpu.VMEM((tm, ATT_HD), F32), pltpu.VMEM((tm, ATT_HD), F32)]
    state_scratch = pltpu.VMEM((RET_HEADS, RET_DK, RET_DV), F32)
    weight_stage = [pltpu.VMEM((2, D_MODEL, W_CHUNK), F32), pltpu.SemaphoreType.DMA((2,))]
    hbm = pl.BlockSpec(memory_space=pl.ANY)

    kcpt = KV_CHUNKS_PER_TILE
    ktm, knt = kcpt * CHUNK, N_CHUNKS // kcpt
    slab = D_MODEL // (B * knt)
    assert slab * B * knt == D_MODEL and slab % (2 * SUBLANES) == 0 and B * knt >= X_SLOTS

    def slab_in(width):
        return pl.BlockSpec((None, slab, width), lambda b, t: (0, b * knt + t, 0))

    def slab_out(width):
        return pl.BlockSpec((slab, width), lambda b, t: (b * knt + t, 0))

    rkv, ak, avt, sb, kvf0, km, vmt, w_main_b, w_rb_b, w_ab_b, w_o_b = pl.pallas_call(
        functools.partial(_kv_kernel, cpt=kcpt),
        grid=(B, knt),
        in_specs=[hbm, _resident((N_META, D_MODEL)),
                  _resident((1, D_MODEL)), hbm, _resident((1, ATT_HD)), dec_f_spec, dec_b_spec,
                  slab_in(D_IN), slab_in(D_MODEL), slab_in(D_MODEL), slab_in(D_MODEL)],
        out_specs=(
            pl.BlockSpec((None, ktm, RKV_COLS), lambda b, t: (b, knt - 1 - t, 0)),
            pl.BlockSpec((None, ktm, ATT_KV), lambda b, t: (b, knt - 1 - t, 0)),
            pl.BlockSpec((None, ATT_KV, ktm), lambda b, t: (b, 0, knt - 1 - t)),
            pl.BlockSpec((None, ktm // RET_CHUNK, RET_HEADS, RET_DK, RET_DV), lambda b, t: (b, knt - 1 - t, 0, 0, 0)),
            pl.BlockSpec((RET_HEADS, RET_DK, RET_DV), lambda b, t: (0, 0, 0)),
            pl.BlockSpec((N_META, ATT_KV), lambda b, t: (0, 0)),
            pl.BlockSpec((ATT_KV, N_META), lambda b, t: (0, 0)),
            slab_out(MAIN_COLS), slab_out(D_MODEL), slab_out(D_MODEL), slab_out(D_MODEL),
        ),
        out_shape=(jax.ShapeDtypeStruct((B, SEQ, RKV_COLS), BF16),
                   jax.ShapeDtypeStruct((B, SEQ, ATT_KV), BF16),
                   jax.ShapeDtypeStruct((B, ATT_KV, SEQ), BF16),
                   jax.ShapeDtypeStruct((B, N_RET_CHUNKS, RET_HEADS, RET_DK, RET_DV), BF16),
                   jax.ShapeDtypeStruct((RET_HEADS, RET_DK, RET_DV), F32),
                   jax.ShapeDtypeStruct((N_META, ATT_KV), BF16),
                   jax.ShapeDtypeStruct((ATT_KV, N_META), BF16),
                   jax.ShapeDtypeStruct((D_MODEL, MAIN_COLS), BF16),
                   jax.ShapeDtypeStruct((D_MODEL, D_MODEL), BF16),
                   jax.ShapeDtypeStruct((D_MODEL, D_MODEL), BF16),
                   jax.ShapeDtypeStruct((D_MODEL, D_MODEL), BF16)),
        scratch_shapes=[state_scratch, pltpu.VMEM((ktm, ATT_HD), F32), pltpu.VMEM((ktm, ATT_HD), F32),
                        pltpu.VMEM((D_MODEL, KV_COLS), BF16)] + weight_stage
        + [pltpu.VMEM((X_SLOTS, ktm, D_MODEL), F32), pltpu.SemaphoreType.DMA((X_SLOTS,))],
        compiler_params=params,
        name="kv",
    )(x, meta_tokens.astype(F32), pre_w, w_in, inv, per_head, per_head, w_in, w_rb, w_ab, w_o)

    prev_chunk = lambda t: jnp.maximum(t * cpt - 1, 0)
    next_chunk = lambda t: jnp.minimum((t + 1) * cpt, N_CHUNKS - 1)
    out = pl.pallas_call(
        functools.partial(_main_kernel, cpt=cpt),
        grid=(B, nt),
        in_specs=[
            pl.BlockSpec((None, tm, D_MODEL), lambda b, t: (b, t, 0)),
            _resident((1, D_MODEL)),
            _resident((1, D_MODEL)),
            _resident((1, RET_V)),
            _resident((D_MODEL, MAIN_COLS)),
            _resident((RET_V, D_MODEL)),
            _resident((ATT_Q, D_MODEL)),
            _resident((D_MODEL, D_MODEL)),
            _resident((1, ATT_HD)),
            dec_f_spec,
            dec_b_spec,
            sink_spec,
            pl.BlockSpec((None, tm, RKV_COLS), lambda b, t: (b, t, 0)),
            pl.BlockSpec((None, CHUNK, ATT_KV), lambda b, t: (b, prev_chunk(t), 0)),
            pl.BlockSpec((None, tm, ATT_KV), lambda b, t: (b, t, 0)),
            pl.BlockSpec((None, CHUNK, ATT_KV), lambda b, t: (b, next_chunk(t), 0)),
            pl.BlockSpec((None, ATT_KV, CHUNK), lambda b, t: (b, 0, prev_chunk(t))),
            pl.BlockSpec((None, ATT_KV, tm), lambda b, t: (b, 0, t)),
            pl.BlockSpec((None, ATT_KV, CHUNK), lambda b, t: (b, 0, next_chunk(t))),
            pl.BlockSpec((None, tm // RET_CHUNK, RET_HEADS, RET_DK, RET_DV), lambda b, t: (b, t, 0, 0, 0)),
            _resident((RET_HEADS, RET_DK, RET_DV)),
            _resident((N_META, ATT_KV)),
            _resident((ATT_KV, N_META)),
        ],
        out_specs=pl.BlockSpec((None, tm, D_MODEL), lambda b, t: (b, t, 0)),
        out_shape=jax.ShapeDtypeStruct((B, SEQ, D_MODEL), x.dtype),
        scratch_shapes=(
            [state_scratch] + rope_scratch
            + [pltpu.VMEM((N_KEYS, Q_ROWS), F32),
               pltpu.VMEM((tm, D_MODEL), BF16),
               pltpu.VMEM((tm, RET_QK), F32),
               pltpu.VMEM((tm, ATT_Q), BF16),
               pltpu.VMEM((tm, RET_V), F32),
               pltpu.VMEM((tm, ATT_Q), F32)]
            + [pltpu.VMEM((tm, D_MODEL), F32)] * 4
            + [pltpu.VMEM((tm, D_MODEL), BF16)] * 3
            + [pltpu.VMEM((RET_HEADS, RET_CHUNK, RET_CHUNK), F32),
               pltpu.VMEM((RET_HEADS, RET_CHUNK, 2 * RET_DK), F32),
               pltpu.VMEM((RET_HEADS, RET_CHUNK, RET_DK), F32)]
        ),
        compiler_params=params,
        name="main",
    )(x, pre_w, post_w, ret_nw, w_main_b, w_rb_b, w_ab_b, w_o_b, inv, per_head, per_head, per_head,
      rkv, ak, ak, ak, avt, avt, avt, sb, kvf0, km, vmt)
    return out
```

```python
import functools

import jax
import jax.numpy as jnp
from jax import lax
from jax.experimental import pallas as pl
from jax.experimental.pallas import tpu as pltpu

D_MODEL = 1024
SEQ = 8192
N_META = 16
CHUNK = 128
RET_HEADS = 4
RET_DK = 128
RET_DV = 256
ATT_HEADS = 8
ATT_GROUPS = 2
ATT_REP = ATT_HEADS // ATT_GROUPS
ATT_HD = 128
ROPE_THETA = 10000.0
EPS = 1e-6
NEG_INF = -1e30
LOG2_E = 1.4426950408889634
RET_QK = RET_HEADS * RET_DK
RET_V = RET_HEADS * RET_DV
ATT_Q = ATT_HEADS * ATT_HD
ATT_KV = ATT_GROUPS * ATT_HD
D_IN = 2 * RET_QK + 2 * RET_V + 2 * ATT_Q + 2 * ATT_KV + 2 * D_MODEL
N_CHUNKS = SEQ // CHUNK
N_BAND = 3 * CHUNK
N_KEYS = N_BAND + N_META
Q_ROWS = ATT_REP * CHUNK

OFF_RQ = 0
OFF_RK = OFF_RQ + RET_QK
OFF_RV = OFF_RK + RET_QK
OFF_RG = OFF_RV + RET_V
OFF_AQ = OFF_RG + RET_V
OFF_AK = OFF_AQ + ATT_Q
OFF_AV = OFF_AK + ATT_KV
OFF_AG = OFF_AV + ATT_KV
OFF_GR = OFF_AG + ATT_Q
OFF_GA = OFF_GR + D_MODEL
RKV_COLS = RET_QK + RET_V

W_CHUNK = 512
KV_SRC = tuple(range(OFF_RK, OFF_RG, W_CHUNK)) + tuple(range(OFF_AK, OFF_AG, W_CHUNK))
KV_RK, KV_RV, KV_AKV = 0, RET_QK, RET_QK + RET_V
KV_COLS = len(KV_SRC) * W_CHUNK
MAIN_SEGMENTS = (("rq", OFF_RQ, RET_QK), ("aq", OFF_AQ, ATT_Q), ("rg", OFF_RG, RET_V), ("ag", OFF_AG, ATT_Q),
                 ("gr", OFF_GR, D_MODEL), ("ga", OFF_GA, D_MODEL))
MAIN_SRC = tuple(c for _, off, width in MAIN_SEGMENTS for c in range(off, off + width, W_CHUNK))
MAIN_OFF = {}
_o = 0
for _name, _, _width in MAIN_SEGMENTS:
    MAIN_OFF[_name] = _o
    _o += _width
MAIN_COLS = _o
MAIN_ROW_WINDOWS = ((OFF_RQ, RET_QK), (OFF_RG, RET_V + ATT_Q), (OFF_AG, (D_IN - OFF_AG) // 2),
                    (OFF_AG + (D_IN - OFF_AG) // 2, (D_IN - OFF_AG) // 2))

CHUNKS_PER_TILE = 4
KV_CHUNKS_PER_TILE = 8
RET_CHUNK = 2 * CHUNK
N_RET_CHUNKS = SEQ // RET_CHUNK
COL_BLOCK = 256
SUBLANES = 8
V7X_VMEM_LIMIT_BYTES = 56 * 1024 * 1024

F32 = jnp.float32
BF16 = jnp.bfloat16


def _rms_norm(x, w):
    return x * lax.rsqrt(jnp.mean(x * x, axis=-1, keepdims=True) + EPS) * w


def _log_sigmoid(x):
    return jnp.minimum(x, 0.0) - jnp.log(1.0 + jnp.exp(-jnp.abs(x)))


def _sign_fold(sin):
    lane = lax.broadcasted_iota(jnp.int32, sin.shape, 1)
    return jnp.where(lane < ATT_HD // 2, -sin, sin)


def _rope_tables(rows, inv):
    ang = lax.broadcasted_iota(jnp.int32, (rows, ATT_HD), 0).astype(F32) * inv
    return jnp.cos(ang), _sign_fold(jnp.sin(ang))


def _fill_rope_base(inv, cosb_ref, sinb_ref):
    ang = lax.broadcasted_iota(jnp.int32, (CHUNK, ATT_HD), 0).astype(F32) * inv
    cb, sb = jnp.cos(ang), jnp.sin(ang)
    for r0 in range(0, cosb_ref.shape[0], CHUNK):
        base = float(r0) * inv
        ca, sa = jnp.cos(base), jnp.sin(base)
        cosb_ref[r0:r0 + CHUNK] = ca * cb - sa * sb
        sinb_ref[r0:r0 + CHUNK] = sa * cb + ca * sb


def _rope_tables_from_base(pos0, inv, cosb_ref, sinb_ref):
    base = pos0.astype(F32) * inv
    ca, sa = jnp.cos(base), jnp.sin(base)
    cb, sb = cosb_ref[...], sinb_ref[...]
    return ca * cb - sa * sb, _sign_fold(sa * cb + ca * sb)


def _sigmoid(x):
    return 0.5 * jnp.tanh(0.5 * x) + 0.5


def _silu(x):
    return x * _sigmoid(x)


def _rope(t, cos, sin_signed):
    return t * cos + pltpu.roll(t, ATT_HD // 2, axis=1) * sin_signed


def _dot(a, b):
    return jnp.dot(a, b, preferred_element_type=F32)


def _dot_nt(a, b):
    return lax.dot_general(a, b, (((1,), (1,)), ((), ())), preferred_element_type=F32)


def _dot_tn(a, b):
    return lax.dot_general(a, b, (((0,), (0,)), ((), ())), preferred_element_type=F32)


def _row_index(shape):
    return lax.broadcasted_iota(jnp.int32, shape, 0).astype(F32)


def _weight_chunk_copy(src_ref, col, stage_ref, sem_ref, slot):
    return pltpu.make_async_copy(src_ref.at[0, :, pl.ds(col, W_CHUNK)], stage_ref.at[slot], sem_ref.at[slot])


def _load_weights_bf16(jobs, stage_ref, sem_ref):
    copies = [_weight_chunk_copy(src, col, stage_ref, sem_ref, i % 2) for i, (src, col, _, _) in enumerate(jobs)]
    copies[0].start()
    for i, (_, _, dst, dcol) in enumerate(jobs):
        if i + 1 < len(jobs):
            copies[i + 1].start()
        copies[i].wait()
        dst[:, dcol:dcol + W_CHUNK] = stage_ref[i % 2].astype(BF16)


def _meta_block(meta_ref, prew_ref, wkv_ref, inv_ref, decf_ref, kvf0_ref, km_ref, vmt_ref):
    u = _rms_norm(meta_ref[...], prew_ref[...]).astype(BF16)
    zk = _dot(u, wkv_ref[:, KV_RK:KV_RV])
    zv = _dot(u, wkv_ref[:, KV_RV:KV_AKV])
    za = _dot(u, wkv_ref[:, KV_AKV:])
    cos, sin = _rope_tables(N_META, inv_ref[...])
    kdec_rows = (N_META - 1) - _row_index((N_META, RET_DK))
    for h in range(RET_HEADS):
        lg = _log_sigmoid(decf_ref[h:h + 1, :RET_DK])
        k = _rope(zk[:, h * RET_DK:(h + 1) * RET_DK], cos, sin) * (RET_DK ** -0.5)
        k = (k * jnp.exp(kdec_rows * lg)).astype(BF16)
        v = zv[:, h * RET_DV:(h + 1) * RET_DV].astype(BF16)
        kvf0_ref[h] = _dot_tn(k, v)
    for g in range(ATT_GROUPS):
        km_ref[:, g * ATT_HD:(g + 1) * ATT_HD] = _rope(za[:, g * ATT_HD:(g + 1) * ATT_HD], cos, sin).astype(BF16)
    vmt_ref[...] = za[:, ATT_KV:].T.astype(BF16)


def _kv_kernel(x_ref, meta_ref, prew_ref, win_ref, inv_ref, decf_ref, decb_ref, win_a_ref, win_b_ref, win_c_ref,
               win_d_ref, wrb_rows_ref,
               wab_rows_ref, wo_rows_ref, rkv_ref, ak_ref, avt_ref, sb_ref, kvf0_ref, km_ref, vmt_ref, wmain_out_ref,
               wrb_out_ref, wab_out_ref, wo_out_ref, state_ref, cosb_ref, sinb_ref, wkv_ref, stage_ref, sem_ref, *,
               cpt):
    t = pl.program_id(1)
    tile = pl.num_programs(1) - 1 - t
    tm = cpt * CHUNK

    for i, col in enumerate(MAIN_SRC):
        (off, _), src = next((w, r) for w, r in zip(MAIN_ROW_WINDOWS, (win_a_ref, win_b_ref, win_c_ref, win_d_ref))
                             if w[0] <= col and col + W_CHUNK <= w[0] + w[1])
        wmain_out_ref[:, i * W_CHUNK:(i + 1) * W_CHUNK] = src[:, col - off:col - off + W_CHUNK].astype(BF16)
    for src, dst in ((wrb_rows_ref, wrb_out_ref), (wab_rows_ref, wab_out_ref), (wo_rows_ref, wo_out_ref)):
        dst[...] = src[...].astype(BF16)

    @pl.when((pl.program_id(0) == 0) & (t == 0))
    def _():
        _load_weights_bf16([(win_ref, col, wkv_ref, i * W_CHUNK) for i, col in enumerate(KV_SRC)],
                           stage_ref, sem_ref)
        _fill_rope_base(inv_ref[...], cosb_ref, sinb_ref)
        _meta_block(meta_ref, prew_ref, wkv_ref, inv_ref, decf_ref, kvf0_ref, km_ref, vmt_ref)

    @pl.when(t == 0)
    def _():
        state_ref[...] = jnp.zeros_like(state_ref)

    u = _rms_norm(x_ref[...], prew_ref[...]).astype(BF16)
    za = _dot(u, wkv_ref[:, KV_AKV:])
    zk = _dot(u, wkv_ref[:, KV_RK:KV_RV])
    zv = _dot(u, wkv_ref[:, KV_RV:KV_AKV])
    cos, sin = _rope_tables_from_base(N_META + tile * tm, inv_ref[...], cosb_ref, sinb_ref)
    for g in range(ATT_GROUPS):
        ak_ref[:, g * ATT_HD:(g + 1) * ATT_HD] = _rope(za[:, g * ATT_HD:(g + 1) * ATT_HD], cos, sin).astype(BF16)
    avt_ref[...] = za[:, ATT_KV:].T.astype(BF16)
    rk = [_rope(zk[:, h * RET_DK:(h + 1) * RET_DK], cos, sin) * (RET_DK ** -0.5) for h in range(RET_HEADS)]
    for h in range(RET_HEADS):
        rkv_ref[:, h * RET_DK:(h + 1) * RET_DK] = rk[h].astype(BF16)
    rkv_ref[:, RET_QK:] = zv.astype(BF16)

    rows_k = _row_index((RET_CHUNK, RET_DK))
    for h in range(RET_HEADS):
        lg = _log_sigmoid(decb_ref[h:h + 1, :])
        kdec = jnp.exp(rows_k * lg[:, :RET_DK])
        cdec = jnp.exp(RET_CHUNK * lg)
        for rc in reversed(range(tm // RET_CHUNK)):
            r0 = rc * RET_CHUNK
            state = state_ref[h]
            sb_ref[rc, h] = state.astype(BF16)
            k = (rk[h][r0:r0 + RET_CHUNK] * kdec).astype(BF16)
            v = zv[r0:r0 + RET_CHUNK, h * RET_DV:(h + 1) * RET_DV].astype(BF16)
            state_ref[h] = cdec * state + _dot_tn(k, v)


def _fill_band_bias(bias_ref):
    kk = lax.broadcasted_iota(jnp.int32, bias_ref.shape, 0)
    qi = lax.broadcasted_iota(jnp.int32, bias_ref.shape, 1) & (CHUNK - 1)
    visible = (kk >= N_BAND) | ((kk >= qi) & (kk <= qi + 2 * CHUNK))
    bias_ref[...] = jnp.where(visible, 0.0, NEG_INF)


def _fill_decay_tables(decf_ref, decb_ref, dmat_ref, qdec_ref, kdf_ref):
    rows = _row_index((RET_CHUNK, RET_CHUNK))
    rel = rows - lax.broadcasted_iota(jnp.int32, (RET_CHUNK, RET_CHUNK), 1).astype(F32)
    rows_k = _row_index((RET_CHUNK, RET_DK))
    for h in range(RET_HEADS):
        lgf = _log_sigmoid(decf_ref[h:h + 1, :])
        lgb = _log_sigmoid(decb_ref[h:h + 1, :])
        dmat_ref[h] = jnp.where(rel >= 0, jnp.exp(jnp.maximum(rel, 0.0) * lgf),
                                jnp.exp(jnp.maximum(-rel, 0.0) * lgb))
        lgf_k, lgb_k = lgf[:, :RET_DK], lgb[:, :RET_DK]
        qdec_ref[h, :, :RET_DK] = jnp.exp((rows_k + 1.0) * lgf_k)
        qdec_ref[h, :, RET_DK:] = jnp.exp((RET_CHUNK - rows_k) * lgb_k)
        kdf_ref[h] = jnp.exp((RET_CHUNK - 1.0 - rows_k) * lgf_k)


def _main_kernel(x_ref, prew_ref, postw_ref, retnw_ref, wmain_ref, wrb_ref, wab_ref, wo_ref, inv_ref,
                 decf_ref, decb_ref, sink_ref, rkv_ref, akp_ref, akc_ref, akn_ref, avp_ref,
                 avc_ref, avn_ref, sb_ref, kvf0_ref, km_ref, vmt_ref, out_ref,
                 sf_ref, cosb_ref, sinb_ref, bias_ref, u_ref, rq_ref, aq_ref, or_ref, oa_ref, grg_ref, gag_ref,
                 ggr_ref, gga_ref, lhs_r_ref, lhs_a_ref, mix_ref, dmat_ref, qdec_ref, kdf_ref, *, cpt):
    t = pl.program_id(1)
    tm = cpt * CHUNK

    @pl.when((pl.program_id(0) == 0) & (t == 0))
    def _():
        _fill_rope_base(inv_ref[...], cosb_ref, sinb_ref)
        _fill_band_bias(bias_ref)
        _fill_decay_tables(decf_ref, decb_ref, dmat_ref, qdec_ref, kdf_ref)

    @pl.when(t == 0)
    def _():
        sf_ref[...] = kvf0_ref[...]

    for r0 in range(0, tm, CHUNK):
        u_ref[r0:r0 + CHUNK] = _rms_norm(x_ref[r0:r0 + CHUNK], prew_ref[...]).astype(BF16)
    cos, sin = _rope_tables_from_base(N_META + t * tm, inv_ref[...], cosb_ref, sinb_ref)

    def proj(name, c0):
        col = MAIN_OFF[name] + c0
        return _dot(u_ref[...], wmain_ref[:, col:col + COL_BLOCK])

    heads_per_block = COL_BLOCK // ATT_HD
    for c0 in range(0, RET_QK, COL_BLOCK):
        z = proj("rq", c0)
        for i in range(heads_per_block):
            c = c0 + i * RET_DK
            rq_ref[:, c:c + RET_DK] = _rope(z[:, i * RET_DK:(i + 1) * RET_DK], cos, sin)
    for c0 in range(0, ATT_Q, COL_BLOCK):
        z = proj("aq", c0)
        for i in range(heads_per_block):
            c = c0 + i * ATT_HD
            aq_ref[:, c:c + ATT_HD] = (_rope(z[:, i * ATT_HD:(i + 1) * ATT_HD], cos, sin)
                                       * (ATT_HD ** -0.5 * LOG2_E)).astype(BF16)

    gate_jobs = (("rg", _silu, grg_ref), ("ag", _silu, gag_ref), ("gr", _sigmoid, ggr_ref), ("ga", _sigmoid, gga_ref))
    gate_blocks = [(j, c) for j in range(len(gate_jobs)) for c in range(0, D_MODEL, COL_BLOCK)]
    assert len(gate_blocks) % cpt == 0

    def emit_gate_blocks(n):
        for _ in range(n):
            j, c = gate_blocks.pop(0)
            name, act, ref = gate_jobs[j]
            ref[:, c:c + COL_BLOCK] = act(proj(name, c))

    cdf = [jnp.exp(RET_CHUNK * _log_sigmoid(decf_ref[h:h + 1, :])) for h in range(RET_HEADS)]

    k_cat = jnp.concatenate([akp_ref[...], akc_ref[...], akn_ref[...]], axis=0)
    vt_cat = jnp.concatenate([avp_ref[...], avc_ref[...], avn_ref[...]], axis=1)
    sink_rows = [jnp.concatenate([sink_ref[g * ATT_REP + r:g * ATT_REP + r + 1, :] for r in range(ATT_REP)],
                                 axis=1) * LOG2_E for g in range(ATT_GROUPS)]

    def retention(rc):
        r0 = rc * RET_CHUNK
        rs = slice(r0, r0 + RET_CHUNK)
        for h in range(RET_HEADS):
            q = rq_ref[rs, h * RET_DK:(h + 1) * RET_DK]
            k = rkv_ref[rs, h * RET_DK:(h + 1) * RET_DK]
            v = rkv_ref[rs, RET_QK + h * RET_DV:RET_QK + (h + 1) * RET_DV]
            s = _dot_nt(q.astype(BF16), k) * dmat_ref[h]
            sf = sf_ref[h]
            q_cross = (jnp.concatenate([q, q], axis=1) * qdec_ref[h]).astype(BF16)
            s_cross = jnp.concatenate([sf.astype(BF16), sb_ref[rc, h]], axis=0)
            o = _dot(s.astype(BF16), v) + _dot(q_cross, s_cross)
            sf_ref[h] = cdf[h] * sf + _dot_tn((k.astype(F32) * kdf_ref[h]).astype(BF16), v)
            mu = jnp.mean(o, axis=-1, keepdims=True)
            d = o - mu
            var = jnp.mean(d * d, axis=-1, keepdims=True)
            or_ref[rs, h * RET_DV:(h + 1) * RET_DV] = d * lax.rsqrt(var + EPS)

    for rc in range(tm // RET_CHUNK):
        retention(rc)

    gates_per_pair = len(gate_blocks) // (cpt * ATT_GROUPS)
    pairs = [(lc, g) for lc in range(cpt) for g in range(ATT_GROUPS)]
    probs, denoms = {}, {}
    for lc, g in pairs:
        r0 = lc * CHUNK
        chunk = t * cpt + lc
        gs = slice(g * ATT_HD, (g + 1) * ATT_HD)
        qs = jnp.concatenate([aq_ref[r0:r0 + CHUNK, (g * ATT_REP + r) * ATT_HD:(g * ATT_REP + r + 1) * ATT_HD]
                              for r in range(ATT_REP)], axis=0)
        k_all = jnp.concatenate([k_cat[r0:r0 + N_BAND, gs], km_ref[:, gs]], axis=0)
        emit_gate_blocks(gates_per_pair)
        s = _dot_nt(k_all, qs) + bias_ref[...]
        parts = [s[0:CHUNK], s[CHUNK:2 * CHUNK], s[2 * CHUNK:N_BAND], s[N_BAND:]]
        if lc == 0:
            parts[0] = jnp.where(chunk > 0, parts[0], NEG_INF)
        if lc == cpt - 1:
            parts[2] = jnp.where(chunk < N_CHUNKS - 1, parts[2], NEG_INF)
        s = jnp.concatenate(parts, axis=0)
        sk = sink_rows[g]
        m = jnp.maximum(jnp.max(s, axis=0, keepdims=True), sk)
        p = jnp.exp2(s - m)
        denoms[lc, g] = jnp.sum(p, axis=0, keepdims=True) + jnp.exp2(sk - m)
        probs[lc, g] = p.astype(BF16)
    for lc, g in pairs:
        r0 = lc * CHUNK
        gs = slice(g * ATT_HD, (g + 1) * ATT_HD)
        vt_all = jnp.concatenate([vt_cat[gs, r0:r0 + N_BAND], vmt_ref[gs, :]], axis=1)
        ot = _dot(vt_all, probs[lc, g]) / denoms[lc, g]
        for r in range(ATT_REP):
            c = (g * ATT_REP + r) * ATT_HD
            oa_ref[r0:r0 + CHUNK, c:c + ATT_HD] = ot[:, r * CHUNK:(r + 1) * CHUNK].T

    for r0 in range(0, tm, CHUNK):
        rs = slice(r0, r0 + CHUNK)
        lhs_r_ref[rs] = (or_ref[rs] * retnw_ref[...] * grg_ref[rs]).astype(BF16)
        lhs_a_ref[rs] = (oa_ref[rs] * gag_ref[rs]).astype(BF16)
    for c0 in range(0, D_MODEL, COL_BLOCK):
        cs = slice(c0, c0 + COL_BLOCK)
        y_r = _dot(lhs_r_ref[...], wrb_ref[:, cs])
        y_a = _dot(lhs_a_ref[...], wab_ref[:, cs])
        mix_ref[:, cs] = (ggr_ref[:, cs] * y_r + gga_ref[:, cs] * y_a).astype(BF16)
    for c0 in range(0, D_MODEL, COL_BLOCK):
        cs = slice(c0, c0 + COL_BLOCK)
        out_ref[:, cs] = _dot(mix_ref[...], wo_ref[:, cs])
    for r0 in range(0, tm, CHUNK):
        rs = slice(r0, r0 + CHUNK)
        out_ref[rs] = x_ref[rs] + _rms_norm(out_ref[rs], postw_ref[...])


def _resident(shape, index=None):
    nd = len(shape)
    index = (0,) * nd if index is None else index
    return pl.BlockSpec(shape, lambda *_: index, pipeline_mode=pl.Buffered(1))


def kernel(x, meta_tokens, pre_norm_w, w_in, ret_decay_fwd, ret_decay_bwd, ret_norm_w, w_ret_branch, attn_sink,
           w_attn_branch, w_out, post_norm_w):
    B = x.shape[0]
    assert x.shape == (B, SEQ, D_MODEL) and pre_norm_w.shape[0] == 1 and w_in.shape == (1, D_MODEL, D_IN)
    cpt = CHUNKS_PER_TILE
    tm = cpt * CHUNK
    nt = N_CHUNKS // cpt

    w_in = w_in.astype(F32)
    w_rb, w_ab, w_o = (w.astype(F32) for w in (w_ret_branch, w_attn_branch, w_out))
    pre_w = pre_norm_w.astype(F32)
    post_w = post_norm_w.astype(F32)
    ret_nw = ret_norm_w.astype(F32)
    half = ATT_HD // 2
    inv = ROPE_THETA ** (-jnp.arange(half, dtype=F32) * 2.0 / ATT_HD)
    inv = jnp.concatenate([inv, inv])[None, :]
    assert ATT_HEADS == SUBLANES and RET_HEADS <= SUBLANES
    pad = jnp.zeros((SUBLANES - RET_HEADS,), F32)
    per_head = jnp.concatenate([ret_decay_fwd[0].astype(F32), pad, ret_decay_bwd[0].astype(F32), pad,
                                attn_sink[0].astype(F32)])
    per_head = jnp.broadcast_to(per_head[:, None], (2 * SUBLANES + ATT_HEADS, RET_DV))
    dec_f_spec = _resident((SUBLANES, RET_DV), (0, 0))
    dec_b_spec = _resident((SUBLANES, RET_DV), (1, 0))
    sink_spec = _resident((ATT_HEADS, ATT_HD), (2, 0))

    params = pltpu.CompilerParams(dimension_semantics=("arbitrary", "arbitrary"),
                                  vmem_limit_bytes=V7X_VMEM_LIMIT_BYTES)
    rope_scratch = [pltpu.VMEM((tm, ATT_HD), F32), pltpu.VMEM((tm, ATT_HD), F32)]
    state_scratch = pltpu.VMEM((RET_HEADS, RET_DK, RET_DV), F32)
    weight_stage = [pltpu.VMEM((2, D_MODEL, W_CHUNK), F32), pltpu.SemaphoreType.DMA((2,))]
    hbm = pl.BlockSpec(memory_space=pl.ANY)

    kcpt = KV_CHUNKS_PER_TILE
    ktm, knt = kcpt * CHUNK, N_CHUNKS // kcpt
    slab = D_MODEL // (B * knt)
    assert slab * B * knt == D_MODEL and slab % (2 * SUBLANES) == 0

    def slab_in(width):
        return pl.BlockSpec((None, slab, width), lambda b, t: (0, b * knt + t, 0))

    def slab_out(width):
        return pl.BlockSpec((slab, width), lambda b, t: (b * knt + t, 0))

    rkv, ak, avt, sb, kvf0, km, vmt, w_main_b, w_rb_b, w_ab_b, w_o_b = pl.pallas_call(
        functools.partial(_kv_kernel, cpt=kcpt),
        grid=(B, knt),
        in_specs=[pl.BlockSpec((None, ktm, D_MODEL), lambda b, t: (b, knt - 1 - t, 0)), _resident((N_META, D_MODEL)),
                  _resident((1, D_MODEL)), hbm, _resident((1, ATT_HD)), dec_f_spec, dec_b_spec,
                  *[pl.BlockSpec((None, slab, w), functools.partial(lambda b, t, j: (0, b * knt + t, j), j=off // w))
                    for off, w in MAIN_ROW_WINDOWS],
                  slab_in(D_MODEL), slab_in(D_MODEL), slab_in(D_MODEL)],
        out_specs=(
            pl.BlockSpec((None, ktm, RKV_COLS), lambda b, t: (b, knt - 1 - t, 0)),
            pl.BlockSpec((None, ktm, ATT_KV), lambda b, t: (b, knt - 1 - t, 0)),
            pl.BlockSpec((None, ATT_KV, ktm), lambda b, t: (b, 0, knt - 1 - t)),
            pl.BlockSpec((None, ktm // RET_CHUNK, RET_HEADS, RET_DK, RET_DV), lambda b, t: (b, knt - 1 - t, 0, 0, 0)),
            pl.BlockSpec((RET_HEADS, RET_DK, RET_DV), lambda b, t: (0, 0, 0)),
            pl.BlockSpec((N_META, ATT_KV), lambda b, t: (0, 0)),
            pl.BlockSpec((ATT_KV, N_META), lambda b, t: (0, 0)),
            slab_out(MAIN_COLS), slab_out(D_MODEL), slab_out(D_MODEL), slab_out(D_MODEL),
        ),
        out_shape=(jax.ShapeDtypeStruct((B, SEQ, RKV_COLS), BF16),
                   jax.ShapeDtypeStruct((B, SEQ, ATT_KV), BF16),
                   jax.ShapeDtypeStruct((B, ATT_KV, SEQ), BF16),
                   jax.ShapeDtypeStruct((B, N_RET_CHUNKS, RET_HEADS, RET_DK, RET_DV), BF16),
                   jax.ShapeDtypeStruct((RET_HEADS, RET_DK, RET_DV), F32),
                   jax.ShapeDtypeStruct((N_META, ATT_KV), BF16),
                   jax.ShapeDtypeStruct((ATT_KV, N_META), BF16),
                   jax.ShapeDtypeStruct((D_MODEL, MAIN_COLS), BF16),
                   jax.ShapeDtypeStruct((D_MODEL, D_MODEL), BF16),
                   jax.ShapeDtypeStruct((D_MODEL, D_MODEL), BF16),
                   jax.ShapeDtypeStruct((D_MODEL, D_MODEL), BF16)),
        scratch_shapes=[state_scratch, pltpu.VMEM((ktm, ATT_HD), F32), pltpu.VMEM((ktm, ATT_HD), F32),
                        pltpu.VMEM((D_MODEL, KV_COLS), BF16)] + weight_stage,
        compiler_params=params,
        name="kv",
    )(x, meta_tokens.astype(F32), pre_w, w_in, inv, per_head, per_head, w_in, w_in, w_in, w_in, w_rb, w_ab, w_o)

    prev_chunk = lambda t: jnp.maximum(t * cpt - 1, 0)
    next_chunk = lambda t: jnp.minimum((t + 1) * cpt, N_CHUNKS - 1)
    out = pl.pallas_call(
        functools.partial(_main_kernel, cpt=cpt),
        grid=(B, nt),
        in_specs=[
            pl.BlockSpec((None, tm, D_MODEL), lambda b, t: (b, t, 0)),
            _resident((1, D_MODEL)),
            _resident((1, D_MODEL)),
            _resident((1, RET_V)),
            _resident((D_MODEL, MAIN_COLS)),
            _resident((RET_V, D_MODEL)),
            _resident((ATT_Q, D_MODEL)),
            _resident((D_MODEL, D_MODEL)),
            _resident((1, ATT_HD)),
            dec_f_spec,
            dec_b_spec,
            sink_spec,
            pl.BlockSpec((None, tm, RKV_COLS), lambda b, t: (b, t, 0)),
            pl.BlockSpec((None, CHUNK, ATT_KV), lambda b, t: (b, prev_chunk(t), 0)),
            pl.BlockSpec((None, tm, ATT_KV), lambda b, t: (b, t, 0)),
            pl.BlockSpec((None, CHUNK, ATT_KV), lambda b, t: (b, next_chunk(t), 0)),
            pl.BlockSpec((None, ATT_KV, CHUNK), lambda b, t: (b, 0, prev_chunk(t))),
            pl.BlockSpec((None, ATT_KV, tm), lambda b, t: (b, 0, t)),
            pl.BlockSpec((None, ATT_KV, CHUNK), lambda b, t: (b, 0, next_chunk(t))),
            pl.BlockSpec((None, tm // RET_CHUNK, RET_HEADS, RET_DK, RET_DV), lambda b, t: (b, t, 0, 0, 0)),
            _resident((RET_HEADS, RET_DK, RET_DV)),
            _resident((N_META, ATT_KV)),
            _resident((ATT_KV, N_META)),
        ],
        out_specs=pl.BlockSpec((None, tm, D_MODEL), lambda b, t: (b, t, 0)),
        out_shape=jax.ShapeDtypeStruct((B, SEQ, D_MODEL), x.dtype),
        scratch_shapes=(
            [state_scratch] + rope_scratch
            + [pltpu.VMEM((N_KEYS, Q_ROWS), F32),
               pltpu.VMEM((tm, D_MODEL), BF16),
               pltpu.VMEM((tm, RET_QK), F32),
               pltpu.VMEM((tm, ATT_Q), BF16),
               pltpu.VMEM((tm, RET_V), F32),
               pltpu.VMEM((tm, ATT_Q), F32)]
            + [pltpu.VMEM((tm, D_MODEL), F32)] * 4
            + [pltpu.VMEM((tm, D_MODEL), BF16)] * 3
            + [pltpu.VMEM((RET_HEADS, RET_CHUNK, RET_CHUNK), F32),
               pltpu.VMEM((RET_HEADS, RET_CHUNK, 2 * RET_DK), F32),
               pltpu.VMEM((RET_HEADS, RET_CHUNK, RET_DK), F32)]
        ),
        compiler_params=params,
        name="main",
    )(x, pre_w, post_w, ret_nw, w_main_b, w_rb_b, w_ab_b, w_o_b, inv, per_head, per_head, per_head,
      rkv, ak, ak, ak, avt, avt, avt, sb, kvf0, km, vmt)
    return out
```

```python
import functools

import jax
import jax.numpy as jnp
from jax import lax
from jax.experimental import pallas as pl
from jax.experimental.pallas import tpu as pltpu

D_MODEL = 1024
SEQ = 8192
N_META = 16
CHUNK = 128
RET_HEADS = 4
RET_DK = 128
RET_DV = 256
ATT_HEADS = 8
ATT_GROUPS = 2
ATT_REP = ATT_HEADS // ATT_GROUPS
ATT_HD = 128
ROPE_THETA = 10000.0
EPS = 1e-6
NEG_INF = -1e30
LOG2_E = 1.4426950408889634
RET_QK = RET_HEADS * RET_DK
RET_V = RET_HEADS * RET_DV
ATT_Q = ATT_HEADS * ATT_HD
ATT_KV = ATT_GROUPS * ATT_HD
D_IN = 2 * RET_QK + 2 * RET_V + 2 * ATT_Q + 2 * ATT_KV + 2 * D_MODEL
N_CHUNKS = SEQ // CHUNK
N_BAND = 3 * CHUNK
N_KEYS = N_BAND + N_META
Q_ROWS = ATT_REP * CHUNK

OFF_RQ = 0
OFF_RK = OFF_RQ + RET_QK
OFF_RV = OFF_RK + RET_QK
OFF_RG = OFF_RV + RET_V
OFF_AQ = OFF_RG + RET_V
OFF_AK = OFF_AQ + ATT_Q
OFF_AV = OFF_AK + ATT_KV
OFF_AG = OFF_AV + ATT_KV
OFF_GR = OFF_AG + ATT_Q
OFF_GA = OFF_GR + D_MODEL
RKV_COLS = RET_QK + RET_V

W_CHUNK = 512
KV_SRC = tuple(range(OFF_RK, OFF_RG, W_CHUNK)) + tuple(range(OFF_AK, OFF_AG, W_CHUNK))
KV_RK, KV_RV, KV_AKV = 0, RET_QK, RET_QK + RET_V
KV_COLS = len(KV_SRC) * W_CHUNK
MAIN_SEGMENTS = (("rq", OFF_RQ, RET_QK), ("aq", OFF_AQ, ATT_Q), ("rg", OFF_RG, RET_V), ("ag", OFF_AG, ATT_Q),
                 ("gr", OFF_GR, D_MODEL), ("ga", OFF_GA, D_MODEL))
MAIN_SRC = tuple(c for _, off, width in MAIN_SEGMENTS for c in range(off, off + width, W_CHUNK))
MAIN_OFF = {}
_o = 0
for _name, _, _width in MAIN_SEGMENTS:
    MAIN_OFF[_name] = _o
    _o += _width
MAIN_COLS = _o

CHUNKS_PER_TILE = 4
KV_CHUNKS_PER_TILE = 8
RET_CHUNK = 2 * CHUNK
N_RET_CHUNKS = SEQ // RET_CHUNK
COL_BLOCK = 256
SUBLANES = 8
V7X_VMEM_LIMIT_BYTES = 56 * 1024 * 1024

F32 = jnp.float32
BF16 = jnp.bfloat16


def _rms_norm(x, w):
    return x * lax.rsqrt(jnp.mean(x * x, axis=-1, keepdims=True) + EPS) * w


def _log_sigmoid(x):
    return jnp.minimum(x, 0.0) - jnp.log(1.0 + jnp.exp(-jnp.abs(x)))


def _sign_fold(sin):
    lane = lax.broadcasted_iota(jnp.int32, sin.shape, 1)
    return jnp.where(lane < ATT_HD // 2, -sin, sin)


def _rope_tables(rows, inv):
    ang = lax.broadcasted_iota(jnp.int32, (rows, ATT_HD), 0).astype(F32) * inv
    return jnp.cos(ang), _sign_fold(jnp.sin(ang))


def _fill_rope_base(inv, cosb_ref, sinb_ref):
    ang = lax.broadcasted_iota(jnp.int32, (CHUNK, ATT_HD), 0).astype(F32) * inv
    cb, sb = jnp.cos(ang), jnp.sin(ang)
    for r0 in range(0, cosb_ref.shape[0], CHUNK):
        base = float(r0) * inv
        ca, sa = jnp.cos(base), jnp.sin(base)
        cosb_ref[r0:r0 + CHUNK] = ca * cb - sa * sb
        sinb_ref[r0:r0 + CHUNK] = sa * cb + ca * sb


def _rope_tables_from_base(pos0, inv, cosb_ref, sinb_ref):
    base = pos0.astype(F32) * inv
    ca, sa = jnp.cos(base), jnp.sin(base)
    cb, sb = cosb_ref[...], sinb_ref[...]
    return ca * cb - sa * sb, _sign_fold(sa * cb + ca * sb)


def _sigmoid(x):
    return 0.5 * jnp.tanh(0.5 * x) + 0.5


def _silu(x):
    return x * _sigmoid(x)


def _rope(t, cos, sin_signed):
    return t * cos + pltpu.roll(t, ATT_HD // 2, axis=1) * sin_signed


def _dot(a, b):
    return jnp.dot(a, b, preferred_element_type=F32)


def _dot_nt(a, b):
    return lax.dot_general(a, b, (((1,), (1,)), ((), ())), preferred_element_type=F32)


def _dot_tn(a, b):
    return lax.dot_general(a, b, (((0,), (0,)), ((), ())), preferred_element_type=F32)


def _row_index(shape):
    return lax.broadcasted_iota(jnp.int32, shape, 0).astype(F32)


def _weight_chunk_copy(src_ref, col, stage_ref, sem_ref, slot):
    return pltpu.make_async_copy(src_ref.at[0, :, pl.ds(col, W_CHUNK)], stage_ref.at[slot], sem_ref.at[slot])


def _load_weights_bf16(jobs, stage_ref, sem_ref):
    copies = [_weight_chunk_copy(src, col, stage_ref, sem_ref, i % 2) for i, (src, col, _, _) in enumerate(jobs)]
    copies[0].start()
    for i, (_, _, dst, dcol) in enumerate(jobs):
        if i + 1 < len(jobs):
            copies[i + 1].start()
        copies[i].wait()
        dst[:, dcol:dcol + W_CHUNK] = stage_ref[i % 2].astype(BF16)


def _meta_block(meta_ref, prew_ref, wkv_ref, inv_ref, decf_ref, kvf0_ref, km_ref, vmt_ref):
    u = _rms_norm(meta_ref[...], prew_ref[...]).astype(BF16)
    zk = _dot(u, wkv_ref[:, KV_RK:KV_RV])
    zv = _dot(u, wkv_ref[:, KV_RV:KV_AKV])
    za = _dot(u, wkv_ref[:, KV_AKV:])
    cos, sin = _rope_tables(N_META, inv_ref[...])
    kdec_rows = (N_META - 1) - _row_index((N_META, RET_DK))
    for h in range(RET_HEADS):
        lg = _log_sigmoid(decf_ref[h:h + 1, :RET_DK])
        k = _rope(zk[:, h * RET_DK:(h + 1) * RET_DK], cos, sin) * (RET_DK ** -0.5)
        k = (k * jnp.exp(kdec_rows * lg)).astype(BF16)
        v = zv[:, h * RET_DV:(h + 1) * RET_DV].astype(BF16)
        kvf0_ref[h] = _dot_tn(k, v)
    for g in range(ATT_GROUPS):
        km_ref[:, g * ATT_HD:(g + 1) * ATT_HD] = _rope(za[:, g * ATT_HD:(g + 1) * ATT_HD], cos, sin).astype(BF16)
    vmt_ref[...] = za[:, ATT_KV:].T.astype(BF16)


def _kv_kernel(x_ref, meta_ref, prew_ref, win_ref, inv_ref, decf_ref, decb_ref, win_rows_ref, wrb_rows_ref,
               wab_rows_ref, wo_rows_ref, rkv_ref, ak_ref, avt_ref, sb_ref, kvf0_ref, km_ref, vmt_ref, wmain_out_ref,
               wrb_out_ref, wab_out_ref, wo_out_ref, state_ref, cosb_ref, sinb_ref, wkv_ref, stage_ref, sem_ref, *,
               cpt):
    t = pl.program_id(1)
    tile = pl.num_programs(1) - 1 - t
    tm = cpt * CHUNK

    for i, col in enumerate(MAIN_SRC):
        wmain_out_ref[:, i * W_CHUNK:(i + 1) * W_CHUNK] = win_rows_ref[:, col:col + W_CHUNK].astype(BF16)
    for src, dst in ((wrb_rows_ref, wrb_out_ref), (wab_rows_ref, wab_out_ref), (wo_rows_ref, wo_out_ref)):
        dst[...] = src[...].astype(BF16)

    @pl.when((pl.program_id(0) == 0) & (t == 0))
    def _():
        _load_weights_bf16([(win_ref, col, wkv_ref, i * W_CHUNK) for i, col in enumerate(KV_SRC)],
                           stage_ref, sem_ref)
        _fill_rope_base(inv_ref[...], cosb_ref, sinb_ref)
        _meta_block(meta_ref, prew_ref, wkv_ref, inv_ref, decf_ref, kvf0_ref, km_ref, vmt_ref)

    @pl.when(t == 0)
    def _():
        state_ref[...] = jnp.zeros_like(state_ref)

    u = _rms_norm(x_ref[...], prew_ref[...]).astype(BF16)
    za = _dot(u, wkv_ref[:, KV_AKV:])
    zk = _dot(u, wkv_ref[:, KV_RK:KV_RV])
    zv = _dot(u, wkv_ref[:, KV_RV:KV_AKV])
    cos, sin = _rope_tables_from_base(N_META + tile * tm, inv_ref[...], cosb_ref, sinb_ref)
    for g in range(ATT_GROUPS):
        ak_ref[:, g * ATT_HD:(g + 1) * ATT_HD] = _rope(za[:, g * ATT_HD:(g + 1) * ATT_HD], cos, sin).astype(BF16)
    avt_ref[...] = za[:, ATT_KV:].T.astype(BF16)
    rk = [_rope(zk[:, h * RET_DK:(h + 1) * RET_DK], cos, sin) * (RET_DK ** -0.5) for h in range(RET_HEADS)]
    for h in range(RET_HEADS):
        rkv_ref[:, h * RET_DK:(h + 1) * RET_DK] = rk[h].astype(BF16)
    rkv_ref[:, RET_QK:] = zv.astype(BF16)

    rows_k = _row_index((RET_CHUNK, RET_DK))
    for h in range(RET_HEADS):
        lg = _log_sigmoid(decb_ref[h:h + 1, :])
        kdec = jnp.exp(rows_k * lg[:, :RET_DK])
        cdec = jnp.exp(RET_CHUNK * lg)
        for rc in reversed(range(tm // RET_CHUNK)):
            r0 = rc * RET_CHUNK
            state = state_ref[h]
            sb_ref[rc, h] = state.astype(BF16)
            k = (rk[h][r0:r0 + RET_CHUNK] * kdec).astype(BF16)
            v = zv[r0:r0 + RET_CHUNK, h * RET_DV:(h + 1) * RET_DV].astype(BF16)
            state_ref[h] = cdec * state + _dot_tn(k, v)


def _fill_band_bias(bias_ref):
    kk = lax.broadcasted_iota(jnp.int32, bias_ref.shape, 0)
    qi = lax.broadcasted_iota(jnp.int32, bias_ref.shape, 1) & (CHUNK - 1)
    visible = (kk >= N_BAND) | ((kk >= qi) & (kk <= qi + 2 * CHUNK))
    bias_ref[...] = jnp.where(visible, 0.0, NEG_INF)


def _fill_decay_tables(decf_ref, decb_ref, dmat_ref, qdec_ref, kdf_ref):
    rows = _row_index((RET_CHUNK, RET_CHUNK))
    rel = rows - lax.broadcasted_iota(jnp.int32, (RET_CHUNK, RET_CHUNK), 1).astype(F32)
    rows_k = _row_index((RET_CHUNK, RET_DK))
    for h in range(RET_HEADS):
        lgf = _log_sigmoid(decf_ref[h:h + 1, :])
        lgb = _log_sigmoid(decb_ref[h:h + 1, :])
        dmat_ref[h] = jnp.where(rel >= 0, jnp.exp(jnp.maximum(rel, 0.0) * lgf),
                                jnp.exp(jnp.maximum(-rel, 0.0) * lgb))
        lgf_k, lgb_k = lgf[:, :RET_DK], lgb[:, :RET_DK]
        qdec_ref[h, :, :RET_DK] = jnp.exp((rows_k + 1.0) * lgf_k)
        qdec_ref[h, :, RET_DK:] = jnp.exp((RET_CHUNK - rows_k) * lgb_k)
        kdf_ref[h] = jnp.exp((RET_CHUNK - 1.0 - rows_k) * lgf_k)


def _main_kernel(x_ref, prew_ref, postw_ref, retnw_ref, wmain_ref, wrb_ref, wab_ref, wo_ref, inv_ref,
                 decf_ref, decb_ref, sink_ref, rkv_ref, akc_ref, akn_ref,
                 avc_ref, avn_ref, sb_ref, kvf0_ref, km_ref, vmt_ref, out_ref,
                 sf_ref, cosb_ref, sinb_ref, bias_ref, u_ref, rq_ref, aq_ref, or_ref, oa_ref, grg_ref, gag_ref,
                 ggr_ref, gga_ref, lhs_r_ref, lhs_a_ref, mix_ref, dmat_ref, qdec_ref, kdf_ref, akp_ref, avp_ref,
                 *, cpt):
    t = pl.program_id(1)
    tm = cpt * CHUNK

    @pl.when((pl.program_id(0) == 0) & (t == 0))
    def _():
        akp_ref[...] = jnp.zeros_like(akp_ref)
        avp_ref[...] = jnp.zeros_like(avp_ref)
        _fill_rope_base(inv_ref[...], cosb_ref, sinb_ref)
        _fill_band_bias(bias_ref)
        _fill_decay_tables(decf_ref, decb_ref, dmat_ref, qdec_ref, kdf_ref)

    @pl.when(t == 0)
    def _():
        sf_ref[...] = kvf0_ref[...]

    for r0 in range(0, tm, CHUNK):
        u_ref[r0:r0 + CHUNK] = _rms_norm(x_ref[r0:r0 + CHUNK], prew_ref[...]).astype(BF16)
    cos, sin = _rope_tables_from_base(N_META + t * tm, inv_ref[...], cosb_ref, sinb_ref)

    def proj(name, c0):
        col = MAIN_OFF[name] + c0
        return _dot(u_ref[...], wmain_ref[:, col:col + COL_BLOCK])

    heads_per_block = COL_BLOCK // ATT_HD
    for c0 in range(0, RET_QK, COL_BLOCK):
        z = proj("rq", c0)
        for i in range(heads_per_block):
            c = c0 + i * RET_DK
            rq_ref[:, c:c + RET_DK] = _rope(z[:, i * RET_DK:(i + 1) * RET_DK], cos, sin)
    for c0 in range(0, ATT_Q, COL_BLOCK):
        z = proj("aq", c0)
        for i in range(heads_per_block):
            c = c0 + i * ATT_HD
            aq_ref[:, c:c + ATT_HD] = (_rope(z[:, i * ATT_HD:(i + 1) * ATT_HD], cos, sin)
                                       * (ATT_HD ** -0.5 * LOG2_E)).astype(BF16)

    gate_jobs = (("rg", _silu, grg_ref), ("ag", _silu, gag_ref), ("gr", _sigmoid, ggr_ref), ("ga", _sigmoid, gga_ref))
    gate_blocks = [(j, c) for j in range(len(gate_jobs)) for c in range(0, D_MODEL, COL_BLOCK)]
    assert len(gate_blocks) % cpt == 0

    def emit_gate_blocks(n):
        for _ in range(n):
            j, c = gate_blocks.pop(0)
            name, act, ref = gate_jobs[j]
            ref[:, c:c + COL_BLOCK] = act(proj(name, c))

    cdf = [jnp.exp(RET_CHUNK * _log_sigmoid(decf_ref[h:h + 1, :])) for h in range(RET_HEADS)]

    k_cat = jnp.concatenate([akp_ref[...], akc_ref[...], akn_ref[...]], axis=0)
    vt_cat = jnp.concatenate([avp_ref[...], avc_ref[...], avn_ref[...]], axis=1)
    sink_rows = [jnp.concatenate([sink_ref[g * ATT_REP + r:g * ATT_REP + r + 1, :] for r in range(ATT_REP)],
                                 axis=1) * LOG2_E for g in range(ATT_GROUPS)]

    def retention(rc):
        r0 = rc * RET_CHUNK
        rs = slice(r0, r0 + RET_CHUNK)
        for h in range(RET_HEADS):
            q = rq_ref[rs, h * RET_DK:(h + 1) * RET_DK]
            k = rkv_ref[rs, h * RET_DK:(h + 1) * RET_DK]
            v = rkv_ref[rs, RET_QK + h * RET_DV:RET_QK + (h + 1) * RET_DV]
            s = _dot_nt(q.astype(BF16), k) * dmat_ref[h]
            sf = sf_ref[h]
            q_cross = (jnp.concatenate([q, q], axis=1) * qdec_ref[h]).astype(BF16)
            s_cross = jnp.concatenate([sf.astype(BF16), sb_ref[rc, h]], axis=0)
            o = _dot(s.astype(BF16), v) + _dot(q_cross, s_cross)
            sf_ref[h] = cdf[h] * sf + _dot_tn((k.astype(F32) * kdf_ref[h]).astype(BF16), v)
            mu = jnp.mean(o, axis=-1, keepdims=True)
            d = o - mu
            var = jnp.mean(d * d, axis=-1, keepdims=True)
            or_ref[rs, h * RET_DV:(h + 1) * RET_DV] = d * lax.rsqrt(var + EPS)

    for rc in range(tm // RET_CHUNK):
        retention(rc)

    gates_per_pair = len(gate_blocks) // (cpt * ATT_GROUPS)
    pairs = [(lc, g) for lc in range(cpt) for g in range(ATT_GROUPS)]
    probs, denoms = {}, {}
    for lc, g in pairs:
        r0 = lc * CHUNK
        chunk = t * cpt + lc
        gs = slice(g * ATT_HD, (g + 1) * ATT_HD)
        qs = jnp.concatenate([aq_ref[r0:r0 + CHUNK, (g * ATT_REP + r) * ATT_HD:(g * ATT_REP + r + 1) * ATT_HD]
                              for r in range(ATT_REP)], axis=0)
        k_all = jnp.concatenate([k_cat[r0:r0 + N_BAND, gs], km_ref[:, gs]], axis=0)
        emit_gate_blocks(gates_per_pair)
        s = _dot_nt(k_all, qs) + bias_ref[...]
        parts = [s[0:CHUNK], s[CHUNK:2 * CHUNK], s[2 * CHUNK:N_BAND], s[N_BAND:]]
        if lc == 0:
            parts[0] = jnp.where(chunk > 0, parts[0], NEG_INF)
        if lc == cpt - 1:
            parts[2] = jnp.where(chunk < N_CHUNKS - 1, parts[2], NEG_INF)
        s = jnp.concatenate(parts, axis=0)
        sk = sink_rows[g]
        m = jnp.maximum(jnp.max(s, axis=0, keepdims=True), sk)
        p = jnp.exp2(s - m)
        denoms[lc, g] = jnp.sum(p, axis=0, keepdims=True) + jnp.exp2(sk - m)
        probs[lc, g] = p.astype(BF16)
    for lc, g in pairs:
        r0 = lc * CHUNK
        gs = slice(g * ATT_HD, (g + 1) * ATT_HD)
        vt_all = jnp.concatenate([vt_cat[gs, r0:r0 + N_BAND], vmt_ref[gs, :]], axis=1)
        ot = _dot(vt_all, probs[lc, g]) / denoms[lc, g]
        for r in range(ATT_REP):
            c = (g * ATT_REP + r) * ATT_HD
            oa_ref[r0:r0 + CHUNK, c:c + ATT_HD] = ot[:, r * CHUNK:(r + 1) * CHUNK].T

    for r0 in range(0, tm, CHUNK):
        rs = slice(r0, r0 + CHUNK)
        lhs_r_ref[rs] = (or_ref[rs] * retnw_ref[...] * grg_ref[rs]).astype(BF16)
        lhs_a_ref[rs] = (oa_ref[rs] * gag_ref[rs]).astype(BF16)
    for c0 in range(0, D_MODEL, COL_BLOCK):
        cs = slice(c0, c0 + COL_BLOCK)
        y_r = _dot(lhs_r_ref[...], wrb_ref[:, cs])
        y_a = _dot(lhs_a_ref[...], wab_ref[:, cs])
        mix_ref[:, cs] = (ggr_ref[:, cs] * y_r + gga_ref[:, cs] * y_a).astype(BF16)
    for c0 in range(0, D_MODEL, COL_BLOCK):
        cs = slice(c0, c0 + COL_BLOCK)
        out_ref[:, cs] = _dot(mix_ref[...], wo_ref[:, cs])
    for r0 in range(0, tm, CHUNK):
        rs = slice(r0, r0 + CHUNK)
        out_ref[rs] = x_ref[rs] + _rms_norm(out_ref[rs], postw_ref[...])
    akp_ref[...] = akc_ref[tm - CHUNK:tm]
    avp_ref[...] = avc_ref[:, tm - CHUNK:tm]


def _resident(shape, index=None):
    nd = len(shape)
    index = (0,) * nd if index is None else index
    return pl.BlockSpec(shape, lambda *_: index, pipeline_mode=pl.Buffered(1))


def kernel(x, meta_tokens, pre_norm_w, w_in, ret_decay_fwd, ret_decay_bwd, ret_norm_w, w_ret_branch, attn_sink,
           w_attn_branch, w_out, post_norm_w):
    B = x.shape[0]
    assert x.shape == (B, SEQ, D_MODEL) and pre_norm_w.shape[0] == 1 and w_in.shape == (1, D_MODEL, D_IN)
    cpt = CHUNKS_PER_TILE
    tm = cpt * CHUNK
    nt = N_CHUNKS // cpt

    w_in = w_in.astype(F32)
    w_rb, w_ab, w_o = (w.astype(F32) for w in (w_ret_branch, w_attn_branch, w_out))
    pre_w = pre_norm_w.astype(F32)
    post_w = post_norm_w.astype(F32)
    ret_nw = ret_norm_w.astype(F32)
    half = ATT_HD // 2
    inv = ROPE_THETA ** (-jnp.arange(half, dtype=F32) * 2.0 / ATT_HD)
    inv = jnp.concatenate([inv, inv])[None, :]
    assert ATT_HEADS == SUBLANES and RET_HEADS <= SUBLANES
    pad = jnp.zeros((SUBLANES - RET_HEADS,), F32)
    per_head = jnp.concatenate([ret_decay_fwd[0].astype(F32), pad, ret_decay_bwd[0].astype(F32), pad,
                                attn_sink[0].astype(F32)])
    per_head = jnp.broadcast_to(per_head[:, None], (2 * SUBLANES + ATT_HEADS, RET_DV))
    dec_f_spec = _resident((SUBLANES, RET_DV), (0, 0))
    dec_b_spec = _resident((SUBLANES, RET_DV), (1, 0))
    sink_spec = _resident((ATT_HEADS, ATT_HD), (2, 0))

    params = pltpu.CompilerParams(dimension_semantics=("arbitrary", "arbitrary"),
                                  vmem_limit_bytes=V7X_VMEM_LIMIT_BYTES)
    rope_scratch = [pltpu.VMEM((tm, ATT_HD), F32), pltpu.VMEM((tm, ATT_HD), F32)]
    state_scratch = pltpu.VMEM((RET_HEADS, RET_DK, RET_DV), F32)
    weight_stage = [pltpu.VMEM((2, D_MODEL, W_CHUNK), F32), pltpu.SemaphoreType.DMA((2,))]
    hbm = pl.BlockSpec(memory_space=pl.ANY)

    kcpt = KV_CHUNKS_PER_TILE
    ktm, knt = kcpt * CHUNK, N_CHUNKS // kcpt
    slab = D_MODEL // (B * knt)
    assert slab * B * knt == D_MODEL and slab % (2 * SUBLANES) == 0

    def slab_in(width):
        return pl.BlockSpec((None, slab, width), lambda b, t: (0, b * knt + t, 0))

    def slab_out(width):
        return pl.BlockSpec((slab, width), lambda b, t: (b * knt + t, 0))

    rkv, ak, avt, sb, kvf0, km, vmt, w_main_b, w_rb_b, w_ab_b, w_o_b = pl.pallas_call(
        functools.partial(_kv_kernel, cpt=kcpt),
        grid=(B, knt),
        in_specs=[pl.BlockSpec((None, ktm, D_MODEL), lambda b, t: (b, knt - 1 - t, 0)), _resident((N_META, D_MODEL)),
                  _resident((1, D_MODEL)), hbm, _resident((1, ATT_HD)), dec_f_spec, dec_b_spec,
                  slab_in(D_IN), slab_in(D_MODEL), slab_in(D_MODEL), slab_in(D_MODEL)],
        out_specs=(
            pl.BlockSpec((None, ktm, RKV_COLS), lambda b, t: (b, knt - 1 - t, 0)),
            pl.BlockSpec((None, ktm, ATT_KV), lambda b, t: (b, knt - 1 - t, 0)),
            pl.BlockSpec((None, ATT_KV, ktm), lambda b, t: (b, 0, knt - 1 - t)),
            pl.BlockSpec((None, ktm // RET_CHUNK, RET_HEADS, RET_DK, RET_DV), lambda b, t: (b, knt - 1 - t, 0, 0, 0)),
            pl.BlockSpec((RET_HEADS, RET_DK, RET_DV), lambda b, t: (0, 0, 0)),
            pl.BlockSpec((N_META, ATT_KV), lambda b, t: (0, 0)),
            pl.BlockSpec((ATT_KV, N_META), lambda b, t: (0, 0)),
            slab_out(MAIN_COLS), slab_out(D_MODEL), slab_out(D_MODEL), slab_out(D_MODEL),
        ),
        out_shape=(jax.ShapeDtypeStruct((B, SEQ, RKV_COLS), BF16),
                   jax.ShapeDtypeStruct((B, SEQ, ATT_KV), BF16),
                   jax.ShapeDtypeStruct((B, ATT_KV, SEQ), BF16),
                   jax.ShapeDtypeStruct((B, N_RET_CHUNKS, RET_HEADS, RET_DK, RET_DV), BF16),
                   jax.ShapeDtypeStruct((RET_HEADS, RET_DK, RET_DV), F32),
                   jax.ShapeDtypeStruct((N_META, ATT_KV), BF16),
                   jax.ShapeDtypeStruct((ATT_KV, N_META), BF16),
                   jax.ShapeDtypeStruct((D_MODEL, MAIN_COLS), BF16),
                   jax.ShapeDtypeStruct((D_MODEL, D_MODEL), BF16),
                   jax.ShapeDtypeStruct((D_MODEL, D_MODEL), BF16),
                   jax.ShapeDtypeStruct((D_MODEL, D_MODEL), BF16)),
        scratch_shapes=[state_scratch, pltpu.VMEM((ktm, ATT_HD), F32), pltpu.VMEM((ktm, ATT_HD), F32),
                        pltpu.VMEM((D_MODEL, KV_COLS), BF16)] + weight_stage,
        compiler_params=params,
        name="kv",
    )(x, meta_tokens.astype(F32), pre_w, w_in, inv, per_head, per_head, w_in, w_rb, w_ab, w_o)

    next_chunk = lambda t: jnp.minimum((t + 1) * cpt, N_CHUNKS - 1)
    out = pl.pallas_call(
        functools.partial(_main_kernel, cpt=cpt),
        grid=(B, nt),
        in_specs=[
            pl.BlockSpec((None, tm, D_MODEL), lambda b, t: (b, t, 0)),
            _resident((1, D_MODEL)),
            _resident((1, D_MODEL)),
            _resident((1, RET_V)),
            _resident((D_MODEL, MAIN_COLS)),
            _resident((RET_V, D_MODEL)),
            _resident((ATT_Q, D_MODEL)),
            _resident((D_MODEL, D_MODEL)),
            _resident((1, ATT_HD)),
            dec_f_spec,
            dec_b_spec,
            sink_spec,
            pl.BlockSpec((None, tm, RKV_COLS), lambda b, t: (b, t, 0)),
            pl.BlockSpec((None, tm, ATT_KV), lambda b, t: (b, t, 0)),
            pl.BlockSpec((None, CHUNK, ATT_KV), lambda b, t: (b, next_chunk(t), 0)),
            pl.BlockSpec((None, ATT_KV, tm), lambda b, t: (b, 0, t)),
            pl.BlockSpec((None, ATT_KV, CHUNK), lambda b, t: (b, 0, next_chunk(t))),
            pl.BlockSpec((None, tm // RET_CHUNK, RET_HEADS, RET_DK, RET_DV), lambda b, t: (b, t, 0, 0, 0)),
            _resident((RET_HEADS, RET_DK, RET_DV)),
            _resident((N_META, ATT_KV)),
            _resident((ATT_KV, N_META)),
        ],
        out_specs=pl.BlockSpec((None, tm, D_MODEL), lambda b, t: (b, t, 0)),
        out_shape=jax.ShapeDtypeStruct((B, SEQ, D_MODEL), x.dtype),
        scratch_shapes=(
            [state_scratch] + rope_scratch
            + [pltpu.VMEM((N_KEYS, Q_ROWS), F32),
               pltpu.VMEM((tm, D_MODEL), BF16),
               pltpu.VMEM((tm, RET_QK), F32),
               pltpu.VMEM((tm, ATT_Q), BF16),
               pltpu.VMEM((tm, RET_V), F32),
               pltpu.VMEM((tm, ATT_Q), F32)]
            + [pltpu.VMEM((tm, D_MODEL), F32)] * 4
            + [pltpu.VMEM((tm, D_MODEL), BF16)] * 3
            + [pltpu.VMEM((RET_HEADS, RET_CHUNK, RET_CHUNK), F32),
               pltpu.VMEM((RET_HEADS, RET_CHUNK, 2 * RET_DK), F32),
               pltpu.VMEM((RET_HEADS, RET_CHUNK, RET_DK), F32)]
            + [pltpu.VMEM((CHUNK, ATT_KV), BF16), pltpu.VMEM((ATT_KV, CHUNK), BF16)]
        ),
        compiler_params=params,
        name="main",
    )(x, pre_w, post_w, ret_nw, w_main_b, w_rb_b, w_ab_b, w_o_b, inv, per_head, per_head, per_head,
      rkv, ak, ak, avt, avt, sb, kvf0, km, vmt)
    return out
```
